```python
import math
import jax, jax.numpy as jnp
from jax import lax
import numpy as np

D_MODEL = 1024
BATCH = 8
SEQ = 2048
DEPTH = 1

CHUNK = 64
CONV_A_GROUPS = 8
CONV_A_GROUP_DIM = D_MODEL // 16
CONV_A_DIM = CONV_A_GROUPS * CONV_A_GROUP_DIM
CONV_A_WIDTH = 3
DN_HEADS = 4
DN_HEAD_DIM = D_MODEL // 8
DN_DIM = DN_HEADS * DN_HEAD_DIM
DN_CONV_WIDTH = 4
MIX_DIM = CONV_A_DIM + DN_DIM
IN_PROJ_DIM = 3 * CONV_A_DIM + 4 * DN_DIM + 2 * DN_HEADS
N_GROUPS = 4
EXPERTS_PER_GROUP = 8
N_EXPERTS = N_GROUPS * EXPERTS_PER_GROUP
TOP_K = 2
EXPERT_FF = D_MODEL // 2
MOE_BLOCK = 128
EPS = 1e-6

kernel_name = "hybrid_conv_deltanet_hmoe_block"


def rms_norm(x, g):
    xf = x.astype(jnp.float32)
    y = xf * lax.rsqrt(jnp.mean(xf * xf, axis=-1, keepdims=True) + EPS)
    return (y * g.astype(jnp.float32)).astype(x.dtype)


def causal_dwconv(x, w):
    K = w.shape[0]
    S = x.shape[1]
    xp = jnp.pad(x, ((0, 0), (K - 1, 0), (0, 0)))
    y = xp[:, 0:S] * w[0]
    for j in range(1, K):
        y = y + xp[:, j:j + S] * w[j]
    return y


def short_conv_mixer(hx, b, c, conv_w, norm_g):
    Bsz, S, _ = hx.shape
    y = b * causal_dwconv(c * hx, conv_w)
    y = y.reshape(Bsz, S, CONV_A_GROUPS, CONV_A_GROUP_DIM)
    y = rms_norm(y, norm_g.reshape(CONV_A_GROUPS, CONV_A_GROUP_DIM))
    return y.reshape(Bsz, S, CONV_A_DIM)


def chunk_gated_delta(q, k, v, beta, g):
    Bsz, H, S, dk = q.shape
    dv = v.shape[-1]
    n = S // CHUNK

    def chunks(t):
        return t.reshape((Bsz, H, n, CHUNK) + t.shape[3:])

    q, k, v, beta, g = chunks(q), chunks(k), chunks(v), chunks(beta), chunks(g)
    g = jnp.cumsum(g, axis=-1)
    causal = jnp.tril(jnp.ones((CHUNK, CHUNK), bool))
    strict = jnp.tril(jnp.ones((CHUNK, CHUNK), bool), -1)
    decay = jnp.exp(jnp.where(causal, g[..., :, None] - g[..., None, :], -jnp.inf))
    k_beta = k * beta[..., None]
    L = jnp.where(strict, jnp.einsum('bhncd,bhnjd->bhncj', k_beta, k) * decay, 0.0)
    eye = jnp.eye(CHUNK, dtype=jnp.float32)
    rhs = jnp.concatenate([v * beta[..., None], k_beta * jnp.exp(g)[..., None]], axis=-1)
    uw = lax.linalg.triangular_solve(eye + L, rhs, left_side=True, lower=True, unit_diagonal=True)
    u, w = uw[..., :dv], uw[..., dv:]
    intra = jnp.einsum('bhncd,bhnjd->bhncj', q, k) * decay
    q_dec = q * jnp.exp(g)[..., None]
    g_last = g[..., -1]
    k_dec = k * jnp.exp(g_last[..., None] - g)[..., None]

    def to_scan(t):
        return jnp.moveaxis(t, 2, 0)

    xs = (to_scan(u), to_scan(w), to_scan(q_dec), to_scan(intra), to_scan(k_dec), to_scan(g_last))

    def step(state, inp):
        u_i, w_i, qd_i, a_i, kd_i, gl_i = inp
        v_new = u_i - jnp.einsum('bhck,bhkv->bhcv', w_i, state)
        o_i = jnp.einsum('bhck,bhkv->bhcv', qd_i, state) + jnp.einsum('bhcj,bhjv->bhcv', a_i, v_new)
        state = state * jnp.exp(gl_i)[..., None, None] + jnp.einsum('bhck,bhcv->bhkv', kd_i, v_new)
        return state, o_i

    s0 = jnp.zeros((Bsz, H, dk, dv), jnp.float32)
    _, o = lax.scan(step, s0, xs)
    return jnp.moveaxis(o, 0, 2).reshape(Bsz, H, S, dv)


def gated_deltanet(q, k, v, z, beta_logit, a_logit, conv_w, a_log, dt_bias, norm_g):
    Bsz, S, _ = q.shape
    dtype = q.dtype
    qkv = jax.nn.silu(causal_dwconv(jnp.concatenate([q, k, v], axis=-1), conv_w)).astype(jnp.float32)
    q, k, v = jnp.split(qkv, 3, axis=-1)

    def heads(t):
        return t.reshape(Bsz, S, DN_HEADS, DN_HEAD_DIM).transpose(0, 2, 1, 3)

    q, k, v = heads(q), heads(k), heads(v)
    q = q * lax.rsqrt(jnp.sum(q * q, axis=-1, keepdims=True) + EPS) * (DN_HEAD_DIM ** -0.5)
    k = k * lax.rsqrt(jnp.sum(k * k, axis=-1, keepdims=True) + EPS)
    beta = jax.nn.sigmoid(beta_logit.astype(jnp.float32)).transpose(0, 2, 1)
    g = (-jnp.exp(a_log.astype(jnp.float32))
         * jax.nn.softplus(a_logit.astype(jnp.float32) + dt_bias.astype(jnp.float32))).transpose(0, 2, 1)
    o = chunk_gated_delta(q, k, v, beta, g).transpose(0, 2, 1, 3)
    zf = z.astype(jnp.float32).reshape(Bsz, S, DN_HEADS, DN_HEAD_DIM)
    o = rms_norm(o, norm_g) * jax.nn.silu(zf)
    return o.reshape(Bsz, S, DN_DIM).astype(dtype)


def hier_moe(h, router_group_w, router_expert_w, w_gate, w_up, w_down):
    N, D = h.shape
    group_logits = (h @ router_group_w).astype(jnp.float32)
    group_idx = jnp.argmax(group_logits, axis=-1)
    group_prob = jnp.take_along_axis(jax.nn.softmax(group_logits, axis=-1), group_idx[:, None], axis=1)
    exp_logits = (h @ router_expert_w).astype(jnp.float32).reshape(N, N_GROUPS, EXPERTS_PER_GROUP)
    exp_logits = jnp.take_along_axis(exp_logits, group_idx[:, None, None], axis=1)[:, 0]
    top_p, top_local = lax.top_k(jax.nn.softmax(exp_logits, axis=-1), TOP_K)
    gate = group_prob * top_p / jnp.sum(top_p, axis=-1, keepdims=True)
    expert_id = (group_idx[:, None] * EXPERTS_PER_GROUP + top_local).astype(jnp.int32)

    flat_e = expert_id.reshape(-1)
    flat_tok = jnp.repeat(jnp.arange(N, dtype=jnp.int32), TOP_K)
    flat_gate = gate.reshape(-1)
    order = jnp.argsort(flat_e)
    sorted_e = flat_e[order]
    counts = jnp.bincount(flat_e, length=N_EXPERTS)
    padded = (counts + MOE_BLOCK - 1) // MOE_BLOCK * MOE_BLOCK
    pad_end = jnp.cumsum(padded)
    pad_start = pad_end - padded
    start = jnp.cumsum(counts) - counts
    dest = pad_start[sorted_e] + jnp.arange(N * TOP_K, dtype=jnp.int32) - start[sorted_e]
    P = N * TOP_K + N_EXPERTS * MOE_BLOCK
    slot_tok = jnp.full((P,), N, jnp.int32).at[dest].set(flat_tok[order])
    slot_gate = jnp.zeros((P,), jnp.float32).at[dest].set(flat_gate[order])
    n_blocks = P // MOE_BLOCK
    block_expert = jnp.minimum(
        jnp.searchsorted(pad_end, jnp.arange(n_blocks, dtype=jnp.int32) * MOE_BLOCK, side='right'),
        N_EXPERTS - 1)
    h_pad = jnp.concatenate([h, jnp.zeros((1, D), h.dtype)], axis=0)
    xb = h_pad[slot_tok].reshape(n_blocks, MOE_BLOCK, D)

    def expert_block(args):
        xblk, e = args
        a = xblk @ w_gate[e]
        b = xblk @ w_up[e]
        return (jax.nn.silu(a) * b) @ w_down[e]

    yb = lax.map(expert_block, (xb, block_expert)).reshape(P, D)
    out = jnp.zeros((N + 1, D), jnp.float32).at[slot_tok].add(yb.astype(jnp.float32) * slot_gate[:, None])
    return out[:N].astype(h.dtype)


def setup_inputs(seed: int = 0) -> dict:
    key = jax.random.key(seed)
    ks = jax.random.split(key, 18)
    f32 = jnp.float32

    def normal(k, shape, scale):
        return jax.random.normal(k, shape, f32) * scale

    def gain(k, shape):
        return 1.0 + 0.05 * jax.random.normal(k, shape, f32)

    dt = jnp.exp(jax.random.uniform(ks[7], (DEPTH, DN_HEADS), f32, math.log(1e-3), math.log(1e-1)))
    return {
        "x": normal(ks[0], (BATCH, SEQ, D_MODEL), 1.0),
        "mix_norm_g": gain(ks[1], (DEPTH, D_MODEL)),
        "w_in": normal(ks[2], (DEPTH, D_MODEL, IN_PROJ_DIM), D_MODEL ** -0.5),
        "conv_a_w": normal(ks[3], (DEPTH, CONV_A_WIDTH, CONV_A_DIM), CONV_A_WIDTH ** -0.5),
        "conv_a_norm_g": gain(ks[4], (DEPTH, CONV_A_DIM)),
        "dn_conv_w": normal(ks[5], (DEPTH, DN_CONV_WIDTH, 3 * DN_DIM), DN_CONV_WIDTH ** -0.5),
        "dn_a_log": jnp.log(jax.random.uniform(ks[6], (DEPTH, DN_HEADS), f32, 1.0, 16.0)),
        "dn_dt_bias": dt + jnp.log(-jnp.expm1(-dt)),
        "dn_norm_g": gain(ks[8], (DEPTH, DN_HEAD_DIM)),
        "w_out": normal(ks[9], (DEPTH, MIX_DIM, D_MODEL), MIX_DIM ** -0.5),
        "ffn_norm_g": gain(ks[10], (DEPTH, D_MODEL)),
        "router_group_w": normal(ks[11], (DEPTH, D_MODEL, N_GROUPS), D_MODEL ** -0.5),
        "router_expert_w": normal(ks[12], (DEPTH, D_MODEL, N_EXPERTS), D_MODEL ** -0.5),
        "w_gate": normal(ks[13], (DEPTH, N_EXPERTS, D_MODEL, EXPERT_FF), D_MODEL ** -0.5),
        "w_up": normal(ks[14], (DEPTH, N_EXPERTS, D_MODEL, EXPERT_FF), D_MODEL ** -0.5),
        "w_down": normal(ks[15], (DEPTH, N_EXPERTS, EXPERT_FF, D_MODEL), EXPERT_FF ** -0.5),
        "final_norm_g": gain(ks[16], (D_MODEL,)),
    }


def reference(x, mix_norm_g, w_in, conv_a_w, conv_a_norm_g, dn_conv_w, dn_a_log, dn_dt_bias,
              dn_norm_g, w_out, ffn_norm_g, router_group_w, router_expert_w, w_gate, w_up, w_down,
              final_norm_g):
    Bsz, S, D = x.shape
    sizes = (CONV_A_DIM,) * 3 + (DN_DIM,) * 4 + (DN_HEADS,) * 2
    split_idx = []
    acc = 0
    for s in sizes[:-1]:
        acc += s
        split_idx.append(acc)
    for l in range(DEPTH):
        h = rms_norm(x, mix_norm_g[l])
        proj = h @ w_in[l]
        a_h, a_b, a_c, d_q, d_k, d_v, d_z, d_beta, d_alpha = jnp.split(proj, split_idx, axis=-1)
        y_a = short_conv_mixer(a_h, a_b, a_c, conv_a_w[l], conv_a_norm_g[l])
        y_b = gated_deltanet(d_q, d_k, d_v, d_z, d_beta, d_alpha, dn_conv_w[l], dn_a_log[l],
                             dn_dt_bias[l], dn_norm_g[l])
        x = x + jnp.concatenate([y_a, y_b], axis=-1) @ w_out[l]
        h = rms_norm(x, ffn_norm_g[l]).reshape(Bsz * S, D)
        x = x + hier_moe(h, router_group_w[l], router_expert_w[l], w_gate[l], w_up[l], w_down[l]).reshape(Bsz, S, D)
    return rms_norm(x, final_norm_g)
```

```python
import functools

import jax
import jax.numpy as jnp
from jax import lax
from jax.experimental import pallas as pl
from jax.experimental.pallas import tpu as pltpu

F32 = jnp.float32
BF16 = jnp.bfloat16
I32 = jnp.int32

D_MODEL = 1024
CHUNK = 64
CONV_A_GROUP_DIM = 64
CONV_A_DIM = 512
CONV_A_WIDTH = 3
DN_HEADS = 4
DN_HEAD_DIM = 128
DN_DIM = 512
DN_CONV_WIDTH = 4
IN_PROJ_DIM = 3 * CONV_A_DIM + 4 * DN_DIM + 2 * DN_HEADS
N_GROUPS = 4
EXPERTS_PER_GROUP = 8
N_EXPERTS = 32
TOP_K = 2
EXPERT_FF = 512
EPS = 1e-6

LANES = 128
HALO = 8
IN_PROJ_PAD = 29 * LANES
GATE_COL = 3 * CONV_A_DIM + 4 * DN_DIM

TM_IN = 512
HEADS_PER_STEP = 2
TM_OUT = 512
FFN_BLOCK = 256
TM_DISPATCH = 512
TM_COMBINE = 256
VMEM_LIMIT = 56 * 1024 * 1024


def _dot(a, b):
    return jnp.dot(a, b, preferred_element_type=F32)


def _dot_nt(a, b):
    return lax.dot_general(a, b, (((1,), (1,)), ((), ())), preferred_element_type=F32)


def _dot_tn(a, b):
    return lax.dot_general(a, b, (((0,), (0,)), ((), ())), preferred_element_type=F32)


def _silu(x):
    return x * jax.nn.sigmoid(x)


def _softplus(x):
    return jnp.maximum(x, 0.0) + jnp.log1p(jnp.exp(-jnp.abs(x)))


def _inproj_kernel(x_ref, g_ref, w_ref, caw_ref, cag_ref, dcw_ref, alog_ref, dtb_ref, gmat_ref,
                   ya_ref, q_ref, k_ref, v_ref, zg_ref, gcol_ref, grow_ref,
                   exta_ref, extq_ref):
    tm = x_ref.shape[1]

    @pl.when(pl.program_id(1) == 0)
    def _():
        exta_ref[0:HALO, :] = jnp.zeros((HALO, CONV_A_DIM), F32)
        extq_ref[0:HALO, :] = jnp.zeros((HALO, 3 * DN_DIM), F32)

    x = x_ref[0]
    ms = jnp.mean(x * x, axis=-1, keepdims=True)
    hb = (x * lax.rsqrt(ms + EPS) * g_ref[...]).astype(BF16)

    def proj(c0, width):
        return _dot(hb, w_ref[:, c0:c0 + width])

    a_h = proj(0, CONV_A_DIM)
    a_c = proj(2 * CONV_A_DIM, CONV_A_DIM)
    exta_ref[HALO:HALO + tm, :] = a_c * a_h
    conv = caw_ref[0:1, :] * exta_ref[pl.ds(HALO - 2, tm), :]
    conv = conv + caw_ref[1:2, :] * exta_ref[pl.ds(HALO - 1, tm), :]
    conv = conv + caw_ref[2:3, :] * exta_ref[pl.ds(HALO, tm), :]
    y = proj(CONV_A_DIM, CONV_A_DIM) * conv
    ysq = y * y
    hi = ysq.astype(BF16)
    lo = (ysq - hi.astype(F32)).astype(BF16)
    gmean = _dot(hi, gmat_ref[...]) + _dot(lo, gmat_ref[...])
    ya_ref[0] = y * lax.rsqrt(gmean + EPS) * cag_ref[...]
    exta_ref[0:HALO, :] = exta_ref[tm:tm + HALO, :]

    base = 3 * CONV_A_DIM
    for i in range(3):
        extq_ref[HALO:HALO + tm, i * DN_DIM:(i + 1) * DN_DIM] = proj(base + i * DN_DIM, DN_DIM)
    for i, out_ref in enumerate((q_ref, k_ref, v_ref)):
        cols = slice(i * DN_DIM, (i + 1) * DN_DIM)
        acc = dcw_ref[0:1, cols] * extq_ref[pl.ds(HALO - 3, tm), cols]
        for j in range(1, DN_CONV_WIDTH):
            acc = acc + dcw_ref[j:j + 1, cols] * extq_ref[pl.ds(HALO - 3 + j, tm), cols]
        s = _silu(acc)
        if i == 2:
            out_ref[0] = s
        else:
            for h in range(DN_HEADS):
                sh = s[:, h * DN_HEAD_DIM:(h + 1) * DN_HEAD_DIM]
                inv = lax.rsqrt(jnp.sum(sh * sh, axis=-1, keepdims=True) + EPS)
                sh = sh * inv
                if i == 0:
                    sh = sh * (DN_HEAD_DIM ** -0.5)
                out_ref[0, :, h * DN_HEAD_DIM:(h + 1) * DN_HEAD_DIM] = sh
    extq_ref[0:HALO, :] = extq_ref[tm:tm + HALO, :]

    zg_ref[0] = _silu(proj(base + 3 * DN_DIM, DN_DIM))

    p = proj(GATE_COL, LANES)
    beta = jax.nn.sigmoid(p)
    g = -jnp.exp(alog_ref[...]) * _softplus(p + dtb_ref[...])
    r = lax.broadcasted_iota(I32, (tm, tm), 0)
    c = lax.broadcasted_iota(I32, (tm, tm), 1)
    tri = jnp.where((r // CHUNK == c // CHUNK) & (c <= r), 1.0, 0.0).astype(F32)
    gc = jnp.dot(tri, g, preferred_element_type=F32, precision=lax.Precision.HIGHEST)
    lane = lax.broadcasted_iota(I32, (tm, LANES), 1)
    slab = jnp.where(lane < DN_HEADS, beta, gc)
    gcol_ref[0] = slab
    rows = slab.T[0:HALO, :]
    for ci in range(tm // CHUNK):
        grow_ref[0, ci] = rows[:, ci * CHUNK:(ci + 1) * CHUNK]


def _inproj(x, mix_g, w_in_pad, conv_a_w, conv_a_g, dn_conv_w, alog_vec, dtb_vec, gmat):
    B, S, D = x.shape
    tm = TM_IN
    full = lambda shape: pl.BlockSpec(shape, lambda b, t: (0,) * len(shape))
    row = lambda width: pl.BlockSpec((1, tm, width), lambda b, t: (b, t, 0))
    out_shape = (
        jax.ShapeDtypeStruct((B, S, CONV_A_DIM), F32),
        jax.ShapeDtypeStruct((B, S, DN_DIM), F32),
        jax.ShapeDtypeStruct((B, S, DN_DIM), F32),
        jax.ShapeDtypeStruct((B, S, DN_DIM), F32),
        jax.ShapeDtypeStruct((B, S, DN_DIM), F32),
        jax.ShapeDtypeStruct((B, S, LANES), F32),
        jax.ShapeDtypeStruct((B, S // CHUNK, HALO, CHUNK), F32),
    )
    return pl.pallas_call(
        _inproj_kernel,
        grid=(B, S // tm),
        in_specs=[row(D), full((1, D)), full((D, IN_PROJ_PAD)), full((CONV_A_WIDTH, CONV_A_DIM)),
                  full((1, CONV_A_DIM)), full((DN_CONV_WIDTH, 3 * DN_DIM)), full((1, LANES)),
                  full((1, LANES)), full((CONV_A_DIM, CONV_A_DIM))],
        out_specs=(row(CONV_A_DIM), row(DN_DIM), row(DN_DIM), row(DN_DIM), row(DN_DIM), row(LANES),
                   pl.BlockSpec((1, tm // CHUNK, HALO, CHUNK), lambda b, t: (b, t, 0, 0))),
        out_shape=out_shape,
        scratch_shapes=[pltpu.VMEM((tm + HALO, CONV_A_DIM), F32),
                        pltpu.VMEM((tm + HALO, 3 * DN_DIM), F32)],
        compiler_params=pltpu.CompilerParams(
            dimension_semantics=("arbitrary", "arbitrary"), vmem_limit_bytes=VMEM_LIMIT),
        name="inproj",
    )(x, mix_g, w_in_pad, conv_a_w, conv_a_g, dn_conv_w, alog_vec, dtb_vec, gmat)


def _delta_kernel(q_ref, k_ref, v_ref, zg_ref, gcol_ref, grow_ref, ng_ref, o_ref,
                  u_ref, w_ref, qd_ref, kd_ref, at_ref, st_ref):
    S = q_ref.shape[1]
    n_chunks = S // CHUNK
    hp = HEADS_PER_STEP
    head0 = pl.program_id(1) * hp
    ri = lax.broadcasted_iota(I32, (CHUNK, CHUNK), 0)
    ci = lax.broadcasted_iota(I32, (CHUNK, CHUNK), 1)
    causal = ci <= ri
    strict = ci < ri
    eye = jnp.where(ci == ri, 1.0, 0.0).astype(F32)
    lane = lax.broadcasted_iota(I32, (CHUNK, LANES), 1)

    def prep(c, carry):
        r0 = pl.multiple_of(c * CHUNK, CHUNK)
        rows = pl.ds(r0, CHUNK)
        slab = gcol_ref[0, rows, :]
        for j in range(hp):
            cols = slice(j * DN_HEAD_DIM, (j + 1) * DN_HEAD_DIM)
            hh = head0 + j
            beta = jnp.sum(jnp.where(lane == hh, slab, 0.0), axis=-1, keepdims=True)
            gcc = jnp.sum(jnp.where(lane == hh + DN_HEADS, slab, 0.0), axis=-1, keepdims=True)
            gcr = grow_ref[0, c, pl.ds(hh + DN_HEADS, 1), :]
            diff = gcc - gcr
            decay = jnp.where(causal, jnp.exp(jnp.where(causal, diff, 0.0)), 0.0)
            q = q_ref[0, rows, cols]
            k = k_ref[0, rows, cols]
            v = v_ref[0, rows, cols]
            kb = k * beta
            k16 = k.astype(BF16)
            L = jnp.where(strict, _dot_nt(kb.astype(BF16), k16) * decay, 0.0)
            intra = _dot_nt(q.astype(BF16), k16) * decay
            m = -L
            t = eye + m
            for _ in range(5):
                m16 = m.astype(BF16)
                m = _dot(m16, m16)
                t = t + _dot(t.astype(BF16), m.astype(BF16))
            egc = jnp.exp(gcc)
            rhs = jnp.concatenate([v * beta, kb * egc], axis=-1)
            uw = _dot(t.astype(BF16), rhs.astype(BF16))
            gl = gcr[:, CHUNK - 1:CHUNK]
            u_ref[rows, cols] = uw[:, :DN_HEAD_DIM]
            w_ref[rows, cols] = uw[:, DN_HEAD_DIM:].astype(BF16)
            qd_ref[rows, cols] = (q * egc).astype(BF16)
            kd_ref[rows, cols] = (k * jnp.exp(gl - gcc)).astype(BF16)
            at_ref[rows, j * CHUNK:(j + 1) * CHUNK] = intra.astype(BF16)
        return carry

    lax.fori_loop(0, n_chunks, prep, 0)

    st_ref[...] = jnp.zeros(st_ref.shape, F32)

    def scan(c, carry):
        r0 = pl.multiple_of(c * CHUNK, CHUNK)
        rows = pl.ds(r0, CHUNK)
        for j in range(hp):
            cols = slice(j * DN_HEAD_DIM, (j + 1) * DN_HEAD_DIM)
            hh = head0 + j
            gl = grow_ref[0, c, pl.ds(hh + DN_HEADS, 1), CHUNK - 1:CHUNK]
            st = st_ref[j]
            st16 = st.astype(BF16)
            v_new = u_ref[rows, cols] - _dot(w_ref[rows, cols], st16)
            vn16 = v_new.astype(BF16)
            o = _dot(qd_ref[rows, cols], st16) + _dot(at_ref[rows, j * CHUNK:(j + 1) * CHUNK], vn16)
            st_ref[j] = st * jnp.exp(gl) + _dot_tn(kd_ref[rows, cols], vn16)
            on = o * lax.rsqrt(jnp.mean(o * o, axis=-1, keepdims=True) + EPS) * ng_ref[...]
            o_ref[0, rows, cols] = on * zg_ref[0, rows, cols]
        return carry

    lax.fori_loop(0, n_chunks, scan, 0)


def _delta(q, k, v, zg, gcol, grow, norm_g):
    B, S, _ = q.shape
    hp = HEADS_PER_STEP
    width = hp * DN_HEAD_DIM
    blk = pl.BlockSpec((1, S, width), lambda b, h: (b, 0, h))
    return pl.pallas_call(
        _delta_kernel,
        grid=(B, DN_HEADS // hp),
        in_specs=[blk, blk, blk, blk,
                  pl.BlockSpec((1, S, LANES), lambda b, h: (b, 0, 0)),
                  pl.BlockSpec((1, S // CHUNK, HALO, CHUNK), lambda b, h: (b, 0, 0, 0)),
                  pl.BlockSpec((1, DN_HEAD_DIM), lambda b, h: (0, 0))],
        out_specs=blk,
        out_shape=jax.ShapeDtypeStruct((B, S, DN_DIM), F32),
        scratch_shapes=[pltpu.VMEM((S, width), F32),
                        pltpu.VMEM((S, width), BF16),
                        pltpu.VMEM((S, width), BF16),
                        pltpu.VMEM((S, width), BF16),
                        pltpu.VMEM((S, hp * CHUNK), BF16),
                        pltpu.VMEM((hp, DN_HEAD_DIM, DN_HEAD_DIM), F32)],
        compiler_params=pltpu.CompilerParams(
            dimension_semantics=("arbitrary", "arbitrary"), vmem_limit_bytes=VMEM_LIMIT),
        name="delta",
    )(q, k, v, zg, gcol, grow, norm_g)


def _outproj_kernel(ya_ref, yb_ref, x_ref, wo_ref, g_ref, rw_ref,
                    x1_ref, h2_ref, ri_ref, rg_ref, cnt_ref, base_ref):
    tm = x_ref.shape[0]

    @pl.when(pl.program_id(0) == 0)
    def _():
        base_ref[...] = jnp.zeros(base_ref.shape, F32)

    y = _dot(ya_ref[...].astype(BF16), wo_ref[0:CONV_A_DIM, :])
    y = y + _dot(yb_ref[...].astype(BF16), wo_ref[CONV_A_DIM:, :])
    x1 = x_ref[...] + y
    x1_ref[...] = x1
    h = x1 * lax.rsqrt(jnp.mean(x1 * x1, axis=-1, keepdims=True) + EPS) * g_ref[...]
    h2_ref[...] = h

    logits = jnp.dot(h, rw_ref[...], preferred_element_type=F32, precision=lax.Precision.HIGHEST)
    lane = lax.broadcasted_iota(I32, (tm, LANES), 1)
    lanef = lane.astype(F32)
    neg = jnp.float32(-1e30)
    big = jnp.float32(1e9)
    is_g = lane < N_GROUPS
    gl = jnp.where(is_g, logits, neg)
    gmax = jnp.max(gl, axis=-1, keepdims=True)
    gidx = jnp.min(jnp.where(gl == gmax, lanef, big), axis=-1, keepdims=True)
    gsum = jnp.sum(jnp.where(is_g, jnp.exp(gl - gmax), 0.0), axis=-1, keepdims=True)
    gprob = 1.0 / gsum
    lo = N_GROUPS + EXPERTS_PER_GROUP * gidx
    emask = (lanef >= lo) & (lanef < lo + EXPERTS_PER_GROUP)
    el = jnp.where(emask, logits, neg)
    e1 = jnp.max(el, axis=-1, keepdims=True)
    i1 = jnp.min(jnp.where(el == e1, lanef, big), axis=-1, keepdims=True)
    el2 = jnp.where(lanef == i1, neg, el)
    e2 = jnp.max(el2, axis=-1, keepdims=True)
    i2 = jnp.min(jnp.where(el2 == e2, lanef, big), axis=-1, keepdims=True)
    r = jnp.exp(e2 - e1)
    gate1 = gprob / (1.0 + r)
    gate2 = gprob * r / (1.0 + r)
    id1 = i1 - N_GROUPS
    id2 = i2 - N_GROUPS

    oh1 = jnp.where(lanef == id1, 1.0, 0.0).astype(F32)
    oh2 = jnp.where(lanef == id2, 1.0, 0.0).astype(F32)
    oh = oh1 + oh2
    rr = lax.broadcasted_iota(I32, (tm, tm), 0)
    cc = lax.broadcasted_iota(I32, (tm, tm), 1)
    tri = jnp.where(cc < rr, 1.0, 0.0).astype(BF16)
    before = _dot(tri, oh.astype(BF16)) + base_ref[...]
    rank1 = jnp.sum(oh1 * before, axis=-1, keepdims=True)
    rank2 = jnp.sum(oh2 * before, axis=-1, keepdims=True)
    base_ref[...] = base_ref[...] + jnp.sum(oh, axis=0, keepdims=True)
    cnt_ref[...] = base_ref[...]

    ri = jnp.where(lane == 0, id1, jnp.where(lane == 1, id2,
                   jnp.where(lane == 2, rank1, jnp.where(lane == 3, rank2, 0.0))))
    ri_ref[...] = ri.astype(I32)
    rg_ref[...] = jnp.where(lane == 0, gate1, jnp.where(lane == 1, gate2, 0.0))


def _outproj(ya, yb, x, w_out16, ffn_g, rw):
    N, D = x.shape
    tm = TM_OUT
    row = lambda width: pl.BlockSpec((tm, width), lambda i: (i, 0))
    full = lambda shape: pl.BlockSpec(shape, lambda i: (0,) * len(shape))
    return pl.pallas_call(
        _outproj_kernel,
        grid=(N // tm,),
        in_specs=[row(CONV_A_DIM), row(DN_DIM), row(D), full((D, D)), full((1, D)), full((D, LANES))],
        out_specs=(row(D), row(D), row(LANES), row(LANES), full((1, LANES))),
        out_shape=(jax.ShapeDtypeStruct((N, D), F32),
                   jax.ShapeDtypeStruct((N, D), F32),
                   jax.ShapeDtypeStruct((N, LANES), I32),
                   jax.ShapeDtypeStruct((N, LANES), F32),
                   jax.ShapeDtypeStruct((1, LANES), F32)),
        scratch_shapes=[pltpu.VMEM((1, LANES), F32)],
        compiler_params=pltpu.CompilerParams(
            dimension_semantics=("arbitrary",), vmem_limit_bytes=VMEM_LIMIT),
        name="outproj",
    )(ya, yb, x, w_out16, ffn_g, rw)


def _slot_of(route_ref, start_ref, t, k):
    return start_ref[route_ref[4 * t + k]] + route_ref[4 * t + 2 + k]


def _dispatch_kernel(start_ref, route_ref, h_ref, xs_in_ref, xs_ref, sem):
    del xs_in_ref
    tm = TM_DISPATCH
    tok0 = pl.program_id(0) * tm

    def row_copy(t, k):
        d = _slot_of(route_ref, start_ref, t, k)
        return pltpu.make_async_copy(h_ref.at[pl.ds(tok0 + t, 1)], xs_ref.at[pl.ds(d, 1)], sem)

    def issue(t, carry):
        row_copy(t, 0).start()
        row_copy(t, 1).start()
        return carry

    def drain(t, carry):
        row_copy(t, 0).wait()
        row_copy(t, 1).wait()
        return carry

    lax.fori_loop(0, tm, issue, 0)
    lax.fori_loop(0, tm, drain, 0)


def _dispatch(seg_start, route, h2, xs_init):
    N, D = h2.shape
    tm = TM_DISPATCH
    grid_spec = pltpu.PrefetchScalarGridSpec(
        num_scalar_prefetch=1,
        grid=(N // tm,),
        in_specs=[pl.BlockSpec((4 * tm,), lambda i, s: (i,), memory_space=pltpu.SMEM),
                  pl.BlockSpec(memory_space=pl.ANY),
                  pl.BlockSpec(memory_space=pl.ANY)],
        out_specs=pl.BlockSpec(memory_space=pl.ANY),
        scratch_shapes=[pltpu.SemaphoreType.DMA(())],
    )
    return pl.pallas_call(
        _dispatch_kernel,
        grid_spec=grid_spec,
        out_shape=jax.ShapeDtypeStruct(xs_init.shape, xs_init.dtype),
        input_output_aliases={3: 0},
        compiler_params=pltpu.CompilerParams(dimension_semantics=("arbitrary",)),
        name="dispatch",
    )(seg_start, route, h2, xs_init)


def _ffn_kernel(be_ref, nb_ref, x_ref, wg_ref, wu_ref, wd_ref, y_ref, wg16, wu16, wd16):
    i = pl.program_id(0)
    prev = be_ref[jnp.maximum(i - 1, 0)]

    @pl.when((i == 0) | (be_ref[i] != prev))
    def _():
        wg16[...] = wg_ref[0].astype(BF16)
        wu16[...] = wu_ref[0].astype(BF16)
        wd16[...] = wd_ref[0].astype(BF16)

    @pl.when(i < nb_ref[0])
    def _():
        x = x_ref[...].astype(BF16)
        a = _dot(x, wg16[...])
        b = _dot(x, wu16[...])
        y_ref[...] = _dot((_silu(a) * b).astype(BF16), wd16[...])

    @pl.when(i >= nb_ref[0])
    def _():
        y_ref[...] = jnp.zeros(y_ref.shape, F32)


def _ffn(block_expert, n_used, xs, w_gate, w_up, w_down):
    P, D = xs.shape
    bm = FFN_BLOCK
    rows = pl.BlockSpec((bm, D), lambda i, be, nb: (jnp.minimum(i, nb[0] - 1), 0))
    grid_spec = pltpu.PrefetchScalarGridSpec(
        num_scalar_prefetch=2,
        grid=(P // bm,),
        in_specs=[rows,
                  pl.BlockSpec((1, D, EXPERT_FF), lambda i, be, nb: (be[i], 0, 0)),
                  pl.BlockSpec((1, D, EXPERT_FF), lambda i, be, nb: (be[i], 0, 0)),
                  pl.BlockSpec((1, EXPERT_FF, D), lambda i, be, nb: (be[i], 0, 0))],
        out_specs=pl.BlockSpec((bm, D), lambda i, be, nb: (i, 0)),
        scratch_shapes=[pltpu.VMEM((D, EXPERT_FF), BF16), pltpu.VMEM((D, EXPERT_FF), BF16),
                        pltpu.VMEM((EXPERT_FF, D), BF16)],
    )
    return pl.pallas_call(
        _ffn_kernel,
        grid_spec=grid_spec,
        out_shape=jax.ShapeDtypeStruct((P, D), F32),
        compiler_params=pltpu.CompilerParams(
            dimension_semantics=("arbitrary",), vmem_limit_bytes=VMEM_LIMIT),
        name="ffn",
    )(block_expert, n_used, xs, w_gate, w_up, w_down)


def _combine_kernel(start_ref, route_ref, ys_ref, x1_ref, rg_ref, g_ref, o_ref, buf_ref, sem):
    tm = TM_COMBINE

    def row_copy(t, k):
        d = _slot_of(route_ref, start_ref, t, k)
        return pltpu.make_async_copy(ys_ref.at[pl.ds(d, 1)], buf_ref.at[k, pl.ds(t, 1)], sem)

    def issue(t, carry):
        row_copy(t, 0).start()
        row_copy(t, 1).start()
        return carry

    def drain(t, carry):
        row_copy(t, 0).wait()
        row_copy(t, 1).wait()
        return carry

    lax.fori_loop(0, tm, issue, 0)
    lax.fori_loop(0, tm, drain, 0)

    rg = rg_ref[...]
    moe = buf_ref[0] * rg[:, 0:1] + buf_ref[1] * rg[:, 1:2]
    x2 = x1_ref[...] + moe
    o_ref[...] = x2 * lax.rsqrt(jnp.mean(x2 * x2, axis=-1, keepdims=True) + EPS) * g_ref[...]


def _combine(seg_start, route, ys, x1, rg, final_g):
    N, D = x1.shape
    tm = TM_COMBINE
    grid_spec = pltpu.PrefetchScalarGridSpec(
        num_scalar_prefetch=1,
        grid=(N // tm,),
        in_specs=[pl.BlockSpec((4 * tm,), lambda i, s: (i,), memory_space=pltpu.SMEM),
                  pl.BlockSpec(memory_space=pl.ANY),
                  pl.BlockSpec((tm, D), lambda i, s: (i, 0)),
                  pl.BlockSpec((tm, LANES), lambda i, s: (i, 0)),
                  pl.BlockSpec((1, D), lambda i, s: (0, 0))],
        out_specs=pl.BlockSpec((tm, D), lambda i, s: (i, 0)),
        scratch_shapes=[pltpu.VMEM((TOP_K, tm, D), F32), pltpu.SemaphoreType.DMA(())],
    )
    return pl.pallas_call(
        _combine_kernel,
        grid_spec=grid_spec,
        out_shape=jax.ShapeDtypeStruct((N, D), F32),
        compiler_params=pltpu.CompilerParams(
            dimension_semantics=("arbitrary",), vmem_limit_bytes=VMEM_LIMIT),
        name="combine",
    )(seg_start, route, ys, x1, rg, final_g)


def _lane_vec(values, offset):
    return jnp.zeros((1, LANES), F32).at[0, offset:offset + values.shape[0]].set(values.astype(F32))


def kernel(x, mix_norm_g, w_in, conv_a_w, conv_a_norm_g, dn_conv_w, dn_a_log, dn_dt_bias, dn_norm_g,
           w_out, ffn_norm_g, router_group_w, router_expert_w, w_gate, w_up, w_down, final_norm_g):
    B, S, D = x.shape
    N = B * S
    depth = w_in.shape[0]
    assert depth == 1, "single-layer block: the final RMSNorm is fused into the layer's combine step"
    group_of = jnp.arange(CONV_A_DIM, dtype=I32) // CONV_A_GROUP_DIM
    gmat = jnp.where(group_of[:, None] == group_of[None, :], 1.0 / CONV_A_GROUP_DIM, 0.0).astype(BF16)
    bm = FFN_BLOCK
    n_blocks = (N * TOP_K) // bm + N_EXPERTS
    for l in range(depth):
        w_in_pad = jnp.pad(w_in[l], ((0, 0), (0, IN_PROJ_PAD - IN_PROJ_DIM))).astype(BF16)
        ya, q, k, v, zg, gcol, grow = _inproj(
            x, mix_norm_g[l][None, :], w_in_pad, conv_a_w[l], conv_a_norm_g[l][None, :], dn_conv_w[l],
            _lane_vec(dn_a_log[l], DN_HEADS), _lane_vec(dn_dt_bias[l], DN_HEADS), gmat)
        yb = _delta(q, k, v, zg, gcol, grow, dn_norm_g[l][None, :])
        rw = jnp.pad(jnp.concatenate([router_group_w[l], router_expert_w[l]], axis=1),
                     ((0, 0), (0, LANES - N_GROUPS - N_EXPERTS)))
        x1, h2, ri, rg, cnt = _outproj(ya.reshape(N, CONV_A_DIM), yb.reshape(N, DN_DIM), x.reshape(N, D),
                                       w_out[l].astype(BF16), ffn_norm_g[l][None, :], rw)
        counts = cnt[0, :N_EXPERTS].astype(I32)
        padded = (counts + bm - 1) // bm * bm
        seg_end = jnp.cumsum(padded)
        seg_start = (seg_end - padded).astype(I32)
        n_used = (seg_end[-1] // bm).astype(I32)
        blk = jnp.arange(n_blocks, dtype=I32)
        blk_expert = jnp.sum(seg_end[None, :] <= (jnp.minimum(blk, n_used - 1) * bm)[:, None], axis=1)
        blk_expert = jnp.minimum(blk_expert, N_EXPERTS - 1).astype(I32)
        route = ri[:, :4].reshape(-1)
        xs = _dispatch(seg_start, route, h2, jnp.zeros((n_blocks * bm, D), F32))
        ys = _ffn(blk_expert, n_used.reshape(1), xs, w_gate[l], w_up[l], w_down[l])
        x = _combine(seg_start, route, ys, x1, rg, final_norm_g[None, :]).reshape(B, S, D)
    return x
```

```python
import functools

import jax
import jax.numpy as jnp
from jax import lax
from jax.experimental import pallas as pl
from jax.experimental.pallas import tpu as pltpu

F32 = jnp.float32
BF16 = jnp.bfloat16
I32 = jnp.int32

D_MODEL = 1024
CHUNK = 64
CONV_A_GROUP_DIM = 64
CONV_A_DIM = 512
CONV_A_WIDTH = 3
DN_HEADS = 4
DN_HEAD_DIM = 128
DN_DIM = 512
DN_CONV_WIDTH = 4
IN_PROJ_DIM = 3 * CONV_A_DIM + 4 * DN_DIM + 2 * DN_HEADS
N_GROUPS = 4
EXPERTS_PER_GROUP = 8
N_EXPERTS = 32
TOP_K = 2
EXPERT_FF = 512
EPS = 1e-6

LANES = 128
HALO = 8
IN_PROJ_PAD = 29 * LANES
GATE_COL = 3 * CONV_A_DIM + 4 * DN_DIM

TM_IN = 512
HEADS_PER_STEP = 2
TM_OUT = 512
FFN_BLOCK = 256
TM_DISPATCH = 512
TM_COMBINE = 256
VMEM_LIMIT = 56 * 1024 * 1024


def _dot(a, b):
    return jnp.dot(a, b, preferred_element_type=F32)


def _dot_nt(a, b):
    return lax.dot_general(a, b, (((1,), (1,)), ((), ())), preferred_element_type=F32)


def _dot_tn(a, b):
    return lax.dot_general(a, b, (((0,), (0,)), ((), ())), preferred_element_type=F32)


def _silu(x):
    return x * jax.nn.sigmoid(x)


def _softplus(x):
    return jnp.maximum(x, 0.0) + jnp.log1p(jnp.exp(-jnp.abs(x)))


def _inproj_kernel(x_ref, g_ref, w_ref, caw_ref, cag_ref, dcw_ref, alog_ref, dtb_ref, gmat_ref,
                   ya_ref, q_ref, k_ref, v_ref, zg_ref, gcol_ref, grow_ref,
                   exta_ref, extq_ref):
    tm = x_ref.shape[1]

    @pl.when(pl.program_id(1) == 0)
    def _():
        exta_ref[0:HALO, :] = jnp.zeros((HALO, CONV_A_DIM), F32)
        extq_ref[0:HALO, :] = jnp.zeros((HALO, 3 * DN_DIM), F32)

    x = x_ref[0]
    ms = jnp.mean(x * x, axis=-1, keepdims=True)
    hb = (x * lax.rsqrt(ms + EPS) * g_ref[...]).astype(BF16)

    def proj(c0, width):
        return _dot(hb, w_ref[:, c0:c0 + width])

    a_h = proj(0, CONV_A_DIM)
    a_c = proj(2 * CONV_A_DIM, CONV_A_DIM)
    exta_ref[HALO:HALO + tm, :] = a_c * a_h
    conv = caw_ref[0:1, :] * exta_ref[pl.ds(HALO - 2, tm), :]
    conv = conv + caw_ref[1:2, :] * exta_ref[pl.ds(HALO - 1, tm), :]
    conv = conv + caw_ref[2:3, :] * exta_ref[pl.ds(HALO, tm), :]
    y = proj(CONV_A_DIM, CONV_A_DIM) * conv
    ysq = y * y
    hi = ysq.astype(BF16)
    lo = (ysq - hi.astype(F32)).astype(BF16)
    gmean = _dot(hi, gmat_ref[...]) + _dot(lo, gmat_ref[...])
    ya_ref[0] = y * lax.rsqrt(gmean + EPS) * cag_ref[...]
    exta_ref[0:HALO, :] = exta_ref[tm:tm + HALO, :]

    base = 3 * CONV_A_DIM
    for i in range(3):
        extq_ref[HALO:HALO + tm, i * DN_DIM:(i + 1) * DN_DIM] = proj(base + i * DN_DIM, DN_DIM)
    for i, out_ref in enumerate((q_ref, k_ref, v_ref)):
        cols = slice(i * DN_DIM, (i + 1) * DN_DIM)
        acc = dcw_ref[0:1, cols] * extq_ref[pl.ds(HALO - 3, tm), cols]
        for j in range(1, DN_CONV_WIDTH):
            acc = acc + dcw_ref[j:j + 1, cols] * extq_ref[pl.ds(HALO - 3 + j, tm), cols]
        s = _silu(acc)
        if i == 2:
            out_ref[0] = s
        else:
            for h in range(DN_HEADS):
                sh = s[:, h * DN_HEAD_DIM:(h + 1) * DN_HEAD_DIM]
                inv = lax.rsqrt(jnp.sum(sh * sh, axis=-1, keepdims=True) + EPS)
                sh = sh * inv
                if i == 0:
                    sh = sh * (DN_HEAD_DIM ** -0.5)
                out_ref[0, :, h * DN_HEAD_DIM:(h + 1) * DN_HEAD_DIM] = sh
    extq_ref[0:HALO, :] = extq_ref[tm:tm + HALO, :]

    zg_ref[0] = _silu(proj(base + 3 * DN_DIM, DN_DIM))

    p = proj(GATE_COL, LANES)
    beta = jax.nn.sigmoid(p)
    g = -jnp.exp(alog_ref[...]) * _softplus(p + dtb_ref[...])
    r = lax.broadcasted_iota(I32, (tm, tm), 0)
    c = lax.broadcasted_iota(I32, (tm, tm), 1)
    tri = jnp.where((r // CHUNK == c // CHUNK) & (c <= r), 1.0, 0.0).astype(F32)
    gc = jnp.dot(tri, g, preferred_element_type=F32, precision=lax.Precision.HIGHEST)
    lane = lax.broadcasted_iota(I32, (tm, LANES), 1)
    slab = jnp.where(lane < DN_HEADS, beta, gc)
    gcol_ref[0] = slab
    rows = slab.T[0:HALO, :]
    for ci in range(tm // CHUNK):
        grow_ref[0, ci] = rows[:, ci * CHUNK:(ci + 1) * CHUNK]


def _inproj(x, mix_g, w_in_pad, conv_a_w, conv_a_g, dn_conv_w, alog_vec, dtb_vec, gmat):
    B, S, D = x.shape
    tm = TM_IN
    full = lambda shape: pl.BlockSpec(shape, lambda b, t: (0,) * len(shape))
    row = lambda width: pl.BlockSpec((1, tm, width), lambda b, t: (b, t, 0))
    out_shape = (
        jax.ShapeDtypeStruct((B, S, CONV_A_DIM), F32),
        jax.ShapeDtypeStruct((B, S, DN_DIM), F32),
        jax.ShapeDtypeStruct((B, S, DN_DIM), F32),
        jax.ShapeDtypeStruct((B, S, DN_DIM), F32),
        jax.ShapeDtypeStruct((B, S, DN_DIM), F32),
        jax.ShapeDtypeStruct((B, S, LANES), F32),
        jax.ShapeDtypeStruct((B, S // CHUNK, HALO, CHUNK), F32),
    )
    return pl.pallas_call(
        _inproj_kernel,
        grid=(B, S // tm),
        in_specs=[row(D), full((1, D)), full((D, IN_PROJ_PAD)), full((CONV_A_WIDTH, CONV_A_DIM)),
                  full((1, CONV_A_DIM)), full((DN_CONV_WIDTH, 3 * DN_DIM)), full((1, LANES)),
                  full((1, LANES)), full((CONV_A_DIM, CONV_A_DIM))],
        out_specs=(row(CONV_A_DIM), row(DN_DIM), row(DN_DIM), row(DN_DIM), row(DN_DIM), row(LANES),
                   pl.BlockSpec((1, tm // CHUNK, HALO, CHUNK), lambda b, t: (b, t, 0, 0))),
        out_shape=out_shape,
        scratch_shapes=[pltpu.VMEM((tm + HALO, CONV_A_DIM), F32),
                        pltpu.VMEM((tm + HALO, 3 * DN_DIM), F32)],
        compiler_params=pltpu.CompilerParams(
            dimension_semantics=("arbitrary", "arbitrary"), vmem_limit_bytes=VMEM_LIMIT),
        name="inproj",
    )(x, mix_g, w_in_pad, conv_a_w, conv_a_g, dn_conv_w, alog_vec, dtb_vec, gmat)


def _delta_kernel(q_ref, k_ref, v_ref, zg_ref, gcol_ref, grow_ref, ng_ref, o_ref,
                  u_ref, w_ref, qd_ref, kd_ref, at_ref, st_ref):
    S = q_ref.shape[1]
    n_chunks = S // CHUNK
    hp = HEADS_PER_STEP
    head0 = pl.program_id(1) * hp
    ri = lax.broadcasted_iota(I32, (CHUNK, CHUNK), 0)
    ci = lax.broadcasted_iota(I32, (CHUNK, CHUNK), 1)
    causal = ci <= ri
    strict = ci < ri
    eye = jnp.where(ci == ri, 1.0, 0.0).astype(F32)
    lane = lax.broadcasted_iota(I32, (CHUNK, LANES), 1)

    def prep(c, carry):
        r0 = pl.multiple_of(c * CHUNK, CHUNK)
        rows = pl.ds(r0, CHUNK)
        slab = gcol_ref[0, rows, :]
        for j in range(hp):
            cols = slice(j * DN_HEAD_DIM, (j + 1) * DN_HEAD_DIM)
            hh = head0 + j
            beta = jnp.sum(jnp.where(lane == hh, slab, 0.0), axis=-1, keepdims=True)
            gcc = jnp.sum(jnp.where(lane == hh + DN_HEADS, slab, 0.0), axis=-1, keepdims=True)
            gcr = grow_ref[0, c, pl.ds(hh + DN_HEADS, 1), :]
            diff = gcc - gcr
            decay = jnp.where(causal, jnp.exp(jnp.where(causal, diff, 0.0)), 0.0)
            q = q_ref[0, rows, cols]
            k = k_ref[0, rows, cols]
            v = v_ref[0, rows, cols]
            kb = k * beta
            k16 = k.astype(BF16)
            L = jnp.where(strict, _dot_nt(kb.astype(BF16), k16) * decay, 0.0)
            intra = _dot_nt(q.astype(BF16), k16) * decay
            m = -L
            t = eye + m
            for _ in range(5):
                m16 = m.astype(BF16)
                m = _dot(m16, m16)
                t = t + _dot(t.astype(BF16), m.astype(BF16))
            egc = jnp.exp(gcc)
            rhs = jnp.concatenate([v * beta, kb * egc], axis=-1)
            uw = _dot(t.astype(BF16), rhs.astype(BF16))
            gl = gcr[:, CHUNK - 1:CHUNK]
            u_ref[rows, cols] = uw[:, :DN_HEAD_DIM]
            w_ref[rows, cols] = uw[:, DN_HEAD_DIM:].astype(BF16)
            qd_ref[rows, cols] = (q * egc).astype(BF16)
            kd_ref[rows, cols] = (k * jnp.exp(gl - gcc)).astype(BF16)
            at_ref[rows, j * CHUNK:(j + 1) * CHUNK] = intra.astype(BF16)
        return carry

    lax.fori_loop(0, n_chunks, prep, 0)

    st_ref[...] = jnp.zeros(st_ref.shape, F32)

    def scan(c, carry):
        r0 = pl.multiple_of(c * CHUNK, CHUNK)
        rows = pl.ds(r0, CHUNK)
        for j in range(hp):
            cols = slice(j * DN_HEAD_DIM, (j + 1) * DN_HEAD_DIM)
            hh = head0 + j
            gl = grow_ref[0, c, pl.ds(hh + DN_HEADS, 1), CHUNK - 1:CHUNK]
            st = st_ref[j]
            st16 = st.astype(BF16)
            v_new = u_ref[rows, cols] - _dot(w_ref[rows, cols], st16)
            vn16 = v_new.astype(BF16)
            o = _dot(qd_ref[rows, cols], st16) + _dot(at_ref[rows, j * CHUNK:(j + 1) * CHUNK], vn16)
            st_ref[j] = st * jnp.exp(gl) + _dot_tn(kd_ref[rows, cols], vn16)
            on = o * lax.rsqrt(jnp.mean(o * o, axis=-1, keepdims=True) + EPS) * ng_ref[...]
            o_ref[0, rows, cols] = on * zg_ref[0, rows, cols]
        return carry

    lax.fori_loop(0, n_chunks, scan, 0)


def _delta(q, k, v, zg, gcol, grow, norm_g):
    B, S, _ = q.shape
    hp = HEADS_PER_STEP
    width = hp * DN_HEAD_DIM
    blk = pl.BlockSpec((1, S, width), lambda b, h: (b, 0, h))
    return pl.pallas_call(
        _delta_kernel,
        grid=(B, DN_HEADS // hp),
        in_specs=[blk, blk, blk, blk,
                  pl.BlockSpec((1, S, LANES), lambda b, h: (b, 0, 0)),
                  pl.BlockSpec((1, S // CHUNK, HALO, CHUNK), lambda b, h: (b, 0, 0, 0)),
                  pl.BlockSpec((1, DN_HEAD_DIM), lambda b, h: (0, 0))],
        out_specs=blk,
        out_shape=jax.ShapeDtypeStruct((B, S, DN_DIM), F32),
        scratch_shapes=[pltpu.VMEM((S, width), F32),
                        pltpu.VMEM((S, width), BF16),
                        pltpu.VMEM((S, width), BF16),
                        pltpu.VMEM((S, width), BF16),
                        pltpu.VMEM((S, hp * CHUNK), BF16),
                        pltpu.VMEM((hp, DN_HEAD_DIM, DN_HEAD_DIM), F32)],
        compiler_params=pltpu.CompilerParams(
            dimension_semantics=("arbitrary", "arbitrary"), vmem_limit_bytes=VMEM_LIMIT),
        name="delta",
    )(q, k, v, zg, gcol, grow, norm_g)


def _outproj_kernel(ya_ref, yb_ref, x_ref, wo_ref, g_ref, rw_ref,
                    x1_ref, h2_ref, ri_ref, rg_ref, cnt_ref, base_ref):
    tm = x_ref.shape[0]

    @pl.when(pl.program_id(0) == 0)
    def _():
        base_ref[...] = jnp.zeros(base_ref.shape, F32)

    y = _dot(ya_ref[...].astype(BF16), wo_ref[0:CONV_A_DIM, :])
    y = y + _dot(yb_ref[...].astype(BF16), wo_ref[CONV_A_DIM:, :])
    x1 = x_ref[...] + y
    x1_ref[...] = x1
    h = x1 * lax.rsqrt(jnp.mean(x1 * x1, axis=-1, keepdims=True) + EPS) * g_ref[...]
    h2_ref[...] = h

    logits = jnp.dot(h, rw_ref[...], preferred_element_type=F32, precision=lax.Precision.HIGHEST)
    lane = lax.broadcasted_iota(I32, (tm, LANES), 1)
    lanef = lane.astype(F32)
    neg = jnp.float32(-1e30)
    big = jnp.float32(1e9)
    is_g = lane < N_GROUPS
    gl = jnp.where(is_g, logits, neg)
    gmax = jnp.max(gl, axis=-1, keepdims=True)
    gidx = jnp.min(jnp.where(gl == gmax, lanef, big), axis=-1, keepdims=True)
    gsum = jnp.sum(jnp.where(is_g, jnp.exp(gl - gmax), 0.0), axis=-1, keepdims=True)
    gprob = 1.0 / gsum
    lo = N_GROUPS + EXPERTS_PER_GROUP * gidx
    emask = (lanef >= lo) & (lanef < lo + EXPERTS_PER_GROUP)
    el = jnp.where(emask, logits, neg)
    e1 = jnp.max(el, axis=-1, keepdims=True)
    i1 = jnp.min(jnp.where(el == e1, lanef, big), axis=-1, keepdims=True)
    el2 = jnp.where(lanef == i1, neg, el)
    e2 = jnp.max(el2, axis=-1, keepdims=True)
    i2 = jnp.min(jnp.where(el2 == e2, lanef, big), axis=-1, keepdims=True)
    r = jnp.exp(e2 - e1)
    gate1 = gprob / (1.0 + r)
    gate2 = gprob * r / (1.0 + r)
    id1 = i1 - N_GROUPS
    id2 = i2 - N_GROUPS

    oh1 = jnp.where(lanef == id1, 1.0, 0.0).astype(F32)
    oh2 = jnp.where(lanef == id2, 1.0, 0.0).astype(F32)
    oh = oh1 + oh2
    rr = lax.broadcasted_iota(I32, (tm, tm), 0)
    cc = lax.broadcasted_iota(I32, (tm, tm), 1)
    tri = jnp.where(cc < rr, 1.0, 0.0).astype(BF16)
    before = _dot(tri, oh.astype(BF16)) + base_ref[...]
    rank1 = jnp.sum(oh1 * before, axis=-1, keepdims=True)
    rank2 = jnp.sum(oh2 * before, axis=-1, keepdims=True)
    base_ref[...] = base_ref[...] + jnp.sum(oh, axis=0, keepdims=True)
    cnt_ref[...] = base_ref[...]

    ri = jnp.where(lane == 0, id1, jnp.where(lane == 1, id2,
                   jnp.where(lane == 2, rank1, jnp.where(lane == 3, rank2, 0.0))))
    ri_ref[...] = ri.astype(I32)
    rg_ref[...] = jnp.where(lane == 0, gate1, jnp.where(lane == 1, gate2, 0.0))


def _outproj(ya, yb, x, w_out16, ffn_g, rw):
    N, D = x.shape
    tm = TM_OUT
    row = lambda width: pl.BlockSpec((tm, width), lambda i: (i, 0))
    full = lambda shape: pl.BlockSpec(shape, lambda i: (0,) * len(shape))
    return pl.pallas_call(
        _outproj_kernel,
        grid=(N // tm,),
        in_specs=[row(CONV_A_DIM), row(DN_DIM), row(D), full((D, D)), full((1, D)), full((D, LANES))],
        out_specs=(row(D), row(D), row(LANES), row(LANES), full((1, LANES))),
        out_shape=(jax.ShapeDtypeStruct((N, D), F32),
                   jax.ShapeDtypeStruct((N, D), F32),
                   jax.ShapeDtypeStruct((N, LANES), I32),
                   jax.ShapeDtypeStruct((N, LANES), F32),
                   jax.ShapeDtypeStruct((1, LANES), F32)),
        scratch_shapes=[pltpu.VMEM((1, LANES), F32)],
        compiler_params=pltpu.CompilerParams(
            dimension_semantics=("arbitrary",), vmem_limit_bytes=VMEM_LIMIT),
        name="outproj",
    )(ya, yb, x, w_out16, ffn_g, rw)


def _slot_of(route_ref, start_ref, t, k):
    return start_ref[route_ref[4 * t + k]] + route_ref[4 * t + 2 + k]


def _dispatch_kernel(start_ref, route_ref, h_ref, xs_in_ref, xs_ref, sem):
    del xs_in_ref
    tm = TM_DISPATCH

    def row_copy(t, k):
        d = _slot_of(route_ref, start_ref, t, k)
        return pltpu.make_async_copy(h_ref.at[pl.ds(t, 1)], xs_ref.at[pl.ds(d, 1)], sem)

    def issue(t, carry):
        row_copy(t, 0).start()
        row_copy(t, 1).start()
        return carry

    lax.fori_loop(0, tm, issue, 0)
    for _ in range(TOP_K):
        pltpu.make_async_copy(h_ref, xs_ref.at[pl.ds(0, tm)], sem).wait()


def _dispatch(seg_start, route, h2, xs_init):
    N, D = h2.shape
    tm = TM_DISPATCH
    grid_spec = pltpu.PrefetchScalarGridSpec(
        num_scalar_prefetch=1,
        grid=(N // tm,),
        in_specs=[pl.BlockSpec((4 * tm,), lambda i, s: (i,), memory_space=pltpu.SMEM),
                  pl.BlockSpec((tm, D), lambda i, s: (i, 0)),
                  pl.BlockSpec(memory_space=pl.ANY)],
        out_specs=pl.BlockSpec(memory_space=pl.ANY),
        scratch_shapes=[pltpu.SemaphoreType.DMA(())],
    )
    return pl.pallas_call(
        _dispatch_kernel,
        grid_spec=grid_spec,
        out_shape=jax.ShapeDtypeStruct(xs_init.shape, xs_init.dtype),
        input_output_aliases={3: 0},
        compiler_params=pltpu.CompilerParams(dimension_semantics=("arbitrary",)),
        name="dispatch",
    )(seg_start, route, h2, xs_init)


def _ffn_kernel(be_ref, nb_ref, x_ref, wg_ref, wu_ref, wd_ref, y_ref, wg16, wu16, wd16):
    i = pl.program_id(0)
    prev = be_ref[jnp.maximum(i - 1, 0)]

    @pl.when((i == 0) | (be_ref[i] != prev))
    def _():
        wg16[...] = wg_ref[0].astype(BF16)
        wu16[...] = wu_ref[0].astype(BF16)
        wd16[...] = wd_ref[0].astype(BF16)

    @pl.when(i < nb_ref[0])
    def _():
        x = x_ref[...].astype(BF16)
        a = _dot(x, wg16[...])
        b = _dot(x, wu16[...])
        y_ref[...] = _dot((_silu(a) * b).astype(BF16), wd16[...])

    @pl.when(i >= nb_ref[0])
    def _():
        y_ref[...] = jnp.zeros(y_ref.shape, F32)


def _ffn(block_expert, n_used, xs, w_gate, w_up, w_down):
    P, D = xs.shape
    bm = FFN_BLOCK
    rows = pl.BlockSpec((bm, D), lambda i, be, nb: (jnp.minimum(i, nb[0] - 1), 0))
    grid_spec = pltpu.PrefetchScalarGridSpec(
        num_scalar_prefetch=2,
        grid=(P // bm,),
        in_specs=[rows,
                  pl.BlockSpec((1, D, EXPERT_FF), lambda i, be, nb: (be[i], 0, 0)),
                  pl.BlockSpec((1, D, EXPERT_FF), lambda i, be, nb: (be[i], 0, 0)),
                  pl.BlockSpec((1, EXPERT_FF, D), lambda i, be, nb: (be[i], 0, 0))],
        out_specs=pl.BlockSpec((bm, D), lambda i, be, nb: (i, 0)),
        scratch_shapes=[pltpu.VMEM((D, EXPERT_FF), BF16), pltpu.VMEM((D, EXPERT_FF), BF16),
                        pltpu.VMEM((EXPERT_FF, D), BF16)],
    )
    return pl.pallas_call(
        _ffn_kernel,
        grid_spec=grid_spec,
        out_shape=jax.ShapeDtypeStruct((P, D), F32),
        compiler_params=pltpu.CompilerParams(
            dimension_semantics=("arbitrary",), vmem_limit_bytes=VMEM_LIMIT),
        name="ffn",
    )(block_expert, n_used, xs, w_gate, w_up, w_down)


def _combine_kernel(start_ref, route_ref, ys_ref, x1_ref, rg_ref, g_ref, o_ref, buf_ref, sem):
    tm = TM_COMBINE

    def row_copy(t, k):
        d = _slot_of(route_ref, start_ref, t, k)
        return pltpu.make_async_copy(ys_ref.at[pl.ds(d, 1)], buf_ref.at[k, pl.ds(t, 1)], sem)

    def issue(t, carry):
        row_copy(t, 0).start()
        row_copy(t, 1).start()
        return carry

    lax.fori_loop(0, tm, issue, 0)
    for k in range(TOP_K):
        pltpu.make_async_copy(ys_ref.at[pl.ds(0, tm)], buf_ref.at[k], sem).wait()

    rg = rg_ref[...]
    moe = buf_ref[0] * rg[:, 0:1] + buf_ref[1] * rg[:, 1:2]
    x2 = x1_ref[...] + moe
    o_ref[...] = x2 * lax.rsqrt(jnp.mean(x2 * x2, axis=-1, keepdims=True) + EPS) * g_ref[...]


def _combine(seg_start, route, ys, x1, rg, final_g):
    N, D = x1.shape
    tm = TM_COMBINE
    grid_spec = pltpu.PrefetchScalarGridSpec(
        num_scalar_prefetch=1,
        grid=(N // tm,),
        in_specs=[pl.BlockSpec((4 * tm,), lambda i, s: (i,), memory_space=pltpu.SMEM),
                  pl.BlockSpec(memory_space=pl.ANY),
                  pl.BlockSpec((tm, D), lambda i, s: (i, 0)),
                  pl.BlockSpec((tm, LANES), lambda i, s: (i, 0)),
                  pl.BlockSpec((1, D), lambda i, s: (0, 0))],
        out_specs=pl.BlockSpec((tm, D), lambda i, s: (i, 0)),
        scratch_shapes=[pltpu.VMEM((TOP_K, tm, D), F32), pltpu.SemaphoreType.DMA(())],
    )
    return pl.pallas_call(
        _combine_kernel,
        grid_spec=grid_spec,
        out_shape=jax.ShapeDtypeStruct((N, D), F32),
        compiler_params=pltpu.CompilerParams(
            dimension_semantics=("arbitrary",), vmem_limit_bytes=VMEM_LIMIT),
        name="combine",
    )(seg_start, route, ys, x1, rg, final_g)


def _lane_vec(values, offset):
    return jnp.zeros((1, LANES), F32).at[0, offset:offset + values.shape[0]].set(values.astype(F32))


def kernel(x, mix_norm_g, w_in, conv_a_w, conv_a_norm_g, dn_conv_w, dn_a_log, dn_dt_bias, dn_norm_g,
           w_out, ffn_norm_g, router_group_w, router_expert_w, w_gate, w_up, w_down, final_norm_g):
    B, S, D = x.shape
    N = B * S
    depth = w_in.shape[0]
    assert depth == 1, "single-layer block: the final RMSNorm is fused into the layer's combine step"
    group_of = jnp.arange(CONV_A_DIM, dtype=I32) // CONV_A_GROUP_DIM
    gmat = jnp.where(group_of[:, None] == group_of[None, :], 1.0 / CONV_A_GROUP_DIM, 0.0).astype(BF16)
    bm = FFN_BLOCK
    n_blocks = (N * TOP_K) // bm + N_EXPERTS
    for l in range(depth):
        w_in_pad = jnp.pad(w_in[l], ((0, 0), (0, IN_PROJ_PAD - IN_PROJ_DIM))).astype(BF16)
        ya, q, k, v, zg, gcol, grow = _inproj(
            x, mix_norm_g[l][None, :], w_in_pad, conv_a_w[l], conv_a_norm_g[l][None, :], dn_conv_w[l],
            _lane_vec(dn_a_log[l], DN_HEADS), _lane_vec(dn_dt_bias[l], DN_HEADS), gmat)
        yb = _delta(q, k, v, zg, gcol, grow, dn_norm_g[l][None, :])
        rw = jnp.pad(jnp.concatenate([router_group_w[l], router_expert_w[l]], axis=1),
                     ((0, 0), (0, LANES - N_GROUPS - N_EXPERTS)))
        x1, h2, ri, rg, cnt = _outproj(ya.reshape(N, CONV_A_DIM), yb.reshape(N, DN_DIM), x.reshape(N, D),
                                       w_out[l].astype(BF16), ffn_norm_g[l][None, :], rw)
        counts = cnt[0, :N_EXPERTS].astype(I32)
        padded = (counts + bm - 1) // bm * bm
        seg_end = jnp.cumsum(padded)
        seg_start = (seg_end - padded).astype(I32)
        n_used = (seg_end[-1] // bm).astype(I32)
        blk = jnp.arange(n_blocks, dtype=I32)
        blk_expert = jnp.sum(seg_end[None, :] <= (jnp.minimum(blk, n_used - 1) * bm)[:, None], axis=1)
        blk_expert = jnp.minimum(blk_expert, N_EXPERTS - 1).astype(I32)
        route = ri[:, :4].reshape(-1)
        xs = _dispatch(seg_start, route, h2, jnp.zeros((n_blocks * bm, D), F32))
        ys = _ffn(blk_expert, n_used.reshape(1), xs, w_gate[l], w_up[l], w_down[l])
        x = _combine(seg_start, route, ys, x1, rg, final_norm_g[None, :]).reshape(B, S, D)
    return x
```

```python
import functools

import jax
import jax.numpy as jnp
from jax import lax
from jax.experimental import pallas as pl
from jax.experimental.pallas import tpu as pltpu

F32 = jnp.float32
BF16 = jnp.bfloat16
I32 = jnp.int32

D_MODEL = 1024
CHUNK = 64
CONV_A_GROUP_DIM = 64
CONV_A_DIM = 512
CONV_A_WIDTH = 3
DN_HEADS = 4
DN_HEAD_DIM = 128
DN_DIM = 512
DN_CONV_WIDTH = 4
IN_PROJ_DIM = 3 * CONV_A_DIM + 4 * DN_DIM + 2 * DN_HEADS
N_GROUPS = 4
EXPERTS_PER_GROUP = 8
N_EXPERTS = 32
TOP_K = 2
EXPERT_FF = 512
EPS = 1e-6

LANES = 128
HALO = 8
IN_PROJ_PAD = 29 * LANES
GATE_COL = 3 * CONV_A_DIM + 4 * DN_DIM

TM_IN = 512
TM_PREP = 256
TM_SCAN = 512
TM_OUT = 512
FFN_BLOCK = 256
TM_DISPATCH = 512
TM_COMBINE = 256
VMEM_LIMIT = 56 * 1024 * 1024


def _dot(a, b):
    return jnp.dot(a, b, preferred_element_type=F32)


def _dot_nt(a, b):
    return lax.dot_general(a, b, (((1,), (1,)), ((), ())), preferred_element_type=F32)


def _dot_tn(a, b):
    return lax.dot_general(a, b, (((0,), (0,)), ((), ())), preferred_element_type=F32)


def _silu(x):
    return x * jax.nn.sigmoid(x)


def _softplus(x):
    return jnp.maximum(x, 0.0) + jnp.log1p(jnp.exp(-jnp.abs(x)))


def _inproj_kernel(x_ref, g_ref, w_ref, caw_ref, cag_ref, dcw_ref, alog_ref, dtb_ref, gmat_ref,
                   ya_ref, q_ref, k_ref, v_ref, zg_ref, gcol_ref, grow_ref,
                   exta_ref, extq_ref):
    tm = x_ref.shape[1]

    @pl.when(pl.program_id(1) == 0)
    def _():
        exta_ref[0:HALO, :] = jnp.zeros((HALO, CONV_A_DIM), F32)
        extq_ref[0:HALO, :] = jnp.zeros((HALO, 3 * DN_DIM), F32)

    x = x_ref[0]
    ms = jnp.mean(x * x, axis=-1, keepdims=True)
    hb = (x * lax.rsqrt(ms + EPS) * g_ref[...]).astype(BF16)

    def proj(c0, width):
        return _dot(hb, w_ref[:, c0:c0 + width])

    a_h = proj(0, CONV_A_DIM)
    a_c = proj(2 * CONV_A_DIM, CONV_A_DIM)
    exta_ref[HALO:HALO + tm, :] = a_c * a_h
    conv = caw_ref[0:1, :] * exta_ref[pl.ds(HALO - 2, tm), :]
    conv = conv + caw_ref[1:2, :] * exta_ref[pl.ds(HALO - 1, tm), :]
    conv = conv + caw_ref[2:3, :] * exta_ref[pl.ds(HALO, tm), :]
    y = proj(CONV_A_DIM, CONV_A_DIM) * conv
    ysq = y * y
    hi = ysq.astype(BF16)
    lo = (ysq - hi.astype(F32)).astype(BF16)
    gmean = _dot(hi, gmat_ref[...]) + _dot(lo, gmat_ref[...])
    ya_ref[0] = y * lax.rsqrt(gmean + EPS) * cag_ref[...]
    exta_ref[0:HALO, :] = exta_ref[tm:tm + HALO, :]

    base = 3 * CONV_A_DIM
    for i in range(3):
        extq_ref[HALO:HALO + tm, i * DN_DIM:(i + 1) * DN_DIM] = proj(base + i * DN_DIM, DN_DIM)
    for i, out_ref in enumerate((q_ref, k_ref, v_ref)):
        cols = slice(i * DN_DIM, (i + 1) * DN_DIM)
        acc = dcw_ref[0:1, cols] * extq_ref[pl.ds(HALO - 3, tm), cols]
        for j in range(1, DN_CONV_WIDTH):
            acc = acc + dcw_ref[j:j + 1, cols] * extq_ref[pl.ds(HALO - 3 + j, tm), cols]
        s = _silu(acc)
        if i == 2:
            out_ref[0] = s
        else:
            for h in range(DN_HEADS):
                sh = s[:, h * DN_HEAD_DIM:(h + 1) * DN_HEAD_DIM]
                inv = lax.rsqrt(jnp.sum(sh * sh, axis=-1, keepdims=True) + EPS)
                sh = sh * inv
                if i == 0:
                    sh = sh * (DN_HEAD_DIM ** -0.5)
                out_ref[0, :, h * DN_HEAD_DIM:(h + 1) * DN_HEAD_DIM] = sh
    extq_ref[0:HALO, :] = extq_ref[tm:tm + HALO, :]

    zg_ref[0] = _silu(proj(base + 3 * DN_DIM, DN_DIM))

    p = proj(GATE_COL, LANES)
    beta = jax.nn.sigmoid(p)
    g = -jnp.exp(alog_ref[...]) * _softplus(p + dtb_ref[...])
    r = lax.broadcasted_iota(I32, (tm, tm), 0)
    c = lax.broadcasted_iota(I32, (tm, tm), 1)
    tri = jnp.where((r // CHUNK == c // CHUNK) & (c <= r), 1.0, 0.0).astype(F32)
    gc = jnp.dot(tri, g, preferred_element_type=F32, precision=lax.Precision.HIGHEST)
    lane = lax.broadcasted_iota(I32, (tm, LANES), 1)
    slab = jnp.where(lane < DN_HEADS, beta, gc)
    gcol_ref[0] = slab
    rows = slab.T[0:HALO, :]
    for ci in range(tm // CHUNK):
        grow_ref[0, ci] = rows[:, ci * CHUNK:(ci + 1) * CHUNK]


def _inproj(x, mix_g, w_in_pad, conv_a_w, conv_a_g, dn_conv_w, alog_vec, dtb_vec, gmat):
    B, S, D = x.shape
    tm = TM_IN
    full = lambda shape: pl.BlockSpec(shape, lambda b, t: (0,) * len(shape))
    row = lambda width: pl.BlockSpec((1, tm, width), lambda b, t: (b, t, 0))
    out_shape = (
        jax.ShapeDtypeStruct((B, S, CONV_A_DIM), F32),
        jax.ShapeDtypeStruct((B, S, DN_DIM), F32),
        jax.ShapeDtypeStruct((B, S, DN_DIM), F32),
        jax.ShapeDtypeStruct((B, S, DN_DIM), F32),
        jax.ShapeDtypeStruct((B, S, DN_DIM), F32),
        jax.ShapeDtypeStruct((B, S, LANES), F32),
        jax.ShapeDtypeStruct((B, S // CHUNK, HALO, CHUNK), F32),
    )
    return pl.pallas_call(
        _inproj_kernel,
        grid=(B, S // tm),
        in_specs=[row(D), full((1, D)), full((D, IN_PROJ_PAD)), full((CONV_A_WIDTH, CONV_A_DIM)),
                  full((1, CONV_A_DIM)), full((DN_CONV_WIDTH, 3 * DN_DIM)), full((1, LANES)),
                  full((1, LANES)), full((CONV_A_DIM, CONV_A_DIM))],
        out_specs=(row(CONV_A_DIM), row(DN_DIM), row(DN_DIM), row(DN_DIM), row(DN_DIM), row(LANES),
                   pl.BlockSpec((1, tm // CHUNK, HALO, CHUNK), lambda b, t: (b, t, 0, 0))),
        out_shape=out_shape,
        scratch_shapes=[pltpu.VMEM((tm + HALO, CONV_A_DIM), F32),
                        pltpu.VMEM((tm + HALO, 3 * DN_DIM), F32)],
        compiler_params=pltpu.CompilerParams(
            dimension_semantics=("arbitrary", "arbitrary"), vmem_limit_bytes=VMEM_LIMIT),
        name="inproj",
    )(x, mix_g, w_in_pad, conv_a_w, conv_a_g, dn_conv_w, alog_vec, dtb_vec, gmat)


def _delta_prep_kernel(q_ref, k_ref, v_ref, gcol_ref, grow_ref,
                       u_ref, w_ref, qd_ref, kd_ref, at_ref):
    tm = q_ref.shape[1]
    ri = lax.broadcasted_iota(I32, (CHUNK, CHUNK), 0)
    ci = lax.broadcasted_iota(I32, (CHUNK, CHUNK), 1)
    causal = ci <= ri
    strict = ci < ri
    eye = jnp.where(ci == ri, 1.0, 0.0).astype(F32)
    lane = lax.broadcasted_iota(I32, (CHUNK, LANES), 1)
    chains = [(c, h) for c in range(tm // CHUNK) for h in range(DN_HEADS)]
    ms, ts, rhss = [], [], []
    for c, h in chains:
        rows = slice(c * CHUNK, (c + 1) * CHUNK)
        cols = slice(h * DN_HEAD_DIM, (h + 1) * DN_HEAD_DIM)
        slab = gcol_ref[0, rows, :]
        beta = jnp.sum(jnp.where(lane == h, slab, 0.0), axis=-1, keepdims=True)
        gcc = jnp.sum(jnp.where(lane == h + DN_HEADS, slab, 0.0), axis=-1, keepdims=True)
        gcr = grow_ref[0, c, h + DN_HEADS:h + DN_HEADS + 1, :]
        diff = gcc - gcr
        decay = jnp.where(causal, jnp.exp(jnp.where(causal, diff, 0.0)), 0.0)
        q = q_ref[0, rows, cols]
        k = k_ref[0, rows, cols]
        v = v_ref[0, rows, cols]
        kb = k * beta
        k16 = k.astype(BF16)
        L = jnp.where(strict, _dot_nt(kb.astype(BF16), k16) * decay, 0.0)
        intra = _dot_nt(q.astype(BF16), k16) * decay
        egc = jnp.exp(gcc)
        gl = gcr[:, CHUNK - 1:CHUNK]
        qd_ref[0, rows, cols] = (q * egc).astype(BF16)
        kd_ref[0, rows, cols] = (k * jnp.exp(gl - gcc)).astype(BF16)
        at_ref[0, rows, h * CHUNK:(h + 1) * CHUNK] = intra.astype(BF16)
        rhss.append(jnp.concatenate([v * beta, kb * egc], axis=-1).astype(BF16))
        ms.append(-L)
        ts.append(eye - L)
    for _ in range(5):
        m16s = [m.astype(BF16) for m in ms]
        ms = [_dot(m16, m16) for m16 in m16s]
        ts = [t + _dot(t.astype(BF16), m.astype(BF16)) for t, m in zip(ts, ms)]
    for (c, h), t, rhs in zip(chains, ts, rhss):
        rows = slice(c * CHUNK, (c + 1) * CHUNK)
        cols = slice(h * DN_HEAD_DIM, (h + 1) * DN_HEAD_DIM)
        uw = _dot(t.astype(BF16), rhs)
        u_ref[0, rows, cols] = uw[:, :DN_HEAD_DIM]
        w_ref[0, rows, cols] = uw[:, DN_HEAD_DIM:].astype(BF16)


def _delta_prep(q, k, v, gcol, grow):
    B, S, _ = q.shape
    tm = TM_PREP
    row = lambda width: pl.BlockSpec((1, tm, width), lambda b, t: (b, t, 0))
    return pl.pallas_call(
        _delta_prep_kernel,
        grid=(B, S // tm),
        in_specs=[row(DN_DIM), row(DN_DIM), row(DN_DIM), row(LANES),
                  pl.BlockSpec((1, tm // CHUNK, HALO, CHUNK), lambda b, t: (b, t, 0, 0))],
        out_specs=(row(DN_DIM), row(DN_DIM), row(DN_DIM), row(DN_DIM), row(DN_HEADS * CHUNK)),
        out_shape=(jax.ShapeDtypeStruct((B, S, DN_DIM), F32),
                   jax.ShapeDtypeStruct((B, S, DN_DIM), BF16),
                   jax.ShapeDtypeStruct((B, S, DN_DIM), BF16),
                   jax.ShapeDtypeStruct((B, S, DN_DIM), BF16),
                   jax.ShapeDtypeStruct((B, S, DN_HEADS * CHUNK), BF16)),
        compiler_params=pltpu.CompilerParams(
            dimension_semantics=("arbitrary", "arbitrary"), vmem_limit_bytes=VMEM_LIMIT),
        name="delta_prep",
    )(q, k, v, gcol, grow)


def _delta_scan_kernel(u_ref, w_ref, qd_ref, kd_ref, at_ref, zg_ref, grow_ref, ng_ref, o_ref, st_ref):
    tm = u_ref.shape[1]

    @pl.when(pl.program_id(1) == 0)
    def _():
        st_ref[...] = jnp.zeros(st_ref.shape, F32)

    def step(c, carry):
        r0 = pl.multiple_of(c * CHUNK, CHUNK)
        rows = pl.ds(r0, CHUNK)
        heads = range(DN_HEADS)
        cols = [slice(h * DN_HEAD_DIM, (h + 1) * DN_HEAD_DIM) for h in heads]
        sts = [st_ref[h] for h in heads]
        st16s = [st.astype(BF16) for st in sts]
        wss = [_dot(w_ref[0, rows, cols[h]], st16s[h]) for h in heads]
        qss = [_dot(qd_ref[0, rows, cols[h]], st16s[h]) for h in heads]
        vn16s = [(u_ref[0, rows, cols[h]] - wss[h]).astype(BF16) for h in heads]
        upd = [_dot_tn(kd_ref[0, rows, cols[h]], vn16s[h]) for h in heads]
        avs = [_dot(at_ref[0, rows, h * CHUNK:(h + 1) * CHUNK], vn16s[h]) for h in heads]
        for h in heads:
            gl = grow_ref[0, c, h + DN_HEADS:h + DN_HEADS + 1, CHUNK - 1:CHUNK]
            st_ref[h] = sts[h] * jnp.exp(gl) + upd[h]
        for h in heads:
            o = qss[h] + avs[h]
            on = o * lax.rsqrt(jnp.mean(o * o, axis=-1, keepdims=True) + EPS) * ng_ref[...]
            o_ref[0, rows, cols[h]] = on * zg_ref[0, rows, cols[h]]
        return carry

    lax.fori_loop(0, tm // CHUNK, step, 0)


def _delta_scan(u, w, qd, kd, at, zg, grow, norm_g):
    B, S, _ = u.shape
    tm = TM_SCAN
    row = lambda width: pl.BlockSpec((1, tm, width), lambda b, t: (b, t, 0))
    return pl.pallas_call(
        _delta_scan_kernel,
        grid=(B, S // tm),
        in_specs=[row(DN_DIM), row(DN_DIM), row(DN_DIM), row(DN_DIM), row(DN_HEADS * CHUNK), row(DN_DIM),
                  pl.BlockSpec((1, tm // CHUNK, HALO, CHUNK), lambda b, t: (b, t, 0, 0)),
                  pl.BlockSpec((1, DN_HEAD_DIM), lambda b, t: (0, 0))],
        out_specs=row(DN_DIM),
        out_shape=jax.ShapeDtypeStruct((B, S, DN_DIM), F32),
        scratch_shapes=[pltpu.VMEM((DN_HEADS, DN_HEAD_DIM, DN_HEAD_DIM), F32)],
        compiler_params=pltpu.CompilerParams(
            dimension_semantics=("arbitrary", "arbitrary"), vmem_limit_bytes=VMEM_LIMIT),
        name="delta_scan",
    )(u, w, qd, kd, at, zg, grow, norm_g)


def _outproj_kernel(ya_ref, yb_ref, x_ref, wo_ref, g_ref, rw_ref,
                    x1_ref, h2_ref, ri_ref, rg_ref, cnt_ref, base_ref):
    tm = x_ref.shape[0]

    @pl.when(pl.program_id(0) == 0)
    def _():
        base_ref[...] = jnp.zeros(base_ref.shape, F32)

    y = _dot(ya_ref[...].astype(BF16), wo_ref[0:CONV_A_DIM, :])
    y = y + _dot(yb_ref[...].astype(BF16), wo_ref[CONV_A_DIM:, :])
    x1 = x_ref[...] + y
    x1_ref[...] = x1
    h = x1 * lax.rsqrt(jnp.mean(x1 * x1, axis=-1, keepdims=True) + EPS) * g_ref[...]
    h2_ref[...] = h

    logits = jnp.dot(h, rw_ref[...], preferred_element_type=F32, precision=lax.Precision.HIGHEST)
    lane = lax.broadcasted_iota(I32, (tm, LANES), 1)
    lanef = lane.astype(F32)
    neg = jnp.float32(-1e30)
    big = jnp.float32(1e9)
    is_g = lane < N_GROUPS
    gl = jnp.where(is_g, logits, neg)
    gmax = jnp.max(gl, axis=-1, keepdims=True)
    gidx = jnp.min(jnp.where(gl == gmax, lanef, big), axis=-1, keepdims=True)
    gsum = jnp.sum(jnp.where(is_g, jnp.exp(gl - gmax), 0.0), axis=-1, keepdims=True)
    gprob = 1.0 / gsum
    lo = N_GROUPS + EXPERTS_PER_GROUP * gidx
    emask = (lanef >= lo) & (lanef < lo + EXPERTS_PER_GROUP)
    el = jnp.where(emask, logits, neg)
    e1 = jnp.max(el, axis=-1, keepdims=True)
    i1 = jnp.min(jnp.where(el == e1, lanef, big), axis=-1, keepdims=True)
    el2 = jnp.where(lanef == i1, neg, el)
    e2 = jnp.max(el2, axis=-1, keepdims=True)
    i2 = jnp.min(jnp.where(el2 == e2, lanef, big), axis=-1, keepdims=True)
    r = jnp.exp(e2 - e1)
    gate1 = gprob / (1.0 + r)
    gate2 = gprob * r / (1.0 + r)
    id1 = i1 - N_GROUPS
    id2 = i2 - N_GROUPS

    oh1 = jnp.where(lanef == id1, 1.0, 0.0).astype(F32)
    oh2 = jnp.where(lanef == id2, 1.0, 0.0).astype(F32)
    oh = oh1 + oh2
    rr = lax.broadcasted_iota(I32, (tm, tm), 0)
    cc = lax.broadcasted_iota(I32, (tm, tm), 1)
    tri = jnp.where(cc < rr, 1.0, 0.0).astype(BF16)
    before = _dot(tri, oh.astype(BF16)) + base_ref[...]
    rank1 = jnp.sum(oh1 * before, axis=-1, keepdims=True)
    rank2 = jnp.sum(oh2 * before, axis=-1, keepdims=True)
    base_ref[...] = base_ref[...] + jnp.sum(oh, axis=0, keepdims=True)
    cnt_ref[...] = base_ref[...]

    ri = jnp.where(lane == 0, id1, jnp.where(lane == 1, id2,
                   jnp.where(lane == 2, rank1, jnp.where(lane == 3, rank2, 0.0))))
    ri_ref[...] = ri.astype(I32)
    rg_ref[...] = jnp.where(lane == 0, gate1, jnp.where(lane == 1, gate2, 0.0))


def _outproj(ya, yb, x, w_out16, ffn_g, rw):
    N, D = x.shape
    tm = TM_OUT
    row = lambda width: pl.BlockSpec((tm, width), lambda i: (i, 0))
    full = lambda shape: pl.BlockSpec(shape, lambda i: (0,) * len(shape))
    return pl.pallas_call(
        _outproj_kernel,
        grid=(N // tm,),
        in_specs=[row(CONV_A_DIM), row(DN_DIM), row(D), full((D, D)), full((1, D)), full((D, LANES))],
        out_specs=(row(D), row(D), row(LANES), row(LANES), full((1, LANES))),
        out_shape=(jax.ShapeDtypeStruct((N, D), F32),
                   jax.ShapeDtypeStruct((N, D), F32),
                   jax.ShapeDtypeStruct((N, LANES), I32),
                   jax.ShapeDtypeStruct((N, LANES), F32),
                   jax.ShapeDtypeStruct((1, LANES), F32)),
        scratch_shapes=[pltpu.VMEM((1, LANES), F32)],
        compiler_params=pltpu.CompilerParams(
            dimension_semantics=("arbitrary",), vmem_limit_bytes=VMEM_LIMIT),
        name="outproj",
    )(ya, yb, x, w_out16, ffn_g, rw)


def _slot_of(route_ref, start_ref, t, k):
    return start_ref[route_ref[4 * t + k]] + route_ref[4 * t + 2 + k]


def _dispatch_kernel(start_ref, route_ref, h_ref, xs_in_ref, xs_ref, sem):
    del xs_in_ref
    tm = TM_DISPATCH

    def row_copy(t, k):
        d = _slot_of(route_ref, start_ref, t, k)
        return pltpu.make_async_copy(h_ref.at[pl.ds(t, 1)], xs_ref.at[pl.ds(d, 1)], sem)

    def issue(t, carry):
        row_copy(t, 0).start()
        row_copy(t, 1).start()
        return carry

    lax.fori_loop(0, tm, issue, 0)
    for _ in range(TOP_K):
        pltpu.make_async_copy(h_ref, xs_ref.at[pl.ds(0, tm)], sem).wait()


def _dispatch(seg_start, route, h2, xs_init):
    N, D = h2.shape
    tm = TM_DISPATCH
    grid_spec = pltpu.PrefetchScalarGridSpec(
        num_scalar_prefetch=1,
        grid=(N // tm,),
        in_specs=[pl.BlockSpec((4 * tm,), lambda i, s: (i,), memory_space=pltpu.SMEM),
                  pl.BlockSpec((tm, D), lambda i, s: (i, 0)),
                  pl.BlockSpec(memory_space=pl.ANY)],
        out_specs=pl.BlockSpec(memory_space=pl.ANY),
        scratch_shapes=[pltpu.SemaphoreType.DMA(())],
    )
    return pl.pallas_call(
        _dispatch_kernel,
        grid_spec=grid_spec,
        out_shape=jax.ShapeDtypeStruct(xs_init.shape, xs_init.dtype),
        input_output_aliases={3: 0},
        compiler_params=pltpu.CompilerParams(dimension_semantics=("arbitrary",)),
        name="dispatch",
    )(seg_start, route, h2, xs_init)


def _ffn_kernel(be_ref, nb_ref, x_ref, wg_ref, wu_ref, wd_ref, y_ref, wg16, wu16, wd16):
    i = pl.program_id(0)
    prev = be_ref[jnp.maximum(i - 1, 0)]

    @pl.when((i == 0) | (be_ref[i] != prev))
    def _():
        wg16[...] = wg_ref[0].astype(BF16)
        wu16[...] = wu_ref[0].astype(BF16)
        wd16[...] = wd_ref[0].astype(BF16)

    @pl.when(i < nb_ref[0])
    def _():
        x = x_ref[...].astype(BF16)
        a = _dot(x, wg16[...])
        b = _dot(x, wu16[...])
        y_ref[...] = _dot((_silu(a) * b).astype(BF16), wd16[...])

    @pl.when(i >= nb_ref[0])
    def _():
        y_ref[...] = jnp.zeros(y_ref.shape, F32)


def _ffn(block_expert, n_used, xs, w_gate, w_up, w_down):
    P, D = xs.shape
    bm = FFN_BLOCK
    rows = pl.BlockSpec((bm, D), lambda i, be, nb: (jnp.minimum(i, jnp.maximum(nb[0] - 1, 0)), 0))
    grid_spec = pltpu.PrefetchScalarGridSpec(
        num_scalar_prefetch=2,
        grid=(P // bm,),
        in_specs=[rows,
                  pl.BlockSpec((1, D, EXPERT_FF), lambda i, be, nb: (be[i], 0, 0)),
                  pl.BlockSpec((1, D, EXPERT_FF), lambda i, be, nb: (be[i], 0, 0)),
                  pl.BlockSpec((1, EXPERT_FF, D), lambda i, be, nb: (be[i], 0, 0))],
        out_specs=pl.BlockSpec((bm, D), lambda i, be, nb: (i, 0)),
        scratch_shapes=[pltpu.VMEM((D, EXPERT_FF), BF16), pltpu.VMEM((D, EXPERT_FF), BF16),
                        pltpu.VMEM((EXPERT_FF, D), BF16)],
    )
    return pl.pallas_call(
        _ffn_kernel,
        grid_spec=grid_spec,
        out_shape=jax.ShapeDtypeStruct((P, D), F32),
        compiler_params=pltpu.CompilerParams(
            dimension_semantics=("arbitrary",), vmem_limit_bytes=VMEM_LIMIT),
        name="ffn",
    )(block_expert, n_used, xs, w_gate, w_up, w_down)


def _combine_kernel(start_ref, route_ref, ys_ref, x1_ref, rg_ref, g_ref, o_ref, buf_ref, sem):
    tm = TM_COMBINE

    def row_copy(t, k):
        d = _slot_of(route_ref, start_ref, t, k)
        return pltpu.make_async_copy(ys_ref.at[pl.ds(d, 1)], buf_ref.at[k, pl.ds(t, 1)], sem)

    def issue(t, carry):
        row_copy(t, 0).start()
        row_copy(t, 1).start()
        return carry

    lax.fori_loop(0, tm, issue, 0)
    for k in range(TOP_K):
        pltpu.make_async_copy(ys_ref.at[pl.ds(0, tm)], buf_ref.at[k], sem).wait()

    rg = rg_ref[...]
    moe = buf_ref[0] * rg[:, 0:1] + buf_ref[1] * rg[:, 1:2]
    x2 = x1_ref[...] + moe
    o_ref[...] = x2 * lax.rsqrt(jnp.mean(x2 * x2, axis=-1, keepdims=True) + EPS) * g_ref[...]


def _combine(seg_start, route, ys, x1, rg, final_g):
    N, D = x1.shape
    tm = TM_COMBINE
    grid_spec = pltpu.PrefetchScalarGridSpec(
        num_scalar_prefetch=1,
        grid=(N // tm,),
        in_specs=[pl.BlockSpec((4 * tm,), lambda i, s: (i,), memory_space=pltpu.SMEM),
                  pl.BlockSpec(memory_space=pl.ANY),
                  pl.BlockSpec((tm, D), lambda i, s: (i, 0)),
                  pl.BlockSpec((tm, LANES), lambda i, s: (i, 0)),
                  pl.BlockSpec((1, D), lambda i, s: (0, 0))],
        out_specs=pl.BlockSpec((tm, D), lambda i, s: (i, 0)),
        scratch_shapes=[pltpu.VMEM((TOP_K, tm, D), F32), pltpu.SemaphoreType.DMA(())],
    )
    return pl.pallas_call(
        _combine_kernel,
        grid_spec=grid_spec,
        out_shape=jax.ShapeDtypeStruct((N, D), F32),
        compiler_params=pltpu.CompilerParams(
            dimension_semantics=("arbitrary",), vmem_limit_bytes=VMEM_LIMIT),
        name="combine",
    )(seg_start, route, ys, x1, rg, final_g)


def _lane_vec(values, offset):
    return jnp.zeros((1, LANES), F32).at[0, offset:offset + values.shape[0]].set(values.astype(F32))


def kernel(x, mix_norm_g, w_in, conv_a_w, conv_a_norm_g, dn_conv_w, dn_a_log, dn_dt_bias, dn_norm_g,
           w_out, ffn_norm_g, router_group_w, router_expert_w, w_gate, w_up, w_down, final_norm_g):
    B, S, D = x.shape
    N = B * S
    depth = w_in.shape[0]
    assert depth == 1, "single-layer block: the final RMSNorm is fused into the layer's combine step"
    group_of = jnp.arange(CONV_A_DIM, dtype=I32) // CONV_A_GROUP_DIM
    gmat = jnp.where(group_of[:, None] == group_of[None, :], 1.0 / CONV_A_GROUP_DIM, 0.0).astype(BF16)
    bm = FFN_BLOCK
    n_blocks = (N * TOP_K) // bm + N_EXPERTS
    for l in range(depth):
        w_in_pad = jnp.pad(w_in[l], ((0, 0), (0, IN_PROJ_PAD - IN_PROJ_DIM))).astype(BF16)
        ya, q, k, v, zg, gcol, grow = _inproj(
            x, mix_norm_g[l][None, :], w_in_pad, conv_a_w[l], conv_a_norm_g[l][None, :], dn_conv_w[l],
            _lane_vec(dn_a_log[l], DN_HEADS), _lane_vec(dn_dt_bias[l], DN_HEADS), gmat)
        u, w, qd, kd, at = _delta_prep(q, k, v, gcol, grow)
        yb = _delta_scan(u, w, qd, kd, at, zg, grow, dn_norm_g[l][None, :])
        rw = jnp.pad(jnp.concatenate([router_group_w[l], router_expert_w[l]], axis=1),
                     ((0, 0), (0, LANES - N_GROUPS - N_EXPERTS)))
        x1, h2, ri, rg, cnt = _outproj(ya.reshape(N, CONV_A_DIM), yb.reshape(N, DN_DIM), x.reshape(N, D),
                                       w_out[l].astype(BF16), ffn_norm_g[l][None, :], rw)
        counts = cnt[0, :N_EXPERTS].astype(I32)
        padded = (counts + bm - 1) // bm * bm
        seg_end = jnp.cumsum(padded)
        seg_start = (seg_end - padded).astype(I32)
        n_used = (seg_end[-1] // bm).astype(I32)
        blk = jnp.arange(n_blocks, dtype=I32)
        blk_expert = jnp.sum(seg_end[None, :] <= (jnp.minimum(blk, n_used - 1) * bm)[:, None], axis=1)
        blk_expert = jnp.minimum(blk_expert, N_EXPERTS - 1).astype(I32)
        route = ri[:, :4].reshape(-1)
        xs = _dispatch(seg_start, route, h2, jnp.zeros((n_blocks * bm, D), F32))
        ys = _ffn(blk_expert, n_used.reshape(1), xs, w_gate[l], w_up[l], w_down[l])
        x = _combine(seg_start, route, ys, x1, rg, final_norm_g[None, :]).reshape(B, S, D)
    return x
```

```python
import functools

import jax
import jax.numpy as jnp
from jax import lax
from jax.experimental import pallas as pl
from jax.experimental.pallas import tpu as pltpu

F32 = jnp.float32
BF16 = jnp.bfloat16
I32 = jnp.int32

D_MODEL = 1024
CHUNK = 64
CONV_A_GROUP_DIM = 64
CONV_A_DIM = 512
CONV_A_WIDTH = 3
DN_HEADS = 4
DN_HEAD_DIM = 128
DN_DIM = 512
DN_CONV_WIDTH = 4
IN_PROJ_DIM = 3 * CONV_A_DIM + 4 * DN_DIM + 2 * DN_HEADS
N_GROUPS = 4
EXPERTS_PER_GROUP = 8
N_EXPERTS = 32
TOP_K = 2
EXPERT_FF = 512
EPS = 1e-6

LANES = 128
HALO = 8
IN_PROJ_PAD = 29 * LANES
GATE_COL = 3 * CONV_A_DIM + 4 * DN_DIM

TM_IN = 512
TM_PREP = 256
TM_SCAN = 512
TM_OUT = 512
FFN_BLOCK = 256
TM_DISPATCH = 512
TM_COMBINE = 256
ISSUE_UNROLL = 4
VMEM_LIMIT = 56 * 1024 * 1024


def _dot(a, b):
    return jnp.dot(a, b, preferred_element_type=F32)


def _dot_nt(a, b):
    return lax.dot_general(a, b, (((1,), (1,)), ((), ())), preferred_element_type=F32)


def _dot_tn(a, b):
    return lax.dot_general(a, b, (((0,), (0,)), ((), ())), preferred_element_type=F32)


def _silu(x):
    return x * jax.nn.sigmoid(x)


ROW_SLAB = D_MODEL // LANES


def _store_rows(ref, val):
    m = val.shape[0]
    for c in range(ROW_SLAB):
        ref[pl.ds(c, m, stride=ROW_SLAB), :] = val[:, c * LANES:(c + 1) * LANES]


def _load_rows(ref, m):
    return jnp.concatenate([ref[pl.ds(c, m, stride=ROW_SLAB), :] for c in range(ROW_SLAB)], axis=-1)


def _slab(ref, row):
    return ref.at[pl.ds(pl.multiple_of(row * ROW_SLAB, ROW_SLAB), ROW_SLAB)]


def _slab_block(ref, block):
    n = FFN_BLOCK * ROW_SLAB
    return ref.at[pl.ds(pl.multiple_of(block * n, n), n)]


def _softplus(x):
    return jnp.maximum(x, 0.0) + jnp.log1p(jnp.exp(-jnp.abs(x)))


def _inproj_kernel(x_ref, g_ref, w_ref, caw_ref, cag_ref, dcw_ref, alog_ref, dtb_ref, gmat_ref,
                   ya_ref, q_ref, k_ref, v_ref, zg_ref, gcol_ref, grow_ref,
                   exta_ref, extq_ref):
    tm = x_ref.shape[1]

    @pl.when(pl.program_id(1) == 0)
    def _():
        exta_ref[0:HALO, :] = jnp.zeros((HALO, CONV_A_DIM), F32)
        extq_ref[0:HALO, :] = jnp.zeros((HALO, 3 * DN_DIM), F32)

    x = x_ref[0]
    ms = jnp.mean(x * x, axis=-1, keepdims=True)
    hb = (x * lax.rsqrt(ms + EPS) * g_ref[...]).astype(BF16)

    def proj(c0, width):
        return _dot(hb, w_ref[:, c0:c0 + width])

    a_h = proj(0, CONV_A_DIM)
    a_c = proj(2 * CONV_A_DIM, CONV_A_DIM)
    exta_ref[HALO:HALO + tm, :] = a_c * a_h
    conv = caw_ref[0:1, :] * exta_ref[pl.ds(HALO - 2, tm), :]
    conv = conv + caw_ref[1:2, :] * exta_ref[pl.ds(HALO - 1, tm), :]
    conv = conv + caw_ref[2:3, :] * exta_ref[pl.ds(HALO, tm), :]
    y = proj(CONV_A_DIM, CONV_A_DIM) * conv
    ysq = y * y
    hi = ysq.astype(BF16)
    lo = (ysq - hi.astype(F32)).astype(BF16)
    gmean = _dot(hi, gmat_ref[...]) + _dot(lo, gmat_ref[...])
    ya_ref[0] = y * lax.rsqrt(gmean + EPS) * cag_ref[...]
    exta_ref[0:HALO, :] = exta_ref[tm:tm + HALO, :]

    base = 3 * CONV_A_DIM
    for i in range(3):
        extq_ref[HALO:HALO + tm, i * DN_DIM:(i + 1) * DN_DIM] = proj(base + i * DN_DIM, DN_DIM)
    for i, out_ref in enumerate((q_ref, k_ref, v_ref)):
        cols = slice(i * DN_DIM, (i + 1) * DN_DIM)
        acc = dcw_ref[0:1, cols] * extq_ref[pl.ds(HALO - 3, tm), cols]
        for j in range(1, DN_CONV_WIDTH):
            acc = acc + dcw_ref[j:j + 1, cols] * extq_ref[pl.ds(HALO - 3 + j, tm), cols]
        s = _silu(acc)
        if i == 2:
            out_ref[0] = s
        else:
            for h in range(DN_HEADS):
                sh = s[:, h * DN_HEAD_DIM:(h + 1) * DN_HEAD_DIM]
                inv = lax.rsqrt(jnp.sum(sh * sh, axis=-1, keepdims=True) + EPS)
                sh = sh * inv
                if i == 0:
                    sh = sh * (DN_HEAD_DIM ** -0.5)
                out_ref[0, :, h * DN_HEAD_DIM:(h + 1) * DN_HEAD_DIM] = sh
    extq_ref[0:HALO, :] = extq_ref[tm:tm + HALO, :]

    zg_ref[0] = _silu(proj(base + 3 * DN_DIM, DN_DIM))

    p = proj(GATE_COL, LANES)
    beta = jax.nn.sigmoid(p)
    g = -jnp.exp(alog_ref[...]) * _softplus(p + dtb_ref[...])
    r = lax.broadcasted_iota(I32, (tm, tm), 0)
    c = lax.broadcasted_iota(I32, (tm, tm), 1)
    tri = jnp.where((r // CHUNK == c // CHUNK) & (c <= r), 1.0, 0.0).astype(F32)
    gc = jnp.dot(tri, g, preferred_element_type=F32, precision=lax.Precision.HIGHEST)
    lane = lax.broadcasted_iota(I32, (tm, LANES), 1)
    slab = jnp.where(lane < DN_HEADS, beta, gc)
    gcol_ref[0] = slab
    rows = slab.T[0:HALO, :]
    for ci in range(tm // CHUNK):
        grow_ref[0, ci] = rows[:, ci * CHUNK:(ci + 1) * CHUNK]


def _inproj(x, mix_g, w_in_pad, conv_a_w, conv_a_g, dn_conv_w, alog_vec, dtb_vec, gmat):
    B, S, D = x.shape
    tm = TM_IN
    full = lambda shape: pl.BlockSpec(shape, lambda b, t: (0,) * len(shape))
    row = lambda width: pl.BlockSpec((1, tm, width), lambda b, t: (b, t, 0))
    out_shape = (
        jax.ShapeDtypeStruct((B, S, CONV_A_DIM), F32),
        jax.ShapeDtypeStruct((B, S, DN_DIM), F32),
        jax.ShapeDtypeStruct((B, S, DN_DIM), F32),
        jax.ShapeDtypeStruct((B, S, DN_DIM), F32),
        jax.ShapeDtypeStruct((B, S, DN_DIM), F32),
        jax.ShapeDtypeStruct((B, S, LANES), F32),
        jax.ShapeDtypeStruct((B, S // CHUNK, HALO, CHUNK), F32),
    )
    return pl.pallas_call(
        _inproj_kernel,
        grid=(B, S // tm),
        in_specs=[row(D), full((1, D)), full((D, IN_PROJ_PAD)), full((CONV_A_WIDTH, CONV_A_DIM)),
                  full((1, CONV_A_DIM)), full((DN_CONV_WIDTH, 3 * DN_DIM)), full((1, LANES)),
                  full((1, LANES)), full((CONV_A_DIM, CONV_A_DIM))],
        out_specs=(row(CONV_A_DIM), row(DN_DIM), row(DN_DIM), row(DN_DIM), row(DN_DIM), row(LANES),
                   pl.BlockSpec((1, tm // CHUNK, HALO, CHUNK), lambda b, t: (b, t, 0, 0))),
        out_shape=out_shape,
        scratch_shapes=[pltpu.VMEM((tm + HALO, CONV_A_DIM), F32),
                        pltpu.VMEM((tm + HALO, 3 * DN_DIM), F32)],
        compiler_params=pltpu.CompilerParams(
            dimension_semantics=("arbitrary", "arbitrary"), vmem_limit_bytes=VMEM_LIMIT),
        name="inproj",
    )(x, mix_g, w_in_pad, conv_a_w, conv_a_g, dn_conv_w, alog_vec, dtb_vec, gmat)


def _delta_prep_kernel(q_ref, k_ref, v_ref, gcol_ref, grow_ref,
                       u_ref, w_ref, qd_ref, kd_ref, at_ref):
    tm = q_ref.shape[1]
    ri = lax.broadcasted_iota(I32, (CHUNK, CHUNK), 0)
    ci = lax.broadcasted_iota(I32, (CHUNK, CHUNK), 1)
    causal = ci <= ri
    strict = ci < ri
    eye = jnp.where(ci == ri, 1.0, 0.0).astype(F32)
    lane = lax.broadcasted_iota(I32, (CHUNK, LANES), 1)
    chains = [(c, h) for c in range(tm // CHUNK) for h in range(DN_HEADS)]
    ms, ts, rhss = [], [], []
    for c, h in chains:
        rows = slice(c * CHUNK, (c + 1) * CHUNK)
        cols = slice(h * DN_HEAD_DIM, (h + 1) * DN_HEAD_DIM)
        slab = gcol_ref[0, rows, :]
        beta = jnp.sum(jnp.where(lane == h, slab, 0.0), axis=-1, keepdims=True)
        gcc = jnp.sum(jnp.where(lane == h + DN_HEADS, slab, 0.0), axis=-1, keepdims=True)
        gcr = grow_ref[0, c, h + DN_HEADS:h + DN_HEADS + 1, :]
        diff = gcc - gcr
        decay = jnp.where(causal, jnp.exp(jnp.where(causal, diff, 0.0)), 0.0)
        q = q_ref[0, rows, cols]
        k = k_ref[0, rows, cols]
        v = v_ref[0, rows, cols]
        kb = k * beta
        k16 = k.astype(BF16)
        L = jnp.where(strict, _dot_nt(kb.astype(BF16), k16) * decay, 0.0)
        intra = _dot_nt(q.astype(BF16), k16) * decay
        egc = jnp.exp(gcc)
        gl = gcr[:, CHUNK - 1:CHUNK]
        qd_ref[0, rows, cols] = (q * egc).astype(BF16)
        kd_ref[0, rows, cols] = (k * jnp.exp(gl - gcc)).astype(BF16)
        at_ref[0, rows, h * CHUNK:(h + 1) * CHUNK] = intra.astype(BF16)
        rhss.append(jnp.concatenate([v * beta, kb * egc], axis=-1).astype(BF16))
        ms.append(-L)
        ts.append(eye - L)
    for _ in range(5):
        m16s = [m.astype(BF16) for m in ms]
        ms = [_dot(m16, m16) for m16 in m16s]
        ts = [t + _dot(t.astype(BF16), m.astype(BF16)) for t, m in zip(ts, ms)]
    for (c, h), t, rhs in zip(chains, ts, rhss):
        rows = slice(c * CHUNK, (c + 1) * CHUNK)
        cols = slice(h * DN_HEAD_DIM, (h + 1) * DN_HEAD_DIM)
        uw = _dot(t.astype(BF16), rhs)
        u_ref[0, rows, cols] = uw[:, :DN_HEAD_DIM]
        w_ref[0, rows, cols] = uw[:, DN_HEAD_DIM:].astype(BF16)


def _delta_prep(q, k, v, gcol, grow):
    B, S, _ = q.shape
    tm = TM_PREP
    row = lambda width: pl.BlockSpec((1, tm, width), lambda b, t: (b, t, 0))
    return pl.pallas_call(
        _delta_prep_kernel,
        grid=(B, S // tm),
        in_specs=[row(DN_DIM), row(DN_DIM), row(DN_DIM), row(LANES),
                  pl.BlockSpec((1, tm // CHUNK, HALO, CHUNK), lambda b, t: (b, t, 0, 0))],
        out_specs=(row(DN_DIM), row(DN_DIM), row(DN_DIM), row(DN_DIM), row(DN_HEADS * CHUNK)),
        out_shape=(jax.ShapeDtypeStruct((B, S, DN_DIM), F32),
                   jax.ShapeDtypeStruct((B, S, DN_DIM), BF16),
                   jax.ShapeDtypeStruct((B, S, DN_DIM), BF16),
                   jax.ShapeDtypeStruct((B, S, DN_DIM), BF16),
                   jax.ShapeDtypeStruct((B, S, DN_HEADS * CHUNK), BF16)),
        compiler_params=pltpu.CompilerParams(
            dimension_semantics=("arbitrary", "arbitrary"), vmem_limit_bytes=VMEM_LIMIT),
        name="delta_prep",
    )(q, k, v, gcol, grow)


def _delta_scan_kernel(u_ref, w_ref, qd_ref, kd_ref, at_ref, zg_ref, grow_ref, ng_ref, o_ref, st_ref):
    tm = u_ref.shape[1]

    @pl.when(pl.program_id(1) == 0)
    def _():
        st_ref[...] = jnp.zeros(st_ref.shape, F32)

    def step(c, carry):
        r0 = pl.multiple_of(c * CHUNK, CHUNK)
        rows = pl.ds(r0, CHUNK)
        heads = range(DN_HEADS)
        cols = [slice(h * DN_HEAD_DIM, (h + 1) * DN_HEAD_DIM) for h in heads]
        sts = [st_ref[h] for h in heads]
        st16s = [st.astype(BF16) for st in sts]
        wss = [_dot(w_ref[0, rows, cols[h]], st16s[h]) for h in heads]
        qss = [_dot(qd_ref[0, rows, cols[h]], st16s[h]) for h in heads]
        vn16s = [(u_ref[0, rows, cols[h]] - wss[h]).astype(BF16) for h in heads]
        upd = [_dot_tn(kd_ref[0, rows, cols[h]], vn16s[h]) for h in heads]
        avs = [_dot(at_ref[0, rows, h * CHUNK:(h + 1) * CHUNK], vn16s[h]) for h in heads]
        for h in heads:
            gl = grow_ref[0, c, h + DN_HEADS:h + DN_HEADS + 1, CHUNK - 1:CHUNK]
            st_ref[h] = sts[h] * jnp.exp(gl) + upd[h]
        for h in heads:
            o = qss[h] + avs[h]
            on = o * lax.rsqrt(jnp.mean(o * o, axis=-1, keepdims=True) + EPS) * ng_ref[...]
            o_ref[0, rows, cols[h]] = on * zg_ref[0, rows, cols[h]]
        return carry

    lax.fori_loop(0, tm // CHUNK, step, 0)


def _delta_scan(u, w, qd, kd, at, zg, grow, norm_g):
    B, S, _ = u.shape
    tm = TM_SCAN
    row = lambda width: pl.BlockSpec((1, tm, width), lambda b, t: (b, t, 0))
    return pl.pallas_call(
        _delta_scan_kernel,
        grid=(B, S // tm),
        in_specs=[row(DN_DIM), row(DN_DIM), row(DN_DIM), row(DN_DIM), row(DN_HEADS * CHUNK), row(DN_DIM),
                  pl.BlockSpec((1, tm // CHUNK, HALO, CHUNK), lambda b, t: (b, t, 0, 0)),
                  pl.BlockSpec((1, DN_HEAD_DIM), lambda b, t: (0, 0))],
        out_specs=row(DN_DIM),
        out_shape=jax.ShapeDtypeStruct((B, S, DN_DIM), F32),
        scratch_shapes=[pltpu.VMEM((DN_HEADS, DN_HEAD_DIM, DN_HEAD_DIM), F32)],
        compiler_params=pltpu.CompilerParams(
            dimension_semantics=("arbitrary", "arbitrary"), vmem_limit_bytes=VMEM_LIMIT),
        name="delta_scan",
    )(u, w, qd, kd, at, zg, grow, norm_g)


def _outproj_kernel(ya_ref, yb_ref, x_ref, wo_ref, g_ref, rw_ref,
                    x1_ref, h2_ref, ri_ref, rg_ref, cnt_ref, base_ref):
    tm = x_ref.shape[0]

    @pl.when(pl.program_id(0) == 0)
    def _():
        base_ref[...] = jnp.zeros(base_ref.shape, F32)

    y = _dot(ya_ref[...].astype(BF16), wo_ref[0:CONV_A_DIM, :])
    y = y + _dot(yb_ref[...].astype(BF16), wo_ref[CONV_A_DIM:, :])
    x1 = x_ref[...] + y
    x1_ref[...] = x1
    h = x1 * lax.rsqrt(jnp.mean(x1 * x1, axis=-1, keepdims=True) + EPS) * g_ref[...]
    _store_rows(h2_ref, h)

    logits = jnp.dot(h, rw_ref[...], preferred_element_type=F32, precision=lax.Precision.HIGHEST)
    lane = lax.broadcasted_iota(I32, (tm, LANES), 1)
    lanef = lane.astype(F32)
    neg = jnp.float32(-1e30)
    big = jnp.float32(1e9)
    is_g = lane < N_GROUPS
    gl = jnp.where(is_g, logits, neg)
    gmax = jnp.max(gl, axis=-1, keepdims=True)
    gidx = jnp.min(jnp.where(gl == gmax, lanef, big), axis=-1, keepdims=True)
    gsum = jnp.sum(jnp.where(is_g, jnp.exp(gl - gmax), 0.0), axis=-1, keepdims=True)
    gprob = 1.0 / gsum
    lo = N_GROUPS + EXPERTS_PER_GROUP * gidx
    emask = (lanef >= lo) & (lanef < lo + EXPERTS_PER_GROUP)
    el = jnp.where(emask, logits, neg)
    e1 = jnp.max(el, axis=-1, keepdims=True)
    i1 = jnp.min(jnp.where(el == e1, lanef, big), axis=-1, keepdims=True)
    el2 = jnp.where(lanef == i1, neg, el)
    e2 = jnp.max(el2, axis=-1, keepdims=True)
    i2 = jnp.min(jnp.where(el2 == e2, lanef, big), axis=-1, keepdims=True)
    r = jnp.exp(e2 - e1)
    gate1 = gprob / (1.0 + r)
    gate2 = gprob * r / (1.0 + r)
    id1 = i1 - N_GROUPS
    id2 = i2 - N_GROUPS

    oh1 = jnp.where(lanef == id1, 1.0, 0.0).astype(F32)
    oh2 = jnp.where(lanef == id2, 1.0, 0.0).astype(F32)
    oh = oh1 + oh2
    rr = lax.broadcasted_iota(I32, (tm, tm), 0)
    cc = lax.broadcasted_iota(I32, (tm, tm), 1)
    tri = jnp.where(cc < rr, 1.0, 0.0).astype(BF16)
    before = _dot(tri, oh.astype(BF16)) + base_ref[...]
    rank1 = jnp.sum(oh1 * before, axis=-1, keepdims=True)
    rank2 = jnp.sum(oh2 * before, axis=-1, keepdims=True)
    base_ref[...] = base_ref[...] + jnp.sum(oh, axis=0, keepdims=True)
    cnt_ref[...] = base_ref[...]

    ri = jnp.where(lane == 0, id1, jnp.where(lane == 1, id2,
                   jnp.where(lane == 2, rank1, jnp.where(lane == 3, rank2, 0.0))))
    ri_ref[...] = ri.astype(I32)
    rg_ref[...] = jnp.where(lane == 0, gate1, jnp.where(lane == 1, gate2, 0.0))


def _outproj(ya, yb, x, w_out16, ffn_g, rw):
    N, D = x.shape
    tm = TM_OUT
    row = lambda width: pl.BlockSpec((tm, width), lambda i: (i, 0))
    full = lambda shape: pl.BlockSpec(shape, lambda i: (0,) * len(shape))
    return pl.pallas_call(
        _outproj_kernel,
        grid=(N // tm,),
        in_specs=[row(CONV_A_DIM), row(DN_DIM), row(D), full((D, D)), full((1, D)), full((D, LANES))],
        out_specs=(row(D), pl.BlockSpec((tm * ROW_SLAB, LANES), lambda i: (i, 0)), row(LANES), row(LANES),
                   full((1, LANES))),
        out_shape=(jax.ShapeDtypeStruct((N, D), F32),
                   jax.ShapeDtypeStruct((N * ROW_SLAB, LANES), F32),
                   jax.ShapeDtypeStruct((N, LANES), I32),
                   jax.ShapeDtypeStruct((N, LANES), F32),
                   jax.ShapeDtypeStruct((1, LANES), F32)),
        scratch_shapes=[pltpu.VMEM((1, LANES), F32)],
        compiler_params=pltpu.CompilerParams(
            dimension_semantics=("arbitrary",), vmem_limit_bytes=VMEM_LIMIT),
        name="outproj",
    )(ya, yb, x, w_out16, ffn_g, rw)


def _slot_of(route_ref, start_ref, t, k):
    return start_ref[route_ref[4 * t + k]] + route_ref[4 * t + 2 + k]


def _dispatch_kernel(seg_ref, route_ref, h_ref, xs_ref, zero_ref, sem, zsem):
    tm = TM_DISPATCH
    bm = FFN_BLOCK
    n_blocks = xs_ref.shape[0] // (bm * ROW_SLAB)
    start_ref = seg_ref

    @pl.when(pl.program_id(0) == 0)
    def _():
        zero_ref[...] = jnp.zeros(zero_ref.shape, F32)

        def pad_copy(d):
            return pltpu.make_async_copy(zero_ref.at[pl.ds(0, ROW_SLAB)], _slab(xs_ref, d), zsem)

        def block_copy(b):
            return pltpu.make_async_copy(zero_ref, _slab_block(xs_ref, b), zsem)

        def each_pad(fn):
            def per_expert(e, carry):
                def per_row(d, c):
                    fn(pad_copy(d))
                    return c
                return lax.fori_loop(seg_ref[N_EXPERTS + e], seg_ref[2 * N_EXPERTS + e], per_row, carry)
            lax.fori_loop(0, N_EXPERTS, per_expert, 0)

            def per_block(b, c):
                fn(block_copy(b))
                return c
            lax.fori_loop(seg_ref[3 * N_EXPERTS], n_blocks, per_block, 0)

        each_pad(lambda cp: cp.start())
        each_pad(lambda cp: cp.wait())

    def issue(t, carry):
        for k in range(TOP_K):
            d = _slot_of(route_ref, start_ref, t, k)
            pltpu.make_async_copy(_slab(h_ref, t), _slab(xs_ref, d), sem).start()
        return carry

    lax.fori_loop(0, tm, issue, 0, unroll=ISSUE_UNROLL)
    for _ in range(TOP_K):
        pltpu.make_async_copy(h_ref, xs_ref.at[pl.ds(0, tm * ROW_SLAB)], sem).wait()


def _dispatch(seg, route, h2, n_slots):
    N = h2.shape[0] // ROW_SLAB
    tm = TM_DISPATCH
    grid_spec = pltpu.PrefetchScalarGridSpec(
        num_scalar_prefetch=1,
        grid=(N // tm,),
        in_specs=[pl.BlockSpec((4 * tm,), lambda i, s: (i,), memory_space=pltpu.SMEM),
                  pl.BlockSpec((tm * ROW_SLAB, LANES), lambda i, s: (i, 0))],
        out_specs=pl.BlockSpec(memory_space=pl.ANY),
        scratch_shapes=[pltpu.VMEM((FFN_BLOCK * ROW_SLAB, LANES), F32),
                        pltpu.SemaphoreType.DMA(()), pltpu.SemaphoreType.DMA(())],
    )
    return pl.pallas_call(
        _dispatch_kernel,
        grid_spec=grid_spec,
        out_shape=jax.ShapeDtypeStruct((n_slots * ROW_SLAB, LANES), F32),
        compiler_params=pltpu.CompilerParams(dimension_semantics=("arbitrary",)),
        name="dispatch",
    )(seg, route, h2)


def _ffn_kernel(be_ref, nb_ref, x_ref, wg_ref, wu_ref, wd_ref, y_ref, wg16, wu16, wd16):
    i = pl.program_id(0)
    bm = FFN_BLOCK
    prev = be_ref[jnp.maximum(i - 1, 0)]

    @pl.when((i == 0) | (be_ref[i] != prev))
    def _():
        wg16[...] = wg_ref[0].astype(BF16)
        wu16[...] = wu_ref[0].astype(BF16)
        wd16[...] = wd_ref[0].astype(BF16)

    @pl.when(i < nb_ref[0])
    def _():
        x = _load_rows(x_ref, bm).astype(BF16)
        a = _dot(x, wg16[...])
        b = _dot(x, wu16[...])
        _store_rows(y_ref, _dot((_silu(a) * b).astype(BF16), wd16[...]))

    @pl.when(i >= nb_ref[0])
    def _():
        y_ref[...] = jnp.zeros(y_ref.shape, F32)


def _ffn(block_expert, n_used, xs, w_gate, w_up, w_down):
    bm = FFN_BLOCK
    n_blocks = xs.shape[0] // (bm * ROW_SLAB)
    D = D_MODEL
    last = lambda nb: jnp.maximum(nb[0] - 1, 0)
    grid_spec = pltpu.PrefetchScalarGridSpec(
        num_scalar_prefetch=2,
        grid=(n_blocks,),
        in_specs=[pl.BlockSpec((bm * ROW_SLAB, LANES), lambda i, be, nb: (jnp.minimum(i, last(nb)), 0)),
                  pl.BlockSpec((1, D, EXPERT_FF), lambda i, be, nb: (be[i], 0, 0)),
                  pl.BlockSpec((1, D, EXPERT_FF), lambda i, be, nb: (be[i], 0, 0)),
                  pl.BlockSpec((1, EXPERT_FF, D), lambda i, be, nb: (be[i], 0, 0))],
        out_specs=pl.BlockSpec((bm * ROW_SLAB, LANES), lambda i, be, nb: (i, 0)),
        scratch_shapes=[pltpu.VMEM((D, EXPERT_FF), BF16), pltpu.VMEM((D, EXPERT_FF), BF16),
                        pltpu.VMEM((EXPERT_FF, D), BF16)],
    )

    return pl.pallas_call(
        _ffn_kernel,
        grid_spec=grid_spec,
        out_shape=jax.ShapeDtypeStruct(xs.shape, F32),
        compiler_params=pltpu.CompilerParams(
            dimension_semantics=("arbitrary",), vmem_limit_bytes=VMEM_LIMIT),
        name="ffn",
    )(block_expert, n_used, xs, w_gate, w_up, w_down)


def _combine_kernel(start_ref, route_ref, ys_ref, x1_ref, rg_ref, g_ref, o_ref, buf_ref, sem):
    tm = TM_COMBINE

    def issue(t, carry):
        for k in range(TOP_K):
            d = _slot_of(route_ref, start_ref, t, k)
            pltpu.make_async_copy(_slab(ys_ref, d), _slab(buf_ref.at[k], t), sem).start()
        return carry

    lax.fori_loop(0, tm, issue, 0, unroll=ISSUE_UNROLL)
    for k in range(TOP_K):
        pltpu.make_async_copy(ys_ref.at[pl.ds(0, tm * ROW_SLAB)], buf_ref.at[k], sem).wait()

    rg = rg_ref[...]
    moe = _load_rows(buf_ref.at[0], tm) * rg[:, 0:1] + _load_rows(buf_ref.at[1], tm) * rg[:, 1:2]
    x2 = x1_ref[...] + moe
    o_ref[...] = x2 * lax.rsqrt(jnp.mean(x2 * x2, axis=-1, keepdims=True) + EPS) * g_ref[...]


def _combine(seg_start, route, ys, x1, rg, final_g):
    N, D = x1.shape
    tm = TM_COMBINE
    grid_spec = pltpu.PrefetchScalarGridSpec(
        num_scalar_prefetch=1,
        grid=(N // tm,),
        in_specs=[pl.BlockSpec((4 * tm,), lambda i, s: (i,), memory_space=pltpu.SMEM),
                  pl.BlockSpec(memory_space=pl.ANY),
                  pl.BlockSpec((tm, D), lambda i, s: (i, 0)),
                  pl.BlockSpec((tm, LANES), lambda i, s: (i, 0)),
                  pl.BlockSpec((1, D), lambda i, s: (0, 0))],
        out_specs=pl.BlockSpec((tm, D), lambda i, s: (i, 0)),
        scratch_shapes=[pltpu.VMEM((TOP_K, tm * ROW_SLAB, LANES), F32), pltpu.SemaphoreType.DMA(())],
    )
    return pl.pallas_call(
        _combine_kernel,
        grid_spec=grid_spec,
        out_shape=jax.ShapeDtypeStruct((N, D), F32),
        compiler_params=pltpu.CompilerParams(
            dimension_semantics=("arbitrary",), vmem_limit_bytes=VMEM_LIMIT),
        name="combine",
    )(seg_start, route, ys, x1, rg, final_g)


def _lane_vec(values, offset):
    return jnp.zeros((1, LANES), F32).at[0, offset:offset + values.shape[0]].set(values.astype(F32))


def kernel(x, mix_norm_g, w_in, conv_a_w, conv_a_norm_g, dn_conv_w, dn_a_log, dn_dt_bias, dn_norm_g,
           w_out, ffn_norm_g, router_group_w, router_expert_w, w_gate, w_up, w_down, final_norm_g):
    B, S, D = x.shape
    N = B * S
    depth = w_in.shape[0]
    assert depth == 1, "single-layer block: the final RMSNorm is fused into the layer's combine step"
    group_of = jnp.arange(CONV_A_DIM, dtype=I32) // CONV_A_GROUP_DIM
    gmat = jnp.where(group_of[:, None] == group_of[None, :], 1.0 / CONV_A_GROUP_DIM, 0.0).astype(BF16)
    bm = FFN_BLOCK
    n_blocks = (N * TOP_K) // bm + N_EXPERTS
    for l in range(depth):
        w_in_pad = jnp.pad(w_in[l], ((0, 0), (0, IN_PROJ_PAD - IN_PROJ_DIM))).astype(BF16)
        ya, q, k, v, zg, gcol, grow = _inproj(
            x, mix_norm_g[l][None, :], w_in_pad, conv_a_w[l], conv_a_norm_g[l][None, :], dn_conv_w[l],
            _lane_vec(dn_a_log[l], DN_HEADS), _lane_vec(dn_dt_bias[l], DN_HEADS), gmat)
        u, w, qd, kd, at = _delta_prep(q, k, v, gcol, grow)
        yb = _delta_scan(u, w, qd, kd, at, zg, grow, dn_norm_g[l][None, :])
        rw = jnp.pad(jnp.concatenate([router_group_w[l], router_expert_w[l]], axis=1),
                     ((0, 0), (0, LANES - N_GROUPS - N_EXPERTS)))
        x1, h2, ri, rg, cnt = _outproj(ya.reshape(N, CONV_A_DIM), yb.reshape(N, DN_DIM), x.reshape(N, D),
                                       w_out[l].astype(BF16), ffn_norm_g[l][None, :], rw)
        counts = cnt[0, :N_EXPERTS].astype(I32)
        padded = (counts + bm - 1) // bm * bm
        seg_end = jnp.cumsum(padded)
        seg_start = (seg_end - padded).astype(I32)
        n_used = (seg_end[-1] // bm).astype(I32)
        blk = jnp.arange(n_blocks, dtype=I32)
        blk_expert = jnp.sum(seg_end[None, :] <= (jnp.minimum(blk, n_used - 1) * bm)[:, None], axis=1)
        blk_expert = jnp.minimum(blk_expert, N_EXPERTS - 1).astype(I32)
        route = ri[:, :4].reshape(-1)
        seg = jnp.concatenate([seg_start, seg_start + counts, seg_end.astype(I32), n_used.reshape(1)])
        xs = _dispatch(seg, route, h2, n_blocks * bm)
        ys = _ffn(blk_expert, n_used.reshape(1), xs, w_gate[l], w_up[l], w_down[l])
        x = _combine(seg, route, ys, x1, rg, final_norm_g[None, :]).reshape(B, S, D)
    return x
```

```python
import functools

import jax
import jax.numpy as jnp
from jax import lax
from jax.experimental import pallas as pl
from jax.experimental.pallas import tpu as pltpu

F32 = jnp.float32
BF16 = jnp.bfloat16
I32 = jnp.int32

D_MODEL = 1024
CHUNK = 64
CONV_A_GROUP_DIM = 64
CONV_A_DIM = 512
CONV_A_WIDTH = 3
DN_HEADS = 4
DN_HEAD_DIM = 128
DN_DIM = 512
DN_CONV_WIDTH = 4
IN_PROJ_DIM = 3 * CONV_A_DIM + 4 * DN_DIM + 2 * DN_HEADS
N_GROUPS = 4
EXPERTS_PER_GROUP = 8
N_EXPERTS = 32
TOP_K = 2
EXPERT_FF = 512
EPS = 1e-6

LANES = 128
HALO = 8
IN_PROJ_PAD = 29 * LANES
GATE_COL = 3 * CONV_A_DIM + 4 * DN_DIM

TM_IN = 512
TM_PREP = 256
TM_SCAN = 512
TM_OUT = 512
FFN_BLOCK = 256
TM_DISPATCH = 1024
TM_COMBINE = 256
ISSUE_UNROLL = 4
VMEM_LIMIT = 56 * 1024 * 1024


def _dot(a, b):
    return jnp.dot(a, b, preferred_element_type=F32)


def _dot_nt(a, b):
    return lax.dot_general(a, b, (((1,), (1,)), ((), ())), preferred_element_type=F32)


def _dot_tn(a, b):
    return lax.dot_general(a, b, (((0,), (0,)), ((), ())), preferred_element_type=F32)


def _split_bf16(x, parts):
    out = []
    for _ in range(parts):
        p = x.astype(BF16)
        out.append(p)
        x = x - p.astype(F32)
    return out


def _dot_split3(exact16, x):
    return sum(_dot(exact16, p) for p in _split_bf16(x, 3))


def _silu(x):
    return x * jax.nn.sigmoid(x)


ROW_SLAB = D_MODEL // LANES


def _store_rows(ref, val):
    m = val.shape[0]
    for c in range(ROW_SLAB):
        ref[pl.ds(c, m, stride=ROW_SLAB), :] = val[:, c * LANES:(c + 1) * LANES]


def _load_rows(ref, m):
    return jnp.concatenate([ref[pl.ds(c, m, stride=ROW_SLAB), :] for c in range(ROW_SLAB)], axis=-1)


def _slab(ref, row):
    return ref.at[pl.ds(pl.multiple_of(row * ROW_SLAB, ROW_SLAB), ROW_SLAB)]


def _slab_block(ref, block):
    n = FFN_BLOCK * ROW_SLAB
    return ref.at[pl.ds(pl.multiple_of(block * n, n), n)]


def _softplus(x):
    return jnp.maximum(x, 0.0) + jnp.log1p(jnp.exp(-jnp.abs(x)))


def _inproj_kernel(x_ref, g_ref, w_ref, caw_ref, cag_ref, dcw_ref, alog_ref, dtb_ref, gmat_ref,
                   ya_ref, q_ref, k_ref, v_ref, zg_ref, gcol_ref, grow_ref,
                   exta_ref, extq_ref):
    tm = x_ref.shape[1]

    @pl.when(pl.program_id(1) == 0)
    def _():
        exta_ref[0:HALO, :] = jnp.zeros((HALO, CONV_A_DIM), F32)
        extq_ref[0:HALO, :] = jnp.zeros((HALO, 3 * DN_DIM), F32)

    x = x_ref[0]
    ms = jnp.mean(x * x, axis=-1, keepdims=True)
    hb = (x * lax.rsqrt(ms + EPS) * g_ref[...]).astype(BF16)

    def proj(c0, width):
        return _dot(hb, w_ref[:, c0:c0 + width])

    a_h = proj(0, CONV_A_DIM)
    a_c = proj(2 * CONV_A_DIM, CONV_A_DIM)
    exta_ref[HALO:HALO + tm, :] = a_c * a_h
    conv = caw_ref[0:1, :] * exta_ref[pl.ds(HALO - 2, tm), :]
    conv = conv + caw_ref[1:2, :] * exta_ref[pl.ds(HALO - 1, tm), :]
    conv = conv + caw_ref[2:3, :] * exta_ref[pl.ds(HALO, tm), :]
    y = proj(CONV_A_DIM, CONV_A_DIM) * conv
    ysq = y * y
    hi = ysq.astype(BF16)
    lo = (ysq - hi.astype(F32)).astype(BF16)
    gmean = _dot(hi, gmat_ref[...]) + _dot(lo, gmat_ref[...])
    ya_ref[0] = (y * lax.rsqrt(gmean + EPS) * cag_ref[...]).astype(ya_ref.dtype)
    exta_ref[0:HALO, :] = exta_ref[tm:tm + HALO, :]

    base = 3 * CONV_A_DIM
    for i in range(3):
        extq_ref[HALO:HALO + tm, i * DN_DIM:(i + 1) * DN_DIM] = proj(base + i * DN_DIM, DN_DIM)
    for i, out_ref in enumerate((q_ref, k_ref, v_ref)):
        cols = slice(i * DN_DIM, (i + 1) * DN_DIM)
        acc = dcw_ref[0:1, cols] * extq_ref[pl.ds(HALO - 3, tm), cols]
        for j in range(1, DN_CONV_WIDTH):
            acc = acc + dcw_ref[j:j + 1, cols] * extq_ref[pl.ds(HALO - 3 + j, tm), cols]
        s = _silu(acc)
        if i == 2:
            out_ref[0] = s
        else:
            for h in range(DN_HEADS):
                sh = s[:, h * DN_HEAD_DIM:(h + 1) * DN_HEAD_DIM]
                inv = lax.rsqrt(jnp.sum(sh * sh, axis=-1, keepdims=True) + EPS)
                sh = sh * inv
                if i == 0:
                    sh = sh * (DN_HEAD_DIM ** -0.5)
                out_ref[0, :, h * DN_HEAD_DIM:(h + 1) * DN_HEAD_DIM] = sh
    extq_ref[0:HALO, :] = extq_ref[tm:tm + HALO, :]

    zg_ref[0] = _silu(proj(base + 3 * DN_DIM, DN_DIM))

    p = proj(GATE_COL, LANES)
    beta = jax.nn.sigmoid(p)
    g = -jnp.exp(alog_ref[...]) * _softplus(p + dtb_ref[...])
    r = lax.broadcasted_iota(I32, (tm, tm), 0)
    c = lax.broadcasted_iota(I32, (tm, tm), 1)
    tri = jnp.where((r // CHUNK == c // CHUNK) & (c <= r), 1.0, 0.0).astype(BF16)
    gc = _dot_split3(tri, g)
    lane = lax.broadcasted_iota(I32, (tm, LANES), 1)
    slab = jnp.where(lane < DN_HEADS, beta, gc)
    gcol_ref[0] = slab
    rows = slab.T[0:HALO, :]
    for ci in range(tm // CHUNK):
        grow_ref[0, ci] = rows[:, ci * CHUNK:(ci + 1) * CHUNK]


def _inproj(x, mix_g, w_in_pad, conv_a_w, conv_a_g, dn_conv_w, alog_vec, dtb_vec, gmat):
    B, S, D = x.shape
    tm = TM_IN
    full = lambda shape: pl.BlockSpec(shape, lambda b, t: (0,) * len(shape))
    row = lambda width: pl.BlockSpec((1, tm, width), lambda b, t: (b, t, 0))
    out_shape = (
        jax.ShapeDtypeStruct((B, S, CONV_A_DIM), BF16),
        jax.ShapeDtypeStruct((B, S, DN_DIM), F32),
        jax.ShapeDtypeStruct((B, S, DN_DIM), F32),
        jax.ShapeDtypeStruct((B, S, DN_DIM), F32),
        jax.ShapeDtypeStruct((B, S, DN_DIM), F32),
        jax.ShapeDtypeStruct((B, S, LANES), F32),
        jax.ShapeDtypeStruct((B, S // CHUNK, HALO, CHUNK), F32),
    )
    return pl.pallas_call(
        _inproj_kernel,
        grid=(B, S // tm),
        in_specs=[row(D), full((1, D)), full((D, IN_PROJ_PAD)), full((CONV_A_WIDTH, CONV_A_DIM)),
                  full((1, CONV_A_DIM)), full((DN_CONV_WIDTH, 3 * DN_DIM)), full((1, LANES)),
                  full((1, LANES)), full((CONV_A_DIM, CONV_A_DIM))],
        out_specs=(row(CONV_A_DIM), row(DN_DIM), row(DN_DIM), row(DN_DIM), row(DN_DIM), row(LANES),
                   pl.BlockSpec((1, tm // CHUNK, HALO, CHUNK), lambda b, t: (b, t, 0, 0))),
        out_shape=out_shape,
        scratch_shapes=[pltpu.VMEM((tm + HALO, CONV_A_DIM), F32),
                        pltpu.VMEM((tm + HALO, 3 * DN_DIM), F32)],
        compiler_params=pltpu.CompilerParams(
            dimension_semantics=("arbitrary", "arbitrary"), vmem_limit_bytes=VMEM_LIMIT),
        name="inproj",
    )(x, mix_g, w_in_pad, conv_a_w, conv_a_g, dn_conv_w, alog_vec, dtb_vec, gmat)


def _delta_prep_kernel(q_ref, k_ref, v_ref, gcol_ref, grow_ref,
                       u_ref, w_ref, qd_ref, kd_ref, at_ref):
    tm = q_ref.shape[1]
    ri = lax.broadcasted_iota(I32, (CHUNK, CHUNK), 0)
    ci = lax.broadcasted_iota(I32, (CHUNK, CHUNK), 1)
    causal = ci <= ri
    strict = ci < ri
    eye = jnp.where(ci == ri, 1.0, 0.0).astype(F32)
    lane = lax.broadcasted_iota(I32, (CHUNK, LANES), 1)
    chains = [(c, h) for c in range(tm // CHUNK) for h in range(DN_HEADS)]
    ms, ts, rhss = [], [], []
    for c, h in chains:
        rows = slice(c * CHUNK, (c + 1) * CHUNK)
        cols = slice(h * DN_HEAD_DIM, (h + 1) * DN_HEAD_DIM)
        slab = gcol_ref[0, rows, :]
        beta = jnp.sum(jnp.where(lane == h, slab, 0.0), axis=-1, keepdims=True)
        gcc = jnp.sum(jnp.where(lane == h + DN_HEADS, slab, 0.0), axis=-1, keepdims=True)
        gcr = grow_ref[0, c, h + DN_HEADS:h + DN_HEADS + 1, :]
        diff = gcc - gcr
        decay = jnp.where(causal, jnp.exp(jnp.where(causal, diff, 0.0)), 0.0)
        q = q_ref[0, rows, cols]
        k = k_ref[0, rows, cols]
        v = v_ref[0, rows, cols]
        kb = k * beta
        k16 = k.astype(BF16)
        L = jnp.where(strict, _dot_nt(kb.astype(BF16), k16) * decay, 0.0)
        intra = _dot_nt(q.astype(BF16), k16) * decay
        egc = jnp.exp(gcc)
        gl = gcr[:, CHUNK - 1:CHUNK]
        qd_ref[0, rows, cols] = (q * egc).astype(BF16)
        kd_ref[0, rows, cols] = (k * jnp.exp(gl - gcc)).astype(BF16)
        at_ref[0, rows, h * CHUNK:(h + 1) * CHUNK] = intra.astype(BF16)
        rhss.append(jnp.concatenate([v * beta, kb * egc], axis=-1).astype(BF16))
        ms.append(-L)
        ts.append(eye - L)
    for _ in range(5):
        m16s = [m.astype(BF16) for m in ms]
        ms = [_dot(m16, m16) for m16 in m16s]
        ts = [t + _dot(t.astype(BF16), m.astype(BF16)) for t, m in zip(ts, ms)]
    for (c, h), t, rhs in zip(chains, ts, rhss):
        rows = slice(c * CHUNK, (c + 1) * CHUNK)
        cols = slice(h * DN_HEAD_DIM, (h + 1) * DN_HEAD_DIM)
        uw = _dot(t.astype(BF16), rhs)
        u_ref[0, rows, cols] = uw[:, :DN_HEAD_DIM]
        w_ref[0, rows, cols] = uw[:, DN_HEAD_DIM:].astype(BF16)


def _delta_prep(q, k, v, gcol, grow):
    B, S, _ = q.shape
    tm = TM_PREP
    row = lambda width: pl.BlockSpec((1, tm, width), lambda b, t: (b, t, 0))
    return pl.pallas_call(
        _delta_prep_kernel,
        grid=(B, S // tm),
        in_specs=[row(DN_DIM), row(DN_DIM), row(DN_DIM), row(LANES),
                  pl.BlockSpec((1, tm // CHUNK, HALO, CHUNK), lambda b, t: (b, t, 0, 0))],
        out_specs=(row(DN_DIM), row(DN_DIM), row(DN_DIM), row(DN_DIM), row(DN_HEADS * CHUNK)),
        out_shape=(jax.ShapeDtypeStruct((B, S, DN_DIM), F32),
                   jax.ShapeDtypeStruct((B, S, DN_DIM), BF16),
                   jax.ShapeDtypeStruct((B, S, DN_DIM), BF16),
                   jax.ShapeDtypeStruct((B, S, DN_DIM), BF16),
                   jax.ShapeDtypeStruct((B, S, DN_HEADS * CHUNK), BF16)),
        compiler_params=pltpu.CompilerParams(
            dimension_semantics=("arbitrary", "arbitrary"), vmem_limit_bytes=VMEM_LIMIT),
        name="delta_prep",
    )(q, k, v, gcol, grow)


def _delta_scan_kernel(u_ref, w_ref, qd_ref, kd_ref, at_ref, zg_ref, grow_ref, ng_ref, o_ref, st_ref):
    tm = u_ref.shape[1]

    @pl.when(pl.program_id(1) == 0)
    def _():
        st_ref[...] = jnp.zeros(st_ref.shape, F32)

    def step(c, carry):
        r0 = pl.multiple_of(c * CHUNK, CHUNK)
        rows = pl.ds(r0, CHUNK)
        heads = range(DN_HEADS)
        cols = [slice(h * DN_HEAD_DIM, (h + 1) * DN_HEAD_DIM) for h in heads]
        sts = [st_ref[h] for h in heads]
        st16s = [st.astype(BF16) for st in sts]
        wss = [_dot(w_ref[0, rows, cols[h]], st16s[h]) for h in heads]
        qss = [_dot(qd_ref[0, rows, cols[h]], st16s[h]) for h in heads]
        vn16s = [(u_ref[0, rows, cols[h]] - wss[h]).astype(BF16) for h in heads]
        upd = [_dot_tn(kd_ref[0, rows, cols[h]], vn16s[h]) for h in heads]
        avs = [_dot(at_ref[0, rows, h * CHUNK:(h + 1) * CHUNK], vn16s[h]) for h in heads]
        for h in heads:
            gl = grow_ref[0, c, h + DN_HEADS:h + DN_HEADS + 1, CHUNK - 1:CHUNK]
            st_ref[h] = sts[h] * jnp.exp(gl) + upd[h]
        for h in heads:
            o = qss[h] + avs[h]
            on = o * lax.rsqrt(jnp.mean(o * o, axis=-1, keepdims=True) + EPS) * ng_ref[...]
            o_ref[0, rows, cols[h]] = (on * zg_ref[0, rows, cols[h]]).astype(o_ref.dtype)
        return carry

    lax.fori_loop(0, tm // CHUNK, step, 0)


def _delta_scan(u, w, qd, kd, at, zg, grow, norm_g):
    B, S, _ = u.shape
    tm = TM_SCAN
    row = lambda width: pl.BlockSpec((1, tm, width), lambda b, t: (b, t, 0))
    return pl.pallas_call(
        _delta_scan_kernel,
        grid=(B, S // tm),
        in_specs=[row(DN_DIM), row(DN_DIM), row(DN_DIM), row(DN_DIM), row(DN_HEADS * CHUNK), row(DN_DIM),
                  pl.BlockSpec((1, tm // CHUNK, HALO, CHUNK), lambda b, t: (b, t, 0, 0)),
                  pl.BlockSpec((1, DN_HEAD_DIM), lambda b, t: (0, 0))],
        out_specs=row(DN_DIM),
        out_shape=jax.ShapeDtypeStruct((B, S, DN_DIM), BF16),
        scratch_shapes=[pltpu.VMEM((DN_HEADS, DN_HEAD_DIM, DN_HEAD_DIM), F32)],
        compiler_params=pltpu.CompilerParams(
            dimension_semantics=("arbitrary", "arbitrary"), vmem_limit_bytes=VMEM_LIMIT),
        name="delta_scan",
    )(u, w, qd, kd, at, zg, grow, norm_g)


def _outproj_kernel(ya_ref, yb_ref, x_ref, wo_ref, g_ref, rw_ref,
                    x1_ref, h2_ref, ri_ref, rg_ref, cnt_ref, base_ref):
    tm = x_ref.shape[0]

    @pl.when(pl.program_id(0) == 0)
    def _():
        base_ref[...] = jnp.zeros(base_ref.shape, F32)

    y = _dot(jnp.concatenate([ya_ref[...], yb_ref[...]], axis=-1), wo_ref[...])
    x1 = x_ref[...] + y
    x1_ref[...] = x1
    h = x1 * lax.rsqrt(jnp.mean(x1 * x1, axis=-1, keepdims=True) + EPS) * g_ref[...]
    _store_rows(h2_ref, h)

    h_hi, h_lo = _split_bf16(h, 2)
    hi_prod = _dot(h_hi, rw_ref[...])
    logits = hi_prod[:, :LANES] + (hi_prod[:, LANES:] + _dot(h_lo, rw_ref[:, :LANES]))
    lane = lax.broadcasted_iota(I32, (tm, LANES), 1)
    lanef = lane.astype(F32)
    neg = jnp.float32(-1e30)
    big = jnp.float32(1e9)
    is_g = lane < N_GROUPS
    gl = jnp.where(is_g, logits, neg)
    gmax = jnp.max(gl, axis=-1, keepdims=True)
    gidx = jnp.min(jnp.where(gl == gmax, lanef, big), axis=-1, keepdims=True)
    gsum = jnp.sum(jnp.where(is_g, jnp.exp(gl - gmax), 0.0), axis=-1, keepdims=True)
    gprob = 1.0 / gsum
    lo = N_GROUPS + EXPERTS_PER_GROUP * gidx
    emask = (lanef >= lo) & (lanef < lo + EXPERTS_PER_GROUP)
    el = jnp.where(emask, logits, neg)
    e1 = jnp.max(el, axis=-1, keepdims=True)
    i1 = jnp.min(jnp.where(el == e1, lanef, big), axis=-1, keepdims=True)
    el2 = jnp.where(lanef == i1, neg, el)
    e2 = jnp.max(el2, axis=-1, keepdims=True)
    i2 = jnp.min(jnp.where(el2 == e2, lanef, big), axis=-1, keepdims=True)
    r = jnp.exp(e2 - e1)
    gate1 = gprob / (1.0 + r)
    gate2 = gprob * r / (1.0 + r)
    id1 = i1 - N_GROUPS
    id2 = i2 - N_GROUPS

    oh1 = jnp.where(lanef == id1, 1.0, 0.0).astype(F32)
    oh2 = jnp.where(lanef == id2, 1.0, 0.0).astype(F32)
    oh = oh1 + oh2
    rr = lax.broadcasted_iota(I32, (tm, tm), 0)
    cc = lax.broadcasted_iota(I32, (tm, tm), 1)
    tri = jnp.where(cc < rr, 1.0, 0.0).astype(BF16)
    before = _dot(tri, oh.astype(BF16)) + base_ref[...]
    rank1 = jnp.sum(oh1 * before, axis=-1, keepdims=True)
    rank2 = jnp.sum(oh2 * before, axis=-1, keepdims=True)
    base_ref[...] = base_ref[...] + jnp.sum(oh, axis=0, keepdims=True)
    cnt_ref[...] = base_ref[...]

    ri = jnp.where(lane == 0, id1, jnp.where(lane == 1, id2,
                   jnp.where(lane == 2, rank1, jnp.where(lane == 3, rank2, 0.0))))
    ri_ref[...] = ri.astype(I32)
    rg_ref[...] = jnp.where(lane == 0, gate1, jnp.where(lane == 1, gate2, 0.0))


def _outproj(ya, yb, x, w_out16, ffn_g, rw):
    N, D = x.shape
    tm = TM_OUT
    row = lambda width: pl.BlockSpec((tm, width), lambda i: (i, 0))
    full = lambda shape: pl.BlockSpec(shape, lambda i: (0,) * len(shape))
    return pl.pallas_call(
        _outproj_kernel,
        grid=(N // tm,),
        in_specs=[row(CONV_A_DIM), row(DN_DIM), row(D), full((D, D)), full((1, D)), full((D, 2 * LANES))],
        out_specs=(row(D), pl.BlockSpec((tm * ROW_SLAB, LANES), lambda i: (i, 0)), row(LANES), row(LANES),
                   full((1, LANES))),
        out_shape=(jax.ShapeDtypeStruct((N, D), F32),
                   jax.ShapeDtypeStruct((N * ROW_SLAB, LANES), F32),
                   jax.ShapeDtypeStruct((N, LANES), I32),
                   jax.ShapeDtypeStruct((N, LANES), F32),
                   jax.ShapeDtypeStruct((1, LANES), F32)),
        scratch_shapes=[pltpu.VMEM((1, LANES), F32)],
        compiler_params=pltpu.CompilerParams(
            dimension_semantics=("arbitrary",), vmem_limit_bytes=VMEM_LIMIT),
        name="outproj",
    )(ya, yb, x, w_out16, ffn_g, rw)


def _slot_of(route_ref, start_ref, t, k):
    return start_ref[route_ref[4 * t + k]] + route_ref[4 * t + 2 + k]


def _dispatch_kernel(seg_ref, route_ref, h_ref, xs_ref, zero_ref, sem, zsem):
    tm = TM_DISPATCH
    bm = FFN_BLOCK
    n_blocks = xs_ref.shape[0] // (bm * ROW_SLAB)
    start_ref = seg_ref

    @pl.when(pl.program_id(0) == 0)
    def _():
        zero_ref[...] = jnp.zeros(zero_ref.shape, F32)

        def pad_copy(d):
            return pltpu.make_async_copy(zero_ref.at[pl.ds(0, ROW_SLAB)], _slab(xs_ref, d), zsem)

        def block_copy(b):
            return pltpu.make_async_copy(zero_ref, _slab_block(xs_ref, b), zsem)

        def each_pad(fn):
            def per_expert(e, carry):
                def per_row(d, c):
                    fn(pad_copy(d))
                    return c
                return lax.fori_loop(seg_ref[N_EXPERTS + e], seg_ref[2 * N_EXPERTS + e], per_row, carry)
            lax.fori_loop(0, N_EXPERTS, per_expert, 0)

            def per_block(b, c):
                fn(block_copy(b))
                return c
            lax.fori_loop(seg_ref[3 * N_EXPERTS], n_blocks, per_block, 0)

        each_pad(lambda cp: cp.start())
        each_pad(lambda cp: cp.wait())

    def issue(t, carry):
        for k in range(TOP_K):
            d = _slot_of(route_ref, start_ref, t, k)
            pltpu.make_async_copy(_slab(h_ref, t), _slab(xs_ref, d), sem).start()
        return carry

    lax.fori_loop(0, tm, issue, 0, unroll=ISSUE_UNROLL)
    for _ in range(TOP_K):
        pltpu.make_async_copy(h_ref, xs_ref.at[pl.ds(0, tm * ROW_SLAB)], sem).wait()


def _dispatch(seg, route, h2, n_slots):
    N = h2.shape[0] // ROW_SLAB
    tm = TM_DISPATCH
    grid_spec = pltpu.PrefetchScalarGridSpec(
        num_scalar_prefetch=1,
        grid=(N // tm,),
        in_specs=[pl.BlockSpec((4 * tm,), lambda i, s: (i,), memory_space=pltpu.SMEM),
                  pl.BlockSpec((tm * ROW_SLAB, LANES), lambda i, s: (i, 0))],
        out_specs=pl.BlockSpec(memory_space=pl.ANY),
        scratch_shapes=[pltpu.VMEM((FFN_BLOCK * ROW_SLAB, LANES), F32),
                        pltpu.SemaphoreType.DMA(()), pltpu.SemaphoreType.DMA(())],
    )
    return pl.pallas_call(
        _dispatch_kernel,
        grid_spec=grid_spec,
        out_shape=jax.ShapeDtypeStruct((n_slots * ROW_SLAB, LANES), F32),
        compiler_params=pltpu.CompilerParams(dimension_semantics=("arbitrary",)),
        name="dispatch",
    )(seg, route, h2)


def _ffn_kernel(be_ref, nb_ref, x_ref, wg_ref, wu_ref, wd_ref, y_ref, wg16, wu16, wd16):
    i = pl.program_id(0)
    bm = FFN_BLOCK
    prev = be_ref[jnp.maximum(i - 1, 0)]

    @pl.when((i == 0) | (be_ref[i] != prev))
    def _():
        wg16[...] = wg_ref[0].astype(BF16)
        wu16[...] = wu_ref[0].astype(BF16)
        wd16[...] = wd_ref[0].astype(BF16)

    @pl.when(i < nb_ref[0])
    def _():
        x = _load_rows(x_ref, bm).astype(BF16)
        a = _dot(x, wg16[...])
        b = _dot(x, wu16[...])
        _store_rows(y_ref, _dot((_silu(a) * b).astype(BF16), wd16[...]))

    @pl.when(i >= nb_ref[0])
    def _():
        y_ref[...] = jnp.zeros(y_ref.shape, F32)


def _ffn(block_expert, n_used, xs, w_gate, w_up, w_down):
    bm = FFN_BLOCK
    n_blocks = xs.shape[0] // (bm * ROW_SLAB)
    D = D_MODEL
    last = lambda nb: jnp.maximum(nb[0] - 1, 0)
    grid_spec = pltpu.PrefetchScalarGridSpec(
        num_scalar_prefetch=2,
        grid=(n_blocks,),
        in_specs=[pl.BlockSpec((bm * ROW_SLAB, LANES), lambda i, be, nb: (jnp.minimum(i, last(nb)), 0)),
                  pl.BlockSpec((1, D, EXPERT_FF), lambda i, be, nb: (be[i], 0, 0)),
                  pl.BlockSpec((1, D, EXPERT_FF), lambda i, be, nb: (be[i], 0, 0)),
                  pl.BlockSpec((1, EXPERT_FF, D), lambda i, be, nb: (be[i], 0, 0))],
        out_specs=pl.BlockSpec((bm * ROW_SLAB, LANES), lambda i, be, nb: (i, 0)),
        scratch_shapes=[pltpu.VMEM((D, EXPERT_FF), BF16), pltpu.VMEM((D, EXPERT_FF), BF16),
                        pltpu.VMEM((EXPERT_FF, D), BF16)],
    )

    return pl.pallas_call(
        _ffn_kernel,
        grid_spec=grid_spec,
        out_shape=jax.ShapeDtypeStruct(xs.shape, F32),
        compiler_params=pltpu.CompilerParams(
            dimension_semantics=("arbitrary",), vmem_limit_bytes=VMEM_LIMIT),
        name="ffn",
    )(block_expert, n_used, xs, w_gate, w_up, w_down)


def _combine_kernel(start_ref, route_ref, route_next_ref, ys_ref, x1_ref, rg_ref, g_ref, o_ref,
                    buf_ref, sems):
    tm = TM_COMBINE
    i = pl.program_id(0)
    half = i % 2

    def gather(r_ref, s):
        def issue(t, carry):
            for k in range(TOP_K):
                d = _slot_of(r_ref, start_ref, t, k)
                pltpu.make_async_copy(_slab(ys_ref, d), _slab(buf_ref.at[s, k], t), sems.at[s]).start()
            return carry
        lax.fori_loop(0, tm, issue, 0, unroll=ISSUE_UNROLL)

    @pl.when(i == 0)
    def _():
        gather(route_ref, 0)

    @pl.when(i + 1 < pl.num_programs(0))
    def _():
        gather(route_next_ref, 1 - half)

    for k in range(TOP_K):
        pltpu.make_async_copy(ys_ref.at[pl.ds(0, tm * ROW_SLAB)], buf_ref.at[half, k], sems.at[half]).wait()

    rg = rg_ref[...]
    moe = (_load_rows(buf_ref.at[half, 0], tm) * rg[:, 0:1]
           + _load_rows(buf_ref.at[half, 1], tm) * rg[:, 1:2])
    x2 = x1_ref[...] + moe
    o_ref[...] = x2 * lax.rsqrt(jnp.mean(x2 * x2, axis=-1, keepdims=True) + EPS) * g_ref[...]


def _combine(seg_start, route, ys, x1, rg, final_g):
    N, D = x1.shape
    tm = TM_COMBINE
    n_tiles = N // tm
    grid_spec = pltpu.PrefetchScalarGridSpec(
        num_scalar_prefetch=1,
        grid=(n_tiles,),
        in_specs=[pl.BlockSpec((4 * tm,), lambda i, s: (i,), memory_space=pltpu.SMEM),
                  pl.BlockSpec((4 * tm,), lambda i, s: (jnp.minimum(i + 1, n_tiles - 1),),
                               memory_space=pltpu.SMEM),
                  pl.BlockSpec(memory_space=pl.ANY),
                  pl.BlockSpec((tm, D), lambda i, s: (i, 0)),
                  pl.BlockSpec((tm, LANES), lambda i, s: (i, 0)),
                  pl.BlockSpec((1, D), lambda i, s: (0, 0))],
        out_specs=pl.BlockSpec((tm, D), lambda i, s: (i, 0)),
        scratch_shapes=[pltpu.VMEM((2, TOP_K, tm * ROW_SLAB, LANES), F32), pltpu.SemaphoreType.DMA((2,))],
    )
    return pl.pallas_call(
        _combine_kernel,
        grid_spec=grid_spec,
        out_shape=jax.ShapeDtypeStruct((N, D), F32),
        compiler_params=pltpu.CompilerParams(
            dimension_semantics=("arbitrary",), vmem_limit_bytes=VMEM_LIMIT),
        name="combine",
    )(seg_start, route, route, ys, x1, rg, final_g)


def _lane_vec(values, offset):
    return jnp.zeros((1, LANES), F32).at[0, offset:offset + values.shape[0]].set(values.astype(F32))


def kernel(x, mix_norm_g, w_in, conv_a_w, conv_a_norm_g, dn_conv_w, dn_a_log, dn_dt_bias, dn_norm_g,
           w_out, ffn_norm_g, router_group_w, router_expert_w, w_gate, w_up, w_down, final_norm_g):
    B, S, D = x.shape
    N = B * S
    depth = w_in.shape[0]
    assert depth == 1, "single-layer block: the final RMSNorm is fused into the layer's combine step"
    group_of = jnp.arange(CONV_A_DIM, dtype=I32) // CONV_A_GROUP_DIM
    gmat = jnp.where(group_of[:, None] == group_of[None, :], 1.0 / CONV_A_GROUP_DIM, 0.0).astype(BF16)
    bm = FFN_BLOCK
    n_blocks = (N * TOP_K) // bm + N_EXPERTS
    for l in range(depth):
        w_in_pad = jnp.pad(w_in[l], ((0, 0), (0, IN_PROJ_PAD - IN_PROJ_DIM))).astype(BF16)
        ya, q, k, v, zg, gcol, grow = _inproj(
            x, mix_norm_g[l][None, :], w_in_pad, conv_a_w[l], conv_a_norm_g[l][None, :], dn_conv_w[l],
            _lane_vec(dn_a_log[l], DN_HEADS), _lane_vec(dn_dt_bias[l], DN_HEADS), gmat)
        u, w, qd, kd, at = _delta_prep(q, k, v, gcol, grow)
        yb = _delta_scan(u, w, qd, kd, at, zg, grow, dn_norm_g[l][None, :])
        rw = jnp.pad(jnp.concatenate([router_group_w[l], router_expert_w[l]], axis=1),
                     ((0, 0), (0, LANES - N_GROUPS - N_EXPERTS)))
        rw_hi = rw.astype(BF16)
        rw = jnp.concatenate([rw_hi, (rw - rw_hi.astype(F32)).astype(BF16)], axis=1)
        x1, h2, ri, rg, cnt = _outproj(ya.reshape(N, CONV_A_DIM), yb.reshape(N, DN_DIM), x.reshape(N, D),
                                       w_out[l].astype(BF16), ffn_norm_g[l][None, :], rw)
        counts = cnt[0, :N_EXPERTS].astype(I32)
        padded = (counts + bm - 1) // bm * bm
        seg_end = jnp.cumsum(padded)
        seg_start = (seg_end - padded).astype(I32)
        n_used = (seg_end[-1] // bm).astype(I32)
        blk = jnp.arange(n_blocks, dtype=I32)
        blk_expert = jnp.sum(seg_end[None, :] <= (jnp.minimum(blk, n_used - 1) * bm)[:, None], axis=1)
        blk_expert = jnp.minimum(blk_expert, N_EXPERTS - 1).astype(I32)
        route = ri[:, :4].reshape(-1)
        seg = jnp.concatenate([seg_start, seg_start + counts, seg_end.astype(I32), n_used.reshape(1)])
        xs = _dispatch(seg, route, h2, n_blocks * bm)
        ys = _ffn(blk_expert, n_used.reshape(1), xs, w_gate[l], w_up[l], w_down[l])
        x = _combine(seg, route, ys, x1, rg, final_norm_g[None, :]).reshape(B, S, D)
    return x
```

```python
import functools

import jax
import jax.numpy as jnp
from jax import lax
from jax.experimental import pallas as pl
from jax.experimental.pallas import tpu as pltpu

F32 = jnp.float32
BF16 = jnp.bfloat16
I32 = jnp.int32

D_MODEL = 1024
CHUNK = 64
CONV_A_GROUP_DIM = 64
CONV_A_DIM = 512
CONV_A_WIDTH = 3
DN_HEADS = 4
DN_HEAD_DIM = 128
DN_DIM = 512
DN_CONV_WIDTH = 4
IN_PROJ_DIM = 3 * CONV_A_DIM + 4 * DN_DIM + 2 * DN_HEADS
N_GROUPS = 4
EXPERTS_PER_GROUP = 8
N_EXPERTS = 32
TOP_K = 2
EXPERT_FF = 512
EPS = 1e-6

LANES = 128
HALO = 8
IN_PROJ_PAD = 29 * LANES
GATE_COL = 3 * CONV_A_DIM + 4 * DN_DIM

TM_IN = 512
TM_PREP = 256
TM_SCAN = 512
TM_OUT = 512
FFN_BLOCK = 256
TM_DISPATCH = 1024
TM_COMBINE = 256
ISSUE_UNROLL = 4
VMEM_LIMIT = 56 * 1024 * 1024


def _dot(a, b):
    return jnp.dot(a, b, preferred_element_type=F32)


def _dot_nt(a, b):
    return lax.dot_general(a, b, (((1,), (1,)), ((), ())), preferred_element_type=F32)


def _dot_tn(a, b):
    return lax.dot_general(a, b, (((0,), (0,)), ((), ())), preferred_element_type=F32)


def _split_bf16(x, parts):
    out = []
    for _ in range(parts):
        p = x.astype(BF16)
        out.append(p)
        x = x - p.astype(F32)
    return out


def _dot_split3(exact16, x):
    return sum(_dot(exact16, p) for p in _split_bf16(x, 3))


def _silu(x):
    return x * jax.nn.sigmoid(x)


ROW_SLAB = D_MODEL // LANES


def _store_rows(ref, val):
    m = val.shape[0]
    for c in range(ROW_SLAB):
        ref[pl.ds(c, m, stride=ROW_SLAB), :] = val[:, c * LANES:(c + 1) * LANES]


def _load_rows(ref, m):
    return jnp.concatenate([ref[pl.ds(c, m, stride=ROW_SLAB), :] for c in range(ROW_SLAB)], axis=-1)


def _slab(ref, row):
    return ref.at[pl.ds(pl.multiple_of(row * ROW_SLAB, ROW_SLAB), ROW_SLAB)]


def _slab_block(ref, block):
    n = FFN_BLOCK * ROW_SLAB
    return ref.at[pl.ds(pl.multiple_of(block * n, n), n)]


def _softplus(x):
    return jnp.maximum(x, 0.0) + jnp.log1p(jnp.exp(-jnp.abs(x)))


def _inproj_kernel(x_ref, g_ref, w_ref, caw_ref, cag_ref, dcw_ref, alog_ref, dtb_ref, gmat_ref,
                   ya_ref, q_ref, k_ref, v_ref, zg_ref, gcol_ref, grow_ref,
                   exta_ref, extq_ref):
    tm = x_ref.shape[1]

    @pl.when(pl.program_id(1) == 0)
    def _():
        exta_ref[0:HALO, :] = jnp.zeros((HALO, CONV_A_DIM), F32)
        extq_ref[0:HALO, :] = jnp.zeros((HALO, 3 * DN_DIM), F32)

    x = x_ref[0]
    ms = jnp.mean(x * x, axis=-1, keepdims=True)
    hb = (x * lax.rsqrt(ms + EPS) * g_ref[...]).astype(BF16)

    def proj(c0, width):
        return _dot(hb, w_ref[:, c0:c0 + width])

    a_h = proj(0, CONV_A_DIM)
    a_c = proj(2 * CONV_A_DIM, CONV_A_DIM)
    exta_ref[HALO:HALO + tm, :] = a_c * a_h
    conv = caw_ref[0:1, :] * exta_ref[pl.ds(HALO - 2, tm), :]
    conv = conv + caw_ref[1:2, :] * exta_ref[pl.ds(HALO - 1, tm), :]
    conv = conv + caw_ref[2:3, :] * exta_ref[pl.ds(HALO, tm), :]
    y = proj(CONV_A_DIM, CONV_A_DIM) * conv
    ysq = y * y
    hi = ysq.astype(BF16)
    lo = (ysq - hi.astype(F32)).astype(BF16)
    gmean = _dot(hi, gmat_ref[...]) + _dot(lo, gmat_ref[...])
    ya_ref[0] = (y * lax.rsqrt(gmean + EPS) * cag_ref[...]).astype(ya_ref.dtype)
    exta_ref[0:HALO, :] = exta_ref[tm:tm + HALO, :]

    base = 3 * CONV_A_DIM
    for i in range(3):
        extq_ref[HALO:HALO + tm, i * DN_DIM:(i + 1) * DN_DIM] = proj(base + i * DN_DIM, DN_DIM)
    for i, out_ref in enumerate((q_ref, k_ref, v_ref)):
        cols = slice(i * DN_DIM, (i + 1) * DN_DIM)
        acc = dcw_ref[0:1, cols] * extq_ref[pl.ds(HALO - 3, tm), cols]
        for j in range(1, DN_CONV_WIDTH):
            acc = acc + dcw_ref[j:j + 1, cols] * extq_ref[pl.ds(HALO - 3 + j, tm), cols]
        s = _silu(acc)
        if i == 2:
            out_ref[0] = s
        else:
            for h in range(DN_HEADS):
                sh = s[:, h * DN_HEAD_DIM:(h + 1) * DN_HEAD_DIM]
                inv = lax.rsqrt(jnp.sum(sh * sh, axis=-1, keepdims=True) + EPS)
                sh = sh * inv
                if i == 0:
                    sh = sh * (DN_HEAD_DIM ** -0.5)
                out_ref[0, :, h * DN_HEAD_DIM:(h + 1) * DN_HEAD_DIM] = sh
    extq_ref[0:HALO, :] = extq_ref[tm:tm + HALO, :]

    zg_ref[0] = _silu(proj(base + 3 * DN_DIM, DN_DIM))

    p = proj(GATE_COL, LANES)
    beta = jax.nn.sigmoid(p)
    g = -jnp.exp(alog_ref[...]) * _softplus(p + dtb_ref[...])
    r = lax.broadcasted_iota(I32, (tm, tm), 0)
    c = lax.broadcasted_iota(I32, (tm, tm), 1)
    tri = jnp.where((r // CHUNK == c // CHUNK) & (c <= r), 1.0, 0.0).astype(BF16)
    gc = _dot_split3(tri, g)
    lane = lax.broadcasted_iota(I32, (tm, LANES), 1)
    slab = jnp.where(lane < DN_HEADS, beta, gc)
    gcol_ref[0] = slab
    rows = slab.T[0:HALO, :]
    for ci in range(tm // CHUNK):
        grow_ref[0, ci] = rows[:, ci * CHUNK:(ci + 1) * CHUNK]


def _inproj(x, mix_g, w_in_pad, conv_a_w, conv_a_g, dn_conv_w, alog_vec, dtb_vec, gmat):
    B, S, D = x.shape
    tm = TM_IN
    full = lambda shape: pl.BlockSpec(shape, lambda b, t: (0,) * len(shape))
    row = lambda width: pl.BlockSpec((1, tm, width), lambda b, t: (b, t, 0))
    out_shape = (
        jax.ShapeDtypeStruct((B, S, CONV_A_DIM), BF16),
        jax.ShapeDtypeStruct((B, S, DN_DIM), F32),
        jax.ShapeDtypeStruct((B, S, DN_DIM), F32),
        jax.ShapeDtypeStruct((B, S, DN_DIM), F32),
        jax.ShapeDtypeStruct((B, S, DN_DIM), F32),
        jax.ShapeDtypeStruct((B, S, LANES), F32),
        jax.ShapeDtypeStruct((B, S // CHUNK, HALO, CHUNK), F32),
    )
    return pl.pallas_call(
        _inproj_kernel,
        grid=(B, S // tm),
        in_specs=[row(D), full((1, D)), full((D, IN_PROJ_PAD)), full((CONV_A_WIDTH, CONV_A_DIM)),
                  full((1, CONV_A_DIM)), full((DN_CONV_WIDTH, 3 * DN_DIM)), full((1, LANES)),
                  full((1, LANES)), full((CONV_A_DIM, CONV_A_DIM))],
        out_specs=(row(CONV_A_DIM), row(DN_DIM), row(DN_DIM), row(DN_DIM), row(DN_DIM), row(LANES),
                   pl.BlockSpec((1, tm // CHUNK, HALO, CHUNK), lambda b, t: (b, t, 0, 0))),
        out_shape=out_shape,
        scratch_shapes=[pltpu.VMEM((tm + HALO, CONV_A_DIM), F32),
                        pltpu.VMEM((tm + HALO, 3 * DN_DIM), F32)],
        compiler_params=pltpu.CompilerParams(
            dimension_semantics=("arbitrary", "arbitrary"), vmem_limit_bytes=VMEM_LIMIT),
        name="inproj",
    )(x, mix_g, w_in_pad, conv_a_w, conv_a_g, dn_conv_w, alog_vec, dtb_vec, gmat)


def _delta_prep_kernel(q_ref, k_ref, v_ref, gcol_ref, grow_ref,
                       u_ref, w_ref, qd_ref, kd_ref, at_ref):
    tm = q_ref.shape[1]
    ri = lax.broadcasted_iota(I32, (CHUNK, CHUNK), 0)
    ci = lax.broadcasted_iota(I32, (CHUNK, CHUNK), 1)
    causal = ci <= ri
    strict = ci < ri
    eye = jnp.where(ci == ri, 1.0, 0.0).astype(F32)
    lane = lax.broadcasted_iota(I32, (CHUNK, LANES), 1)
    chains = [(c, h) for c in range(tm // CHUNK) for h in range(DN_HEADS)]
    ms, ts, rhss = [], [], []
    for c, h in chains:
        rows = slice(c * CHUNK, (c + 1) * CHUNK)
        cols = slice(h * DN_HEAD_DIM, (h + 1) * DN_HEAD_DIM)
        slab = gcol_ref[0, rows, :]
        beta = jnp.sum(jnp.where(lane == h, slab, 0.0), axis=-1, keepdims=True)
        gcc = jnp.sum(jnp.where(lane == h + DN_HEADS, slab, 0.0), axis=-1, keepdims=True)
        gcr = grow_ref[0, c, h + DN_HEADS:h + DN_HEADS + 1, :]
        diff = gcc - gcr
        decay = jnp.where(causal, jnp.exp(jnp.where(causal, diff, 0.0)), 0.0)
        q = q_ref[0, rows, cols]
        k = k_ref[0, rows, cols]
        v = v_ref[0, rows, cols]
        kb = k * beta
        k16 = k.astype(BF16)
        L = jnp.where(strict, _dot_nt(kb.astype(BF16), k16) * decay, 0.0)
        intra = _dot_nt(q.astype(BF16), k16) * decay
        egc = jnp.exp(gcc)
        gl = gcr[:, CHUNK - 1:CHUNK]
        qd_ref[0, rows, cols] = (q * egc).astype(BF16)
        kd_ref[0, rows, cols] = (k * jnp.exp(gl - gcc)).astype(BF16)
        at_ref[0, rows, h * CHUNK:(h + 1) * CHUNK] = intra.astype(BF16)
        rhss.append(jnp.concatenate([v * beta, kb * egc], axis=-1).astype(BF16))
        ms.append(-L)
        ts.append(eye - L)
    for _ in range(5):
        m16s = [m.astype(BF16) for m in ms]
        ms = [_dot(m16, m16) for m16 in m16s]
        ts = [t + _dot(t.astype(BF16), m.astype(BF16)) for t, m in zip(ts, ms)]
    for (c, h), t, rhs in zip(chains, ts, rhss):
        rows = slice(c * CHUNK, (c + 1) * CHUNK)
        cols = slice(h * DN_HEAD_DIM, (h + 1) * DN_HEAD_DIM)
        uw = _dot(t.astype(BF16), rhs)
        u_ref[0, rows, cols] = uw[:, :DN_HEAD_DIM]
        w_ref[0, rows, cols] = uw[:, DN_HEAD_DIM:].astype(BF16)


def _delta_prep(q, k, v, gcol, grow):
    B, S, _ = q.shape
    tm = TM_PREP
    row = lambda width: pl.BlockSpec((1, tm, width), lambda b, t: (b, t, 0))
    return pl.pallas_call(
        _delta_prep_kernel,
        grid=(B, S // tm),
        in_specs=[row(DN_DIM), row(DN_DIM), row(DN_DIM), row(LANES),
                  pl.BlockSpec((1, tm // CHUNK, HALO, CHUNK), lambda b, t: (b, t, 0, 0))],
        out_specs=(row(DN_DIM), row(DN_DIM), row(DN_DIM), row(DN_DIM), row(DN_HEADS * CHUNK)),
        out_shape=(jax.ShapeDtypeStruct((B, S, DN_DIM), F32),
                   jax.ShapeDtypeStruct((B, S, DN_DIM), BF16),
                   jax.ShapeDtypeStruct((B, S, DN_DIM), BF16),
                   jax.ShapeDtypeStruct((B, S, DN_DIM), BF16),
                   jax.ShapeDtypeStruct((B, S, DN_HEADS * CHUNK), BF16)),
        compiler_params=pltpu.CompilerParams(
            dimension_semantics=("arbitrary", "arbitrary"), vmem_limit_bytes=VMEM_LIMIT),
        name="delta_prep",
    )(q, k, v, gcol, grow)


def _delta_scan_kernel(u_ref, w_ref, qd_ref, kd_ref, at_ref, zg_ref, grow_ref, ng_ref, o_ref, st_ref):
    tm = u_ref.shape[1]

    @pl.when(pl.program_id(1) == 0)
    def _():
        st_ref[...] = jnp.zeros(st_ref.shape, F32)

    def step(c, carry):
        r0 = pl.multiple_of(c * CHUNK, CHUNK)
        rows = pl.ds(r0, CHUNK)
        heads = range(DN_HEADS)
        cols = [slice(h * DN_HEAD_DIM, (h + 1) * DN_HEAD_DIM) for h in heads]
        sts = [st_ref[h] for h in heads]
        st16s = [st.astype(BF16) for st in sts]
        wss = [_dot(w_ref[0, rows, cols[h]], st16s[h]) for h in heads]
        qss = [_dot(qd_ref[0, rows, cols[h]], st16s[h]) for h in heads]
        vn16s = [(u_ref[0, rows, cols[h]] - wss[h]).astype(BF16) for h in heads]
        upd = [_dot_tn(kd_ref[0, rows, cols[h]], vn16s[h]) for h in heads]
        avs = [_dot(at_ref[0, rows, h * CHUNK:(h + 1) * CHUNK], vn16s[h]) for h in heads]
        for h in heads:
            gl = grow_ref[0, c, h + DN_HEADS:h + DN_HEADS + 1, CHUNK - 1:CHUNK]
            st_ref[h] = sts[h] * jnp.exp(gl) + upd[h]
        for h in heads:
            o = qss[h] + avs[h]
            on = o * lax.rsqrt(jnp.mean(o * o, axis=-1, keepdims=True) + EPS) * ng_ref[...]
            o_ref[0, rows, cols[h]] = (on * zg_ref[0, rows, cols[h]]).astype(o_ref.dtype)
        return carry

    lax.fori_loop(0, tm // CHUNK, step, 0)


def _delta_scan(u, w, qd, kd, at, zg, grow, norm_g):
    B, S, _ = u.shape
    tm = TM_SCAN
    row = lambda width: pl.BlockSpec((1, tm, width), lambda b, t: (b, t, 0))
    return pl.pallas_call(
        _delta_scan_kernel,
        grid=(B, S // tm),
        in_specs=[row(DN_DIM), row(DN_DIM), row(DN_DIM), row(DN_DIM), row(DN_HEADS * CHUNK), row(DN_DIM),
                  pl.BlockSpec((1, tm // CHUNK, HALO, CHUNK), lambda b, t: (b, t, 0, 0)),
                  pl.BlockSpec((1, DN_HEAD_DIM), lambda b, t: (0, 0))],
        out_specs=row(DN_DIM),
        out_shape=jax.ShapeDtypeStruct((B, S, DN_DIM), BF16),
        scratch_shapes=[pltpu.VMEM((DN_HEADS, DN_HEAD_DIM, DN_HEAD_DIM), F32)],
        compiler_params=pltpu.CompilerParams(
            dimension_semantics=("arbitrary", "arbitrary"), vmem_limit_bytes=VMEM_LIMIT),
        name="delta_scan",
    )(u, w, qd, kd, at, zg, grow, norm_g)


def _outproj_kernel(ya_ref, yb_ref, x_ref, wo_ref, g_ref, rw_ref,
                    x1_ref, h2_ref, ri_ref, rg_ref, cnt_ref, base_ref):
    tm = x_ref.shape[0]

    @pl.when(pl.program_id(0) == 0)
    def _():
        base_ref[...] = jnp.zeros(base_ref.shape, F32)

    y = _dot(jnp.concatenate([ya_ref[...], yb_ref[...]], axis=-1), wo_ref[...])
    x1 = x_ref[...] + y
    x1_ref[...] = x1
    h = x1 * lax.rsqrt(jnp.mean(x1 * x1, axis=-1, keepdims=True) + EPS) * g_ref[...]
    _store_rows(h2_ref, h)

    h_hi, h_lo = _split_bf16(h, 2)
    hi_prod = _dot(h_hi, rw_ref[...])
    logits = hi_prod[:, :LANES] + (hi_prod[:, LANES:] + _dot(h_lo, rw_ref[:, :LANES]))
    lane = lax.broadcasted_iota(I32, (tm, LANES), 1)
    lanef = lane.astype(F32)
    neg = jnp.float32(-1e30)
    big = jnp.float32(1e9)
    is_g = lane < N_GROUPS
    gl = jnp.where(is_g, logits, neg)
    gmax = jnp.max(gl, axis=-1, keepdims=True)
    gidx = jnp.min(jnp.where(gl == gmax, lanef, big), axis=-1, keepdims=True)
    gsum = jnp.sum(jnp.where(is_g, jnp.exp(gl - gmax), 0.0), axis=-1, keepdims=True)
    gprob = 1.0 / gsum
    lo = N_GROUPS + EXPERTS_PER_GROUP * gidx
    emask = (lanef >= lo) & (lanef < lo + EXPERTS_PER_GROUP)
    el = jnp.where(emask, logits, neg)
    e1 = jnp.max(el, axis=-1, keepdims=True)
    i1 = jnp.min(jnp.where(el == e1, lanef, big), axis=-1, keepdims=True)
    el2 = jnp.where(lanef == i1, neg, el)
    e2 = jnp.max(el2, axis=-1, keepdims=True)
    i2 = jnp.min(jnp.where(el2 == e2, lanef, big), axis=-1, keepdims=True)
    r = jnp.exp(e2 - e1)
    gate1 = gprob / (1.0 + r)
    gate2 = gprob * r / (1.0 + r)
    id1 = i1 - N_GROUPS
    id2 = i2 - N_GROUPS

    oh1 = jnp.where(lanef == id1, 1.0, 0.0).astype(F32)
    oh2 = jnp.where(lanef == id2, 1.0, 0.0).astype(F32)
    oh = oh1 + oh2
    rr = lax.broadcasted_iota(I32, (tm, tm), 0)
    cc = lax.broadcasted_iota(I32, (tm, tm), 1)
    tri = jnp.where(cc < rr, 1.0, 0.0).astype(BF16)
    before = _dot(tri, oh.astype(BF16)) + base_ref[...]
    rank1 = jnp.sum(oh1 * before, axis=-1, keepdims=True)
    rank2 = jnp.sum(oh2 * before, axis=-1, keepdims=True)
    base_ref[...] = base_ref[...] + jnp.sum(oh, axis=0, keepdims=True)
    cnt_ref[...] = base_ref[...]

    ri = jnp.where(lane == 0, id1, jnp.where(lane == 1, id2,
                   jnp.where(lane == 2, rank1, jnp.where(lane == 3, rank2, 0.0))))
    ri_ref[...] = ri.astype(I32)
    rg_ref[...] = jnp.where(lane == 0, gate1, jnp.where(lane == 1, gate2, 0.0))


def _outproj(ya, yb, x, w_out16, ffn_g, rw):
    N, D = x.shape
    tm = TM_OUT
    row = lambda width: pl.BlockSpec((tm, width), lambda i: (i, 0))
    full = lambda shape: pl.BlockSpec(shape, lambda i: (0,) * len(shape))
    return pl.pallas_call(
        _outproj_kernel,
        grid=(N // tm,),
        in_specs=[row(CONV_A_DIM), row(DN_DIM), row(D), full((D, D)), full((1, D)), full((D, 2 * LANES))],
        out_specs=(row(D), pl.BlockSpec((tm * ROW_SLAB, LANES), lambda i: (i, 0)), row(LANES), row(LANES),
                   full((1, LANES))),
        out_shape=(jax.ShapeDtypeStruct((N, D), F32),
                   jax.ShapeDtypeStruct((N * ROW_SLAB, LANES), F32),
                   jax.ShapeDtypeStruct((N, LANES), I32),
                   jax.ShapeDtypeStruct((N, LANES), F32),
                   jax.ShapeDtypeStruct((1, LANES), F32)),
        scratch_shapes=[pltpu.VMEM((1, LANES), F32)],
        compiler_params=pltpu.CompilerParams(
            dimension_semantics=("arbitrary",), vmem_limit_bytes=VMEM_LIMIT),
        name="outproj",
    )(ya, yb, x, w_out16, ffn_g, rw)


def _slot_of(route_ref, start_ref, t, k):
    return start_ref[route_ref[4 * t + k]] + route_ref[4 * t + 2 + k]


def _dispatch_kernel(seg_ref, route_ref, h_ref, xs_ref, zero_ref, sem, zsem):
    tm = TM_DISPATCH
    bm = FFN_BLOCK
    n_blocks = xs_ref.shape[0] // (bm * ROW_SLAB)
    start_ref = seg_ref

    @pl.when(pl.program_id(0) == 0)
    def _():
        zero_ref[...] = jnp.zeros(zero_ref.shape, F32)

        def pad_copy(d):
            return pltpu.make_async_copy(zero_ref.at[pl.ds(0, ROW_SLAB)], _slab(xs_ref, d), zsem)

        def block_copy(b):
            return pltpu.make_async_copy(zero_ref, _slab_block(xs_ref, b), zsem)

        def each_pad(fn):
            def per_expert(e, carry):
                def per_row(d, c):
                    fn(pad_copy(d))
                    return c
                return lax.fori_loop(seg_ref[N_EXPERTS + e], seg_ref[2 * N_EXPERTS + e], per_row, carry)
            lax.fori_loop(0, N_EXPERTS, per_expert, 0)

            def per_block(b, c):
                fn(block_copy(b))
                return c
            lax.fori_loop(seg_ref[3 * N_EXPERTS], n_blocks, per_block, 0)

        each_pad(lambda cp: cp.start())
        each_pad(lambda cp: cp.wait())

    def issue(t, carry):
        for k in range(TOP_K):
            d = _slot_of(route_ref, start_ref, t, k)
            pltpu.make_async_copy(_slab(h_ref, t), _slab(xs_ref, d), sem).start(priority=k)
        return carry

    lax.fori_loop(0, tm, issue, 0, unroll=ISSUE_UNROLL)
    for _ in range(TOP_K):
        pltpu.make_async_copy(h_ref, xs_ref.at[pl.ds(0, tm * ROW_SLAB)], sem).wait()


def _dispatch(seg, route, h2, n_slots):
    N = h2.shape[0] // ROW_SLAB
    tm = TM_DISPATCH
    grid_spec = pltpu.PrefetchScalarGridSpec(
        num_scalar_prefetch=1,
        grid=(N // tm,),
        in_specs=[pl.BlockSpec((4 * tm,), lambda i, s: (i,), memory_space=pltpu.SMEM),
                  pl.BlockSpec((tm * ROW_SLAB, LANES), lambda i, s: (i, 0))],
        out_specs=pl.BlockSpec(memory_space=pl.ANY),
        scratch_shapes=[pltpu.VMEM((FFN_BLOCK * ROW_SLAB, LANES), F32),
                        pltpu.SemaphoreType.DMA(()), pltpu.SemaphoreType.DMA(())],
    )
    return pl.pallas_call(
        _dispatch_kernel,
        grid_spec=grid_spec,
        out_shape=jax.ShapeDtypeStruct((n_slots * ROW_SLAB, LANES), F32),
        compiler_params=pltpu.CompilerParams(dimension_semantics=("arbitrary",)),
        name="dispatch",
    )(seg, route, h2)


def _ffn_kernel(be_ref, nb_ref, x_ref, wg_ref, wu_ref, wd_ref, y_ref, wg16, wu16, wd16):
    i = pl.program_id(0)
    bm = FFN_BLOCK
    prev = be_ref[jnp.maximum(i - 1, 0)]

    @pl.when((i == 0) | (be_ref[i] != prev))
    def _():
        wg16[...] = wg_ref[0].astype(BF16)
        wu16[...] = wu_ref[0].astype(BF16)
        wd16[...] = wd_ref[0].astype(BF16)

    @pl.when(i < nb_ref[0])
    def _():
        x = _load_rows(x_ref, bm).astype(BF16)
        a = _dot(x, wg16[...])
        b = _dot(x, wu16[...])
        _store_rows(y_ref, _dot((_silu(a) * b).astype(BF16), wd16[...]))

    @pl.when(i >= nb_ref[0])
    def _():
        y_ref[...] = jnp.zeros(y_ref.shape, F32)


def _ffn(block_expert, n_used, xs, w_gate, w_up, w_down):
    bm = FFN_BLOCK
    n_blocks = xs.shape[0] // (bm * ROW_SLAB)
    D = D_MODEL
    last = lambda nb: jnp.maximum(nb[0] - 1, 0)
    grid_spec = pltpu.PrefetchScalarGridSpec(
        num_scalar_prefetch=2,
        grid=(n_blocks,),
        in_specs=[pl.BlockSpec((bm * ROW_SLAB, LANES), lambda i, be, nb: (jnp.minimum(i, last(nb)), 0)),
                  pl.BlockSpec((1, D, EXPERT_FF), lambda i, be, nb: (be[i], 0, 0)),
                  pl.BlockSpec((1, D, EXPERT_FF), lambda i, be, nb: (be[i], 0, 0)),
                  pl.BlockSpec((1, EXPERT_FF, D), lambda i, be, nb: (be[i], 0, 0))],
        out_specs=pl.BlockSpec((bm * ROW_SLAB, LANES), lambda i, be, nb: (i, 0)),
        scratch_shapes=[pltpu.VMEM((D, EXPERT_FF), BF16), pltpu.VMEM((D, EXPERT_FF), BF16),
                        pltpu.VMEM((EXPERT_FF, D), BF16)],
    )

    return pl.pallas_call(
        _ffn_kernel,
        grid_spec=grid_spec,
        out_shape=jax.ShapeDtypeStruct(xs.shape, F32),
        compiler_params=pltpu.CompilerParams(
            dimension_semantics=("arbitrary",), vmem_limit_bytes=VMEM_LIMIT),
        name="ffn",
    )(block_expert, n_used, xs, w_gate, w_up, w_down)


def _combine_kernel(start_ref, route_ref, route_next_ref, ys_ref, x1_ref, rg_ref, g_ref, o_ref,
                    buf_ref, sems):
    tm = TM_COMBINE
    i = pl.program_id(0)
    half = i % 2

    def gather(r_ref, s):
        def issue(t, carry):
            for k in range(TOP_K):
                d = _slot_of(r_ref, start_ref, t, k)
                pltpu.make_async_copy(_slab(ys_ref, d), _slab(buf_ref.at[s, k], t),
                                      sems.at[s]).start(priority=k)
            return carry
        lax.fori_loop(0, tm, issue, 0, unroll=ISSUE_UNROLL)

    @pl.when(i == 0)
    def _():
        gather(route_ref, 0)

    @pl.when(i + 1 < pl.num_programs(0))
    def _():
        gather(route_next_ref, 1 - half)

    for k in range(TOP_K):
        pltpu.make_async_copy(ys_ref.at[pl.ds(0, tm * ROW_SLAB)], buf_ref.at[half, k], sems.at[half]).wait()

    rg = rg_ref[...]
    moe = (_load_rows(buf_ref.at[half, 0], tm) * rg[:, 0:1]
           + _load_rows(buf_ref.at[half, 1], tm) * rg[:, 1:2])
    x2 = x1_ref[...] + moe
    o_ref[...] = x2 * lax.rsqrt(jnp.mean(x2 * x2, axis=-1, keepdims=True) + EPS) * g_ref[...]


def _combine(seg_start, route, ys, x1, rg, final_g):
    N, D = x1.shape
    tm = TM_COMBINE
    n_tiles = N // tm
    grid_spec = pltpu.PrefetchScalarGridSpec(
        num_scalar_prefetch=1,
        grid=(n_tiles,),
        in_specs=[pl.BlockSpec((4 * tm,), lambda i, s: (i,), memory_space=pltpu.SMEM),
                  pl.BlockSpec((4 * tm,), lambda i, s: (jnp.minimum(i + 1, n_tiles - 1),),
                               memory_space=pltpu.SMEM),
                  pl.BlockSpec(memory_space=pl.ANY),
                  pl.BlockSpec((tm, D), lambda i, s: (i, 0)),
                  pl.BlockSpec((tm, LANES), lambda i, s: (i, 0)),
                  pl.BlockSpec((1, D), lambda i, s: (0, 0))],
        out_specs=pl.BlockSpec((tm, D), lambda i, s: (i, 0)),
        scratch_shapes=[pltpu.VMEM((2, TOP_K, tm * ROW_SLAB, LANES), F32), pltpu.SemaphoreType.DMA((2,))],
    )
    return pl.pallas_call(
        _combine_kernel,
        grid_spec=grid_spec,
        out_shape=jax.ShapeDtypeStruct((N, D), F32),
        compiler_params=pltpu.CompilerParams(
            dimension_semantics=("arbitrary",), vmem_limit_bytes=VMEM_LIMIT),
        name="combine",
    )(seg_start, route, route, ys, x1, rg, final_g)


def _lane_vec(values, offset):
    return jnp.zeros((1, LANES), F32).at[0, offset:offset + values.shape[0]].set(values.astype(F32))


def kernel(x, mix_norm_g, w_in, conv_a_w, conv_a_norm_g, dn_conv_w, dn_a_log, dn_dt_bias, dn_norm_g,
           w_out, ffn_norm_g, router_group_w, router_expert_w, w_gate, w_up, w_down, final_norm_g):
    B, S, D = x.shape
    N = B * S
    depth = w_in.shape[0]
    assert depth == 1, "single-layer block: the final RMSNorm is fused into the layer's combine step"
    group_of = jnp.arange(CONV_A_DIM, dtype=I32) // CONV_A_GROUP_DIM
    gmat = jnp.where(group_of[:, None] == group_of[None, :], 1.0 / CONV_A_GROUP_DIM, 0.0).astype(BF16)
    bm = FFN_BLOCK
    n_blocks = (N * TOP_K) // bm + N_EXPERTS
    for l in range(depth):
        w_in_pad = jnp.pad(w_in[l], ((0, 0), (0, IN_PROJ_PAD - IN_PROJ_DIM))).astype(BF16)
        ya, q, k, v, zg, gcol, grow = _inproj(
            x, mix_norm_g[l][None, :], w_in_pad, conv_a_w[l], conv_a_norm_g[l][None, :], dn_conv_w[l],
            _lane_vec(dn_a_log[l], DN_HEADS), _lane_vec(dn_dt_bias[l], DN_HEADS), gmat)
        u, w, qd, kd, at = _delta_prep(q, k, v, gcol, grow)
        yb = _delta_scan(u, w, qd, kd, at, zg, grow, dn_norm_g[l][None, :])
        rw = jnp.pad(jnp.concatenate([router_group_w[l], router_expert_w[l]], axis=1),
                     ((0, 0), (0, LANES - N_GROUPS - N_EXPERTS)))
        rw_hi = rw.astype(BF16)
        rw = jnp.concatenate([rw_hi, (rw - rw_hi.astype(F32)).astype(BF16)], axis=1)
        x1, h2, ri, rg, cnt = _outproj(ya.reshape(N, CONV_A_DIM), yb.reshape(N, DN_DIM), x.reshape(N, D),
                                       w_out[l].astype(BF16), ffn_norm_g[l][None, :], rw)
        counts = cnt[0, :N_EXPERTS].astype(I32)
        padded = (counts + bm - 1) // bm * bm
        seg_end = jnp.cumsum(padded)
        seg_start = (seg_end - padded).astype(I32)
        n_used = (seg_end[-1] // bm).astype(I32)
        blk = jnp.arange(n_blocks, dtype=I32)
        blk_expert = jnp.sum(seg_end[None, :] <= (jnp.minimum(blk, n_used - 1) * bm)[:, None], axis=1)
        blk_expert = jnp.minimum(blk_expert, N_EXPERTS - 1).astype(I32)
        route = ri[:, :4].reshape(-1)
        seg = jnp.concatenate([seg_start, seg_start + counts, seg_end.astype(I32), n_used.reshape(1)])
        xs = _dispatch(seg, route, h2, n_blocks * bm)
        ys = _ffn(blk_expert, n_used.reshape(1), xs, w_gate[l], w_up[l], w_down[l])
        x = _combine(seg, route, ys, x1, rg, final_norm_g[None, :]).reshape(B, S, D)
    return x
```

```python
import functools

import jax
import jax.numpy as jnp
from jax import lax
from jax.experimental import pallas as pl
from jax.experimental.pallas import tpu as pltpu

F32 = jnp.float32
BF16 = jnp.bfloat16
I32 = jnp.int32

D_MODEL = 1024
CHUNK = 64
CONV_A_GROUP_DIM = 64
CONV_A_DIM = 512
CONV_A_WIDTH = 3
DN_HEADS = 4
DN_HEAD_DIM = 128
DN_DIM = 512
DN_CONV_WIDTH = 4
IN_PROJ_DIM = 3 * CONV_A_DIM + 4 * DN_DIM + 2 * DN_HEADS
N_GROUPS = 4
EXPERTS_PER_GROUP = 8
N_EXPERTS = 32
TOP_K = 2
EXPERT_FF = 512
EPS = 1e-6

LANES = 128
HALO = 8
IN_PROJ_PAD = 29 * LANES
GATE_COL = 3 * CONV_A_DIM + 4 * DN_DIM

TM_IN = 512
TM_PREP = 256
TM_SCAN = 512
TM_OUT = 512
FFN_BLOCK = 256
TM_SLOTS = 2048
TM_DISPATCH = 1024
TM_COMBINE = 256
ISSUE_UNROLL = 4
VMEM_LIMIT = 56 * 1024 * 1024


def _dot(a, b):
    return jnp.dot(a, b, preferred_element_type=F32)


def _dot_nt(a, b):
    return lax.dot_general(a, b, (((1,), (1,)), ((), ())), preferred_element_type=F32)


def _dot_tn(a, b):
    return lax.dot_general(a, b, (((0,), (0,)), ((), ())), preferred_element_type=F32)


def _split_bf16(x, parts):
    out = []
    for _ in range(parts):
        p = x.astype(BF16)
        out.append(p)
        x = x - p.astype(F32)
    return out


def _dot_split3(exact16, x):
    return sum(_dot(exact16, p) for p in _split_bf16(x, 3))


def _silu(x):
    return x * jax.nn.sigmoid(x)


ROW_SLAB = D_MODEL // LANES


def _store_rows(ref, val):
    m = val.shape[0]
    for c in range(ROW_SLAB):
        ref[pl.ds(c, m, stride=ROW_SLAB), :] = val[:, c * LANES:(c + 1) * LANES]


def _load_rows(ref, m):
    return jnp.concatenate([ref[pl.ds(c, m, stride=ROW_SLAB), :] for c in range(ROW_SLAB)], axis=-1)


def _slab(ref, row):
    return ref.at[pl.ds(pl.multiple_of(row * ROW_SLAB, ROW_SLAB), ROW_SLAB)]


def _slab_block(ref, block):
    n = FFN_BLOCK * ROW_SLAB
    return ref.at[pl.ds(pl.multiple_of(block * n, n), n)]


def _softplus(x):
    return jnp.maximum(x, 0.0) + jnp.log1p(jnp.exp(-jnp.abs(x)))


def _inproj_kernel(x_ref, g_ref, w_ref, caw_ref, cag_ref, dcw_ref, alog_ref, dtb_ref, gmat_ref,
                   ya_ref, q_ref, k_ref, v_ref, zg_ref, gcol_ref, grow_ref,
                   exta_ref, extq_ref):
    tm = x_ref.shape[1]

    @pl.when(pl.program_id(1) == 0)
    def _():
        exta_ref[0:HALO, :] = jnp.zeros((HALO, CONV_A_DIM), F32)
        extq_ref[0:HALO, :] = jnp.zeros((HALO, 3 * DN_DIM), F32)

    x = x_ref[0]
    ms = jnp.mean(x * x, axis=-1, keepdims=True)
    hb = (x * lax.rsqrt(ms + EPS) * g_ref[...]).astype(BF16)

    def proj(c0, width):
        return _dot(hb, w_ref[:, c0:c0 + width])

    a_h = proj(0, CONV_A_DIM)
    a_c = proj(2 * CONV_A_DIM, CONV_A_DIM)
    exta_ref[HALO:HALO + tm, :] = a_c * a_h
    conv = caw_ref[0:1, :] * exta_ref[pl.ds(HALO - 2, tm), :]
    conv = conv + caw_ref[1:2, :] * exta_ref[pl.ds(HALO - 1, tm), :]
    conv = conv + caw_ref[2:3, :] * exta_ref[pl.ds(HALO, tm), :]
    y = proj(CONV_A_DIM, CONV_A_DIM) * conv
    ysq = y * y
    hi = ysq.astype(BF16)
    lo = (ysq - hi.astype(F32)).astype(BF16)
    gmean = _dot(hi, gmat_ref[...]) + _dot(lo, gmat_ref[...])
    ya_ref[0] = (y * lax.rsqrt(gmean + EPS) * cag_ref[...]).astype(ya_ref.dtype)
    exta_ref[0:HALO, :] = exta_ref[tm:tm + HALO, :]

    base = 3 * CONV_A_DIM
    for i in range(3):
        extq_ref[HALO:HALO + tm, i * DN_DIM:(i + 1) * DN_DIM] = proj(base + i * DN_DIM, DN_DIM)
    for i, out_ref in enumerate((q_ref, k_ref, v_ref)):
        cols = slice(i * DN_DIM, (i + 1) * DN_DIM)
        acc = dcw_ref[0:1, cols] * extq_ref[pl.ds(HALO - 3, tm), cols]
        for j in range(1, DN_CONV_WIDTH):
            acc = acc + dcw_ref[j:j + 1, cols] * extq_ref[pl.ds(HALO - 3 + j, tm), cols]
        s = _silu(acc)
        if i == 2:
            out_ref[0] = s
        else:
            for h in range(DN_HEADS):
                sh = s[:, h * DN_HEAD_DIM:(h + 1) * DN_HEAD_DIM]
                inv = lax.rsqrt(jnp.sum(sh * sh, axis=-1, keepdims=True) + EPS)
                sh = sh * inv
                if i == 0:
                    sh = sh * (DN_HEAD_DIM ** -0.5)
                out_ref[0, :, h * DN_HEAD_DIM:(h + 1) * DN_HEAD_DIM] = sh
    extq_ref[0:HALO, :] = extq_ref[tm:tm + HALO, :]

    zg_ref[0] = _silu(proj(base + 3 * DN_DIM, DN_DIM))

    p = proj(GATE_COL, LANES)
    beta = jax.nn.sigmoid(p)
    g = -jnp.exp(alog_ref[...]) * _softplus(p + dtb_ref[...])
    r = lax.broadcasted_iota(I32, (tm, tm), 0)
    c = lax.broadcasted_iota(I32, (tm, tm), 1)
    tri = jnp.where((r // CHUNK == c // CHUNK) & (c <= r), 1.0, 0.0).astype(BF16)
    gc = _dot_split3(tri, g)
    lane = lax.broadcasted_iota(I32, (tm, LANES), 1)
    slab = jnp.where(lane < DN_HEADS, beta, gc)
    gcol_ref[0] = slab
    rows = slab.T[0:HALO, :]
    for ci in range(tm // CHUNK):
        grow_ref[0, ci] = rows[:, ci * CHUNK:(ci + 1) * CHUNK]


def _inproj(x, mix_g, w_in_pad, conv_a_w, conv_a_g, dn_conv_w, alog_vec, dtb_vec, gmat):
    B, S, D = x.shape
    tm = TM_IN
    full = lambda shape: pl.BlockSpec(shape, lambda b, t: (0,) * len(shape))
    row = lambda width: pl.BlockSpec((1, tm, width), lambda b, t: (b, t, 0))
    out_shape = (
        jax.ShapeDtypeStruct((B, S, CONV_A_DIM), BF16),
        jax.ShapeDtypeStruct((B, S, DN_DIM), F32),
        jax.ShapeDtypeStruct((B, S, DN_DIM), F32),
        jax.ShapeDtypeStruct((B, S, DN_DIM), F32),
        jax.ShapeDtypeStruct((B, S, DN_DIM), F32),
        jax.ShapeDtypeStruct((B, S, LANES), F32),
        jax.ShapeDtypeStruct((B, S // CHUNK, HALO, CHUNK), F32),
    )
    return pl.pallas_call(
        _inproj_kernel,
        grid=(B, S // tm),
        in_specs=[row(D), full((1, D)), full((D, IN_PROJ_PAD)), full((CONV_A_WIDTH, CONV_A_DIM)),
                  full((1, CONV_A_DIM)), full((DN_CONV_WIDTH, 3 * DN_DIM)), full((1, LANES)),
                  full((1, LANES)), full((CONV_A_DIM, CONV_A_DIM))],
        out_specs=(row(CONV_A_DIM), row(DN_DIM), row(DN_DIM), row(DN_DIM), row(DN_DIM), row(LANES),
                   pl.BlockSpec((1, tm // CHUNK, HALO, CHUNK), lambda b, t: (b, t, 0, 0))),
        out_shape=out_shape,
        scratch_shapes=[pltpu.VMEM((tm + HALO, CONV_A_DIM), F32),
                        pltpu.VMEM((tm + HALO, 3 * DN_DIM), F32)],
        compiler_params=pltpu.CompilerParams(
            dimension_semantics=("arbitrary", "arbitrary"), vmem_limit_bytes=VMEM_LIMIT),
        name="inproj",
    )(x, mix_g, w_in_pad, conv_a_w, conv_a_g, dn_conv_w, alog_vec, dtb_vec, gmat)


def _delta_prep_kernel(q_ref, k_ref, v_ref, gcol_ref, grow_ref,
                       u_ref, w_ref, qd_ref, kd_ref, at_ref):
    tm = q_ref.shape[1]
    ri = lax.broadcasted_iota(I32, (CHUNK, CHUNK), 0)
    ci = lax.broadcasted_iota(I32, (CHUNK, CHUNK), 1)
    causal = ci <= ri
    strict = ci < ri
    eye = jnp.where(ci == ri, 1.0, 0.0).astype(F32)
    lane = lax.broadcasted_iota(I32, (CHUNK, LANES), 1)
    chains = [(c, h) for c in range(tm // CHUNK) for h in range(DN_HEADS)]
    ms, ts, rhss = [], [], []
    for c, h in chains:
        rows = slice(c * CHUNK, (c + 1) * CHUNK)
        cols = slice(h * DN_HEAD_DIM, (h + 1) * DN_HEAD_DIM)
        slab = gcol_ref[0, rows, :]
        beta = jnp.sum(jnp.where(lane == h, slab, 0.0), axis=-1, keepdims=True)
        gcc = jnp.sum(jnp.where(lane == h + DN_HEADS, slab, 0.0), axis=-1, keepdims=True)
        gcr = grow_ref[0, c, h + DN_HEADS:h + DN_HEADS + 1, :]
        diff = gcc - gcr
        decay = jnp.where(causal, jnp.exp(jnp.where(causal, diff, 0.0)), 0.0)
        q = q_ref[0, rows, cols]
        k = k_ref[0, rows, cols]
        v = v_ref[0, rows, cols]
        kb = k * beta
        k16 = k.astype(BF16)
        L = jnp.where(strict, _dot_nt(kb.astype(BF16), k16) * decay, 0.0)
        intra = _dot_nt(q.astype(BF16), k16) * decay
        egc = jnp.exp(gcc)
        gl = gcr[:, CHUNK - 1:CHUNK]
        qd_ref[0, rows, cols] = (q * egc).astype(BF16)
        kd_ref[0, rows, cols] = (k * jnp.exp(gl - gcc)).astype(BF16)
        at_ref[0, rows, h * CHUNK:(h + 1) * CHUNK] = intra.astype(BF16)
        rhss.append(jnp.concatenate([v * beta, kb * egc], axis=-1).astype(BF16))
        ms.append(-L)
        ts.append(eye - L)
    for _ in range(5):
        m16s = [m.astype(BF16) for m in ms]
        ms = [_dot(m16, m16) for m16 in m16s]
        ts = [t + _dot(t.astype(BF16), m.astype(BF16)) for t, m in zip(ts, ms)]
    for (c, h), t, rhs in zip(chains, ts, rhss):
        rows = slice(c * CHUNK, (c + 1) * CHUNK)
        cols = slice(h * DN_HEAD_DIM, (h + 1) * DN_HEAD_DIM)
        uw = _dot(t.astype(BF16), rhs)
        u_ref[0, rows, cols] = uw[:, :DN_HEAD_DIM]
        w_ref[0, rows, cols] = uw[:, DN_HEAD_DIM:].astype(BF16)


def _delta_prep(q, k, v, gcol, grow):
    B, S, _ = q.shape
    tm = TM_PREP
    row = lambda width: pl.BlockSpec((1, tm, width), lambda b, t: (b, t, 0))
    return pl.pallas_call(
        _delta_prep_kernel,
        grid=(B, S // tm),
        in_specs=[row(DN_DIM), row(DN_DIM), row(DN_DIM), row(LANES),
                  pl.BlockSpec((1, tm // CHUNK, HALO, CHUNK), lambda b, t: (b, t, 0, 0))],
        out_specs=(row(DN_DIM), row(DN_DIM), row(DN_DIM), row(DN_DIM), row(DN_HEADS * CHUNK)),
        out_shape=(jax.ShapeDtypeStruct((B, S, DN_DIM), F32),
                   jax.ShapeDtypeStruct((B, S, DN_DIM), BF16),
                   jax.ShapeDtypeStruct((B, S, DN_DIM), BF16),
                   jax.ShapeDtypeStruct((B, S, DN_DIM), BF16),
                   jax.ShapeDtypeStruct((B, S, DN_HEADS * CHUNK), BF16)),
        compiler_params=pltpu.CompilerParams(
            dimension_semantics=("arbitrary", "arbitrary"), vmem_limit_bytes=VMEM_LIMIT),
        name="delta_prep",
    )(q, k, v, gcol, grow)


def _delta_scan_kernel(u_ref, w_ref, qd_ref, kd_ref, at_ref, zg_ref, grow_ref, ng_ref, o_ref, st_ref):
    tm = u_ref.shape[1]

    @pl.when(pl.program_id(1) == 0)
    def _():
        st_ref[...] = jnp.zeros(st_ref.shape, F32)

    def step(c, carry):
        r0 = pl.multiple_of(c * CHUNK, CHUNK)
        rows = pl.ds(r0, CHUNK)
        heads = range(DN_HEADS)
        cols = [slice(h * DN_HEAD_DIM, (h + 1) * DN_HEAD_DIM) for h in heads]
        sts = [st_ref[h] for h in heads]
        st16s = [st.astype(BF16) for st in sts]
        wss = [_dot(w_ref[0, rows, cols[h]], st16s[h]) for h in heads]
        qss = [_dot(qd_ref[0, rows, cols[h]], st16s[h]) for h in heads]
        vn16s = [(u_ref[0, rows, cols[h]] - wss[h]).astype(BF16) for h in heads]
        upd = [_dot_tn(kd_ref[0, rows, cols[h]], vn16s[h]) for h in heads]
        avs = [_dot(at_ref[0, rows, h * CHUNK:(h + 1) * CHUNK], vn16s[h]) for h in heads]
        for h in heads:
            gl = grow_ref[0, c, h + DN_HEADS:h + DN_HEADS + 1, CHUNK - 1:CHUNK]
            st_ref[h] = sts[h] * jnp.exp(gl) + upd[h]
        for h in heads:
            o = qss[h] + avs[h]
            on = o * lax.rsqrt(jnp.mean(o * o, axis=-1, keepdims=True) + EPS) * ng_ref[...]
            o_ref[0, rows, cols[h]] = (on * zg_ref[0, rows, cols[h]]).astype(o_ref.dtype)
        return carry

    lax.fori_loop(0, tm // CHUNK, step, 0)


def _delta_scan(u, w, qd, kd, at, zg, grow, norm_g):
    B, S, _ = u.shape
    tm = TM_SCAN
    row = lambda width: pl.BlockSpec((1, tm, width), lambda b, t: (b, t, 0))
    return pl.pallas_call(
        _delta_scan_kernel,
        grid=(B, S // tm),
        in_specs=[row(DN_DIM), row(DN_DIM), row(DN_DIM), row(DN_DIM), row(DN_HEADS * CHUNK), row(DN_DIM),
                  pl.BlockSpec((1, tm // CHUNK, HALO, CHUNK), lambda b, t: (b, t, 0, 0)),
                  pl.BlockSpec((1, DN_HEAD_DIM), lambda b, t: (0, 0))],
        out_specs=row(DN_DIM),
        out_shape=jax.ShapeDtypeStruct((B, S, DN_DIM), BF16),
        scratch_shapes=[pltpu.VMEM((DN_HEADS, DN_HEAD_DIM, DN_HEAD_DIM), F32)],
        compiler_params=pltpu.CompilerParams(
            dimension_semantics=("arbitrary", "arbitrary"), vmem_limit_bytes=VMEM_LIMIT),
        name="delta_scan",
    )(u, w, qd, kd, at, zg, grow, norm_g)


def _outproj_kernel(ya_ref, yb_ref, x_ref, wo_ref, g_ref, rw_ref,
                    x1_ref, h2_ref, ri_ref, rg_ref, cnt_ref, base_ref):
    tm = x_ref.shape[0]

    @pl.when(pl.program_id(0) == 0)
    def _():
        base_ref[...] = jnp.zeros(base_ref.shape, F32)

    y = _dot(jnp.concatenate([ya_ref[...], yb_ref[...]], axis=-1), wo_ref[...])
    x1 = x_ref[...] + y
    x1_ref[...] = x1
    h = x1 * lax.rsqrt(jnp.mean(x1 * x1, axis=-1, keepdims=True) + EPS) * g_ref[...]
    _store_rows(h2_ref, h)

    h_hi, h_lo = _split_bf16(h, 2)
    hi_prod = _dot(h_hi, rw_ref[...])
    logits = hi_prod[:, :LANES] + (hi_prod[:, LANES:] + _dot(h_lo, rw_ref[:, :LANES]))
    lane = lax.broadcasted_iota(I32, (tm, LANES), 1)
    lanef = lane.astype(F32)
    neg = jnp.float32(-1e30)
    big = jnp.float32(1e9)
    is_g = lane < N_GROUPS
    gl = jnp.where(is_g, logits, neg)
    gmax = jnp.max(gl, axis=-1, keepdims=True)
    gidx = jnp.min(jnp.where(gl == gmax, lanef, big), axis=-1, keepdims=True)
    gsum = jnp.sum(jnp.where(is_g, jnp.exp(gl - gmax), 0.0), axis=-1, keepdims=True)
    gprob = 1.0 / gsum
    lo = N_GROUPS + EXPERTS_PER_GROUP * gidx
    emask = (lanef >= lo) & (lanef < lo + EXPERTS_PER_GROUP)
    el = jnp.where(emask, logits, neg)
    e1 = jnp.max(el, axis=-1, keepdims=True)
    i1 = jnp.min(jnp.where(el == e1, lanef, big), axis=-1, keepdims=True)
    el2 = jnp.where(lanef == i1, neg, el)
    e2 = jnp.max(el2, axis=-1, keepdims=True)
    i2 = jnp.min(jnp.where(el2 == e2, lanef, big), axis=-1, keepdims=True)
    r = jnp.exp(e2 - e1)
    gate1 = gprob / (1.0 + r)
    gate2 = gprob * r / (1.0 + r)
    id1 = i1 - N_GROUPS
    id2 = i2 - N_GROUPS

    oh1 = jnp.where(lanef == id1, 1.0, 0.0).astype(F32)
    oh2 = jnp.where(lanef == id2, 1.0, 0.0).astype(F32)
    oh = oh1 + oh2
    rr = lax.broadcasted_iota(I32, (tm, tm), 0)
    cc = lax.broadcasted_iota(I32, (tm, tm), 1)
    tri = jnp.where(cc < rr, 1.0, 0.0).astype(BF16)
    before = _dot(tri, oh.astype(BF16)) + base_ref[...]
    rank1 = jnp.sum(oh1 * before, axis=-1, keepdims=True)
    rank2 = jnp.sum(oh2 * before, axis=-1, keepdims=True)
    base_ref[...] = base_ref[...] + jnp.sum(oh, axis=0, keepdims=True)
    cnt_ref[...] = base_ref[...]

    ri = jnp.where(lane == 0, id1, jnp.where(lane == 1, id2,
                   jnp.where(lane == 2, rank1, jnp.where(lane == 3, rank2, 0.0))))
    ri_ref[...] = ri.astype(I32)
    rg_ref[...] = jnp.where(lane == 0, gate1, jnp.where(lane == 1, gate2, 0.0))


def _outproj(ya, yb, x, w_out16, ffn_g, rw):
    N, D = x.shape
    tm = TM_OUT
    row = lambda width: pl.BlockSpec((tm, width), lambda i: (i, 0))
    full = lambda shape: pl.BlockSpec(shape, lambda i: (0,) * len(shape))
    return pl.pallas_call(
        _outproj_kernel,
        grid=(N // tm,),
        in_specs=[row(CONV_A_DIM), row(DN_DIM), row(D), full((D, D)), full((1, D)), full((D, 2 * LANES))],
        out_specs=(row(D), pl.BlockSpec((tm * ROW_SLAB, LANES), lambda i: (i, 0)), row(LANES), row(LANES),
                   full((1, LANES))),
        out_shape=(jax.ShapeDtypeStruct((N, D), F32),
                   jax.ShapeDtypeStruct((N * ROW_SLAB, LANES), F32),
                   jax.ShapeDtypeStruct((N, LANES), I32),
                   jax.ShapeDtypeStruct((N, LANES), F32),
                   jax.ShapeDtypeStruct((1, LANES), F32)),
        scratch_shapes=[pltpu.VMEM((1, LANES), F32)],
        compiler_params=pltpu.CompilerParams(
            dimension_semantics=("arbitrary",), vmem_limit_bytes=VMEM_LIMIT),
        name="outproj",
    )(ya, yb, x, w_out16, ffn_g, rw)


def _slots_kernel(ri_ref, start_ref, o_ref):
    rif = ri_ref[...].astype(F32)
    lane = lax.broadcasted_iota(I32, rif.shape, 1)
    lanef = lane.astype(F32)

    def slot(k):
        start = jnp.sum(jnp.where(lanef == rif[:, k:k + 1], start_ref[...], 0.0), axis=-1, keepdims=True)
        return start + rif[:, TOP_K + k:TOP_K + k + 1]

    o_ref[...] = jnp.where(lane == 0, slot(0), jnp.where(lane == 1, slot(1), 0.0)).astype(I32)


def _slots(ri, seg_start):
    N = ri.shape[0]
    tm = TM_SLOTS
    return pl.pallas_call(
        _slots_kernel,
        grid=(N // tm,),
        in_specs=[pl.BlockSpec((tm, LANES), lambda i: (i, 0)), pl.BlockSpec((1, LANES), lambda i: (0, 0))],
        out_specs=pl.BlockSpec((tm, LANES), lambda i: (i, 0)),
        out_shape=jax.ShapeDtypeStruct((N, LANES), I32),
        compiler_params=pltpu.CompilerParams(dimension_semantics=("arbitrary",)),
        name="slots",
    )(ri, _lane_vec(seg_start, 0))


def _slot_of(slots_ref, t, k):
    return slots_ref[TOP_K * t + k]


def _dispatch_kernel(seg_ref, slots_ref, h_ref, xs_ref, zero_ref, sem, zsem):
    tm = TM_DISPATCH
    bm = FFN_BLOCK
    n_blocks = xs_ref.shape[0] // (bm * ROW_SLAB)

    @pl.when(pl.program_id(0) == 0)
    def _():
        zero_ref[...] = jnp.zeros(zero_ref.shape, F32)

        def pad_copy(d):
            return pltpu.make_async_copy(zero_ref.at[pl.ds(0, ROW_SLAB)], _slab(xs_ref, d), zsem)

        def block_copy(b):
            return pltpu.make_async_copy(zero_ref, _slab_block(xs_ref, b), zsem)

        def each_pad(fn):
            def per_expert(e, carry):
                def per_row(d, c):
                    fn(pad_copy(d))
                    return c
                return lax.fori_loop(seg_ref[e], seg_ref[N_EXPERTS + e], per_row, carry)
            lax.fori_loop(0, N_EXPERTS, per_expert, 0)

            def per_block(b, c):
                fn(block_copy(b))
                return c
            lax.fori_loop(seg_ref[2 * N_EXPERTS], n_blocks, per_block, 0)

        each_pad(lambda cp: cp.start())
        each_pad(lambda cp: cp.wait())

    def issue(t, carry):
        for k in range(TOP_K):
            d = _slot_of(slots_ref, t, k)
            pltpu.make_async_copy(_slab(h_ref, t), _slab(xs_ref, d), sem).start(priority=k)
        return carry

    lax.fori_loop(0, tm, issue, 0, unroll=ISSUE_UNROLL)
    for _ in range(TOP_K):
        pltpu.make_async_copy(h_ref, xs_ref.at[pl.ds(0, tm * ROW_SLAB)], sem).wait()


def _dispatch(seg, slots, h2, n_slots):
    N = h2.shape[0] // ROW_SLAB
    tm = TM_DISPATCH
    grid_spec = pltpu.PrefetchScalarGridSpec(
        num_scalar_prefetch=1,
        grid=(N // tm,),
        in_specs=[pl.BlockSpec((TOP_K * tm,), lambda i, s: (i,), memory_space=pltpu.SMEM),
                  pl.BlockSpec((tm * ROW_SLAB, LANES), lambda i, s: (i, 0))],
        out_specs=pl.BlockSpec(memory_space=pl.ANY),
        scratch_shapes=[pltpu.VMEM((FFN_BLOCK * ROW_SLAB, LANES), F32),
                        pltpu.SemaphoreType.DMA(()), pltpu.SemaphoreType.DMA(())],
    )
    return pl.pallas_call(
        _dispatch_kernel,
        grid_spec=grid_spec,
        out_shape=jax.ShapeDtypeStruct((n_slots * ROW_SLAB, LANES), F32),
        compiler_params=pltpu.CompilerParams(dimension_semantics=("arbitrary",)),
        name="dispatch",
    )(seg, slots, h2)


def _ffn_kernel(be_ref, nb_ref, x_ref, wg_ref, wu_ref, wd_ref, y_ref, wg16, wu16, wd16):
    i = pl.program_id(0)
    bm = FFN_BLOCK
    prev = be_ref[jnp.maximum(i - 1, 0)]

    @pl.when((i == 0) | (be_ref[i] != prev))
    def _():
        wg16[...] = wg_ref[0].astype(BF16)
        wu16[...] = wu_ref[0].astype(BF16)
        wd16[...] = wd_ref[0].astype(BF16)

    @pl.when(i < nb_ref[0])
    def _():
        x = _load_rows(x_ref, bm).astype(BF16)
        a = _dot(x, wg16[...])
        b = _dot(x, wu16[...])
        _store_rows(y_ref, _dot((_silu(a) * b).astype(BF16), wd16[...]))

    @pl.when(i >= nb_ref[0])
    def _():
        y_ref[...] = jnp.zeros(y_ref.shape, F32)


def _ffn(block_expert, n_used, xs, w_gate, w_up, w_down):
    bm = FFN_BLOCK
    n_blocks = xs.shape[0] // (bm * ROW_SLAB)
    D = D_MODEL
    last = lambda nb: jnp.maximum(nb[0] - 1, 0)
    grid_spec = pltpu.PrefetchScalarGridSpec(
        num_scalar_prefetch=2,
        grid=(n_blocks,),
        in_specs=[pl.BlockSpec((bm * ROW_SLAB, LANES), lambda i, be, nb: (jnp.minimum(i, last(nb)), 0)),
                  pl.BlockSpec((1, D, EXPERT_FF), lambda i, be, nb: (be[i], 0, 0)),
                  pl.BlockSpec((1, D, EXPERT_FF), lambda i, be, nb: (be[i], 0, 0)),
                  pl.BlockSpec((1, EXPERT_FF, D), lambda i, be, nb: (be[i], 0, 0))],
        out_specs=pl.BlockSpec((bm * ROW_SLAB, LANES), lambda i, be, nb: (i, 0)),
        scratch_shapes=[pltpu.VMEM((D, EXPERT_FF), BF16), pltpu.VMEM((D, EXPERT_FF), BF16),
                        pltpu.VMEM((EXPERT_FF, D), BF16)],
    )

    return pl.pallas_call(
        _ffn_kernel,
        grid_spec=grid_spec,
        out_shape=jax.ShapeDtypeStruct(xs.shape, F32),
        compiler_params=pltpu.CompilerParams(
            dimension_semantics=("arbitrary",), vmem_limit_bytes=VMEM_LIMIT),
        name="ffn",
    )(block_expert, n_used, xs, w_gate, w_up, w_down)


def _combine_kernel(slots_ref, slots_next_ref, ys_ref, x1_ref, rg_ref, g_ref, o_ref, buf_ref, sems):
    tm = TM_COMBINE
    i = pl.program_id(0)
    half = i % 2

    def gather(r_ref, s):
        def issue(t, carry):
            for k in range(TOP_K):
                d = _slot_of(r_ref, t, k)
                pltpu.make_async_copy(_slab(ys_ref, d), _slab(buf_ref.at[s, k], t),
                                      sems.at[s]).start(priority=k)
            return carry
        lax.fori_loop(0, tm, issue, 0, unroll=ISSUE_UNROLL)

    @pl.when(i == 0)
    def _():
        gather(slots_ref, 0)

    @pl.when(i + 1 < pl.num_programs(0))
    def _():
        gather(slots_next_ref, 1 - half)

    for k in range(TOP_K):
        pltpu.make_async_copy(ys_ref.at[pl.ds(0, tm * ROW_SLAB)], buf_ref.at[half, k], sems.at[half]).wait()

    rg = rg_ref[...]
    moe = (_load_rows(buf_ref.at[half, 0], tm) * rg[:, 0:1]
           + _load_rows(buf_ref.at[half, 1], tm) * rg[:, 1:2])
    x2 = x1_ref[...] + moe
    o_ref[...] = x2 * lax.rsqrt(jnp.mean(x2 * x2, axis=-1, keepdims=True) + EPS) * g_ref[...]


def _combine(slots, ys, x1, rg, final_g):
    N, D = x1.shape
    tm = TM_COMBINE
    n_tiles = N // tm
    return pl.pallas_call(
        _combine_kernel,
        grid=(n_tiles,),
        in_specs=[pl.BlockSpec((TOP_K * tm,), lambda i: (i,), memory_space=pltpu.SMEM),
                  pl.BlockSpec((TOP_K * tm,), lambda i: (jnp.minimum(i + 1, n_tiles - 1),),
                               memory_space=pltpu.SMEM),
                  pl.BlockSpec(memory_space=pl.ANY),
                  pl.BlockSpec((tm, D), lambda i: (i, 0)),
                  pl.BlockSpec((tm, LANES), lambda i: (i, 0)),
                  pl.BlockSpec((1, D), lambda i: (0, 0))],
        out_specs=pl.BlockSpec((tm, D), lambda i: (i, 0)),
        out_shape=jax.ShapeDtypeStruct((N, D), F32),
        scratch_shapes=[pltpu.VMEM((2, TOP_K, tm * ROW_SLAB, LANES), F32), pltpu.SemaphoreType.DMA((2,))],
        compiler_params=pltpu.CompilerParams(
            dimension_semantics=("arbitrary",), vmem_limit_bytes=VMEM_LIMIT),
        name="combine",
    )(slots, slots, ys, x1, rg, final_g)


def _lane_vec(values, offset):
    return jnp.zeros((1, LANES), F32).at[0, offset:offset + values.shape[0]].set(values.astype(F32))


def kernel(x, mix_norm_g, w_in, conv_a_w, conv_a_norm_g, dn_conv_w, dn_a_log, dn_dt_bias, dn_norm_g,
           w_out, ffn_norm_g, router_group_w, router_expert_w, w_gate, w_up, w_down, final_norm_g):
    B, S, D = x.shape
    N = B * S
    depth = w_in.shape[0]
    assert depth == 1, "single-layer block: the final RMSNorm is fused into the layer's combine step"
    group_of = jnp.arange(CONV_A_DIM, dtype=I32) // CONV_A_GROUP_DIM
    gmat = jnp.where(group_of[:, None] == group_of[None, :], 1.0 / CONV_A_GROUP_DIM, 0.0).astype(BF16)
    bm = FFN_BLOCK
    n_blocks = (N * TOP_K) // bm + N_EXPERTS
    for l in range(depth):
        w_in_pad = jnp.pad(w_in[l], ((0, 0), (0, IN_PROJ_PAD - IN_PROJ_DIM))).astype(BF16)
        ya, q, k, v, zg, gcol, grow = _inproj(
            x, mix_norm_g[l][None, :], w_in_pad, conv_a_w[l], conv_a_norm_g[l][None, :], dn_conv_w[l],
            _lane_vec(dn_a_log[l], DN_HEADS), _lane_vec(dn_dt_bias[l], DN_HEADS), gmat)
        u, w, qd, kd, at = _delta_prep(q, k, v, gcol, grow)
        yb = _delta_scan(u, w, qd, kd, at, zg, grow, dn_norm_g[l][None, :])
        rw = jnp.pad(jnp.concatenate([router_group_w[l], router_expert_w[l]], axis=1),
                     ((0, 0), (0, LANES - N_GROUPS - N_EXPERTS)))
        rw_hi = rw.astype(BF16)
        rw = jnp.concatenate([rw_hi, (rw - rw_hi.astype(F32)).astype(BF16)], axis=1)
        x1, h2, ri, rg, cnt = _outproj(ya.reshape(N, CONV_A_DIM), yb.reshape(N, DN_DIM), x.reshape(N, D),
                                       w_out[l].astype(BF16), ffn_norm_g[l][None, :], rw)
        counts = cnt[0, :N_EXPERTS].astype(I32)
        padded = (counts + bm - 1) // bm * bm
        seg_end = jnp.cumsum(padded)
        seg_start = (seg_end - padded).astype(I32)
        n_used = (seg_end[-1] // bm).astype(I32)
        blk = jnp.arange(n_blocks, dtype=I32)
        blk_expert = jnp.sum(seg_end[None, :] <= (jnp.minimum(blk, n_used - 1) * bm)[:, None], axis=1)
        blk_expert = jnp.minimum(blk_expert, N_EXPERTS - 1).astype(I32)
        slots = _slots(ri, seg_start)[:, :TOP_K].reshape(-1)
        seg = jnp.concatenate([seg_start + counts, seg_end.astype(I32), n_used.reshape(1)])
        xs = _dispatch(seg, slots, h2, n_blocks * bm)
        ys = _ffn(blk_expert, n_used.reshape(1), xs, w_gate[l], w_up[l], w_down[l])
        x = _combine(slots, ys, x1, rg, final_norm_g[None, :]).reshape(B, S, D)
    return x
```

```python
import functools

import jax
import jax.numpy as jnp
from jax import lax
from jax.experimental import pallas as pl
from jax.experimental.pallas import tpu as pltpu

F32 = jnp.float32
BF16 = jnp.bfloat16
I32 = jnp.int32

D_MODEL = 1024
CHUNK = 64
CONV_A_GROUP_DIM = 64
CONV_A_DIM = 512
CONV_A_WIDTH = 3
DN_HEADS = 4
DN_HEAD_DIM = 128
DN_DIM = 512
DN_CONV_WIDTH = 4
IN_PROJ_DIM = 3 * CONV_A_DIM + 4 * DN_DIM + 2 * DN_HEADS
N_GROUPS = 4
EXPERTS_PER_GROUP = 8
N_EXPERTS = 32
TOP_K = 2
EXPERT_FF = 512
EPS = 1e-6

LANES = 128
HALO = 8
IN_PROJ_PAD = 29 * LANES
GATE_COL = 3 * CONV_A_DIM + 4 * DN_DIM

TM_IN = 512
TM_PREP = 256
TM_SCAN = 512
TM_OUT = 512
FFN_BLOCK = 256
TM_SLOTS = 2048
TM_DISPATCH = 1024
TM_COMBINE = 256
ISSUE_UNROLL = 4
VMEM_LIMIT = 56 * 1024 * 1024


def _dot(a, b):
    return jnp.dot(a, b, preferred_element_type=F32)


def _dot_nt(a, b):
    return lax.dot_general(a, b, (((1,), (1,)), ((), ())), preferred_element_type=F32)


def _dot_tn(a, b):
    return lax.dot_general(a, b, (((0,), (0,)), ((), ())), preferred_element_type=F32)


def _split_bf16(x, parts):
    out = []
    for _ in range(parts):
        p = x.astype(BF16)
        out.append(p)
        x = x - p.astype(F32)
    return out


def _dot_split3(exact16, x):
    return sum(_dot(exact16, p) for p in _split_bf16(x, 3))


def _silu(x):
    return x * jax.nn.sigmoid(x)


ROW_SLAB = D_MODEL // LANES


def _store_rows(ref, val):
    m = val.shape[0]
    for c in range(ROW_SLAB):
        ref[pl.ds(c, m, stride=ROW_SLAB), :] = val[:, c * LANES:(c + 1) * LANES]


def _load_rows(ref, m):
    return jnp.concatenate([ref[pl.ds(c, m, stride=ROW_SLAB), :] for c in range(ROW_SLAB)], axis=-1)


def _slab(ref, row):
    return ref.at[pl.ds(pl.multiple_of(row * ROW_SLAB, ROW_SLAB), ROW_SLAB)]


def _slab_block(ref, block):
    n = FFN_BLOCK * ROW_SLAB
    return ref.at[pl.ds(pl.multiple_of(block * n, n), n)]


def _softplus(x):
    return jnp.maximum(x, 0.0) + jnp.log1p(jnp.exp(-jnp.abs(x)))


def _inproj_kernel(x_ref, g_ref, w_ref, caw_ref, cag_ref, dcw_ref, alog_ref, dtb_ref, gmat_ref,
                   ya_ref, q_ref, k_ref, v_ref, zg_ref, gcol_ref, grow_ref,
                   exta_ref, extq_ref):
    tm = x_ref.shape[1]

    @pl.when(pl.program_id(1) == 0)
    def _():
        exta_ref[0:HALO, :] = jnp.zeros((HALO, CONV_A_DIM), F32)
        extq_ref[0:HALO, :] = jnp.zeros((HALO, 3 * DN_DIM), F32)

    x = x_ref[0]
    ms = jnp.mean(x * x, axis=-1, keepdims=True)
    hb = (x * lax.rsqrt(ms + EPS) * g_ref[...]).astype(BF16)

    def proj(c0, width):
        return _dot(hb, w_ref[:, c0:c0 + width])

    a_h = proj(0, CONV_A_DIM)
    a_c = proj(2 * CONV_A_DIM, CONV_A_DIM)
    exta_ref[HALO:HALO + tm, :] = a_c * a_h
    conv = caw_ref[0:1, :] * exta_ref[pl.ds(HALO - 2, tm), :]
    conv = conv + caw_ref[1:2, :] * exta_ref[pl.ds(HALO - 1, tm), :]
    conv = conv + caw_ref[2:3, :] * exta_ref[pl.ds(HALO, tm), :]
    y = proj(CONV_A_DIM, CONV_A_DIM) * conv
    ysq = y * y
    hi = ysq.astype(BF16)
    lo = (ysq - hi.astype(F32)).astype(BF16)
    gmean = _dot(hi, gmat_ref[...]) + _dot(lo, gmat_ref[...])
    ya_ref[0] = (y * lax.rsqrt(gmean + EPS) * cag_ref[...]).astype(ya_ref.dtype)
    exta_ref[0:HALO, :] = exta_ref[tm:tm + HALO, :]

    base = 3 * CONV_A_DIM
    for i in range(3):
        extq_ref[HALO:HALO + tm, i * DN_DIM:(i + 1) * DN_DIM] = proj(base + i * DN_DIM, DN_DIM)
    for i, out_ref in enumerate((q_ref, k_ref, v_ref)):
        cols = slice(i * DN_DIM, (i + 1) * DN_DIM)
        acc = dcw_ref[0:1, cols] * extq_ref[pl.ds(HALO - 3, tm), cols]
        for j in range(1, DN_CONV_WIDTH):
            acc = acc + dcw_ref[j:j + 1, cols] * extq_ref[pl.ds(HALO - 3 + j, tm), cols]
        s = _silu(acc)
        if i == 2:
            out_ref[0] = s
        else:
            for h in range(DN_HEADS):
                sh = s[:, h * DN_HEAD_DIM:(h + 1) * DN_HEAD_DIM]
                inv = lax.rsqrt(jnp.sum(sh * sh, axis=-1, keepdims=True) + EPS)
                sh = sh * inv
                if i == 0:
                    sh = sh * (DN_HEAD_DIM ** -0.5)
                out_ref[0, :, h * DN_HEAD_DIM:(h + 1) * DN_HEAD_DIM] = sh
    extq_ref[0:HALO, :] = extq_ref[tm:tm + HALO, :]

    zg_ref[0] = _silu(proj(base + 3 * DN_DIM, DN_DIM))

    p = proj(GATE_COL, LANES)
    beta = jax.nn.sigmoid(p)
    g = -jnp.exp(alog_ref[...]) * _softplus(p + dtb_ref[...])
    r = lax.broadcasted_iota(I32, (tm, tm), 0)
    c = lax.broadcasted_iota(I32, (tm, tm), 1)
    tri = jnp.where((r // CHUNK == c // CHUNK) & (c <= r), 1.0, 0.0).astype(BF16)
    gc = _dot_split3(tri, g)
    lane = lax.broadcasted_iota(I32, (tm, LANES), 1)
    slab = jnp.where(lane < DN_HEADS, beta, gc)
    gcol_ref[0] = slab
    rows = slab.T[0:HALO, :]
    for ci in range(tm // CHUNK):
        grow_ref[0, ci] = rows[:, ci * CHUNK:(ci + 1) * CHUNK]


def _inproj(x, mix_g, w_in_pad, conv_a_w, conv_a_g, dn_conv_w, alog_vec, dtb_vec, gmat):
    B, S, D = x.shape
    tm = TM_IN
    full = lambda shape: pl.BlockSpec(shape, lambda b, t: (0,) * len(shape))
    row = lambda width: pl.BlockSpec((1, tm, width), lambda b, t: (b, t, 0))
    out_shape = (
        jax.ShapeDtypeStruct((B, S, CONV_A_DIM), BF16),
        jax.ShapeDtypeStruct((B, S, DN_DIM), F32),
        jax.ShapeDtypeStruct((B, S, DN_DIM), F32),
        jax.ShapeDtypeStruct((B, S, DN_DIM), F32),
        jax.ShapeDtypeStruct((B, S, DN_DIM), F32),
        jax.ShapeDtypeStruct((B, S, LANES), F32),
        jax.ShapeDtypeStruct((B, S // CHUNK, HALO, CHUNK), F32),
    )
    return pl.pallas_call(
        _inproj_kernel,
        grid=(B, S // tm),
        in_specs=[row(D), full((1, D)), full((D, IN_PROJ_PAD)), full((CONV_A_WIDTH, CONV_A_DIM)),
                  full((1, CONV_A_DIM)), full((DN_CONV_WIDTH, 3 * DN_DIM)), full((1, LANES)),
                  full((1, LANES)), full((CONV_A_DIM, CONV_A_DIM))],
        out_specs=(row(CONV_A_DIM), row(DN_DIM), row(DN_DIM), row(DN_DIM), row(DN_DIM), row(LANES),
                   pl.BlockSpec((1, tm // CHUNK, HALO, CHUNK), lambda b, t: (b, t, 0, 0))),
        out_shape=out_shape,
        scratch_shapes=[pltpu.VMEM((tm + HALO, CONV_A_DIM), F32),
                        pltpu.VMEM((tm + HALO, 3 * DN_DIM), F32)],
        compiler_params=pltpu.CompilerParams(
            dimension_semantics=("arbitrary", "arbitrary"), vmem_limit_bytes=VMEM_LIMIT),
        name="inproj",
    )(x, mix_g, w_in_pad, conv_a_w, conv_a_g, dn_conv_w, alog_vec, dtb_vec, gmat)


def _delta_prep_kernel(q_ref, k_ref, v_ref, gcol_ref, grow_ref,
                       u_ref, w_ref, qd_ref, kd_ref, at_ref):
    tm = q_ref.shape[1]
    ri = lax.broadcasted_iota(I32, (CHUNK, CHUNK), 0)
    ci = lax.broadcasted_iota(I32, (CHUNK, CHUNK), 1)
    causal = ci <= ri
    strict = ci < ri
    eye = jnp.where(ci == ri, 1.0, 0.0).astype(F32)
    lane = lax.broadcasted_iota(I32, (CHUNK, LANES), 1)
    chains = [(c, h) for c in range(tm // CHUNK) for h in range(DN_HEADS)]
    ms, ts, rhss = [], [], []
    for c, h in chains:
        rows = slice(c * CHUNK, (c + 1) * CHUNK)
        cols = slice(h * DN_HEAD_DIM, (h + 1) * DN_HEAD_DIM)
        slab = gcol_ref[0, rows, :]
        beta = jnp.sum(jnp.where(lane == h, slab, 0.0), axis=-1, keepdims=True)
        gcc = jnp.sum(jnp.where(lane == h + DN_HEADS, slab, 0.0), axis=-1, keepdims=True)
        gcr = grow_ref[0, c, h + DN_HEADS:h + DN_HEADS + 1, :]
        diff = gcc - gcr
        decay = jnp.where(causal, jnp.exp(jnp.where(causal, diff, 0.0)), 0.0)
        q = q_ref[0, rows, cols]
        k = k_ref[0, rows, cols]
        v = v_ref[0, rows, cols]
        kb = k * beta
        k16 = k.astype(BF16)
        L = jnp.where(strict, _dot_nt(kb.astype(BF16), k16) * decay, 0.0)
        intra = _dot_nt(q.astype(BF16), k16) * decay
        egc = jnp.exp(gcc)
        gl = gcr[:, CHUNK - 1:CHUNK]
        qd_ref[0, rows, cols] = (q * egc).astype(BF16)
        kd_ref[0, rows, cols] = (k * jnp.exp(gl - gcc)).astype(BF16)
        at_ref[0, rows, h * CHUNK:(h + 1) * CHUNK] = intra.astype(BF16)
        rhss.append(jnp.concatenate([v * beta, kb * egc], axis=-1).astype(BF16))
        ms.append(-L)
        ts.append(eye - L)
    for _ in range(5):
        m16s = [m.astype(BF16) for m in ms]
        ms = [_dot(m16, m16) for m16 in m16s]
        ts = [t + _dot(t.astype(BF16), m.astype(BF16)) for t, m in zip(ts, ms)]
    for (c, h), t, rhs in zip(chains, ts, rhss):
        rows = slice(c * CHUNK, (c + 1) * CHUNK)
        cols = slice(h * DN_HEAD_DIM, (h + 1) * DN_HEAD_DIM)
        uw = _dot(t.astype(BF16), rhs)
        u_ref[0, rows, cols] = uw[:, :DN_HEAD_DIM]
        w_ref[0, rows, cols] = uw[:, DN_HEAD_DIM:].astype(BF16)


def _delta_prep(q, k, v, gcol, grow):
    B, S, _ = q.shape
    tm = TM_PREP
    row = lambda width: pl.BlockSpec((1, tm, width), lambda b, t: (b, t, 0))
    return pl.pallas_call(
        _delta_prep_kernel,
        grid=(B, S // tm),
        in_specs=[row(DN_DIM), row(DN_DIM), row(DN_DIM), row(LANES),
                  pl.BlockSpec((1, tm // CHUNK, HALO, CHUNK), lambda b, t: (b, t, 0, 0))],
        out_specs=(row(DN_DIM), row(DN_DIM), row(DN_DIM), row(DN_DIM), row(DN_HEADS * CHUNK)),
        out_shape=(jax.ShapeDtypeStruct((B, S, DN_DIM), F32),
                   jax.ShapeDtypeStruct((B, S, DN_DIM), BF16),
                   jax.ShapeDtypeStruct((B, S, DN_DIM), BF16),
                   jax.ShapeDtypeStruct((B, S, DN_DIM), BF16),
                   jax.ShapeDtypeStruct((B, S, DN_HEADS * CHUNK), BF16)),
        compiler_params=pltpu.CompilerParams(
            dimension_semantics=("arbitrary", "arbitrary"), vmem_limit_bytes=VMEM_LIMIT),
        name="delta_prep",
    )(q, k, v, gcol, grow)


def _delta_scan_kernel(u_ref, w_ref, qd_ref, kd_ref, at_ref, zg_ref, grow_ref, ng_ref, o_ref, st_ref):
    tm = u_ref.shape[1]

    @pl.when(pl.program_id(1) == 0)
    def _():
        st_ref[...] = jnp.zeros(st_ref.shape, F32)

    def step(c, carry):
        r0 = pl.multiple_of(c * CHUNK, CHUNK)
        rows = pl.ds(r0, CHUNK)
        heads = range(DN_HEADS)
        cols = [slice(h * DN_HEAD_DIM, (h + 1) * DN_HEAD_DIM) for h in heads]
        sts = [st_ref[h] for h in heads]
        st16s = [st.astype(BF16) for st in sts]
        wss = [_dot(w_ref[0, rows, cols[h]], st16s[h]) for h in heads]
        qss = [_dot(qd_ref[0, rows, cols[h]], st16s[h]) for h in heads]
        vn16s = [(u_ref[0, rows, cols[h]] - wss[h]).astype(BF16) for h in heads]
        upd = [_dot_tn(kd_ref[0, rows, cols[h]], vn16s[h]) for h in heads]
        avs = [_dot(at_ref[0, rows, h * CHUNK:(h + 1) * CHUNK], vn16s[h]) for h in heads]
        for h in heads:
            gl = grow_ref[0, c, h + DN_HEADS:h + DN_HEADS + 1, CHUNK - 1:CHUNK]
            st_ref[h] = sts[h] * jnp.exp(gl) + upd[h]
        for h in heads:
            o = qss[h] + avs[h]
            on = o * lax.rsqrt(jnp.mean(o * o, axis=-1, keepdims=True) + EPS) * ng_ref[...]
            o_ref[0, rows, cols[h]] = (on * zg_ref[0, rows, cols[h]]).astype(o_ref.dtype)
        return carry

    lax.fori_loop(0, tm // CHUNK, step, 0)


def _delta_scan(u, w, qd, kd, at, zg, grow, norm_g):
    B, S, _ = u.shape
    tm = TM_SCAN
    row = lambda width: pl.BlockSpec((1, tm, width), lambda b, t: (b, t, 0))
    return pl.pallas_call(
        _delta_scan_kernel,
        grid=(B, S // tm),
        in_specs=[row(DN_DIM), row(DN_DIM), row(DN_DIM), row(DN_DIM), row(DN_HEADS * CHUNK), row(DN_DIM),
                  pl.BlockSpec((1, tm // CHUNK, HALO, CHUNK), lambda b, t: (b, t, 0, 0)),
                  pl.BlockSpec((1, DN_HEAD_DIM), lambda b, t: (0, 0))],
        out_specs=row(DN_DIM),
        out_shape=jax.ShapeDtypeStruct((B, S, DN_DIM), BF16),
        scratch_shapes=[pltpu.VMEM((DN_HEADS, DN_HEAD_DIM, DN_HEAD_DIM), F32)],
        compiler_params=pltpu.CompilerParams(
            dimension_semantics=("arbitrary", "arbitrary"), vmem_limit_bytes=VMEM_LIMIT),
        name="delta_scan",
    )(u, w, qd, kd, at, zg, grow, norm_g)


def _outproj_kernel(ya_ref, yb_ref, x_ref, wo_ref, g_ref, rw_ref,
                    x1_ref, h2_ref, ri_ref, rg_ref, cnt_ref, base_ref):
    tm = x_ref.shape[0]

    @pl.when(pl.program_id(0) == 0)
    def _():
        base_ref[...] = jnp.zeros(base_ref.shape, F32)

    y = _dot(jnp.concatenate([ya_ref[...], yb_ref[...]], axis=-1), wo_ref[...])
    x1 = x_ref[...] + y
    x1_ref[...] = x1
    h = x1 * lax.rsqrt(jnp.mean(x1 * x1, axis=-1, keepdims=True) + EPS) * g_ref[...]
    _store_rows(h2_ref, h)

    h_hi, h_lo = _split_bf16(h, 2)
    hi_prod = _dot(h_hi, rw_ref[...])
    logits = hi_prod[:, :LANES] + (hi_prod[:, LANES:] + _dot(h_lo, rw_ref[:, :LANES]))
    lane = lax.broadcasted_iota(I32, (tm, LANES), 1)
    lanef = lane.astype(F32)
    neg = jnp.float32(-1e30)
    big = jnp.float32(1e9)
    is_g = lane < N_GROUPS
    gl = jnp.where(is_g, logits, neg)
    gmax = jnp.max(gl, axis=-1, keepdims=True)
    gidx = jnp.min(jnp.where(gl == gmax, lanef, big), axis=-1, keepdims=True)
    gsum = jnp.sum(jnp.where(is_g, jnp.exp(gl - gmax), 0.0), axis=-1, keepdims=True)
    gprob = 1.0 / gsum
    lo = N_GROUPS + EXPERTS_PER_GROUP * gidx
    emask = (lanef >= lo) & (lanef < lo + EXPERTS_PER_GROUP)
    el = jnp.where(emask, logits, neg)
    e1 = jnp.max(el, axis=-1, keepdims=True)
    i1 = jnp.min(jnp.where(el == e1, lanef, big), axis=-1, keepdims=True)
    el2 = jnp.where(lanef == i1, neg, el)
    e2 = jnp.max(el2, axis=-1, keepdims=True)
    i2 = jnp.min(jnp.where(el2 == e2, lanef, big), axis=-1, keepdims=True)
    r = jnp.exp(e2 - e1)
    gate1 = gprob / (1.0 + r)
    gate2 = gprob * r / (1.0 + r)
    id1 = i1 - N_GROUPS
    id2 = i2 - N_GROUPS

    oh1 = jnp.where(lanef == id1, 1.0, 0.0).astype(F32)
    oh2 = jnp.where(lanef == id2, 1.0, 0.0).astype(F32)
    oh = oh1 + oh2
    rr = lax.broadcasted_iota(I32, (tm, tm), 0)
    cc = lax.broadcasted_iota(I32, (tm, tm), 1)
    tri = jnp.where(cc < rr, 1.0, 0.0).astype(BF16)
    before = _dot(tri, oh.astype(BF16)) + base_ref[...]
    rank1 = jnp.sum(oh1 * before, axis=-1, keepdims=True)
    rank2 = jnp.sum(oh2 * before, axis=-1, keepdims=True)
    base_ref[...] = base_ref[...] + jnp.sum(oh, axis=0, keepdims=True)
    cnt_ref[...] = base_ref[...]

    ri = jnp.where(lane == 0, id1, jnp.where(lane == 1, id2,
                   jnp.where(lane == 2, rank1, jnp.where(lane == 3, rank2, 0.0))))
    ri_ref[...] = ri.astype(I32)
    rg_ref[...] = jnp.where(lane == 0, gate1, jnp.where(lane == 1, gate2, 0.0))


def _outproj(ya, yb, x, w_out16, ffn_g, rw):
    N, D = x.shape
    tm = TM_OUT
    row = lambda width: pl.BlockSpec((tm, width), lambda i: (i, 0))
    full = lambda shape: pl.BlockSpec(shape, lambda i: (0,) * len(shape))
    return pl.pallas_call(
        _outproj_kernel,
        grid=(N // tm,),
        in_specs=[row(CONV_A_DIM), row(DN_DIM), row(D), full((D, D)), full((1, D)), full((D, 2 * LANES))],
        out_specs=(row(D), pl.BlockSpec((tm * ROW_SLAB, LANES), lambda i: (i, 0)), row(LANES), row(LANES),
                   full((1, LANES))),
        out_shape=(jax.ShapeDtypeStruct((N, D), F32),
                   jax.ShapeDtypeStruct((N * ROW_SLAB, LANES), F32),
                   jax.ShapeDtypeStruct((N, LANES), I32),
                   jax.ShapeDtypeStruct((N, LANES), F32),
                   jax.ShapeDtypeStruct((1, LANES), F32)),
        scratch_shapes=[pltpu.VMEM((1, LANES), F32)],
        compiler_params=pltpu.CompilerParams(
            dimension_semantics=("arbitrary",), vmem_limit_bytes=VMEM_LIMIT),
        name="outproj",
    )(ya, yb, x, w_out16, ffn_g, rw)


def _slots_kernel(ri_ref, start_ref, o_ref):
    rif = ri_ref[...].astype(F32)
    lane = lax.broadcasted_iota(I32, rif.shape, 1)
    lanef = lane.astype(F32)

    def slot(k):
        start = jnp.sum(jnp.where(lanef == rif[:, k:k + 1], start_ref[...], 0.0), axis=-1, keepdims=True)
        return start + rif[:, TOP_K + k:TOP_K + k + 1]

    o_ref[...] = jnp.where(lane == 0, slot(0), jnp.where(lane == 1, slot(1), 0.0)).astype(I32)


def _slots(ri, seg_start):
    N = ri.shape[0]
    tm = TM_SLOTS
    return pl.pallas_call(
        _slots_kernel,
        grid=(N // tm,),
        in_specs=[pl.BlockSpec((tm, LANES), lambda i: (i, 0)), pl.BlockSpec((1, LANES), lambda i: (0, 0))],
        out_specs=pl.BlockSpec((tm, LANES), lambda i: (i, 0)),
        out_shape=jax.ShapeDtypeStruct((N, LANES), I32),
        compiler_params=pltpu.CompilerParams(dimension_semantics=("arbitrary",)),
        name="slots",
    )(ri, _lane_vec(seg_start, 0))


def _slot_of(slots_ref, t, k):
    return slots_ref[TOP_K * t + k]


def _dispatch_kernel(seg_ref, slots_ref, h_ref, xs_ref, zero_ref, sem, zsem):
    tm = TM_DISPATCH
    bm = FFN_BLOCK
    n_blocks = xs_ref.shape[0] // (bm * ROW_SLAB)

    @pl.when(pl.program_id(0) == 0)
    def _():
        zero_ref[...] = jnp.zeros(zero_ref.shape, F32)

        def pad_copy(d):
            return pltpu.make_async_copy(zero_ref.at[pl.ds(0, ROW_SLAB)], _slab(xs_ref, d), zsem)

        def block_copy(b):
            return pltpu.make_async_copy(zero_ref, _slab_block(xs_ref, b), zsem)

        def each_pad(fn):
            def per_expert(e, carry):
                def per_row(d, c):
                    fn(pad_copy(d))
                    return c
                return lax.fori_loop(seg_ref[e], seg_ref[N_EXPERTS + e], per_row, carry)
            lax.fori_loop(0, N_EXPERTS, per_expert, 0)

            def per_block(b, c):
                fn(block_copy(b))
                return c
            lax.fori_loop(seg_ref[2 * N_EXPERTS], n_blocks, per_block, 0)

        each_pad(lambda cp: cp.start())
        each_pad(lambda cp: cp.wait())

    def issue(t, carry):
        for k in range(TOP_K):
            d = _slot_of(slots_ref, t, k)
            pltpu.make_async_copy(_slab(h_ref, t), _slab(xs_ref, d), sem).start(priority=k)
        return carry

    lax.fori_loop(0, tm, issue, 0, unroll=ISSUE_UNROLL)
    for _ in range(TOP_K):
        pltpu.make_async_copy(h_ref, xs_ref.at[pl.ds(0, tm * ROW_SLAB)], sem).wait()


def _dispatch(seg, slots, h2, n_slots):
    N = h2.shape[0] // ROW_SLAB
    tm = TM_DISPATCH
    grid_spec = pltpu.PrefetchScalarGridSpec(
        num_scalar_prefetch=1,
        grid=(N // tm,),
        in_specs=[pl.BlockSpec((TOP_K * tm,), lambda i, s: (i,), memory_space=pltpu.SMEM),
                  pl.BlockSpec((tm * ROW_SLAB, LANES), lambda i, s: (i, 0))],
        out_specs=pl.BlockSpec(memory_space=pl.ANY),
        scratch_shapes=[pltpu.VMEM((FFN_BLOCK * ROW_SLAB, LANES), F32),
                        pltpu.SemaphoreType.DMA(()), pltpu.SemaphoreType.DMA(())],
    )
    return pl.pallas_call(
        _dispatch_kernel,
        grid_spec=grid_spec,
        out_shape=jax.ShapeDtypeStruct((n_slots * ROW_SLAB, LANES), F32),
        compiler_params=pltpu.CompilerParams(dimension_semantics=("arbitrary",)),
        name="dispatch",
    )(seg, slots, h2)


def _ffn_kernel(blk_ref, xs_ref, wg_ref, wu_ref, wd_ref, ys_ref,
                xbuf, ybuf, wg16, wu16, wd16, xsem, ysem, zsem):
    e = pl.program_id(0)
    bm = FFN_BLOCK
    n_blocks = ys_ref.shape[0] // (bm * ROW_SLAB)
    first = blk_ref[e]
    count = blk_ref[N_EXPERTS + e]
    n_used = blk_ref[2 * N_EXPERTS]

    def x_copy(g):
        return pltpu.make_async_copy(_slab_block(xs_ref, g), xbuf.at[g % 2], xsem.at[g % 2])

    def y_copy(g):
        return pltpu.make_async_copy(ybuf.at[g % 2], _slab_block(ys_ref, g), ysem.at[g % 2])

    @pl.when(e == 0)
    def _():
        x_copy(0).start()

    @pl.when(count > 0)
    def _():
        wg16[...] = wg_ref[0].astype(BF16)
        wu16[...] = wu_ref[0].astype(BF16)
        wd16[...] = wd_ref[0].astype(BF16)

    def block(j, carry):
        g = first + j
        x_copy(g).wait()

        @pl.when(g + 1 < n_used)
        def _():
            x_copy(g + 1).start()

        x = _load_rows(xbuf.at[g % 2], bm).astype(BF16)
        a = _dot(x, wg16[...])
        b = _dot(x, wu16[...])
        y = _dot((_silu(a) * b).astype(BF16), wd16[...])

        @pl.when(g >= 2)
        def _():
            y_copy(g - 2).wait()

        _store_rows(ybuf.at[g % 2], y)
        y_copy(g).start()
        return carry

    lax.fori_loop(0, count, block, 0)

    @pl.when(e == pl.num_programs(0) - 1)
    def _():
        @pl.when(n_used >= 2)
        def _():
            y_copy(n_used - 2).wait()
        y_copy(n_used - 1).wait()
        ybuf[0] = jnp.zeros(ybuf.shape[1:], F32)

        def zero_copy(g):
            return pltpu.make_async_copy(ybuf.at[0], _slab_block(ys_ref, g), zsem)

        def start(g, c):
            zero_copy(g).start()
            return c

        def wait(g, c):
            zero_copy(g).wait()
            return c

        lax.fori_loop(n_used, n_blocks, start, 0)
        lax.fori_loop(n_used, n_blocks, wait, 0)


def _ffn(blk, xs, w_gate, w_up, w_down):
    bm = FFN_BLOCK
    D = D_MODEL
    weights = lambda shape: pl.BlockSpec((1,) + shape, lambda e, blk: (e, 0, 0))
    grid_spec = pltpu.PrefetchScalarGridSpec(
        num_scalar_prefetch=1,
        grid=(N_EXPERTS,),
        in_specs=[pl.BlockSpec(memory_space=pl.ANY),
                  weights((D, EXPERT_FF)), weights((D, EXPERT_FF)), weights((EXPERT_FF, D))],
        out_specs=pl.BlockSpec(memory_space=pl.ANY),
        scratch_shapes=[pltpu.VMEM((2, bm * ROW_SLAB, LANES), F32), pltpu.VMEM((2, bm * ROW_SLAB, LANES), F32),
                        pltpu.VMEM((D, EXPERT_FF), BF16), pltpu.VMEM((D, EXPERT_FF), BF16),
                        pltpu.VMEM((EXPERT_FF, D), BF16),
                        pltpu.SemaphoreType.DMA((2,)), pltpu.SemaphoreType.DMA((2,)),
                        pltpu.SemaphoreType.DMA(())],
    )
    return pl.pallas_call(
        _ffn_kernel,
        grid_spec=grid_spec,
        out_shape=jax.ShapeDtypeStruct(xs.shape, F32),
        compiler_params=pltpu.CompilerParams(
            dimension_semantics=("arbitrary",), vmem_limit_bytes=VMEM_LIMIT),
        name="ffn",
    )(blk, xs, w_gate, w_up, w_down)


def _combine_kernel(slots_ref, slots_next_ref, ys_ref, x1_ref, rg_ref, g_ref, o_ref, buf_ref, sems):
    tm = TM_COMBINE
    i = pl.program_id(0)
    half = i % 2

    def gather(r_ref, s):
        def issue(t, carry):
            for k in range(TOP_K):
                d = _slot_of(r_ref, t, k)
                pltpu.make_async_copy(_slab(ys_ref, d), _slab(buf_ref.at[s, k], t),
                                      sems.at[s]).start(priority=k)
            return carry
        lax.fori_loop(0, tm, issue, 0, unroll=ISSUE_UNROLL)

    @pl.when(i == 0)
    def _():
        gather(slots_ref, 0)

    @pl.when(i + 1 < pl.num_programs(0))
    def _():
        gather(slots_next_ref, 1 - half)

    for k in range(TOP_K):
        pltpu.make_async_copy(ys_ref.at[pl.ds(0, tm * ROW_SLAB)], buf_ref.at[half, k], sems.at[half]).wait()

    rg = rg_ref[...]
    moe = (_load_rows(buf_ref.at[half, 0], tm) * rg[:, 0:1]
           + _load_rows(buf_ref.at[half, 1], tm) * rg[:, 1:2])
    x2 = x1_ref[...] + moe
    o_ref[...] = x2 * lax.rsqrt(jnp.mean(x2 * x2, axis=-1, keepdims=True) + EPS) * g_ref[...]


def _combine(slots, ys, x1, rg, final_g):
    N, D = x1.shape
    tm = TM_COMBINE
    n_tiles = N // tm
    return pl.pallas_call(
        _combine_kernel,
        grid=(n_tiles,),
        in_specs=[pl.BlockSpec((TOP_K * tm,), lambda i: (i,), memory_space=pltpu.SMEM),
                  pl.BlockSpec((TOP_K * tm,), lambda i: (jnp.minimum(i + 1, n_tiles - 1),),
                               memory_space=pltpu.SMEM),
                  pl.BlockSpec(memory_space=pl.ANY),
                  pl.BlockSpec((tm, D), lambda i: (i, 0)),
                  pl.BlockSpec((tm, LANES), lambda i: (i, 0)),
                  pl.BlockSpec((1, D), lambda i: (0, 0))],
        out_specs=pl.BlockSpec((tm, D), lambda i: (i, 0)),
        out_shape=jax.ShapeDtypeStruct((N, D), F32),
        scratch_shapes=[pltpu.VMEM((2, TOP_K, tm * ROW_SLAB, LANES), F32), pltpu.SemaphoreType.DMA((2,))],
        compiler_params=pltpu.CompilerParams(
            dimension_semantics=("arbitrary",), vmem_limit_bytes=VMEM_LIMIT),
        name="combine",
    )(slots, slots, ys, x1, rg, final_g)


def _lane_vec(values, offset):
    return jnp.zeros((1, LANES), F32).at[0, offset:offset + values.shape[0]].set(values.astype(F32))


def kernel(x, mix_norm_g, w_in, conv_a_w, conv_a_norm_g, dn_conv_w, dn_a_log, dn_dt_bias, dn_norm_g,
           w_out, ffn_norm_g, router_group_w, router_expert_w, w_gate, w_up, w_down, final_norm_g):
    B, S, D = x.shape
    N = B * S
    depth = w_in.shape[0]
    assert depth == 1, "single-layer block: the final RMSNorm is fused into the layer's combine step"
    group_of = jnp.arange(CONV_A_DIM, dtype=I32) // CONV_A_GROUP_DIM
    gmat = jnp.where(group_of[:, None] == group_of[None, :], 1.0 / CONV_A_GROUP_DIM, 0.0).astype(BF16)
    bm = FFN_BLOCK
    n_blocks = (N * TOP_K) // bm + N_EXPERTS
    for l in range(depth):
        w_in_pad = jnp.pad(w_in[l], ((0, 0), (0, IN_PROJ_PAD - IN_PROJ_DIM))).astype(BF16)
        ya, q, k, v, zg, gcol, grow = _inproj(
            x, mix_norm_g[l][None, :], w_in_pad, conv_a_w[l], conv_a_norm_g[l][None, :], dn_conv_w[l],
            _lane_vec(dn_a_log[l], DN_HEADS), _lane_vec(dn_dt_bias[l], DN_HEADS), gmat)
        u, w, qd, kd, at = _delta_prep(q, k, v, gcol, grow)
        yb = _delta_scan(u, w, qd, kd, at, zg, grow, dn_norm_g[l][None, :])
        rw = jnp.pad(jnp.concatenate([router_group_w[l], router_expert_w[l]], axis=1),
                     ((0, 0), (0, LANES - N_GROUPS - N_EXPERTS)))
        rw_hi = rw.astype(BF16)
        rw = jnp.concatenate([rw_hi, (rw - rw_hi.astype(F32)).astype(BF16)], axis=1)
        x1, h2, ri, rg, cnt = _outproj(ya.reshape(N, CONV_A_DIM), yb.reshape(N, DN_DIM), x.reshape(N, D),
                                       w_out[l].astype(BF16), ffn_norm_g[l][None, :], rw)
        counts = cnt[0, :N_EXPERTS].astype(I32)
        padded = (counts + bm - 1) // bm * bm
        seg_end = jnp.cumsum(padded).astype(I32)
        seg_start = seg_end - padded
        n_used = (seg_end[-1:] // bm)
        slots = _slots(ri, seg_start)[:, :TOP_K].reshape(-1)
        seg = jnp.concatenate([seg_start + counts, seg_end, n_used])
        xs = _dispatch(seg, slots, h2, n_blocks * bm)
        blk = jnp.concatenate([seg_start // bm, padded // bm, n_used])
        ys = _ffn(blk, xs, w_gate[l], w_up[l], w_down[l])
        x = _combine(slots, ys, x1, rg, final_norm_g[None, :]).reshape(B, S, D)
    return x
```

```python
import functools

import jax
import jax.numpy as jnp
from jax import lax
from jax.experimental import pallas as pl
from jax.experimental.pallas import tpu as pltpu

F32 = jnp.float32
BF16 = jnp.bfloat16
I32 = jnp.int32

D_MODEL = 1024
CHUNK = 64
CONV_A_GROUP_DIM = 64
CONV_A_DIM = 512
CONV_A_WIDTH = 3
DN_HEADS = 4
DN_HEAD_DIM = 128
DN_DIM = 512
DN_CONV_WIDTH = 4
IN_PROJ_DIM = 3 * CONV_A_DIM + 4 * DN_DIM + 2 * DN_HEADS
N_GROUPS = 4
EXPERTS_PER_GROUP = 8
N_EXPERTS = 32
TOP_K = 2
EXPERT_FF = 512
EPS = 1e-6

LANES = 128
HALO = 8
IN_PROJ_PAD = 29 * LANES
GATE_COL = 3 * CONV_A_DIM + 4 * DN_DIM

TM_IN = 512
TM_PREP = 256
TM_SCAN = 512
TM_OUT = 512
FFN_BLOCK = 256
TM_SLOTS = 2048
TM_DISPATCH = 1024
TM_COMBINE = 256
FFN_RING = 4
RING_PRIORITY = 1
ISSUE_UNROLL = 4
VMEM_LIMIT = 56 * 1024 * 1024


def _dot(a, b):
    return jnp.dot(a, b, preferred_element_type=F32)


def _dot_nt(a, b):
    return lax.dot_general(a, b, (((1,), (1,)), ((), ())), preferred_element_type=F32)


def _dot_tn(a, b):
    return lax.dot_general(a, b, (((0,), (0,)), ((), ())), preferred_element_type=F32)


def _split_bf16(x, parts):
    out = []
    for _ in range(parts):
        p = x.astype(BF16)
        out.append(p)
        x = x - p.astype(F32)
    return out


def _dot_split3(exact16, x):
    return sum(_dot(exact16, p) for p in _split_bf16(x, 3))


def _silu(x):
    return x * jax.nn.sigmoid(x)


ROW_SLAB = D_MODEL // LANES


def _store_rows(ref, val):
    m = val.shape[0]
    for c in range(ROW_SLAB):
        ref[pl.ds(c, m, stride=ROW_SLAB), :] = val[:, c * LANES:(c + 1) * LANES]


def _load_rows(ref, m):
    return jnp.concatenate([ref[pl.ds(c, m, stride=ROW_SLAB), :] for c in range(ROW_SLAB)], axis=-1)


def _slab(ref, row):
    return ref.at[pl.ds(pl.multiple_of(row * ROW_SLAB, ROW_SLAB), ROW_SLAB)]


def _slab_block(ref, block):
    n = FFN_BLOCK * ROW_SLAB
    return ref.at[pl.ds(pl.multiple_of(block * n, n), n)]


def _softplus(x):
    return jnp.maximum(x, 0.0) + jnp.log1p(jnp.exp(-jnp.abs(x)))


def _inproj_kernel(x_ref, g_ref, w_ref, caw_ref, cag_ref, dcw_ref, alog_ref, dtb_ref, gmat_ref,
                   ya_ref, q_ref, k_ref, v_ref, zg_ref, gcol_ref, grow_ref,
                   exta_ref, extq_ref):
    tm = x_ref.shape[1]

    @pl.when(pl.program_id(1) == 0)
    def _():
        exta_ref[0:HALO, :] = jnp.zeros((HALO, CONV_A_DIM), F32)
        extq_ref[0:HALO, :] = jnp.zeros((HALO, 3 * DN_DIM), F32)

    x = x_ref[0]
    ms = jnp.mean(x * x, axis=-1, keepdims=True)
    hb = (x * lax.rsqrt(ms + EPS) * g_ref[...]).astype(BF16)

    def proj(c0, width):
        return _dot(hb, w_ref[:, c0:c0 + width])

    a_h = proj(0, CONV_A_DIM)
    a_c = proj(2 * CONV_A_DIM, CONV_A_DIM)
    exta_ref[HALO:HALO + tm, :] = a_c * a_h
    conv = caw_ref[0:1, :] * exta_ref[pl.ds(HALO - 2, tm), :]
    conv = conv + caw_ref[1:2, :] * exta_ref[pl.ds(HALO - 1, tm), :]
    conv = conv + caw_ref[2:3, :] * exta_ref[pl.ds(HALO, tm), :]
    y = proj(CONV_A_DIM, CONV_A_DIM) * conv
    ysq = y * y
    hi = ysq.astype(BF16)
    lo = (ysq - hi.astype(F32)).astype(BF16)
    gmean = _dot(hi, gmat_ref[...]) + _dot(lo, gmat_ref[...])
    ya_ref[0] = (y * lax.rsqrt(gmean + EPS) * cag_ref[...]).astype(ya_ref.dtype)
    exta_ref[0:HALO, :] = exta_ref[tm:tm + HALO, :]

    base = 3 * CONV_A_DIM
    for i in range(3):
        extq_ref[HALO:HALO + tm, i * DN_DIM:(i + 1) * DN_DIM] = proj(base + i * DN_DIM, DN_DIM)
    for i, out_ref in enumerate((q_ref, k_ref, v_ref)):
        cols = slice(i * DN_DIM, (i + 1) * DN_DIM)
        acc = dcw_ref[0:1, cols] * extq_ref[pl.ds(HALO - 3, tm), cols]
        for j in range(1, DN_CONV_WIDTH):
            acc = acc + dcw_ref[j:j + 1, cols] * extq_ref[pl.ds(HALO - 3 + j, tm), cols]
        s = _silu(acc)
        if i == 2:
            out_ref[0] = s
        else:
            for h in range(DN_HEADS):
                sh = s[:, h * DN_HEAD_DIM:(h + 1) * DN_HEAD_DIM]
                inv = lax.rsqrt(jnp.sum(sh * sh, axis=-1, keepdims=True) + EPS)
                sh = sh * inv
                if i == 0:
                    sh = sh * (DN_HEAD_DIM ** -0.5)
                out_ref[0, :, h * DN_HEAD_DIM:(h + 1) * DN_HEAD_DIM] = sh
    extq_ref[0:HALO, :] = extq_ref[tm:tm + HALO, :]

    zg_ref[0] = _silu(proj(base + 3 * DN_DIM, DN_DIM))

    p = proj(GATE_COL, LANES)
    beta = jax.nn.sigmoid(p)
    g = -jnp.exp(alog_ref[...]) * _softplus(p + dtb_ref[...])
    r = lax.broadcasted_iota(I32, (tm, tm), 0)
    c = lax.broadcasted_iota(I32, (tm, tm), 1)
    tri = jnp.where((r // CHUNK == c // CHUNK) & (c <= r), 1.0, 0.0).astype(BF16)
    gc = _dot_split3(tri, g)
    lane = lax.broadcasted_iota(I32, (tm, LANES), 1)
    slab = jnp.where(lane < DN_HEADS, beta, gc)
    gcol_ref[0] = slab
    rows = slab.T[0:HALO, :]
    for ci in range(tm // CHUNK):
        grow_ref[0, ci] = rows[:, ci * CHUNK:(ci + 1) * CHUNK]


def _inproj(x, mix_g, w_in_pad, conv_a_w, conv_a_g, dn_conv_w, alog_vec, dtb_vec, gmat):
    B, S, D = x.shape
    tm = TM_IN
    full = lambda shape: pl.BlockSpec(shape, lambda b, t: (0,) * len(shape))
    row = lambda width: pl.BlockSpec((1, tm, width), lambda b, t: (b, t, 0))
    out_shape = (
        jax.ShapeDtypeStruct((B, S, CONV_A_DIM), BF16),
        jax.ShapeDtypeStruct((B, S, DN_DIM), F32),
        jax.ShapeDtypeStruct((B, S, DN_DIM), F32),
        jax.ShapeDtypeStruct((B, S, DN_DIM), F32),
        jax.ShapeDtypeStruct((B, S, DN_DIM), F32),
        jax.ShapeDtypeStruct((B, S, LANES), F32),
        jax.ShapeDtypeStruct((B, S // CHUNK, HALO, CHUNK), F32),
    )
    return pl.pallas_call(
        _inproj_kernel,
        grid=(B, S // tm),
        in_specs=[row(D), full((1, D)), full((D, IN_PROJ_PAD)), full((CONV_A_WIDTH, CONV_A_DIM)),
                  full((1, CONV_A_DIM)), full((DN_CONV_WIDTH, 3 * DN_DIM)), full((1, LANES)),
                  full((1, LANES)), full((CONV_A_DIM, CONV_A_DIM))],
        out_specs=(row(CONV_A_DIM), row(DN_DIM), row(DN_DIM), row(DN_DIM), row(DN_DIM), row(LANES),
                   pl.BlockSpec((1, tm // CHUNK, HALO, CHUNK), lambda b, t: (b, t, 0, 0))),
        out_shape=out_shape,
        scratch_shapes=[pltpu.VMEM((tm + HALO, CONV_A_DIM), F32),
                        pltpu.VMEM((tm + HALO, 3 * DN_DIM), F32)],
        compiler_params=pltpu.CompilerParams(
            dimension_semantics=("arbitrary", "arbitrary"), vmem_limit_bytes=VMEM_LIMIT),
        name="inproj",
    )(x, mix_g, w_in_pad, conv_a_w, conv_a_g, dn_conv_w, alog_vec, dtb_vec, gmat)


def _delta_prep_kernel(q_ref, k_ref, v_ref, gcol_ref, grow_ref,
                       u_ref, w_ref, qd_ref, kd_ref, at_ref):
    tm = q_ref.shape[1]
    ri = lax.broadcasted_iota(I32, (CHUNK, CHUNK), 0)
    ci = lax.broadcasted_iota(I32, (CHUNK, CHUNK), 1)
    causal = ci <= ri
    strict = ci < ri
    eye = jnp.where(ci == ri, 1.0, 0.0).astype(F32)
    lane = lax.broadcasted_iota(I32, (CHUNK, LANES), 1)
    chains = [(c, h) for c in range(tm // CHUNK) for h in range(DN_HEADS)]
    ms, ts, rhss = [], [], []
    for c, h in chains:
        rows = slice(c * CHUNK, (c + 1) * CHUNK)
        cols = slice(h * DN_HEAD_DIM, (h + 1) * DN_HEAD_DIM)
        slab = gcol_ref[0, rows, :]
        beta = jnp.sum(jnp.where(lane == h, slab, 0.0), axis=-1, keepdims=True)
        gcc = jnp.sum(jnp.where(lane == h + DN_HEADS, slab, 0.0), axis=-1, keepdims=True)
        gcr = grow_ref[0, c, h + DN_HEADS:h + DN_HEADS + 1, :]
        diff = gcc - gcr
        decay = jnp.where(causal, jnp.exp(jnp.where(causal, diff, 0.0)), 0.0)
        q = q_ref[0, rows, cols]
        k = k_ref[0, rows, cols]
        v = v_ref[0, rows, cols]
        kb = k * beta
        k16 = k.astype(BF16)
        L = jnp.where(strict, _dot_nt(kb.astype(BF16), k16) * decay, 0.0)
        intra = _dot_nt(q.astype(BF16), k16) * decay
        egc = jnp.exp(gcc)
        gl = gcr[:, CHUNK - 1:CHUNK]
        qd_ref[0, rows, cols] = (q * egc).astype(BF16)
        kd_ref[0, rows, cols] = (k * jnp.exp(gl - gcc)).astype(BF16)
        at_ref[0, rows, h * CHUNK:(h + 1) * CHUNK] = intra.astype(BF16)
        rhss.append(jnp.concatenate([v * beta, kb * egc], axis=-1).astype(BF16))
        ms.append(-L)
        ts.append(eye - L)
    for _ in range(5):
        m16s = [m.astype(BF16) for m in ms]
        ms = [_dot(m16, m16) for m16 in m16s]
        ts = [t + _dot(t.astype(BF16), m.astype(BF16)) for t, m in zip(ts, ms)]
    for (c, h), t, rhs in zip(chains, ts, rhss):
        rows = slice(c * CHUNK, (c + 1) * CHUNK)
        cols = slice(h * DN_HEAD_DIM, (h + 1) * DN_HEAD_DIM)
        uw = _dot(t.astype(BF16), rhs)
        u_ref[0, rows, cols] = uw[:, :DN_HEAD_DIM]
        w_ref[0, rows, cols] = uw[:, DN_HEAD_DIM:].astype(BF16)


def _delta_prep(q, k, v, gcol, grow):
    B, S, _ = q.shape
    tm = TM_PREP
    row = lambda width: pl.BlockSpec((1, tm, width), lambda b, t: (b, t, 0))
    return pl.pallas_call(
        _delta_prep_kernel,
        grid=(B, S // tm),
        in_specs=[row(DN_DIM), row(DN_DIM), row(DN_DIM), row(LANES),
                  pl.BlockSpec((1, tm // CHUNK, HALO, CHUNK), lambda b, t: (b, t, 0, 0))],
        out_specs=(row(DN_DIM), row(DN_DIM), row(DN_DIM), row(DN_DIM), row(DN_HEADS * CHUNK)),
        out_shape=(jax.ShapeDtypeStruct((B, S, DN_DIM), F32),
                   jax.ShapeDtypeStruct((B, S, DN_DIM), BF16),
                   jax.ShapeDtypeStruct((B, S, DN_DIM), BF16),
                   jax.ShapeDtypeStruct((B, S, DN_DIM), BF16),
                   jax.ShapeDtypeStruct((B, S, DN_HEADS * CHUNK), BF16)),
        compiler_params=pltpu.CompilerParams(
            dimension_semantics=("arbitrary", "arbitrary"), vmem_limit_bytes=VMEM_LIMIT),
        name="delta_prep",
    )(q, k, v, gcol, grow)


def _delta_scan_kernel(u_ref, w_ref, qd_ref, kd_ref, at_ref, zg_ref, grow_ref, ng_ref, o_ref, st_ref):
    tm = u_ref.shape[1]

    @pl.when(pl.program_id(1) == 0)
    def _():
        st_ref[...] = jnp.zeros(st_ref.shape, F32)

    def step(c, carry):
        r0 = pl.multiple_of(c * CHUNK, CHUNK)
        rows = pl.ds(r0, CHUNK)
        heads = range(DN_HEADS)
        cols = [slice(h * DN_HEAD_DIM, (h + 1) * DN_HEAD_DIM) for h in heads]
        sts = [st_ref[h] for h in heads]
        st16s = [st.astype(BF16) for st in sts]
        wss = [_dot(w_ref[0, rows, cols[h]], st16s[h]) for h in heads]
        qss = [_dot(qd_ref[0, rows, cols[h]], st16s[h]) for h in heads]
        vn16s = [(u_ref[0, rows, cols[h]] - wss[h]).astype(BF16) for h in heads]
        upd = [_dot_tn(kd_ref[0, rows, cols[h]], vn16s[h]) for h in heads]
        avs = [_dot(at_ref[0, rows, h * CHUNK:(h + 1) * CHUNK], vn16s[h]) for h in heads]
        for h in heads:
            gl = grow_ref[0, c, h + DN_HEADS:h + DN_HEADS + 1, CHUNK - 1:CHUNK]
            st_ref[h] = sts[h] * jnp.exp(gl) + upd[h]
        for h in heads:
            o = qss[h] + avs[h]
            on = o * lax.rsqrt(jnp.mean(o * o, axis=-1, keepdims=True) + EPS) * ng_ref[...]
            o_ref[0, rows, cols[h]] = (on * zg_ref[0, rows, cols[h]]).astype(o_ref.dtype)
        return carry

    lax.fori_loop(0, tm // CHUNK, step, 0)


def _delta_scan(u, w, qd, kd, at, zg, grow, norm_g):
    B, S, _ = u.shape
    tm = TM_SCAN
    row = lambda width: pl.BlockSpec((1, tm, width), lambda b, t: (b, t, 0))
    return pl.pallas_call(
        _delta_scan_kernel,
        grid=(B, S // tm),
        in_specs=[row(DN_DIM), row(DN_DIM), row(DN_DIM), row(DN_DIM), row(DN_HEADS * CHUNK), row(DN_DIM),
                  pl.BlockSpec((1, tm // CHUNK, HALO, CHUNK), lambda b, t: (b, t, 0, 0)),
                  pl.BlockSpec((1, DN_HEAD_DIM), lambda b, t: (0, 0))],
        out_specs=row(DN_DIM),
        out_shape=jax.ShapeDtypeStruct((B, S, DN_DIM), BF16),
        scratch_shapes=[pltpu.VMEM((DN_HEADS, DN_HEAD_DIM, DN_HEAD_DIM), F32)],
        compiler_params=pltpu.CompilerParams(
            dimension_semantics=("arbitrary", "arbitrary"), vmem_limit_bytes=VMEM_LIMIT),
        name="delta_scan",
    )(u, w, qd, kd, at, zg, grow, norm_g)


def _outproj_kernel(ya_ref, yb_ref, x_ref, wo_ref, g_ref, rw_ref,
                    x1_ref, h2_ref, ri_ref, rg_ref, cnt_ref, base_ref):
    tm = x_ref.shape[0]

    @pl.when(pl.program_id(0) == 0)
    def _():
        base_ref[...] = jnp.zeros(base_ref.shape, F32)

    y = _dot(jnp.concatenate([ya_ref[...], yb_ref[...]], axis=-1), wo_ref[...])
    x1 = x_ref[...] + y
    x1_ref[...] = x1
    h = x1 * lax.rsqrt(jnp.mean(x1 * x1, axis=-1, keepdims=True) + EPS) * g_ref[...]
    _store_rows(h2_ref, h)

    h_hi, h_lo = _split_bf16(h, 2)
    hi_prod = _dot(h_hi, rw_ref[...])
    logits = hi_prod[:, :LANES] + (hi_prod[:, LANES:] + _dot(h_lo, rw_ref[:, :LANES]))
    lane = lax.broadcasted_iota(I32, (tm, LANES), 1)
    lanef = lane.astype(F32)
    neg = jnp.float32(-1e30)
    big = jnp.float32(1e9)
    is_g = lane < N_GROUPS
    gl = jnp.where(is_g, logits, neg)
    gmax = jnp.max(gl, axis=-1, keepdims=True)
    gidx = jnp.min(jnp.where(gl == gmax, lanef, big), axis=-1, keepdims=True)
    gsum = jnp.sum(jnp.where(is_g, jnp.exp(gl - gmax), 0.0), axis=-1, keepdims=True)
    gprob = 1.0 / gsum
    lo = N_GROUPS + EXPERTS_PER_GROUP * gidx
    emask = (lanef >= lo) & (lanef < lo + EXPERTS_PER_GROUP)
    el = jnp.where(emask, logits, neg)
    e1 = jnp.max(el, axis=-1, keepdims=True)
    i1 = jnp.min(jnp.where(el == e1, lanef, big), axis=-1, keepdims=True)
    el2 = jnp.where(lanef == i1, neg, el)
    e2 = jnp.max(el2, axis=-1, keepdims=True)
    i2 = jnp.min(jnp.where(el2 == e2, lanef, big), axis=-1, keepdims=True)
    r = jnp.exp(e2 - e1)
    gate1 = gprob / (1.0 + r)
    gate2 = gprob * r / (1.0 + r)
    id1 = i1 - N_GROUPS
    id2 = i2 - N_GROUPS

    oh1 = jnp.where(lanef == id1, 1.0, 0.0).astype(F32)
    oh2 = jnp.where(lanef == id2, 1.0, 0.0).astype(F32)
    oh = oh1 + oh2
    rr = lax.broadcasted_iota(I32, (tm, tm), 0)
    cc = lax.broadcasted_iota(I32, (tm, tm), 1)
    tri = jnp.where(cc < rr, 1.0, 0.0).astype(BF16)
    before = _dot(tri, oh.astype(BF16)) + base_ref[...]
    rank1 = jnp.sum(oh1 * before, axis=-1, keepdims=True)
    rank2 = jnp.sum(oh2 * before, axis=-1, keepdims=True)
    base_ref[...] = base_ref[...] + jnp.sum(oh, axis=0, keepdims=True)
    cnt_ref[...] = base_ref[...]

    ri = jnp.where(lane == 0, id1, jnp.where(lane == 1, id2,
                   jnp.where(lane == 2, rank1, jnp.where(lane == 3, rank2, 0.0))))
    ri_ref[...] = ri.astype(I32)
    rg_ref[...] = jnp.where(lane == 0, gate1, jnp.where(lane == 1, gate2, 0.0))


def _outproj(ya, yb, x, w_out16, ffn_g, rw):
    N, D = x.shape
    tm = TM_OUT
    row = lambda width: pl.BlockSpec((tm, width), lambda i: (i, 0))
    full = lambda shape: pl.BlockSpec(shape, lambda i: (0,) * len(shape))
    return pl.pallas_call(
        _outproj_kernel,
        grid=(N // tm,),
        in_specs=[row(CONV_A_DIM), row(DN_DIM), row(D), full((D, D)), full((1, D)), full((D, 2 * LANES))],
        out_specs=(row(D), pl.BlockSpec((tm * ROW_SLAB, LANES), lambda i: (i, 0)), row(LANES), row(LANES),
                   full((1, LANES))),
        out_shape=(jax.ShapeDtypeStruct((N, D), F32),
                   jax.ShapeDtypeStruct((N * ROW_SLAB, LANES), F32),
                   jax.ShapeDtypeStruct((N, LANES), I32),
                   jax.ShapeDtypeStruct((N, LANES), F32),
                   jax.ShapeDtypeStruct((1, LANES), F32)),
        scratch_shapes=[pltpu.VMEM((1, LANES), F32)],
        compiler_params=pltpu.CompilerParams(
            dimension_semantics=("arbitrary",), vmem_limit_bytes=VMEM_LIMIT),
        name="outproj",
    )(ya, yb, x, w_out16, ffn_g, rw)


def _slots_kernel(ri_ref, start_ref, o_ref):
    rif = ri_ref[...].astype(F32)
    lane = lax.broadcasted_iota(I32, rif.shape, 1)
    lanef = lane.astype(F32)

    def slot(k):
        start = jnp.sum(jnp.where(lanef == rif[:, k:k + 1], start_ref[...], 0.0), axis=-1, keepdims=True)
        return start + rif[:, TOP_K + k:TOP_K + k + 1]

    o_ref[...] = jnp.where(lane == 0, slot(0), jnp.where(lane == 1, slot(1), 0.0)).astype(I32)


def _slots(ri, seg_start):
    N = ri.shape[0]
    tm = TM_SLOTS
    return pl.pallas_call(
        _slots_kernel,
        grid=(N // tm,),
        in_specs=[pl.BlockSpec((tm, LANES), lambda i: (i, 0)), pl.BlockSpec((1, LANES), lambda i: (0, 0))],
        out_specs=pl.BlockSpec((tm, LANES), lambda i: (i, 0)),
        out_shape=jax.ShapeDtypeStruct((N, LANES), I32),
        compiler_params=pltpu.CompilerParams(dimension_semantics=("arbitrary",)),
        name="slots",
    )(ri, _lane_vec(seg_start, 0))


def _slot_of(slots_ref, t, k):
    return slots_ref[TOP_K * t + k]


def _dispatch_kernel(seg_ref, slots_ref, h_ref, xs_ref, zero_ref, sem, zsem):
    tm = TM_DISPATCH
    bm = FFN_BLOCK
    n_blocks = xs_ref.shape[0] // (bm * ROW_SLAB)

    @pl.when(pl.program_id(0) == 0)
    def _():
        zero_ref[...] = jnp.zeros(zero_ref.shape, F32)

        def pad_copy(d):
            return pltpu.make_async_copy(zero_ref.at[pl.ds(0, ROW_SLAB)], _slab(xs_ref, d), zsem)

        def block_copy(b):
            return pltpu.make_async_copy(zero_ref, _slab_block(xs_ref, b), zsem)

        def each_pad(fn):
            def per_expert(e, carry):
                def per_row(d, c):
                    fn(pad_copy(d))
                    return c
                return lax.fori_loop(seg_ref[e], seg_ref[N_EXPERTS + e], per_row, carry)
            lax.fori_loop(0, N_EXPERTS, per_expert, 0)

            def per_block(b, c):
                fn(block_copy(b))
                return c
            lax.fori_loop(seg_ref[2 * N_EXPERTS], n_blocks, per_block, 0)

        each_pad(lambda cp: cp.start())
        each_pad(lambda cp: cp.wait())

    def issue(t, carry):
        for k in range(TOP_K):
            d = _slot_of(slots_ref, t, k)
            pltpu.make_async_copy(_slab(h_ref, t), _slab(xs_ref, d), sem).start(priority=k)
        return carry

    lax.fori_loop(0, tm, issue, 0, unroll=ISSUE_UNROLL)
    for _ in range(TOP_K):
        pltpu.make_async_copy(h_ref, xs_ref.at[pl.ds(0, tm * ROW_SLAB)], sem).wait()


def _dispatch(seg, slots, h2, n_slots):
    N = h2.shape[0] // ROW_SLAB
    tm = TM_DISPATCH
    grid_spec = pltpu.PrefetchScalarGridSpec(
        num_scalar_prefetch=1,
        grid=(N // tm,),
        in_specs=[pl.BlockSpec((TOP_K * tm,), lambda i, s: (i,), memory_space=pltpu.SMEM),
                  pl.BlockSpec((tm * ROW_SLAB, LANES), lambda i, s: (i, 0))],
        out_specs=pl.BlockSpec(memory_space=pl.ANY),
        scratch_shapes=[pltpu.VMEM((FFN_BLOCK * ROW_SLAB, LANES), F32),
                        pltpu.SemaphoreType.DMA(()), pltpu.SemaphoreType.DMA(())],
    )
    return pl.pallas_call(
        _dispatch_kernel,
        grid_spec=grid_spec,
        out_shape=jax.ShapeDtypeStruct((n_slots * ROW_SLAB, LANES), F32),
        compiler_params=pltpu.CompilerParams(dimension_semantics=("arbitrary",)),
        name="dispatch",
    )(seg, slots, h2)


def _ffn_kernel(blk_ref, xs_ref, wg_ref, wu_ref, wd_ref, ys_ref,
                xbuf, ybuf, wg16, wu16, wd16, xsem, ysem, zsem):
    e = pl.program_id(0)
    bm = FFN_BLOCK
    ring = FFN_RING
    n_blocks = ys_ref.shape[0] // (bm * ROW_SLAB)
    first = blk_ref[e]
    count = blk_ref[N_EXPERTS + e]
    n_used = blk_ref[2 * N_EXPERTS]

    def x_copy(g):
        return pltpu.make_async_copy(_slab_block(xs_ref, g), xbuf.at[g % ring], xsem.at[g % ring])

    def y_copy(g):
        return pltpu.make_async_copy(ybuf.at[g % ring], _slab_block(ys_ref, g), ysem.at[g % ring])

    def when_block(g, fn):
        @pl.when((g >= 0) & (g < n_used))
        def _():
            fn(g)

    @pl.when(e == 0)
    def _():
        for g in range(ring - 1):
            when_block(g, lambda g: x_copy(g).start(priority=RING_PRIORITY))

    @pl.when(count > 0)
    def _():
        wg16[...] = wg_ref[0].astype(BF16)
        wu16[...] = wu_ref[0].astype(BF16)
        wd16[...] = wd_ref[0].astype(BF16)

    def block(j, carry):
        g = first + j
        x_copy(g).wait()
        when_block(g + ring - 1, lambda g: x_copy(g).start(priority=RING_PRIORITY))

        x = _load_rows(xbuf.at[g % ring], bm).astype(BF16)
        a = _dot(x, wg16[...])
        b = _dot(x, wu16[...])
        y = _dot((_silu(a) * b).astype(BF16), wd16[...])

        when_block(g - ring, lambda g: y_copy(g).wait())
        _store_rows(ybuf.at[g % ring], y)
        y_copy(g).start(priority=RING_PRIORITY)
        return carry

    lax.fori_loop(0, count, block, 0)

    @pl.when(e == pl.num_programs(0) - 1)
    def _():
        for back in range(ring, 0, -1):
            when_block(n_used - back, lambda g: y_copy(g).wait())
        ybuf[0] = jnp.zeros(ybuf.shape[1:], F32)

        def zero_copy(g):
            return pltpu.make_async_copy(ybuf.at[0], _slab_block(ys_ref, g), zsem)

        def start(g, c):
            zero_copy(g).start()
            return c

        def wait(g, c):
            zero_copy(g).wait()
            return c

        lax.fori_loop(n_used, n_blocks, start, 0)
        lax.fori_loop(n_used, n_blocks, wait, 0)


def _ffn(blk, xs, w_gate, w_up, w_down):
    bm = FFN_BLOCK
    D = D_MODEL
    weights = lambda shape: pl.BlockSpec((1,) + shape, lambda e, blk: (e, 0, 0))
    grid_spec = pltpu.PrefetchScalarGridSpec(
        num_scalar_prefetch=1,
        grid=(N_EXPERTS,),
        in_specs=[pl.BlockSpec(memory_space=pl.ANY),
                  weights((D, EXPERT_FF)), weights((D, EXPERT_FF)), weights((EXPERT_FF, D))],
        out_specs=pl.BlockSpec(memory_space=pl.ANY),
        scratch_shapes=[pltpu.VMEM((FFN_RING, bm * ROW_SLAB, LANES), F32),
                        pltpu.VMEM((FFN_RING, bm * ROW_SLAB, LANES), F32),
                        pltpu.VMEM((D, EXPERT_FF), BF16), pltpu.VMEM((D, EXPERT_FF), BF16),
                        pltpu.VMEM((EXPERT_FF, D), BF16),
                        pltpu.SemaphoreType.DMA((FFN_RING,)), pltpu.SemaphoreType.DMA((FFN_RING,)),
                        pltpu.SemaphoreType.DMA(())],
    )
    return pl.pallas_call(
        _ffn_kernel,
        grid_spec=grid_spec,
        out_shape=jax.ShapeDtypeStruct(xs.shape, F32),
        compiler_params=pltpu.CompilerParams(
            dimension_semantics=("arbitrary",), vmem_limit_bytes=VMEM_LIMIT),
        name="ffn",
    )(blk, xs, w_gate, w_up, w_down)


def _combine_kernel(slots_ref, slots_next_ref, ys_ref, x1_ref, rg_ref, g_ref, o_ref, buf_ref, sems):
    tm = TM_COMBINE
    i = pl.program_id(0)
    half = i % 2

    def gather(r_ref, s):
        def issue(t, carry):
            for k in range(TOP_K):
                d = _slot_of(r_ref, t, k)
                pltpu.make_async_copy(_slab(ys_ref, d), _slab(buf_ref.at[s, k], t),
                                      sems.at[s]).start(priority=k)
            return carry
        lax.fori_loop(0, tm, issue, 0, unroll=ISSUE_UNROLL)

    @pl.when(i == 0)
    def _():
        gather(slots_ref, 0)

    @pl.when(i + 1 < pl.num_programs(0))
    def _():
        gather(slots_next_ref, 1 - half)

    for k in range(TOP_K):
        pltpu.make_async_copy(ys_ref.at[pl.ds(0, tm * ROW_SLAB)], buf_ref.at[half, k], sems.at[half]).wait()

    rg = rg_ref[...]
    moe = (_load_rows(buf_ref.at[half, 0], tm) * rg[:, 0:1]
           + _load_rows(buf_ref.at[half, 1], tm) * rg[:, 1:2])
    x2 = x1_ref[...] + moe
    o_ref[...] = x2 * lax.rsqrt(jnp.mean(x2 * x2, axis=-1, keepdims=True) + EPS) * g_ref[...]


def _combine(slots, ys, x1, rg, final_g):
    N, D = x1.shape
    tm = TM_COMBINE
    n_tiles = N // tm
    return pl.pallas_call(
        _combine_kernel,
        grid=(n_tiles,),
        in_specs=[pl.BlockSpec((TOP_K * tm,), lambda i: (i,), memory_space=pltpu.SMEM),
                  pl.BlockSpec((TOP_K * tm,), lambda i: (jnp.minimum(i + 1, n_tiles - 1),),
                               memory_space=pltpu.SMEM),
                  pl.BlockSpec(memory_space=pl.ANY),
                  pl.BlockSpec((tm, D), lambda i: (i, 0)),
                  pl.BlockSpec((tm, LANES), lambda i: (i, 0)),
                  pl.BlockSpec((1, D), lambda i: (0, 0))],
        out_specs=pl.BlockSpec((tm, D), lambda i: (i, 0)),
        out_shape=jax.ShapeDtypeStruct((N, D), F32),
        scratch_shapes=[pltpu.VMEM((2, TOP_K, tm * ROW_SLAB, LANES), F32), pltpu.SemaphoreType.DMA((2,))],
        compiler_params=pltpu.CompilerParams(
            dimension_semantics=("arbitrary",), vmem_limit_bytes=VMEM_LIMIT),
        name="combine",
    )(slots, slots, ys, x1, rg, final_g)


def _lane_vec(values, offset):
    return jnp.zeros((1, LANES), F32).at[0, offset:offset + values.shape[0]].set(values.astype(F32))


def kernel(x, mix_norm_g, w_in, conv_a_w, conv_a_norm_g, dn_conv_w, dn_a_log, dn_dt_bias, dn_norm_g,
           w_out, ffn_norm_g, router_group_w, router_expert_w, w_gate, w_up, w_down, final_norm_g):
    B, S, D = x.shape
    N = B * S
    depth = w_in.shape[0]
    assert depth == 1, "single-layer block: the final RMSNorm is fused into the layer's combine step"
    group_of = jnp.arange(CONV_A_DIM, dtype=I32) // CONV_A_GROUP_DIM
    gmat = jnp.where(group_of[:, None] == group_of[None, :], 1.0 / CONV_A_GROUP_DIM, 0.0).astype(BF16)
    bm = FFN_BLOCK
    n_blocks = (N * TOP_K) // bm + N_EXPERTS
    for l in range(depth):
        w_in_pad = jnp.pad(w_in[l], ((0, 0), (0, IN_PROJ_PAD - IN_PROJ_DIM))).astype(BF16)
        ya, q, k, v, zg, gcol, grow = _inproj(
            x, mix_norm_g[l][None, :], w_in_pad, conv_a_w[l], conv_a_norm_g[l][None, :], dn_conv_w[l],
            _lane_vec(dn_a_log[l], DN_HEADS), _lane_vec(dn_dt_bias[l], DN_HEADS), gmat)
        u, w, qd, kd, at = _delta_prep(q, k, v, gcol, grow)
        yb = _delta_scan(u, w, qd, kd, at, zg, grow, dn_norm_g[l][None, :])
        rw = jnp.pad(jnp.concatenate([router_group_w[l], router_expert_w[l]], axis=1),
                     ((0, 0), (0, LANES - N_GROUPS - N_EXPERTS)))
        rw_hi = rw.astype(BF16)
        rw = jnp.concatenate([rw_hi, (rw - rw_hi.astype(F32)).astype(BF16)], axis=1)
        x1, h2, ri, rg, cnt = _outproj(ya.reshape(N, CONV_A_DIM), yb.reshape(N, DN_DIM), x.reshape(N, D),
                                       w_out[l].astype(BF16), ffn_norm_g[l][None, :], rw)
        counts = cnt[0, :N_EXPERTS].astype(I32)
        padded = (counts + bm - 1) // bm * bm
        seg_end = jnp.cumsum(padded).astype(I32)
        seg_start = seg_end - padded
        n_used = (seg_end[-1:] // bm)
        slots = _slots(ri, seg_start)[:, :TOP_K].reshape(-1)
        seg = jnp.concatenate([seg_start + counts, seg_end, n_used])
        xs = _dispatch(seg, slots, h2, n_blocks * bm)
        blk = jnp.concatenate([seg_start // bm, padded // bm, n_used])
        ys = _ffn(blk, xs, w_gate[l], w_up[l], w_down[l])
        x = _combine(slots, ys, x1, rg, final_norm_g[None, :]).reshape(B, S, D)
    return x
```

```python
import functools

import jax
import jax.numpy as jnp
from jax import lax
from jax.experimental import pallas as pl
from jax.experimental.pallas import tpu as pltpu

F32 = jnp.float32
BF16 = jnp.bfloat16
I32 = jnp.int32

D_MODEL = 1024
CHUNK = 64
CONV_A_GROUP_DIM = 64
CONV_A_DIM = 512
CONV_A_WIDTH = 3
DN_HEADS = 4
DN_HEAD_DIM = 128
DN_DIM = 512
DN_CONV_WIDTH = 4
IN_PROJ_DIM = 3 * CONV_A_DIM + 4 * DN_DIM + 2 * DN_HEADS
N_GROUPS = 4
EXPERTS_PER_GROUP = 8
N_EXPERTS = 32
TOP_K = 2
EXPERT_FF = 512
EPS = 1e-6

LANES = 128
HALO = 8
IN_PROJ_PAD = 29 * LANES
GATE_COL = 3 * CONV_A_DIM + 4 * DN_DIM

TM_IN = 512
TM_PREP = 256
TM_SCAN = 512
TM_OUT = 512
FFN_BLOCK = 256
TM_SLOTS = 2048
TM_DISPATCH = 1024
TM_COMBINE = 256
FFN_RING = 4
RING_PRIORITY = 1
ISSUE_UNROLL = 4
VMEM_LIMIT = 56 * 1024 * 1024


def _dot(a, b):
    return jnp.dot(a, b, preferred_element_type=F32)


def _dot_nt(a, b):
    return lax.dot_general(a, b, (((1,), (1,)), ((), ())), preferred_element_type=F32)


def _dot_tn(a, b):
    return lax.dot_general(a, b, (((0,), (0,)), ((), ())), preferred_element_type=F32)


def _split_bf16(x, parts):
    out = []
    for _ in range(parts):
        p = x.astype(BF16)
        out.append(p)
        x = x - p.astype(F32)
    return out


def _dot_split3(exact16, x):
    return sum(_dot(exact16, p) for p in _split_bf16(x, 3))


def _silu(x):
    return x * jax.nn.sigmoid(x)


ROW_SLAB = D_MODEL // LANES


def _store_rows(ref, val):
    m = val.shape[0]
    for c in range(ROW_SLAB):
        ref[pl.ds(c, m, stride=ROW_SLAB), :] = val[:, c * LANES:(c + 1) * LANES]


def _load_rows(ref, m):
    return jnp.concatenate([ref[pl.ds(c, m, stride=ROW_SLAB), :] for c in range(ROW_SLAB)], axis=-1)


def _slab(ref, row):
    return ref.at[pl.ds(pl.multiple_of(row * ROW_SLAB, ROW_SLAB), ROW_SLAB)]


def _slab_block(ref, block):
    n = FFN_BLOCK * ROW_SLAB
    return ref.at[pl.ds(pl.multiple_of(block * n, n), n)]


def _causal_conv(ext, w):
    taps = w.shape[0]
    acc = None
    for j in range(taps):
        shift = taps - 1 - j
        rows = pltpu.roll(ext, shift, 0)[HALO:] if shift else ext[HALO:]
        term = w[j:j + 1, :] * rows
        acc = term if acc is None else acc + term
    return acc


def _softplus(x):
    return jnp.maximum(x, 0.0) + jnp.log1p(jnp.exp(-jnp.abs(x)))


def _inproj_kernel(x_ref, g_ref, w_ref, caw_ref, cag_ref, dcw_ref, alog_ref, dtb_ref, gmat_ref,
                   ya_ref, q_ref, k_ref, v_ref, zg_ref, gcol_ref, grow_ref,
                   exta_ref, *extq_refs):
    tm = x_ref.shape[1]
    ext_refs = (exta_ref,) + extq_refs

    @pl.when(pl.program_id(1) == 0)
    def _():
        for ext_ref in ext_refs:
            ext_ref[0:HALO, :] = jnp.zeros((HALO, ext_ref.shape[1]), F32)

    x = x_ref[0]
    ms = jnp.mean(x * x, axis=-1, keepdims=True)
    hb = (x * lax.rsqrt(ms + EPS) * g_ref[...]).astype(BF16)

    def proj(c0, width):
        return _dot(hb, w_ref[:, c0:c0 + width])

    base = 3 * CONV_A_DIM

    def qkv_project(i):
        extq_refs[i][HALO:HALO + tm, :] = proj(base + i * DN_DIM, DN_DIM)

    def qkv_finish(i, out_ref):
        cols = slice(i * DN_DIM, (i + 1) * DN_DIM)
        s = _silu(_causal_conv(extq_refs[i][...], dcw_ref[:, cols]))
        if i == 2:
            out_ref[0] = s
        else:
            for h in range(DN_HEADS):
                sh = s[:, h * DN_HEAD_DIM:(h + 1) * DN_HEAD_DIM]
                inv = lax.rsqrt(jnp.sum(sh * sh, axis=-1, keepdims=True) + EPS)
                sh = sh * inv
                if i == 0:
                    sh = sh * (DN_HEAD_DIM ** -0.5)
                out_ref[0, :, h * DN_HEAD_DIM:(h + 1) * DN_HEAD_DIM] = sh

    def mixer_a_finish(a_b):
        y = a_b * _causal_conv(exta_ref[...], caw_ref[...])
        ysq = y * y
        hi = ysq.astype(BF16)
        lo = (ysq - hi.astype(F32)).astype(BF16)
        gmean = _dot(hi, gmat_ref[...]) + _dot(lo, gmat_ref[...])
        ya_ref[0] = (y * lax.rsqrt(gmean + EPS) * cag_ref[...]).astype(ya_ref.dtype)

    qkv_project(0)
    qkv_project(1)
    exta_ref[HALO:HALO + tm, :] = proj(2 * CONV_A_DIM, CONV_A_DIM) * proj(0, CONV_A_DIM)
    qkv_finish(0, q_ref)
    qkv_project(2)
    a_b = proj(CONV_A_DIM, CONV_A_DIM)
    qkv_finish(1, k_ref)
    z = proj(base + 3 * DN_DIM, DN_DIM)
    mixer_a_finish(a_b)
    p = proj(GATE_COL, LANES)
    qkv_finish(2, v_ref)
    zg_ref[0] = _silu(z)
    for ext_ref in ext_refs:
        ext_ref[0:HALO, :] = ext_ref[tm:tm + HALO, :]

    beta = jax.nn.sigmoid(p)
    g = -jnp.exp(alog_ref[...]) * _softplus(p + dtb_ref[...])
    r = lax.broadcasted_iota(I32, (tm, tm), 0)
    c = lax.broadcasted_iota(I32, (tm, tm), 1)
    tri = jnp.where((r // CHUNK == c // CHUNK) & (c <= r), 1.0, 0.0).astype(BF16)
    gc = _dot_split3(tri, g)
    lane = lax.broadcasted_iota(I32, (tm, LANES), 1)
    slab = jnp.where(lane < DN_HEADS, beta, gc)
    gcol_ref[0] = slab
    rows = slab.T[0:HALO, :]
    for ci in range(tm // CHUNK):
        grow_ref[0, ci] = rows[:, ci * CHUNK:(ci + 1) * CHUNK]


def _inproj(x, mix_g, w_in_pad, conv_a_w, conv_a_g, dn_conv_w, alog_vec, dtb_vec, gmat):
    B, S, D = x.shape
    tm = TM_IN
    full = lambda shape: pl.BlockSpec(shape, lambda b, t: (0,) * len(shape))
    row = lambda width: pl.BlockSpec((1, tm, width), lambda b, t: (b, t, 0))
    out_shape = (
        jax.ShapeDtypeStruct((B, S, CONV_A_DIM), BF16),
        jax.ShapeDtypeStruct((B, S, DN_DIM), F32),
        jax.ShapeDtypeStruct((B, S, DN_DIM), F32),
        jax.ShapeDtypeStruct((B, S, DN_DIM), F32),
        jax.ShapeDtypeStruct((B, S, DN_DIM), F32),
        jax.ShapeDtypeStruct((B, S, LANES), F32),
        jax.ShapeDtypeStruct((B, S // CHUNK, HALO, CHUNK), F32),
    )
    return pl.pallas_call(
        _inproj_kernel,
        grid=(B, S // tm),
        in_specs=[row(D), full((1, D)), full((D, IN_PROJ_PAD)), full((CONV_A_WIDTH, CONV_A_DIM)),
                  full((1, CONV_A_DIM)), full((DN_CONV_WIDTH, 3 * DN_DIM)), full((1, LANES)),
                  full((1, LANES)), full((CONV_A_DIM, CONV_A_DIM))],
        out_specs=(row(CONV_A_DIM), row(DN_DIM), row(DN_DIM), row(DN_DIM), row(DN_DIM), row(LANES),
                   pl.BlockSpec((1, tm // CHUNK, HALO, CHUNK), lambda b, t: (b, t, 0, 0))),
        out_shape=out_shape,
        scratch_shapes=[pltpu.VMEM((tm + HALO, CONV_A_DIM), F32)] + [pltpu.VMEM((tm + HALO, DN_DIM), F32)] * 3,
        compiler_params=pltpu.CompilerParams(
            dimension_semantics=("arbitrary", "arbitrary"), vmem_limit_bytes=VMEM_LIMIT),
        name="inproj",
    )(x, mix_g, w_in_pad, conv_a_w, conv_a_g, dn_conv_w, alog_vec, dtb_vec, gmat)


def _delta_prep_kernel(q_ref, k_ref, v_ref, gcol_ref, grow_ref,
                       u_ref, w_ref, qd_ref, kd_ref, at_ref):
    tm = q_ref.shape[1]
    ri = lax.broadcasted_iota(I32, (CHUNK, CHUNK), 0)
    ci = lax.broadcasted_iota(I32, (CHUNK, CHUNK), 1)
    causal = ci <= ri
    strict = ci < ri
    eye = jnp.where(ci == ri, 1.0, 0.0).astype(F32)
    lane = lax.broadcasted_iota(I32, (CHUNK, LANES), 1)
    chains = [(c, h) for c in range(tm // CHUNK) for h in range(DN_HEADS)]
    ms, ts, rhss = [], [], []
    for c, h in chains:
        rows = slice(c * CHUNK, (c + 1) * CHUNK)
        cols = slice(h * DN_HEAD_DIM, (h + 1) * DN_HEAD_DIM)
        slab = gcol_ref[0, rows, :]
        beta = jnp.sum(jnp.where(lane == h, slab, 0.0), axis=-1, keepdims=True)
        gcc = jnp.sum(jnp.where(lane == h + DN_HEADS, slab, 0.0), axis=-1, keepdims=True)
        gcr = grow_ref[0, c, h + DN_HEADS:h + DN_HEADS + 1, :]
        diff = gcc - gcr
        decay = jnp.where(causal, jnp.exp(jnp.where(causal, diff, 0.0)), 0.0)
        q = q_ref[0, rows, cols]
        k = k_ref[0, rows, cols]
        v = v_ref[0, rows, cols]
        kb = k * beta
        k16 = k.astype(BF16)
        L = jnp.where(strict, _dot_nt(kb.astype(BF16), k16) * decay, 0.0)
        intra = _dot_nt(q.astype(BF16), k16) * decay
        egc = jnp.exp(gcc)
        gl = gcr[:, CHUNK - 1:CHUNK]
        qd_ref[0, rows, cols] = (q * egc).astype(BF16)
        kd_ref[0, rows, cols] = (k * jnp.exp(gl - gcc)).astype(BF16)
        at_ref[0, rows, h * CHUNK:(h + 1) * CHUNK] = intra.astype(BF16)
        rhss.append(jnp.concatenate([v * beta, kb * egc], axis=-1).astype(BF16))
        ms.append(-L)
        ts.append(eye - L)
    for _ in range(5):
        m16s = [m.astype(BF16) for m in ms]
        ms = [_dot(m16, m16) for m16 in m16s]
        ts = [t + _dot(t.astype(BF16), m.astype(BF16)) for t, m in zip(ts, ms)]
    for (c, h), t, rhs in zip(chains, ts, rhss):
        rows = slice(c * CHUNK, (c + 1) * CHUNK)
        cols = slice(h * DN_HEAD_DIM, (h + 1) * DN_HEAD_DIM)
        uw = _dot(t.astype(BF16), rhs)
        u_ref[0, rows, cols] = uw[:, :DN_HEAD_DIM]
        w_ref[0, rows, cols] = uw[:, DN_HEAD_DIM:].astype(BF16)


def _delta_prep(q, k, v, gcol, grow):
    B, S, _ = q.shape
    tm = TM_PREP
    row = lambda width: pl.BlockSpec((1, tm, width), lambda b, t: (b, t, 0))
    return pl.pallas_call(
        _delta_prep_kernel,
        grid=(B, S // tm),
        in_specs=[row(DN_DIM), row(DN_DIM), row(DN_DIM), row(LANES),
                  pl.BlockSpec((1, tm // CHUNK, HALO, CHUNK), lambda b, t: (b, t, 0, 0))],
        out_specs=(row(DN_DIM), row(DN_DIM), row(DN_DIM), row(DN_DIM), row(DN_HEADS * CHUNK)),
        out_shape=(jax.ShapeDtypeStruct((B, S, DN_DIM), F32),
                   jax.ShapeDtypeStruct((B, S, DN_DIM), BF16),
                   jax.ShapeDtypeStruct((B, S, DN_DIM), BF16),
                   jax.ShapeDtypeStruct((B, S, DN_DIM), BF16),
                   jax.ShapeDtypeStruct((B, S, DN_HEADS * CHUNK), BF16)),
        compiler_params=pltpu.CompilerParams(
            dimension_semantics=("arbitrary", "arbitrary"), vmem_limit_bytes=VMEM_LIMIT),
        name="delta_prep",
    )(q, k, v, gcol, grow)


def _delta_scan_kernel(u_ref, w_ref, qd_ref, kd_ref, at_ref, zg_ref, grow_ref, ng_ref, o_ref, st_ref):
    tm = u_ref.shape[1]

    @pl.when(pl.program_id(1) == 0)
    def _():
        st_ref[...] = jnp.zeros(st_ref.shape, F32)

    def step(c, carry):
        r0 = pl.multiple_of(c * CHUNK, CHUNK)
        rows = pl.ds(r0, CHUNK)
        heads = range(DN_HEADS)
        cols = [slice(h * DN_HEAD_DIM, (h + 1) * DN_HEAD_DIM) for h in heads]
        sts = [st_ref[h] for h in heads]
        st16s = [st.astype(BF16) for st in sts]
        wss = [_dot(w_ref[0, rows, cols[h]], st16s[h]) for h in heads]
        qss = [_dot(qd_ref[0, rows, cols[h]], st16s[h]) for h in heads]
        vn16s = [(u_ref[0, rows, cols[h]] - wss[h]).astype(BF16) for h in heads]
        upd = [_dot_tn(kd_ref[0, rows, cols[h]], vn16s[h]) for h in heads]
        avs = [_dot(at_ref[0, rows, h * CHUNK:(h + 1) * CHUNK], vn16s[h]) for h in heads]
        for h in heads:
            gl = grow_ref[0, c, h + DN_HEADS:h + DN_HEADS + 1, CHUNK - 1:CHUNK]
            st_ref[h] = sts[h] * jnp.exp(gl) + upd[h]
        for h in heads:
            o = qss[h] + avs[h]
            on = o * lax.rsqrt(jnp.mean(o * o, axis=-1, keepdims=True) + EPS) * ng_ref[...]
            o_ref[0, rows, cols[h]] = (on * zg_ref[0, rows, cols[h]]).astype(o_ref.dtype)
        return carry

    lax.fori_loop(0, tm // CHUNK, step, 0)


def _delta_scan(u, w, qd, kd, at, zg, grow, norm_g):
    B, S, _ = u.shape
    tm = TM_SCAN
    row = lambda width: pl.BlockSpec((1, tm, width), lambda b, t: (b, t, 0))
    return pl.pallas_call(
        _delta_scan_kernel,
        grid=(B, S // tm),
        in_specs=[row(DN_DIM), row(DN_DIM), row(DN_DIM), row(DN_DIM), row(DN_HEADS * CHUNK), row(DN_DIM),
                  pl.BlockSpec((1, tm // CHUNK, HALO, CHUNK), lambda b, t: (b, t, 0, 0)),
                  pl.BlockSpec((1, DN_HEAD_DIM), lambda b, t: (0, 0))],
        out_specs=row(DN_DIM),
        out_shape=jax.ShapeDtypeStruct((B, S, DN_DIM), BF16),
        scratch_shapes=[pltpu.VMEM((DN_HEADS, DN_HEAD_DIM, DN_HEAD_DIM), F32)],
        compiler_params=pltpu.CompilerParams(
            dimension_semantics=("arbitrary", "arbitrary"), vmem_limit_bytes=VMEM_LIMIT),
        name="delta_scan",
    )(u, w, qd, kd, at, zg, grow, norm_g)


def _outproj_kernel(ya_ref, yb_ref, x_ref, wo_ref, g_ref, rw_ref,
                    x1_ref, h2_ref, ri_ref, rg_ref, cnt_ref, base_ref):
    tm = x_ref.shape[0]

    @pl.when(pl.program_id(0) == 0)
    def _():
        base_ref[...] = jnp.zeros(base_ref.shape, F32)

    y = _dot(jnp.concatenate([ya_ref[...], yb_ref[...]], axis=-1), wo_ref[...])
    x1 = x_ref[...] + y
    x1_ref[...] = x1
    h = x1 * lax.rsqrt(jnp.mean(x1 * x1, axis=-1, keepdims=True) + EPS) * g_ref[...]
    _store_rows(h2_ref, h)

    h_hi, h_lo = _split_bf16(h, 2)
    hi_prod = _dot(h_hi, rw_ref[...])
    logits = hi_prod[:, :LANES] + (hi_prod[:, LANES:] + _dot(h_lo, rw_ref[:, :LANES]))
    lane = lax.broadcasted_iota(I32, (tm, LANES), 1)
    lanef = lane.astype(F32)
    neg = jnp.float32(-1e30)
    big = jnp.float32(1e9)
    is_g = lane < N_GROUPS
    gl = jnp.where(is_g, logits, neg)
    gmax = jnp.max(gl, axis=-1, keepdims=True)
    gidx = jnp.min(jnp.where(gl == gmax, lanef, big), axis=-1, keepdims=True)
    gsum = jnp.sum(jnp.where(is_g, jnp.exp(gl - gmax), 0.0), axis=-1, keepdims=True)
    gprob = 1.0 / gsum
    lo = N_GROUPS + EXPERTS_PER_GROUP * gidx
    emask = (lanef >= lo) & (lanef < lo + EXPERTS_PER_GROUP)
    el = jnp.where(emask, logits, neg)
    e1 = jnp.max(el, axis=-1, keepdims=True)
    i1 = jnp.min(jnp.where(el == e1, lanef, big), axis=-1, keepdims=True)
    el2 = jnp.where(lanef == i1, neg, el)
    e2 = jnp.max(el2, axis=-1, keepdims=True)
    i2 = jnp.min(jnp.where(el2 == e2, lanef, big), axis=-1, keepdims=True)
    r = jnp.exp(e2 - e1)
    gate1 = gprob / (1.0 + r)
    gate2 = gprob * r / (1.0 + r)
    id1 = i1 - N_GROUPS
    id2 = i2 - N_GROUPS

    oh1 = jnp.where(lanef == id1, 1.0, 0.0).astype(F32)
    oh2 = jnp.where(lanef == id2, 1.0, 0.0).astype(F32)
    oh = oh1 + oh2
    rr = lax.broadcasted_iota(I32, (tm, tm), 0)
    cc = lax.broadcasted_iota(I32, (tm, tm), 1)
    tri = jnp.where(cc < rr, 1.0, 0.0).astype(BF16)
    before = _dot(tri, oh.astype(BF16)) + base_ref[...]
    rank1 = jnp.sum(oh1 * before, axis=-1, keepdims=True)
    rank2 = jnp.sum(oh2 * before, axis=-1, keepdims=True)
    base_ref[...] = base_ref[...] + jnp.sum(oh, axis=0, keepdims=True)
    cnt_ref[...] = base_ref[...]

    ri = jnp.where(lane == 0, id1, jnp.where(lane == 1, id2,
                   jnp.where(lane == 2, rank1, jnp.where(lane == 3, rank2, 0.0))))
    ri_ref[...] = ri.astype(I32)
    rg_ref[...] = jnp.where(lane == 0, gate1, jnp.where(lane == 1, gate2, 0.0))


def _outproj(ya, yb, x, w_out16, ffn_g, rw):
    N, D = x.shape
    tm = TM_OUT
    row = lambda width: pl.BlockSpec((tm, width), lambda i: (i, 0))
    full = lambda shape: pl.BlockSpec(shape, lambda i: (0,) * len(shape))
    return pl.pallas_call(
        _outproj_kernel,
        grid=(N // tm,),
        in_specs=[row(CONV_A_DIM), row(DN_DIM), row(D), full((D, D)), full((1, D)), full((D, 2 * LANES))],
        out_specs=(row(D), pl.BlockSpec((tm * ROW_SLAB, LANES), lambda i: (i, 0)), row(LANES), row(LANES),
                   full((1, LANES))),
        out_shape=(jax.ShapeDtypeStruct((N, D), F32),
                   jax.ShapeDtypeStruct((N * ROW_SLAB, LANES), F32),
                   jax.ShapeDtypeStruct((N, LANES), I32),
                   jax.ShapeDtypeStruct((N, LANES), F32),
                   jax.ShapeDtypeStruct((1, LANES), F32)),
        scratch_shapes=[pltpu.VMEM((1, LANES), F32)],
        compiler_params=pltpu.CompilerParams(
            dimension_semantics=("arbitrary",), vmem_limit_bytes=VMEM_LIMIT),
        name="outproj",
    )(ya, yb, x, w_out16, ffn_g, rw)


def _slots_kernel(ri_ref, start_ref, o_ref):
    rif = ri_ref[...].astype(F32)
    lane = lax.broadcasted_iota(I32, rif.shape, 1)
    lanef = lane.astype(F32)

    def slot(k):
        start = jnp.sum(jnp.where(lanef == rif[:, k:k + 1], start_ref[...], 0.0), axis=-1, keepdims=True)
        return start + rif[:, TOP_K + k:TOP_K + k + 1]

    o_ref[...] = jnp.where(lane == 0, slot(0), jnp.where(lane == 1, slot(1), 0.0)).astype(I32)


def _slots(ri, seg_start):
    N = ri.shape[0]
    tm = TM_SLOTS
    return pl.pallas_call(
        _slots_kernel,
        grid=(N // tm,),
        in_specs=[pl.BlockSpec((tm, LANES), lambda i: (i, 0)), pl.BlockSpec((1, LANES), lambda i: (0, 0))],
        out_specs=pl.BlockSpec((tm, LANES), lambda i: (i, 0)),
        out_shape=jax.ShapeDtypeStruct((N, LANES), I32),
        compiler_params=pltpu.CompilerParams(dimension_semantics=("arbitrary",)),
        name="slots",
    )(ri, _lane_vec(seg_start, 0))


def _slot_of(slots_ref, t, k):
    return slots_ref[TOP_K * t + k]


def _dispatch_kernel(seg_ref, slots_ref, h_ref, xs_ref, zero_ref, sem, zsem):
    tm = TM_DISPATCH
    bm = FFN_BLOCK
    n_blocks = xs_ref.shape[0] // (bm * ROW_SLAB)

    @pl.when(pl.program_id(0) == 0)
    def _():
        zero_ref[...] = jnp.zeros(zero_ref.shape, F32)

        def pad_copy(d):
            return pltpu.make_async_copy(zero_ref.at[pl.ds(0, ROW_SLAB)], _slab(xs_ref, d), zsem)

        def block_copy(b):
            return pltpu.make_async_copy(zero_ref, _slab_block(xs_ref, b), zsem)

        def each_pad(fn):
            def per_expert(e, carry):
                def per_row(d, c):
                    fn(pad_copy(d))
                    return c
                return lax.fori_loop(seg_ref[e], seg_ref[N_EXPERTS + e], per_row, carry)
            lax.fori_loop(0, N_EXPERTS, per_expert, 0)

            def per_block(b, c):
                fn(block_copy(b))
                return c
            lax.fori_loop(seg_ref[2 * N_EXPERTS], n_blocks, per_block, 0)

        each_pad(lambda cp: cp.start())
        each_pad(lambda cp: cp.wait())

    def issue(t, carry):
        for k in range(TOP_K):
            d = _slot_of(slots_ref, t, k)
            pltpu.make_async_copy(_slab(h_ref, t), _slab(xs_ref, d), sem).start(priority=k)
        return carry

    lax.fori_loop(0, tm, issue, 0, unroll=ISSUE_UNROLL)
    for _ in range(TOP_K):
        pltpu.make_async_copy(h_ref, xs_ref.at[pl.ds(0, tm * ROW_SLAB)], sem).wait()


def _dispatch(seg, slots, h2, n_slots):
    N = h2.shape[0] // ROW_SLAB
    tm = TM_DISPATCH
    grid_spec = pltpu.PrefetchScalarGridSpec(
        num_scalar_prefetch=1,
        grid=(N // tm,),
        in_specs=[pl.BlockSpec((TOP_K * tm,), lambda i, s: (i,), memory_space=pltpu.SMEM),
                  pl.BlockSpec((tm * ROW_SLAB, LANES), lambda i, s: (i, 0))],
        out_specs=pl.BlockSpec(memory_space=pl.ANY),
        scratch_shapes=[pltpu.VMEM((FFN_BLOCK * ROW_SLAB, LANES), F32),
                        pltpu.SemaphoreType.DMA(()), pltpu.SemaphoreType.DMA(())],
    )
    return pl.pallas_call(
        _dispatch_kernel,
        grid_spec=grid_spec,
        out_shape=jax.ShapeDtypeStruct((n_slots * ROW_SLAB, LANES), F32),
        compiler_params=pltpu.CompilerParams(dimension_semantics=("arbitrary",)),
        name="dispatch",
    )(seg, slots, h2)


def _ffn_kernel(blk_ref, xs_ref, wg_ref, wu_ref, wd_ref, ys_ref,
                xbuf, ybuf, wg16, wu16, wd16, xsem, ysem, zsem):
    e = pl.program_id(0)
    bm = FFN_BLOCK
    ring = FFN_RING
    n_blocks = ys_ref.shape[0] // (bm * ROW_SLAB)
    first = blk_ref[e]
    count = blk_ref[N_EXPERTS + e]
    n_used = blk_ref[2 * N_EXPERTS]

    def x_copy(g):
        return pltpu.make_async_copy(_slab_block(xs_ref, g), xbuf.at[g % ring], xsem.at[g % ring])

    def y_copy(g):
        return pltpu.make_async_copy(ybuf.at[g % ring], _slab_block(ys_ref, g), ysem.at[g % ring])

    def when_block(g, fn):
        @pl.when((g >= 0) & (g < n_used))
        def _():
            fn(g)

    @pl.when(e == 0)
    def _():
        for g in range(ring - 1):
            when_block(g, lambda g: x_copy(g).start(priority=RING_PRIORITY))

    @pl.when(count > 0)
    def _():
        wg16[...] = wg_ref[0].astype(BF16)
        wu16[...] = wu_ref[0].astype(BF16)
        wd16[...] = wd_ref[0].astype(BF16)

    def block(j, carry):
        g = first + j
        x_copy(g).wait()
        when_block(g + ring - 1, lambda g: x_copy(g).start(priority=RING_PRIORITY))

        x = _load_rows(xbuf.at[g % ring], bm).astype(BF16)
        a = _dot(x, wg16[...])
        b = _dot(x, wu16[...])
        y = _dot((_silu(a) * b).astype(BF16), wd16[...])

        when_block(g - ring, lambda g: y_copy(g).wait())
        _store_rows(ybuf.at[g % ring], y)
        y_copy(g).start(priority=RING_PRIORITY)
        return carry

    lax.fori_loop(0, count, block, 0)

    @pl.when(e == pl.num_programs(0) - 1)
    def _():
        for back in range(ring, 0, -1):
            when_block(n_used - back, lambda g: y_copy(g).wait())
        ybuf[0] = jnp.zeros(ybuf.shape[1:], F32)

        def zero_copy(g):
            return pltpu.make_async_copy(ybuf.at[0], _slab_block(ys_ref, g), zsem)

        def start(g, c):
            zero_copy(g).start()
            return c

        def wait(g, c):
            zero_copy(g).wait()
            return c

        lax.fori_loop(n_used, n_blocks, start, 0)
        lax.fori_loop(n_used, n_blocks, wait, 0)


def _ffn(blk, xs, w_gate, w_up, w_down):
    bm = FFN_BLOCK
    D = D_MODEL
    weights = lambda shape: pl.BlockSpec((1,) + shape, lambda e, blk: (e, 0, 0))
    grid_spec = pltpu.PrefetchScalarGridSpec(
        num_scalar_prefetch=1,
        grid=(N_EXPERTS,),
        in_specs=[pl.BlockSpec(memory_space=pl.ANY),
                  weights((D, EXPERT_FF)), weights((D, EXPERT_FF)), weights((EXPERT_FF, D))],
        out_specs=pl.BlockSpec(memory_space=pl.ANY),
        scratch_shapes=[pltpu.VMEM((FFN_RING, bm * ROW_SLAB, LANES), F32),
                        pltpu.VMEM((FFN_RING, bm * ROW_SLAB, LANES), F32),
                        pltpu.VMEM((D, EXPERT_FF), BF16), pltpu.VMEM((D, EXPERT_FF), BF16),
                        pltpu.VMEM((EXPERT_FF, D), BF16),
                        pltpu.SemaphoreType.DMA((FFN_RING,)), pltpu.SemaphoreType.DMA((FFN_RING,)),
                        pltpu.SemaphoreType.DMA(())],
    )
    return pl.pallas_call(
        _ffn_kernel,
        grid_spec=grid_spec,
        out_shape=jax.ShapeDtypeStruct(xs.shape, F32),
        compiler_params=pltpu.CompilerParams(
            dimension_semantics=("arbitrary",), vmem_limit_bytes=VMEM_LIMIT),
        name="ffn",
    )(blk, xs, w_gate, w_up, w_down)


def _combine_kernel(slots_ref, slots_next_ref, ys_ref, x1_ref, rg_ref, g_ref, o_ref, buf_ref, sems):
    tm = TM_COMBINE
    i = pl.program_id(0)
    half = i % 2

    def gather(r_ref, s):
        def issue(t, carry):
            for k in range(TOP_K):
                d = _slot_of(r_ref, t, k)
                pltpu.make_async_copy(_slab(ys_ref, d), _slab(buf_ref.at[s, k], t),
                                      sems.at[s]).start(priority=k)
            return carry
        lax.fori_loop(0, tm, issue, 0, unroll=ISSUE_UNROLL)

    @pl.when(i == 0)
    def _():
        gather(slots_ref, 0)

    @pl.when(i + 1 < pl.num_programs(0))
    def _():
        gather(slots_next_ref, 1 - half)

    for k in range(TOP_K):
        pltpu.make_async_copy(ys_ref.at[pl.ds(0, tm * ROW_SLAB)], buf_ref.at[half, k], sems.at[half]).wait()

    rg = rg_ref[...]
    moe = (_load_rows(buf_ref.at[half, 0], tm) * rg[:, 0:1]
           + _load_rows(buf_ref.at[half, 1], tm) * rg[:, 1:2])
    x2 = x1_ref[...] + moe
    o_ref[...] = x2 * lax.rsqrt(jnp.mean(x2 * x2, axis=-1, keepdims=True) + EPS) * g_ref[...]


def _combine(slots, ys, x1, rg, final_g):
    N, D = x1.shape
    tm = TM_COMBINE
    n_tiles = N // tm
    return pl.pallas_call(
        _combine_kernel,
        grid=(n_tiles,),
        in_specs=[pl.BlockSpec((TOP_K * tm,), lambda i: (i,), memory_space=pltpu.SMEM),
                  pl.BlockSpec((TOP_K * tm,), lambda i: (jnp.minimum(i + 1, n_tiles - 1),),
                               memory_space=pltpu.SMEM),
                  pl.BlockSpec(memory_space=pl.ANY),
                  pl.BlockSpec((tm, D), lambda i: (i, 0)),
                  pl.BlockSpec((tm, LANES), lambda i: (i, 0)),
                  pl.BlockSpec((1, D), lambda i: (0, 0))],
        out_specs=pl.BlockSpec((tm, D), lambda i: (i, 0)),
        out_shape=jax.ShapeDtypeStruct((N, D), F32),
        scratch_shapes=[pltpu.VMEM((2, TOP_K, tm * ROW_SLAB, LANES), F32), pltpu.SemaphoreType.DMA((2,))],
        compiler_params=pltpu.CompilerParams(
            dimension_semantics=("arbitrary",), vmem_limit_bytes=VMEM_LIMIT),
        name="combine",
    )(slots, slots, ys, x1, rg, final_g)


def _lane_vec(values, offset):
    return jnp.zeros((1, LANES), F32).at[0, offset:offset + values.shape[0]].set(values.astype(F32))


def kernel(x, mix_norm_g, w_in, conv_a_w, conv_a_norm_g, dn_conv_w, dn_a_log, dn_dt_bias, dn_norm_g,
           w_out, ffn_norm_g, router_group_w, router_expert_w, w_gate, w_up, w_down, final_norm_g):
    B, S, D = x.shape
    N = B * S
    depth = w_in.shape[0]
    assert depth == 1, "single-layer block: the final RMSNorm is fused into the layer's combine step"
    group_of = jnp.arange(CONV_A_DIM, dtype=I32) // CONV_A_GROUP_DIM
    gmat = jnp.where(group_of[:, None] == group_of[None, :], 1.0 / CONV_A_GROUP_DIM, 0.0).astype(BF16)
    bm = FFN_BLOCK
    n_blocks = (N * TOP_K) // bm + N_EXPERTS
    for l in range(depth):
        w_in_pad = jnp.pad(w_in[l], ((0, 0), (0, IN_PROJ_PAD - IN_PROJ_DIM))).astype(BF16)
        ya, q, k, v, zg, gcol, grow = _inproj(
            x, mix_norm_g[l][None, :], w_in_pad, conv_a_w[l], conv_a_norm_g[l][None, :], dn_conv_w[l],
            _lane_vec(dn_a_log[l], DN_HEADS), _lane_vec(dn_dt_bias[l], DN_HEADS), gmat)
        u, w, qd, kd, at = _delta_prep(q, k, v, gcol, grow)
        yb = _delta_scan(u, w, qd, kd, at, zg, grow, dn_norm_g[l][None, :])
        rw = jnp.pad(jnp.concatenate([router_group_w[l], router_expert_w[l]], axis=1),
                     ((0, 0), (0, LANES - N_GROUPS - N_EXPERTS)))
        rw_hi = rw.astype(BF16)
        rw = jnp.concatenate([rw_hi, (rw - rw_hi.astype(F32)).astype(BF16)], axis=1)
        x1, h2, ri, rg, cnt = _outproj(ya.reshape(N, CONV_A_DIM), yb.reshape(N, DN_DIM), x.reshape(N, D),
                                       w_out[l].astype(BF16), ffn_norm_g[l][None, :], rw)
        counts = cnt[0, :N_EXPERTS].astype(I32)
        padded = (counts + bm - 1) // bm * bm
        seg_end = jnp.cumsum(padded).astype(I32)
        seg_start = seg_end - padded
        n_used = (seg_end[-1:] // bm)
        slots = _slots(ri, seg_start)[:, :TOP_K].reshape(-1)
        seg = jnp.concatenate([seg_start + counts, seg_end, n_used])
        xs = _dispatch(seg, slots, h2, n_blocks * bm)
        blk = jnp.concatenate([seg_start // bm, padded // bm, n_used])
        ys = _ffn(blk, xs, w_gate[l], w_up[l], w_down[l])
        x = _combine(slots, ys, x1, rg, final_norm_g[None, :]).reshape(B, S, D)
    return x
```

```python
import functools

import jax
import jax.numpy as jnp
from jax import lax
from jax.experimental import pallas as pl
from jax.experimental.pallas import tpu as pltpu

F32 = jnp.float32
BF16 = jnp.bfloat16
I32 = jnp.int32

D_MODEL = 1024
CHUNK = 64
CONV_A_GROUP_DIM = 64
CONV_A_DIM = 512
CONV_A_WIDTH = 3
DN_HEADS = 4
DN_HEAD_DIM = 128
DN_DIM = 512
DN_CONV_WIDTH = 4
IN_PROJ_DIM = 3 * CONV_A_DIM + 4 * DN_DIM + 2 * DN_HEADS
N_GROUPS = 4
EXPERTS_PER_GROUP = 8
N_EXPERTS = 32
TOP_K = 2
EXPERT_FF = 512
EPS = 1e-6

LANES = 128
HALO = 8
IN_PROJ_PAD = 29 * LANES
GATE_COL = 3 * CONV_A_DIM + 4 * DN_DIM

TM_IN = 512
TM_PREP = 256
SCAN_BATCH = 4
TM_SCAN = 512
TM_OUT = 512
FFN_BLOCK = 256
TM_SLOTS = 2048
TM_DISPATCH = 1024
TM_COMBINE = 256
FFN_RING = 4
RING_PRIORITY = 1
ISSUE_UNROLL = 4
VMEM_LIMIT = 56 * 1024 * 1024


def _dot(a, b):
    return jnp.dot(a, b, preferred_element_type=F32)


def _dot_nt(a, b):
    return lax.dot_general(a, b, (((1,), (1,)), ((), ())), preferred_element_type=F32)


def _dot_tn(a, b):
    return lax.dot_general(a, b, (((0,), (0,)), ((), ())), preferred_element_type=F32)


def _split_bf16(x, parts):
    out = []
    for _ in range(parts):
        p = x.astype(BF16)
        out.append(p)
        x = x - p.astype(F32)
    return out


def _silu(x):
    return x * jax.nn.sigmoid(x)


ROW_SLAB = D_MODEL // LANES


def _store_rows(ref, val):
    m = val.shape[0]
    for c in range(ROW_SLAB):
        ref[pl.ds(c, m, stride=ROW_SLAB), :] = val[:, c * LANES:(c + 1) * LANES]


def _load_rows(ref, m):
    return jnp.concatenate([ref[pl.ds(c, m, stride=ROW_SLAB), :] for c in range(ROW_SLAB)], axis=-1)


def _slab(ref, row):
    return ref.at[pl.ds(pl.multiple_of(row * ROW_SLAB, ROW_SLAB), ROW_SLAB)]


def _slab_block(ref, block):
    n = FFN_BLOCK * ROW_SLAB
    return ref.at[pl.ds(pl.multiple_of(block * n, n), n)]


def _causal_conv(ext, w):
    taps = w.shape[0]
    acc = None
    for j in range(taps):
        shift = taps - 1 - j
        rows = pltpu.roll(ext, shift, 0)[HALO:] if shift else ext[HALO:]
        term = w[j:j + 1, :] * rows
        acc = term if acc is None else acc + term
    return acc


def _softplus(x):
    return jnp.maximum(x, 0.0) + jnp.log1p(jnp.exp(-jnp.abs(x)))


def _inproj_kernel(x_ref, g_ref, w_ref, caw_ref, cag_ref, dcw_ref, alog_ref, dtb_ref, gmat_ref,
                   ya_ref, q_ref, k_ref, v_ref, zg_ref, gcol_ref, grow_ref,
                   exta_ref, *extq_refs):
    tm = x_ref.shape[1]
    ext_refs = (exta_ref,) + extq_refs

    @pl.when(pl.program_id(1) == 0)
    def _():
        for ext_ref in ext_refs:
            ext_ref[0:HALO, :] = jnp.zeros((HALO, ext_ref.shape[1]), F32)

    x = x_ref[0]
    ms = jnp.mean(x * x, axis=-1, keepdims=True)
    hb = (x * lax.rsqrt(ms + EPS) * g_ref[...]).astype(BF16)

    def proj(c0, width):
        return _dot(hb, w_ref[:, c0:c0 + width])

    base = 3 * CONV_A_DIM

    def qkv_project(i):
        extq_refs[i][HALO:HALO + tm, :] = proj(base + i * DN_DIM, DN_DIM)

    def qkv_finish(i, out_ref):
        cols = slice(i * DN_DIM, (i + 1) * DN_DIM)
        s = _silu(_causal_conv(extq_refs[i][...], dcw_ref[:, cols]))
        if i == 2:
            out_ref[0] = s
        else:
            for h in range(DN_HEADS):
                sh = s[:, h * DN_HEAD_DIM:(h + 1) * DN_HEAD_DIM]
                inv = lax.rsqrt(jnp.sum(sh * sh, axis=-1, keepdims=True) + EPS)
                sh = sh * inv
                if i == 0:
                    sh = sh * (DN_HEAD_DIM ** -0.5)
                out_ref[0, :, h * DN_HEAD_DIM:(h + 1) * DN_HEAD_DIM] = sh

    def mixer_a_finish(a_b):
        y = a_b * _causal_conv(exta_ref[...], caw_ref[...])
        ysq = y * y
        hi = ysq.astype(BF16)
        lo = (ysq - hi.astype(F32)).astype(BF16)
        gmean = _dot(hi, gmat_ref[...]) + _dot(lo, gmat_ref[...])
        ya_ref[0] = (y * lax.rsqrt(gmean + EPS) * cag_ref[...]).astype(ya_ref.dtype)

    qkv_project(0)
    qkv_project(1)
    exta_ref[HALO:HALO + tm, :] = proj(2 * CONV_A_DIM, CONV_A_DIM) * proj(0, CONV_A_DIM)
    qkv_finish(0, q_ref)
    qkv_project(2)
    a_b = proj(CONV_A_DIM, CONV_A_DIM)
    qkv_finish(1, k_ref)
    z = proj(base + 3 * DN_DIM, DN_DIM)
    mixer_a_finish(a_b)
    p = proj(GATE_COL, LANES)
    qkv_finish(2, v_ref)
    zg_ref[0] = _silu(z)
    for ext_ref in ext_refs:
        ext_ref[0:HALO, :] = ext_ref[tm:tm + HALO, :]

    beta = jax.nn.sigmoid(p)
    g = -jnp.exp(alog_ref[...]) * _softplus(p + dtb_ref[...])
    r = lax.broadcasted_iota(I32, (CHUNK, CHUNK), 0)
    c = lax.broadcasted_iota(I32, (CHUNK, CHUNK), 1)
    tri = jnp.where(c <= r, 1.0, 0.0).astype(BF16)
    parts = _split_bf16(g, 3)
    gc = jnp.concatenate(
        [sum(_dot(tri, p[ci * CHUNK:(ci + 1) * CHUNK]) for p in parts) for ci in range(tm // CHUNK)], axis=0)
    lane = lax.broadcasted_iota(I32, (tm, LANES), 1)
    slab = jnp.where(lane < DN_HEADS, beta, gc)
    gcol_ref[0] = slab
    rows = slab.T[0:HALO, :]
    for ci in range(tm // CHUNK):
        grow_ref[0, ci] = rows[:, ci * CHUNK:(ci + 1) * CHUNK]


def _inproj(x, mix_g, w_in_pad, conv_a_w, conv_a_g, dn_conv_w, alog_vec, dtb_vec, gmat):
    B, S, D = x.shape
    tm = TM_IN
    full = lambda shape: pl.BlockSpec(shape, lambda b, t: (0,) * len(shape))
    row = lambda width: pl.BlockSpec((1, tm, width), lambda b, t: (b, t, 0))
    out_shape = (
        jax.ShapeDtypeStruct((B, S, CONV_A_DIM), BF16),
        jax.ShapeDtypeStruct((B, S, DN_DIM), F32),
        jax.ShapeDtypeStruct((B, S, DN_DIM), F32),
        jax.ShapeDtypeStruct((B, S, DN_DIM), F32),
        jax.ShapeDtypeStruct((B, S, DN_DIM), F32),
        jax.ShapeDtypeStruct((B, S, LANES), F32),
        jax.ShapeDtypeStruct((B, S // CHUNK, HALO, CHUNK), F32),
    )
    return pl.pallas_call(
        _inproj_kernel,
        grid=(B, S // tm),
        in_specs=[row(D), full((1, D)), full((D, IN_PROJ_PAD)), full((CONV_A_WIDTH, CONV_A_DIM)),
                  full((1, CONV_A_DIM)), full((DN_CONV_WIDTH, 3 * DN_DIM)), full((1, LANES)),
                  full((1, LANES)), full((CONV_A_DIM, CONV_A_DIM))],
        out_specs=(row(CONV_A_DIM), row(DN_DIM), row(DN_DIM), row(DN_DIM), row(DN_DIM), row(LANES),
                   pl.BlockSpec((1, tm // CHUNK, HALO, CHUNK), lambda b, t: (b, t, 0, 0))),
        out_shape=out_shape,
        scratch_shapes=[pltpu.VMEM((tm + HALO, CONV_A_DIM), F32)] + [pltpu.VMEM((tm + HALO, DN_DIM), F32)] * 3,
        compiler_params=pltpu.CompilerParams(
            dimension_semantics=("arbitrary", "arbitrary"), vmem_limit_bytes=VMEM_LIMIT),
        name="inproj",
    )(x, mix_g, w_in_pad, conv_a_w, conv_a_g, dn_conv_w, alog_vec, dtb_vec, gmat)


def _delta_prep_kernel(q_ref, k_ref, v_ref, gcol_ref, grow_ref,
                       u_ref, w_ref, qd_ref, kd_ref, at_ref):
    tm = q_ref.shape[1]
    ri = lax.broadcasted_iota(I32, (CHUNK, CHUNK), 0)
    ci = lax.broadcasted_iota(I32, (CHUNK, CHUNK), 1)
    causal = ci <= ri
    strict = ci < ri
    eye = jnp.where(ci == ri, 1.0, 0.0).astype(F32)
    lane = lax.broadcasted_iota(I32, (CHUNK, LANES), 1)
    left_half = lane < CHUNK

    def setup(c, h):
        rows = slice(c * CHUNK, (c + 1) * CHUNK)
        cols = slice(h * DN_HEAD_DIM, (h + 1) * DN_HEAD_DIM)
        slab = gcol_ref[0, rows, :]
        beta = jnp.sum(jnp.where(lane == h, slab, 0.0), axis=-1, keepdims=True)
        gcc = jnp.sum(jnp.where(lane == h + DN_HEADS, slab, 0.0), axis=-1, keepdims=True)
        gcr = grow_ref[0, c, h + DN_HEADS:h + DN_HEADS + 1, :]
        diff = gcc - gcr
        decay = jnp.where(causal, jnp.exp(jnp.where(causal, diff, 0.0)), 0.0)
        q = q_ref[0, rows, cols]
        k = k_ref[0, rows, cols]
        v = v_ref[0, rows, cols]
        kb = k * beta
        kq = _dot_nt(jnp.concatenate([kb.astype(BF16), q.astype(BF16)], axis=0), k.astype(BF16))
        L = jnp.where(strict, kq[:CHUNK] * decay, 0.0)
        intra = kq[CHUNK:] * decay
        egc = jnp.exp(gcc)
        gl = gcr[:, CHUNK - 1:CHUNK]
        qd_ref[0, rows, cols] = (q * egc).astype(BF16)
        kd_ref[0, rows, cols] = (k * jnp.exp(gl - gcc)).astype(BF16)
        at_ref[0, rows, h * CHUNK:(h + 1) * CHUNK] = intra.astype(BF16)
        return -L, jnp.concatenate([v * beta, kb * egc], axis=-1).astype(BF16)

    def first_stage(m):
        m16 = m.astype(BF16)
        return jnp.concatenate([eye + m, _dot(m16, m16)], axis=-1)

    def stage(r):
        return _dot(r[:, CHUNK:].astype(BF16), r.astype(BF16)) + jnp.where(left_half, r, 0.0)

    def last_stage(r):
        return r[:, :CHUNK] + _dot(r[:, CHUNK:].astype(BF16), r[:, :CHUNK].astype(BF16))

    def solve(c, h, t, rhs):
        rows = slice(c * CHUNK, (c + 1) * CHUNK)
        cols = slice(h * DN_HEAD_DIM, (h + 1) * DN_HEAD_DIM)
        uw = _dot(t.astype(BF16), rhs)
        u_ref[0, rows, cols] = uw[:, :DN_HEAD_DIM]
        w_ref[0, rows, cols] = uw[:, DN_HEAD_DIM:].astype(BF16)

    chains = [(c, h) for c in range(tm // CHUNK) for h in range(DN_HEADS)]
    ms, rhss = zip(*[setup(c, h) for c, h in chains])
    rs = [first_stage(m) for m in ms]
    for _ in range(4):
        rs = [stage(r) for r in rs]
    ts = [last_stage(r) for r in rs]
    for (c, h), t, rhs in zip(chains, ts, rhss):
        solve(c, h, t, rhs)


def _delta_prep(q, k, v, gcol, grow):
    B, S, _ = q.shape
    tm = TM_PREP
    row = lambda width: pl.BlockSpec((1, tm, width), lambda b, t: (b, t, 0))
    return pl.pallas_call(
        _delta_prep_kernel,
        grid=(B, S // tm),
        in_specs=[row(DN_DIM), row(DN_DIM), row(DN_DIM), row(LANES),
                  pl.BlockSpec((1, tm // CHUNK, HALO, CHUNK), lambda b, t: (b, t, 0, 0))],
        out_specs=(row(DN_DIM), row(DN_DIM), row(DN_DIM), row(DN_DIM), row(DN_HEADS * CHUNK)),
        out_shape=(jax.ShapeDtypeStruct((B, S, DN_DIM), F32),
                   jax.ShapeDtypeStruct((B, S, DN_DIM), BF16),
                   jax.ShapeDtypeStruct((B, S, DN_DIM), BF16),
                   jax.ShapeDtypeStruct((B, S, DN_DIM), BF16),
                   jax.ShapeDtypeStruct((B, S, DN_HEADS * CHUNK), BF16)),
        compiler_params=pltpu.CompilerParams(
            dimension_semantics=("arbitrary", "arbitrary"), vmem_limit_bytes=VMEM_LIMIT),
        name="delta_prep",
    )(q, k, v, gcol, grow)


def _delta_scan_kernel(u_ref, w_ref, qd_ref, kd_ref, at_ref, zg_ref, grow_ref, ng_ref, o_ref, st_ref):
    nb, tm = u_ref.shape[0], u_ref.shape[1]

    @pl.when(pl.program_id(1) == 0)
    def _():
        st_ref[...] = jnp.zeros(st_ref.shape, F32)

    def step(c, carry):
        r0 = pl.multiple_of(c * CHUNK, CHUNK)
        rows = pl.ds(r0, CHUNK)
        chains = [(b, h) for b in range(nb) for h in range(DN_HEADS)]
        cols = lambda h: slice(h * DN_HEAD_DIM, (h + 1) * DN_HEAD_DIM)
        sts = [st_ref[b, h] for b, h in chains]
        st16s = [st.astype(BF16) for st in sts]
        wqs = [_dot(jnp.concatenate([w_ref[b, rows, cols(h)], qd_ref[b, rows, cols(h)]], axis=0), st16)
               for (b, h), st16 in zip(chains, st16s)]
        vn16s = [(u_ref[b, rows, cols(h)] - wq[:CHUNK]).astype(BF16) for (b, h), wq in zip(chains, wqs)]
        upd = [_dot_tn(kd_ref[b, rows, cols(h)], vn16) for (b, h), vn16 in zip(chains, vn16s)]
        avs = [_dot(at_ref[b, rows, h * CHUNK:(h + 1) * CHUNK], vn16) for (b, h), vn16 in zip(chains, vn16s)]
        for (b, h), st, up in zip(chains, sts, upd):
            gl = grow_ref[b, c, h + DN_HEADS:h + DN_HEADS + 1, CHUNK - 1:CHUNK]
            st_ref[b, h] = st * jnp.exp(gl) + up
        for (b, h), wq, av in zip(chains, wqs, avs):
            o = wq[CHUNK:] + av
            on = o * lax.rsqrt(jnp.mean(o * o, axis=-1, keepdims=True) + EPS) * ng_ref[...]
            o_ref[b, rows, cols(h)] = (on * zg_ref[b, rows, cols(h)]).astype(o_ref.dtype)
        return carry

    lax.fori_loop(0, tm // CHUNK, step, 0)


def _delta_scan(u, w, qd, kd, at, zg, grow, norm_g):
    B, S, _ = u.shape
    tm = TM_SCAN
    nb = SCAN_BATCH
    row = lambda width: pl.BlockSpec((nb, tm, width), lambda b, t: (b, t, 0))
    return pl.pallas_call(
        _delta_scan_kernel,
        grid=(B // nb, S // tm),
        in_specs=[row(DN_DIM), row(DN_DIM), row(DN_DIM), row(DN_DIM), row(DN_HEADS * CHUNK), row(DN_DIM),
                  pl.BlockSpec((nb, tm // CHUNK, HALO, CHUNK), lambda b, t: (b, t, 0, 0)),
                  pl.BlockSpec((1, DN_HEAD_DIM), lambda b, t: (0, 0))],
        out_specs=row(DN_DIM),
        out_shape=jax.ShapeDtypeStruct((B, S, DN_DIM), BF16),
        scratch_shapes=[pltpu.VMEM((nb, DN_HEADS, DN_HEAD_DIM, DN_HEAD_DIM), F32)],
        compiler_params=pltpu.CompilerParams(
            dimension_semantics=("arbitrary", "arbitrary"), vmem_limit_bytes=VMEM_LIMIT),
        name="delta_scan",
    )(u, w, qd, kd, at, zg, grow, norm_g)


def _outproj_kernel(ya_ref, yb_ref, x_ref, wo_ref, g_ref, rw_ref,
                    x1_ref, h2_ref, ri_ref, rg_ref, cnt_ref, base_ref):
    tm = x_ref.shape[0]

    @pl.when(pl.program_id(0) == 0)
    def _():
        base_ref[...] = jnp.zeros(base_ref.shape, F32)

    y = _dot(jnp.concatenate([ya_ref[...], yb_ref[...]], axis=-1), wo_ref[...])
    x1 = x_ref[...] + y
    x1_ref[...] = x1
    h = x1 * lax.rsqrt(jnp.mean(x1 * x1, axis=-1, keepdims=True) + EPS) * g_ref[...]
    _store_rows(h2_ref, h)

    h_hi, h_lo = _split_bf16(h, 2)
    hi_prod = _dot(h_hi, rw_ref[...])
    logits = hi_prod[:, :LANES] + (hi_prod[:, LANES:] + _dot(h_lo, rw_ref[:, :LANES]))
    lane = lax.broadcasted_iota(I32, (tm, LANES), 1)
    lanef = lane.astype(F32)
    neg = jnp.float32(-1e30)
    big = jnp.float32(1e9)
    is_g = lane < N_GROUPS
    gl = jnp.where(is_g, logits, neg)
    gmax = jnp.max(gl, axis=-1, keepdims=True)
    gidx = jnp.min(jnp.where(gl == gmax, lanef, big), axis=-1, keepdims=True)
    gsum = jnp.sum(jnp.where(is_g, jnp.exp(gl - gmax), 0.0), axis=-1, keepdims=True)
    gprob = 1.0 / gsum
    lo = N_GROUPS + EXPERTS_PER_GROUP * gidx
    emask = (lanef >= lo) & (lanef < lo + EXPERTS_PER_GROUP)
    el = jnp.where(emask, logits, neg)
    e1 = jnp.max(el, axis=-1, keepdims=True)
    i1 = jnp.min(jnp.where(el == e1, lanef, big), axis=-1, keepdims=True)
    el2 = jnp.where(lanef == i1, neg, el)
    e2 = jnp.max(el2, axis=-1, keepdims=True)
    i2 = jnp.min(jnp.where(el2 == e2, lanef, big), axis=-1, keepdims=True)
    r = jnp.exp(e2 - e1)
    gate1 = gprob / (1.0 + r)
    gate2 = gprob * r / (1.0 + r)
    id1 = i1 - N_GROUPS
    id2 = i2 - N_GROUPS

    oh1 = jnp.where(lanef == id1, 1.0, 0.0).astype(F32)
    oh2 = jnp.where(lanef == id2, 1.0, 0.0).astype(F32)
    oh = oh1 + oh2
    rr = lax.broadcasted_iota(I32, (tm, tm), 0)
    cc = lax.broadcasted_iota(I32, (tm, tm), 1)
    tri = jnp.where(cc < rr, 1.0, 0.0).astype(BF16)
    before = _dot(tri, oh.astype(BF16)) + base_ref[...]
    rank1 = jnp.sum(oh1 * before, axis=-1, keepdims=True)
    rank2 = jnp.sum(oh2 * before, axis=-1, keepdims=True)
    base_ref[...] = base_ref[...] + jnp.sum(oh, axis=0, keepdims=True)
    cnt_ref[...] = base_ref[...]

    ri = jnp.where(lane == 0, id1, jnp.where(lane == 1, id2,
                   jnp.where(lane == 2, rank1, jnp.where(lane == 3, rank2, 0.0))))
    ri_ref[...] = ri.astype(I32)
    rg_ref[...] = jnp.where(lane == 0, gate1, jnp.where(lane == 1, gate2, 0.0))


def _outproj(ya, yb, x, w_out16, ffn_g, rw):
    N, D = x.shape
    tm = TM_OUT
    row = lambda width: pl.BlockSpec((tm, width), lambda i: (i, 0))
    full = lambda shape: pl.BlockSpec(shape, lambda i: (0,) * len(shape))
    return pl.pallas_call(
        _outproj_kernel,
        grid=(N // tm,),
        in_specs=[row(CONV_A_DIM), row(DN_DIM), row(D), full((D, D)), full((1, D)), full((D, 2 * LANES))],
        out_specs=(row(D), pl.BlockSpec((tm * ROW_SLAB, LANES), lambda i: (i, 0)), row(LANES), row(LANES),
                   full((1, LANES))),
        out_shape=(jax.ShapeDtypeStruct((N, D), F32),
                   jax.ShapeDtypeStruct((N * ROW_SLAB, LANES), F32),
                   jax.ShapeDtypeStruct((N, LANES), I32),
                   jax.ShapeDtypeStruct((N, LANES), F32),
                   jax.ShapeDtypeStruct((1, LANES), F32)),
        scratch_shapes=[pltpu.VMEM((1, LANES), F32)],
        compiler_params=pltpu.CompilerParams(
            dimension_semantics=("arbitrary",), vmem_limit_bytes=VMEM_LIMIT),
        name="outproj",
    )(ya, yb, x, w_out16, ffn_g, rw)


def _slots_kernel(ri_ref, start_ref, o_ref):
    rif = ri_ref[...].astype(F32)
    lane = lax.broadcasted_iota(I32, rif.shape, 1)
    lanef = lane.astype(F32)

    def slot(k):
        start = jnp.sum(jnp.where(lanef == rif[:, k:k + 1], start_ref[...], 0.0), axis=-1, keepdims=True)
        return start + rif[:, TOP_K + k:TOP_K + k + 1]

    o_ref[...] = jnp.where(lane == 0, slot(0), jnp.where(lane == 1, slot(1), 0.0)).astype(I32)


def _slots(ri, seg_start):
    N = ri.shape[0]
    tm = TM_SLOTS
    return pl.pallas_call(
        _slots_kernel,
        grid=(N // tm,),
        in_specs=[pl.BlockSpec((tm, LANES), lambda i: (i, 0)), pl.BlockSpec((1, LANES), lambda i: (0, 0))],
        out_specs=pl.BlockSpec((tm, LANES), lambda i: (i, 0)),
        out_shape=jax.ShapeDtypeStruct((N, LANES), I32),
        compiler_params=pltpu.CompilerParams(dimension_semantics=("arbitrary",)),
        name="slots",
    )(ri, _lane_vec(seg_start, 0))


def _slot_of(slots_ref, t, k):
    return slots_ref[TOP_K * t + k]


def _dispatch_kernel(seg_ref, slots_ref, h_ref, xs_ref, zero_ref, sem, zsem):
    tm = TM_DISPATCH
    bm = FFN_BLOCK
    n_blocks = xs_ref.shape[0] // (bm * ROW_SLAB)

    @pl.when(pl.program_id(0) == 0)
    def _():
        zero_ref[...] = jnp.zeros(zero_ref.shape, F32)

        def pad_copy(d):
            return pltpu.make_async_copy(zero_ref.at[pl.ds(0, ROW_SLAB)], _slab(xs_ref, d), zsem)

        def block_copy(b):
            return pltpu.make_async_copy(zero_ref, _slab_block(xs_ref, b), zsem)

        def each_pad(fn):
            def per_expert(e, carry):
                def per_row(d, c):
                    fn(pad_copy(d))
                    return c
                return lax.fori_loop(seg_ref[e], seg_ref[N_EXPERTS + e], per_row, carry)
            lax.fori_loop(0, N_EXPERTS, per_expert, 0)

            def per_block(b, c):
                fn(block_copy(b))
                return c
            lax.fori_loop(seg_ref[2 * N_EXPERTS], n_blocks, per_block, 0)

        each_pad(lambda cp: cp.start())
        each_pad(lambda cp: cp.wait())

    def issue(t, carry):
        for k in range(TOP_K):
            d = _slot_of(slots_ref, t, k)
            pltpu.make_async_copy(_slab(h_ref, t), _slab(xs_ref, d), sem).start(priority=k)
        return carry

    lax.fori_loop(0, tm, issue, 0, unroll=ISSUE_UNROLL)
    for _ in range(TOP_K):
        pltpu.make_async_copy(h_ref, xs_ref.at[pl.ds(0, tm * ROW_SLAB)], sem).wait()


def _dispatch(seg, slots, h2, n_slots):
    N = h2.shape[0] // ROW_SLAB
    tm = TM_DISPATCH
    grid_spec = pltpu.PrefetchScalarGridSpec(
        num_scalar_prefetch=1,
        grid=(N // tm,),
        in_specs=[pl.BlockSpec((TOP_K * tm,), lambda i, s: (i,), memory_space=pltpu.SMEM),
                  pl.BlockSpec((tm * ROW_SLAB, LANES), lambda i, s: (i, 0))],
        out_specs=pl.BlockSpec(memory_space=pl.ANY),
        scratch_shapes=[pltpu.VMEM((FFN_BLOCK * ROW_SLAB, LANES), F32),
                        pltpu.SemaphoreType.DMA(()), pltpu.SemaphoreType.DMA(())],
    )
    return pl.pallas_call(
        _dispatch_kernel,
        grid_spec=grid_spec,
        out_shape=jax.ShapeDtypeStruct((n_slots * ROW_SLAB, LANES), F32),
        compiler_params=pltpu.CompilerParams(dimension_semantics=("arbitrary",)),
        name="dispatch",
    )(seg, slots, h2)


def _ffn_kernel(blk_ref, xs_ref, wg_ref, wu_ref, wd_ref, ys_ref,
                xbuf, ybuf, wg16, wu16, wd16, xsem, ysem, zsem):
    e = pl.program_id(0)
    bm = FFN_BLOCK
    ring = FFN_RING
    n_blocks = ys_ref.shape[0] // (bm * ROW_SLAB)
    first = blk_ref[e]
    count = blk_ref[N_EXPERTS + e]
    n_used = blk_ref[2 * N_EXPERTS]

    def x_copy(g):
        return pltpu.make_async_copy(_slab_block(xs_ref, g), xbuf.at[g % ring], xsem.at[g % ring])

    def y_copy(g):
        return pltpu.make_async_copy(ybuf.at[g % ring], _slab_block(ys_ref, g), ysem.at[g % ring])

    def when_block(g, fn):
        @pl.when((g >= 0) & (g < n_used))
        def _():
            fn(g)

    @pl.when(e == 0)
    def _():
        for g in range(ring - 1):
            when_block(g, lambda g: x_copy(g).start(priority=RING_PRIORITY))

    @pl.when(count > 0)
    def _():
        wg16[...] = wg_ref[0].astype(BF16)
        wu16[...] = wu_ref[0].astype(BF16)
        wd16[...] = wd_ref[0].astype(BF16)

    def block(j, carry):
        g = first + j
        x_copy(g).wait()
        when_block(g + ring - 1, lambda g: x_copy(g).start(priority=RING_PRIORITY))

        x = _load_rows(xbuf.at[g % ring], bm).astype(BF16)
        a = _dot(x, wg16[...])
        b = _dot(x, wu16[...])
        y = _dot((_silu(a) * b).astype(BF16), wd16[...])

        when_block(g - ring, lambda g: y_copy(g).wait())
        _store_rows(ybuf.at[g % ring], y)
        y_copy(g).start(priority=RING_PRIORITY)
        return carry

    lax.fori_loop(0, count, block, 0)

    @pl.when(e == pl.num_programs(0) - 1)
    def _():
        for back in range(ring, 0, -1):
            when_block(n_used - back, lambda g: y_copy(g).wait())
        ybuf[0] = jnp.zeros(ybuf.shape[1:], F32)

        def zero_copy(g):
            return pltpu.make_async_copy(ybuf.at[0], _slab_block(ys_ref, g), zsem)

        def start(g, c):
            zero_copy(g).start()
            return c

        def wait(g, c):
            zero_copy(g).wait()
            return c

        lax.fori_loop(n_used, n_blocks, start, 0)
        lax.fori_loop(n_used, n_blocks, wait, 0)


def _ffn(blk, xs, w_gate, w_up, w_down):
    bm = FFN_BLOCK
    D = D_MODEL
    weights = lambda shape: pl.BlockSpec((1,) + shape, lambda e, blk: (e, 0, 0))
    grid_spec = pltpu.PrefetchScalarGridSpec(
        num_scalar_prefetch=1,
        grid=(N_EXPERTS,),
        in_specs=[pl.BlockSpec(memory_space=pl.ANY),
                  weights((D, EXPERT_FF)), weights((D, EXPERT_FF)), weights((EXPERT_FF, D))],
        out_specs=pl.BlockSpec(memory_space=pl.ANY),
        scratch_shapes=[pltpu.VMEM((FFN_RING, bm * ROW_SLAB, LANES), F32),
                        pltpu.VMEM((FFN_RING, bm * ROW_SLAB, LANES), F32),
                        pltpu.VMEM((D, EXPERT_FF), BF16), pltpu.VMEM((D, EXPERT_FF), BF16),
                        pltpu.VMEM((EXPERT_FF, D), BF16),
                        pltpu.SemaphoreType.DMA((FFN_RING,)), pltpu.SemaphoreType.DMA((FFN_RING,)),
                        pltpu.SemaphoreType.DMA(())],
    )
    return pl.pallas_call(
        _ffn_kernel,
        grid_spec=grid_spec,
        out_shape=jax.ShapeDtypeStruct(xs.shape, F32),
        compiler_params=pltpu.CompilerParams(
            dimension_semantics=("arbitrary",), vmem_limit_bytes=VMEM_LIMIT),
        name="ffn",
    )(blk, xs, w_gate, w_up, w_down)


def _combine_kernel(slots_ref, slots_next_ref, ys_ref, x1_ref, rg_ref, g_ref, o_ref, buf_ref, sems):
    tm = TM_COMBINE
    i = pl.program_id(0)
    half = i % 2

    def gather(r_ref, s):
        def issue(t, carry):
            for k in range(TOP_K):
                d = _slot_of(r_ref, t, k)
                pltpu.make_async_copy(_slab(ys_ref, d), _slab(buf_ref.at[s, k], t),
                                      sems.at[s]).start(priority=k)
            return carry
        lax.fori_loop(0, tm, issue, 0, unroll=ISSUE_UNROLL)

    @pl.when(i == 0)
    def _():
        gather(slots_ref, 0)

    @pl.when(i + 1 < pl.num_programs(0))
    def _():
        gather(slots_next_ref, 1 - half)

    for k in range(TOP_K):
        pltpu.make_async_copy(ys_ref.at[pl.ds(0, tm * ROW_SLAB)], buf_ref.at[half, k], sems.at[half]).wait()

    rg = rg_ref[...]
    moe = (_load_rows(buf_ref.at[half, 0], tm) * rg[:, 0:1]
           + _load_rows(buf_ref.at[half, 1], tm) * rg[:, 1:2])
    x2 = x1_ref[...] + moe
    o_ref[...] = x2 * lax.rsqrt(jnp.mean(x2 * x2, axis=-1, keepdims=True) + EPS) * g_ref[...]


def _combine(slots, ys, x1, rg, final_g):
    N, D = x1.shape
    tm = TM_COMBINE
    n_tiles = N // tm
    return pl.pallas_call(
        _combine_kernel,
        grid=(n_tiles,),
        in_specs=[pl.BlockSpec((TOP_K * tm,), lambda i: (i,), memory_space=pltpu.SMEM),
                  pl.BlockSpec((TOP_K * tm,), lambda i: (jnp.minimum(i + 1, n_tiles - 1),),
                               memory_space=pltpu.SMEM),
                  pl.BlockSpec(memory_space=pl.ANY),
                  pl.BlockSpec((tm, D), lambda i: (i, 0)),
                  pl.BlockSpec((tm, LANES), lambda i: (i, 0)),
                  pl.BlockSpec((1, D), lambda i: (0, 0))],
        out_specs=pl.BlockSpec((tm, D), lambda i: (i, 0)),
        out_shape=jax.ShapeDtypeStruct((N, D), F32),
        scratch_shapes=[pltpu.VMEM((2, TOP_K, tm * ROW_SLAB, LANES), F32), pltpu.SemaphoreType.DMA((2,))],
        compiler_params=pltpu.CompilerParams(
            dimension_semantics=("arbitrary",), vmem_limit_bytes=VMEM_LIMIT),
        name="combine",
    )(slots, slots, ys, x1, rg, final_g)


def _lane_vec(values, offset):
    return jnp.zeros((1, LANES), F32).at[0, offset:offset + values.shape[0]].set(values.astype(F32))


def kernel(x, mix_norm_g, w_in, conv_a_w, conv_a_norm_g, dn_conv_w, dn_a_log, dn_dt_bias, dn_norm_g,
           w_out, ffn_norm_g, router_group_w, router_expert_w, w_gate, w_up, w_down, final_norm_g):
    B, S, D = x.shape
    N = B * S
    depth = w_in.shape[0]
    assert depth == 1, "single-layer block: the final RMSNorm is fused into the layer's combine step"
    group_of = jnp.arange(CONV_A_DIM, dtype=I32) // CONV_A_GROUP_DIM
    gmat = jnp.where(group_of[:, None] == group_of[None, :], 1.0 / CONV_A_GROUP_DIM, 0.0).astype(BF16)
    bm = FFN_BLOCK
    n_blocks = (N * TOP_K) // bm + N_EXPERTS
    for l in range(depth):
        w_in_pad = jnp.pad(w_in[l], ((0, 0), (0, IN_PROJ_PAD - IN_PROJ_DIM))).astype(BF16)
        ya, q, k, v, zg, gcol, grow = _inproj(
            x, mix_norm_g[l][None, :], w_in_pad, conv_a_w[l], conv_a_norm_g[l][None, :], dn_conv_w[l],
            _lane_vec(dn_a_log[l], DN_HEADS), _lane_vec(dn_dt_bias[l], DN_HEADS), gmat)
        u, w, qd, kd, at = _delta_prep(q, k, v, gcol, grow)
        yb = _delta_scan(u, w, qd, kd, at, zg, grow, dn_norm_g[l][None, :])
        rw = jnp.pad(jnp.concatenate([router_group_w[l], router_expert_w[l]], axis=1),
                     ((0, 0), (0, LANES - N_GROUPS - N_EXPERTS)))
        rw_hi = rw.astype(BF16)
        rw = jnp.concatenate([rw_hi, (rw - rw_hi.astype(F32)).astype(BF16)], axis=1)
        x1, h2, ri, rg, cnt = _outproj(ya.reshape(N, CONV_A_DIM), yb.reshape(N, DN_DIM), x.reshape(N, D),
                                       w_out[l].astype(BF16), ffn_norm_g[l][None, :], rw)
        counts = cnt[0, :N_EXPERTS].astype(I32)
        padded = (counts + bm - 1) // bm * bm
        seg_end = jnp.cumsum(padded).astype(I32)
        seg_start = seg_end - padded
        n_used = (seg_end[-1:] // bm)
        slots = _slots(ri, seg_start)[:, :TOP_K].reshape(-1)
        seg = jnp.concatenate([seg_start + counts, seg_end, n_used])
        xs = _dispatch(seg, slots, h2, n_blocks * bm)
        blk = jnp.concatenate([seg_start // bm, padded // bm, n_used])
        ys = _ffn(blk, xs, w_gate[l], w_up[l], w_down[l])
        x = _combine(slots, ys, x1, rg, final_norm_g[None, :]).reshape(B, S, D)
    return x
```

```python
import functools

import jax
import jax.numpy as jnp
from jax import lax
from jax.experimental import pallas as pl
from jax.experimental.pallas import tpu as pltpu

F32 = jnp.float32
BF16 = jnp.bfloat16
I32 = jnp.int32

D_MODEL = 1024
CHUNK = 64
CONV_A_GROUP_DIM = 64
CONV_A_DIM = 512
CONV_A_WIDTH = 3
DN_HEADS = 4
DN_HEAD_DIM = 128
DN_DIM = 512
DN_CONV_WIDTH = 4
IN_PROJ_DIM = 3 * CONV_A_DIM + 4 * DN_DIM + 2 * DN_HEADS
N_GROUPS = 4
EXPERTS_PER_GROUP = 8
N_EXPERTS = 32
TOP_K = 2
EXPERT_FF = 512
EPS = 1e-6

LANES = 128
HALO = 8
IN_PROJ_PAD = 29 * LANES
GATE_COL = 3 * CONV_A_DIM + 4 * DN_DIM

TM_IN = 512
TM_PREP = 256
SCAN_BATCH = 4
TM_SCAN = 512
TM_OUT = 512
FFN_BLOCK = 256
TM_SLOTS = 2048
TM_DISPATCH = 1024
TM_COMBINE = 256
FFN_RING = 4
RING_PRIORITY = 1
ISSUE_UNROLL = 4
VMEM_LIMIT = 56 * 1024 * 1024


def _dot(a, b):
    return jnp.dot(a, b, preferred_element_type=F32)


def _dot_nt(a, b):
    return lax.dot_general(a, b, (((1,), (1,)), ((), ())), preferred_element_type=F32)


def _dot_tn(a, b):
    return lax.dot_general(a, b, (((0,), (0,)), ((), ())), preferred_element_type=F32)


def _split_bf16(x, parts):
    out = []
    for _ in range(parts):
        p = x.astype(BF16)
        out.append(p)
        x = x - p.astype(F32)
    return out


def _silu(x):
    return x * jax.nn.sigmoid(x)


ROW_SLAB = D_MODEL // LANES


def _store_rows(ref, val):
    m = val.shape[0]
    for c in range(ROW_SLAB):
        ref[pl.ds(c, m, stride=ROW_SLAB), :] = val[:, c * LANES:(c + 1) * LANES]


def _load_rows(ref, m):
    return jnp.concatenate([ref[pl.ds(c, m, stride=ROW_SLAB), :] for c in range(ROW_SLAB)], axis=-1)


def _slab(ref, row):
    return ref.at[pl.ds(pl.multiple_of(row * ROW_SLAB, ROW_SLAB), ROW_SLAB)]


def _slab_block(ref, block):
    n = FFN_BLOCK * ROW_SLAB
    return ref.at[pl.ds(pl.multiple_of(block * n, n), n)]


def _causal_conv(ext, w):
    taps = w.shape[0]
    acc = None
    for j in range(taps):
        shift = taps - 1 - j
        rows = pltpu.roll(ext, shift, 0)[HALO:] if shift else ext[HALO:]
        term = w[j:j + 1, :] * rows
        acc = term if acc is None else acc + term
    return acc


def _softplus(x):
    return jnp.maximum(x, 0.0) + jnp.log1p(jnp.exp(-jnp.abs(x)))


def _inproj_kernel(x_ref, g_ref, w_ref, caw_ref, cag_ref, dcw_ref, alog_ref, dtb_ref, gmat_ref,
                   ya_ref, q_ref, k_ref, v_ref, zg_ref, gcol_ref, grow_ref,
                   exta_ref, *extq_refs):
    tm = x_ref.shape[1]
    ext_refs = (exta_ref,) + extq_refs

    @pl.when(pl.program_id(1) == 0)
    def _():
        for ext_ref in ext_refs:
            ext_ref[0:HALO, :] = jnp.zeros((HALO, ext_ref.shape[1]), F32)

    x = x_ref[0]
    ms = jnp.mean(x * x, axis=-1, keepdims=True)
    hb = (x * lax.rsqrt(ms + EPS) * g_ref[...]).astype(BF16)

    def proj(c0, width):
        return _dot(hb, w_ref[:, c0:c0 + width])

    base = 3 * CONV_A_DIM

    def qkv_project(i):
        extq_refs[i][HALO:HALO + tm, :] = proj(base + i * DN_DIM, DN_DIM)

    def qkv_finish(i, out_ref):
        cols = slice(i * DN_DIM, (i + 1) * DN_DIM)
        s = _silu(_causal_conv(extq_refs[i][...], dcw_ref[:, cols]))
        if i == 2:
            out_ref[0] = s
        else:
            for h in range(DN_HEADS):
                sh = s[:, h * DN_HEAD_DIM:(h + 1) * DN_HEAD_DIM]
                inv = lax.rsqrt(jnp.sum(sh * sh, axis=-1, keepdims=True) + EPS)
                sh = sh * inv
                if i == 0:
                    sh = sh * (DN_HEAD_DIM ** -0.5)
                out_ref[0, :, h * DN_HEAD_DIM:(h + 1) * DN_HEAD_DIM] = sh

    def mixer_a_finish(a_b):
        y = a_b * _causal_conv(exta_ref[...], caw_ref[...])
        ysq = y * y
        hi = ysq.astype(BF16)
        lo = (ysq - hi.astype(F32)).astype(BF16)
        gmean = _dot(hi, gmat_ref[...]) + _dot(lo, gmat_ref[...])
        ya_ref[0] = (y * lax.rsqrt(gmean + EPS) * cag_ref[...]).astype(ya_ref.dtype)

    qkv_project(0)
    qkv_project(1)
    exta_ref[HALO:HALO + tm, :] = proj(2 * CONV_A_DIM, CONV_A_DIM) * proj(0, CONV_A_DIM)
    qkv_finish(0, q_ref)
    qkv_project(2)
    a_b = proj(CONV_A_DIM, CONV_A_DIM)
    qkv_finish(1, k_ref)
    z = proj(base + 3 * DN_DIM, DN_DIM)
    mixer_a_finish(a_b)
    p = proj(GATE_COL, LANES)
    qkv_finish(2, v_ref)
    zg_ref[0] = _silu(z)
    for ext_ref in ext_refs:
        ext_ref[0:HALO, :] = ext_ref[tm:tm + HALO, :]

    beta = jax.nn.sigmoid(p)
    g = -jnp.exp(alog_ref[...]) * _softplus(p + dtb_ref[...])
    r = lax.broadcasted_iota(I32, (CHUNK, CHUNK), 0)
    c = lax.broadcasted_iota(I32, (CHUNK, CHUNK), 1)
    tri = jnp.where(c <= r, 1.0, 0.0).astype(BF16)
    parts = _split_bf16(g, 3)
    gc = jnp.concatenate(
        [sum(_dot(tri, p[ci * CHUNK:(ci + 1) * CHUNK]) for p in parts) for ci in range(tm // CHUNK)], axis=0)
    lane = lax.broadcasted_iota(I32, (tm, LANES), 1)
    slab = jnp.where(lane < DN_HEADS, beta, gc)
    gcol_ref[0] = slab
    rows = slab.T[0:HALO, :]
    for ci in range(tm // CHUNK):
        grow_ref[0, ci] = rows[:, ci * CHUNK:(ci + 1) * CHUNK]


def _inproj(x, mix_g, w_in_pad, conv_a_w, conv_a_g, dn_conv_w, alog_vec, dtb_vec, gmat):
    B, S, D = x.shape
    tm = TM_IN
    full = lambda shape: pl.BlockSpec(shape, lambda b, t: (0,) * len(shape))
    row = lambda width: pl.BlockSpec((1, tm, width), lambda b, t: (b, t, 0))
    out_shape = (
        jax.ShapeDtypeStruct((B, S, CONV_A_DIM), BF16),
        jax.ShapeDtypeStruct((B, S, DN_DIM), F32),
        jax.ShapeDtypeStruct((B, S, DN_DIM), F32),
        jax.ShapeDtypeStruct((B, S, DN_DIM), F32),
        jax.ShapeDtypeStruct((B, S, DN_DIM), F32),
        jax.ShapeDtypeStruct((B, S, LANES), F32),
        jax.ShapeDtypeStruct((B, S // CHUNK, HALO, CHUNK), F32),
    )
    return pl.pallas_call(
        _inproj_kernel,
        grid=(B, S // tm),
        in_specs=[row(D), full((1, D)), full((D, IN_PROJ_PAD)), full((CONV_A_WIDTH, CONV_A_DIM)),
                  full((1, CONV_A_DIM)), full((DN_CONV_WIDTH, 3 * DN_DIM)), full((1, LANES)),
                  full((1, LANES)), full((CONV_A_DIM, CONV_A_DIM))],
        out_specs=(row(CONV_A_DIM), row(DN_DIM), row(DN_DIM), row(DN_DIM), row(DN_DIM), row(LANES),
                   pl.BlockSpec((1, tm // CHUNK, HALO, CHUNK), lambda b, t: (b, t, 0, 0))),
        out_shape=out_shape,
        scratch_shapes=[pltpu.VMEM((tm + HALO, CONV_A_DIM), F32)] + [pltpu.VMEM((tm + HALO, DN_DIM), F32)] * 3,
        compiler_params=pltpu.CompilerParams(
            dimension_semantics=("arbitrary", "arbitrary"), vmem_limit_bytes=VMEM_LIMIT),
        name="inproj",
    )(x, mix_g, w_in_pad, conv_a_w, conv_a_g, dn_conv_w, alog_vec, dtb_vec, gmat)


def _delta_prep_kernel(q_ref, k_ref, v_ref, gcol_ref, grow_ref,
                       u_ref, w_ref, qd_ref, kd_ref, at_ref):
    tm = q_ref.shape[1]
    ri = lax.broadcasted_iota(I32, (CHUNK, CHUNK), 0)
    ci = lax.broadcasted_iota(I32, (CHUNK, CHUNK), 1)
    causal = ci <= ri
    strict = ci < ri
    eye = jnp.where(ci == ri, 1.0, 0.0).astype(F32)
    lane = lax.broadcasted_iota(I32, (CHUNK, LANES), 1)
    left_half = lane < CHUNK

    def setup(c, h):
        rows = slice(c * CHUNK, (c + 1) * CHUNK)
        cols = slice(h * DN_HEAD_DIM, (h + 1) * DN_HEAD_DIM)
        slab = gcol_ref[0, rows, :]
        beta = jnp.sum(jnp.where(lane == h, slab, 0.0), axis=-1, keepdims=True)
        gcc = jnp.sum(jnp.where(lane == h + DN_HEADS, slab, 0.0), axis=-1, keepdims=True)
        gcr = grow_ref[0, c, h + DN_HEADS:h + DN_HEADS + 1, :]
        diff = gcc - gcr
        decay = jnp.where(causal, jnp.exp(jnp.where(causal, diff, 0.0)), 0.0)
        q = q_ref[0, rows, cols]
        k = k_ref[0, rows, cols]
        v = v_ref[0, rows, cols]
        kb = k * beta
        kq = _dot_nt(jnp.concatenate([kb.astype(BF16), q.astype(BF16)], axis=0), k.astype(BF16))
        L = jnp.where(strict, kq[:CHUNK] * decay, 0.0)
        intra = kq[CHUNK:] * decay
        egc = jnp.exp(gcc)
        gl = gcr[:, CHUNK - 1:CHUNK]
        qd_ref[0, rows, cols] = (q * egc).astype(BF16)
        kd_ref[0, rows, cols] = (k * jnp.exp(gl - gcc)).astype(BF16)
        at_ref[0, rows, h * CHUNK:(h + 1) * CHUNK] = intra.astype(BF16)
        return -L, jnp.concatenate([v * beta, kb * egc], axis=-1).astype(BF16)

    def first_stage(m):
        m16 = m.astype(BF16)
        return jnp.concatenate([eye + m, _dot(m16, m16)], axis=-1)

    def stage(r):
        return _dot(r[:, CHUNK:].astype(BF16), r.astype(BF16)) + jnp.where(left_half, r, 0.0)

    def last_stage(r):
        return r[:, :CHUNK] + _dot(r[:, CHUNK:].astype(BF16), r[:, :CHUNK].astype(BF16))

    def solve(c, h, t, rhs):
        rows = slice(c * CHUNK, (c + 1) * CHUNK)
        cols = slice(h * DN_HEAD_DIM, (h + 1) * DN_HEAD_DIM)
        uw = _dot(t.astype(BF16), rhs)
        u_ref[0, rows, cols] = uw[:, :DN_HEAD_DIM]
        w_ref[0, rows, cols] = uw[:, DN_HEAD_DIM:].astype(BF16)

    chains = [(c, h) for c in range(tm // CHUNK) for h in range(DN_HEADS)]
    ms, rhss = zip(*[setup(c, h) for c, h in chains])
    rs = [first_stage(m) for m in ms]
    for _ in range(4):
        rs = [stage(r) for r in rs]
    ts = [last_stage(r) for r in rs]
    for (c, h), t, rhs in zip(chains, ts, rhss):
        solve(c, h, t, rhs)


def _delta_prep(q, k, v, gcol, grow):
    B, S, _ = q.shape
    tm = TM_PREP
    row = lambda width: pl.BlockSpec((1, tm, width), lambda b, t: (b, t, 0))
    return pl.pallas_call(
        _delta_prep_kernel,
        grid=(B, S // tm),
        in_specs=[row(DN_DIM), row(DN_DIM), row(DN_DIM), row(LANES),
                  pl.BlockSpec((1, tm // CHUNK, HALO, CHUNK), lambda b, t: (b, t, 0, 0))],
        out_specs=(row(DN_DIM), row(DN_DIM), row(DN_DIM), row(DN_DIM), row(DN_HEADS * CHUNK)),
        out_shape=(jax.ShapeDtypeStruct((B, S, DN_DIM), F32),
                   jax.ShapeDtypeStruct((B, S, DN_DIM), BF16),
                   jax.ShapeDtypeStruct((B, S, DN_DIM), BF16),
                   jax.ShapeDtypeStruct((B, S, DN_DIM), BF16),
                   jax.ShapeDtypeStruct((B, S, DN_HEADS * CHUNK), BF16)),
        compiler_params=pltpu.CompilerParams(
            dimension_semantics=("arbitrary", "arbitrary"), vmem_limit_bytes=VMEM_LIMIT),
        name="delta_prep",
    )(q, k, v, gcol, grow)


def _delta_scan_kernel(u_ref, w_ref, qd_ref, kd_ref, at_ref, zg_ref, grow_ref, ng_ref, o_ref, st_ref):
    nb, tm = u_ref.shape[0], u_ref.shape[1]

    @pl.when(pl.program_id(1) == 0)
    def _():
        st_ref[...] = jnp.zeros(st_ref.shape, F32)

    def step(c, carry):
        r0 = pl.multiple_of(c * CHUNK, CHUNK)
        rows = pl.ds(r0, CHUNK)
        chains = [(b, h) for b in range(nb) for h in range(DN_HEADS)]
        cols = lambda h: slice(h * DN_HEAD_DIM, (h + 1) * DN_HEAD_DIM)
        sts = [st_ref[b, h] for b, h in chains]
        st16s = [st.astype(BF16) for st in sts]
        wqs = [_dot(jnp.concatenate([w_ref[b, rows, cols(h)], qd_ref[b, rows, cols(h)]], axis=0), st16)
               for (b, h), st16 in zip(chains, st16s)]
        vn16s = [(u_ref[b, rows, cols(h)] - wq[:CHUNK]).astype(BF16) for (b, h), wq in zip(chains, wqs)]
        upd = [_dot_tn(kd_ref[b, rows, cols(h)], vn16) for (b, h), vn16 in zip(chains, vn16s)]
        avs = [_dot(at_ref[b, rows, h * CHUNK:(h + 1) * CHUNK], vn16) for (b, h), vn16 in zip(chains, vn16s)]
        for (b, h), st, up in zip(chains, sts, upd):
            gl = grow_ref[b, c, h + DN_HEADS:h + DN_HEADS + 1, CHUNK - 1:CHUNK]
            st_ref[b, h] = st * jnp.exp(gl) + up
        for (b, h), wq, av in zip(chains, wqs, avs):
            o = wq[CHUNK:] + av
            on = o * lax.rsqrt(jnp.mean(o * o, axis=-1, keepdims=True) + EPS) * ng_ref[...]
            o_ref[b, rows, cols(h)] = (on * zg_ref[b, rows, cols(h)]).astype(o_ref.dtype)
        return carry

    lax.fori_loop(0, tm // CHUNK, step, 0)


def _delta_scan(u, w, qd, kd, at, zg, grow, norm_g):
    B, S, _ = u.shape
    tm = TM_SCAN
    nb = SCAN_BATCH
    row = lambda width: pl.BlockSpec((nb, tm, width), lambda b, t: (b, t, 0))
    return pl.pallas_call(
        _delta_scan_kernel,
        grid=(B // nb, S // tm),
        in_specs=[row(DN_DIM), row(DN_DIM), row(DN_DIM), row(DN_DIM), row(DN_HEADS * CHUNK), row(DN_DIM),
                  pl.BlockSpec((nb, tm // CHUNK, HALO, CHUNK), lambda b, t: (b, t, 0, 0)),
                  pl.BlockSpec((1, DN_HEAD_DIM), lambda b, t: (0, 0))],
        out_specs=row(DN_DIM),
        out_shape=jax.ShapeDtypeStruct((B, S, DN_DIM), BF16),
        scratch_shapes=[pltpu.VMEM((nb, DN_HEADS, DN_HEAD_DIM, DN_HEAD_DIM), F32)],
        compiler_params=pltpu.CompilerParams(
            dimension_semantics=("arbitrary", "arbitrary"), vmem_limit_bytes=VMEM_LIMIT),
        name="delta_scan",
    )(u, w, qd, kd, at, zg, grow, norm_g)


def _outproj_kernel(ya_ref, yb_ref, x_ref, wo_ref, g_ref, rw_ref,
                    x1_ref, h2_ref, ri_ref, rg_ref, cnt_ref, base_ref):
    tm = x_ref.shape[0]

    @pl.when(pl.program_id(0) == 0)
    def _():
        base_ref[...] = jnp.zeros(base_ref.shape, F32)

    y = _dot(jnp.concatenate([ya_ref[...], yb_ref[...]], axis=-1), wo_ref[...])
    x1 = x_ref[...] + y
    x1_ref[...] = x1
    h = x1 * lax.rsqrt(jnp.mean(x1 * x1, axis=-1, keepdims=True) + EPS) * g_ref[...]
    _store_rows(h2_ref, h)

    h_hi, h_lo = _split_bf16(h, 2)
    hi_prod = _dot(h_hi, rw_ref[...])
    logits = hi_prod[:, :LANES] + (hi_prod[:, LANES:] + _dot(h_lo, rw_ref[:, :LANES]))
    lane = lax.broadcasted_iota(I32, (tm, LANES), 1)
    lanef = lane.astype(F32)
    neg = jnp.float32(-1e30)
    big = jnp.float32(1e9)
    is_g = lane < N_GROUPS
    gl = jnp.where(is_g, logits, neg)
    gmax = jnp.max(gl, axis=-1, keepdims=True)
    gidx = jnp.min(jnp.where(gl == gmax, lanef, big), axis=-1, keepdims=True)
    gsum = jnp.sum(jnp.where(is_g, jnp.exp(gl - gmax), 0.0), axis=-1, keepdims=True)
    gprob = 1.0 / gsum
    lo = N_GROUPS + EXPERTS_PER_GROUP * gidx
    emask = (lanef >= lo) & (lanef < lo + EXPERTS_PER_GROUP)
    el = jnp.where(emask, logits, neg)
    e1 = jnp.max(el, axis=-1, keepdims=True)
    i1 = jnp.min(jnp.where(el == e1, lanef, big), axis=-1, keepdims=True)
    el2 = jnp.where(lanef == i1, neg, el)
    e2 = jnp.max(el2, axis=-1, keepdims=True)
    i2 = jnp.min(jnp.where(el2 == e2, lanef, big), axis=-1, keepdims=True)
    r = jnp.exp(e2 - e1)
    gate1 = gprob / (1.0 + r)
    gate2 = gprob * r / (1.0 + r)
    id1 = i1 - N_GROUPS
    id2 = i2 - N_GROUPS

    oh1 = jnp.where(lanef == id1, 1.0, 0.0).astype(F32)
    oh2 = jnp.where(lanef == id2, 1.0, 0.0).astype(F32)
    oh = oh1 + oh2
    rr = lax.broadcasted_iota(I32, (tm, tm), 0)
    cc = lax.broadcasted_iota(I32, (tm, tm), 1)
    tri = jnp.where(cc < rr, 1.0, 0.0).astype(BF16)
    before = _dot(tri, oh.astype(BF16)) + base_ref[...]
    rank1 = jnp.sum(oh1 * before, axis=-1, keepdims=True)
    rank2 = jnp.sum(oh2 * before, axis=-1, keepdims=True)
    base_ref[...] = base_ref[...] + jnp.sum(oh, axis=0, keepdims=True)
    cnt_ref[...] = base_ref[...]

    ri = jnp.where(lane == 0, id1, jnp.where(lane == 1, id2,
                   jnp.where(lane == 2, rank1, jnp.where(lane == 3, rank2, 0.0))))
    ri_ref[...] = ri.astype(I32)
    rg_ref[...] = jnp.where(lane == 0, gate1, jnp.where(lane == 1, gate2, 0.0))


def _outproj(ya, yb, x, w_out16, ffn_g, rw):
    N, D = x.shape
    tm = TM_OUT
    row = lambda width: pl.BlockSpec((tm, width), lambda i: (i, 0))
    full = lambda shape: pl.BlockSpec(shape, lambda i: (0,) * len(shape))
    return pl.pallas_call(
        _outproj_kernel,
        grid=(N // tm,),
        in_specs=[row(CONV_A_DIM), row(DN_DIM), row(D), full((D, D)), full((1, D)), full((D, 2 * LANES))],
        out_specs=(row(D), pl.BlockSpec((tm * ROW_SLAB, LANES), lambda i: (i, 0)), row(LANES), row(LANES),
                   full((1, LANES))),
        out_shape=(jax.ShapeDtypeStruct((N, D), F32),
                   jax.ShapeDtypeStruct((N * ROW_SLAB, LANES), F32),
                   jax.ShapeDtypeStruct((N, LANES), I32),
                   jax.ShapeDtypeStruct((N, LANES), F32),
                   jax.ShapeDtypeStruct((1, LANES), F32)),
        scratch_shapes=[pltpu.VMEM((1, LANES), F32)],
        compiler_params=pltpu.CompilerParams(
            dimension_semantics=("arbitrary",), vmem_limit_bytes=VMEM_LIMIT),
        name="outproj",
    )(ya, yb, x, w_out16, ffn_g, rw)


def _slots_kernel(ri_ref, start_ref, o_ref):
    rif = ri_ref[...].astype(F32)
    lane = lax.broadcasted_iota(I32, rif.shape, 1)
    lanef = lane.astype(F32)

    def slot(k):
        start = jnp.sum(jnp.where(lanef == rif[:, k:k + 1], start_ref[...], 0.0), axis=-1, keepdims=True)
        return start + rif[:, TOP_K + k:TOP_K + k + 1]

    slab = jnp.where(lane == 0, slot(0), jnp.where(lane == 1, slot(1), 0.0))
    o_ref[...] = slab.T[0:HALO, :].astype(I32)


def _slots(ri, seg_start):
    N = ri.shape[0]
    tm = TM_SLOTS
    out = pl.pallas_call(
        _slots_kernel,
        grid=(N // tm,),
        in_specs=[pl.BlockSpec((tm, LANES), lambda i: (i, 0)), pl.BlockSpec((1, LANES), lambda i: (0, 0))],
        out_specs=pl.BlockSpec((HALO, tm), lambda i: (0, i)),
        out_shape=jax.ShapeDtypeStruct((HALO, N), I32),
        compiler_params=pltpu.CompilerParams(dimension_semantics=("arbitrary",)),
        name="slots",
    )(ri, _lane_vec(seg_start, 0))
    return out[:TOP_K].reshape(-1)


def _dispatch_kernel(seg_ref, slot0_ref, slot1_ref, h_ref, xs_ref, zero_ref, sem, zsem):
    tm = TM_DISPATCH
    bm = FFN_BLOCK
    n_blocks = xs_ref.shape[0] // (bm * ROW_SLAB)

    @pl.when(pl.program_id(0) == 0)
    def _():
        zero_ref[...] = jnp.zeros(zero_ref.shape, F32)

        def pad_copy(d):
            return pltpu.make_async_copy(zero_ref.at[pl.ds(0, ROW_SLAB)], _slab(xs_ref, d), zsem)

        def block_copy(b):
            return pltpu.make_async_copy(zero_ref, _slab_block(xs_ref, b), zsem)

        def each_pad(fn):
            def per_expert(e, carry):
                def per_row(d, c):
                    fn(pad_copy(d))
                    return c
                return lax.fori_loop(seg_ref[e], seg_ref[N_EXPERTS + e], per_row, carry)
            lax.fori_loop(0, N_EXPERTS, per_expert, 0)

            def per_block(b, c):
                fn(block_copy(b))
                return c
            lax.fori_loop(seg_ref[2 * N_EXPERTS], n_blocks, per_block, 0)

        each_pad(lambda cp: cp.start())
        each_pad(lambda cp: cp.wait())

    def issue(t, carry):
        for k, slot_ref in enumerate((slot0_ref, slot1_ref)):
            pltpu.make_async_copy(_slab(h_ref, t), _slab(xs_ref, slot_ref[t]), sem).start(priority=k)
        return carry

    lax.fori_loop(0, tm, issue, 0, unroll=ISSUE_UNROLL)
    for _ in range(TOP_K):
        pltpu.make_async_copy(h_ref, xs_ref.at[pl.ds(0, tm * ROW_SLAB)], sem).wait()


def _dispatch(seg, slots, h2, n_slots):
    N = h2.shape[0] // ROW_SLAB
    tm = TM_DISPATCH
    grid_spec = pltpu.PrefetchScalarGridSpec(
        num_scalar_prefetch=1,
        grid=(N // tm,),
        in_specs=[pl.BlockSpec((tm,), lambda i, s: (i,), memory_space=pltpu.SMEM),
                  pl.BlockSpec((tm,), lambda i, s: (N // tm + i,), memory_space=pltpu.SMEM),
                  pl.BlockSpec((tm * ROW_SLAB, LANES), lambda i, s: (i, 0))],
        out_specs=pl.BlockSpec(memory_space=pl.ANY),
        scratch_shapes=[pltpu.VMEM((FFN_BLOCK * ROW_SLAB, LANES), F32),
                        pltpu.SemaphoreType.DMA(()), pltpu.SemaphoreType.DMA(())],
    )
    return pl.pallas_call(
        _dispatch_kernel,
        grid_spec=grid_spec,
        out_shape=jax.ShapeDtypeStruct((n_slots * ROW_SLAB, LANES), F32),
        compiler_params=pltpu.CompilerParams(dimension_semantics=("arbitrary",)),
        name="dispatch",
    )(seg, slots, slots, h2)


def _ffn_kernel(blk_ref, xs_ref, wg_ref, wu_ref, wd_ref, ys_ref,
                xbuf, ybuf, wg16, wu16, wd16, xsem, ysem, zsem):
    e = pl.program_id(0)
    bm = FFN_BLOCK
    ring = FFN_RING
    n_blocks = ys_ref.shape[0] // (bm * ROW_SLAB)
    first = blk_ref[e]
    count = blk_ref[N_EXPERTS + e]
    n_used = blk_ref[2 * N_EXPERTS]

    def x_copy(g):
        return pltpu.make_async_copy(_slab_block(xs_ref, g), xbuf.at[g % ring], xsem.at[g % ring])

    def y_copy(g):
        return pltpu.make_async_copy(ybuf.at[g % ring], _slab_block(ys_ref, g), ysem.at[g % ring])

    def when_block(g, fn):
        @pl.when((g >= 0) & (g < n_used))
        def _():
            fn(g)

    @pl.when(e == 0)
    def _():
        for g in range(ring - 1):
            when_block(g, lambda g: x_copy(g).start(priority=RING_PRIORITY))

    @pl.when(count > 0)
    def _():
        wg16[...] = wg_ref[0].astype(BF16)
        wu16[...] = wu_ref[0].astype(BF16)
        wd16[...] = wd_ref[0].astype(BF16)

    def block(j, carry):
        g = first + j
        x_copy(g).wait()
        when_block(g + ring - 1, lambda g: x_copy(g).start(priority=RING_PRIORITY))

        x = _load_rows(xbuf.at[g % ring], bm).astype(BF16)
        a = _dot(x, wg16[...])
        b = _dot(x, wu16[...])
        y = _dot((_silu(a) * b).astype(BF16), wd16[...])

        when_block(g - ring, lambda g: y_copy(g).wait())
        _store_rows(ybuf.at[g % ring], y)
        y_copy(g).start(priority=RING_PRIORITY)
        return carry

    lax.fori_loop(0, count, block, 0)

    @pl.when(e == pl.num_programs(0) - 1)
    def _():
        for back in range(ring, 0, -1):
            when_block(n_used - back, lambda g: y_copy(g).wait())
        ybuf[0] = jnp.zeros(ybuf.shape[1:], F32)

        def zero_copy(g):
            return pltpu.make_async_copy(ybuf.at[0], _slab_block(ys_ref, g), zsem)

        def start(g, c):
            zero_copy(g).start()
            return c

        def wait(g, c):
            zero_copy(g).wait()
            return c

        lax.fori_loop(n_used, n_blocks, start, 0)
        lax.fori_loop(n_used, n_blocks, wait, 0)


def _ffn(blk, xs, w_gate, w_up, w_down):
    bm = FFN_BLOCK
    D = D_MODEL
    weights = lambda shape: pl.BlockSpec((1,) + shape, lambda e, blk: (e, 0, 0))
    grid_spec = pltpu.PrefetchScalarGridSpec(
        num_scalar_prefetch=1,
        grid=(N_EXPERTS,),
        in_specs=[pl.BlockSpec(memory_space=pl.ANY),
                  weights((D, EXPERT_FF)), weights((D, EXPERT_FF)), weights((EXPERT_FF, D))],
        out_specs=pl.BlockSpec(memory_space=pl.ANY),
        scratch_shapes=[pltpu.VMEM((FFN_RING, bm * ROW_SLAB, LANES), F32),
                        pltpu.VMEM((FFN_RING, bm * ROW_SLAB, LANES), F32),
                        pltpu.VMEM((D, EXPERT_FF), BF16), pltpu.VMEM((D, EXPERT_FF), BF16),
                        pltpu.VMEM((EXPERT_FF, D), BF16),
                        pltpu.SemaphoreType.DMA((FFN_RING,)), pltpu.SemaphoreType.DMA((FFN_RING,)),
                        pltpu.SemaphoreType.DMA(())],
    )
    return pl.pallas_call(
        _ffn_kernel,
        grid_spec=grid_spec,
        out_shape=jax.ShapeDtypeStruct(xs.shape, F32),
        compiler_params=pltpu.CompilerParams(
            dimension_semantics=("arbitrary",), vmem_limit_bytes=VMEM_LIMIT),
        name="ffn",
    )(blk, xs, w_gate, w_up, w_down)


def _combine_kernel(slot0_ref, slot1_ref, next0_ref, next1_ref, ys_ref, x1_ref, rg_ref, g_ref, o_ref,
                    buf_ref, sems):
    tm = TM_COMBINE
    i = pl.program_id(0)
    half = i % 2

    def gather(slot_refs, s):
        def issue(t, carry):
            for k, slot_ref in enumerate(slot_refs):
                pltpu.make_async_copy(_slab(ys_ref, slot_ref[t]), _slab(buf_ref.at[s, k], t),
                                      sems.at[s]).start(priority=k)
            return carry
        lax.fori_loop(0, tm, issue, 0, unroll=ISSUE_UNROLL)

    @pl.when(i == 0)
    def _():
        gather((slot0_ref, slot1_ref), 0)

    @pl.when(i + 1 < pl.num_programs(0))
    def _():
        gather((next0_ref, next1_ref), 1 - half)

    for k in range(TOP_K):
        pltpu.make_async_copy(ys_ref.at[pl.ds(0, tm * ROW_SLAB)], buf_ref.at[half, k], sems.at[half]).wait()

    rg = rg_ref[...]
    moe = (_load_rows(buf_ref.at[half, 0], tm) * rg[:, 0:1]
           + _load_rows(buf_ref.at[half, 1], tm) * rg[:, 1:2])
    x2 = x1_ref[...] + moe
    o_ref[...] = x2 * lax.rsqrt(jnp.mean(x2 * x2, axis=-1, keepdims=True) + EPS) * g_ref[...]


def _combine(slots, ys, x1, rg, final_g):
    N, D = x1.shape
    tm = TM_COMBINE
    n_tiles = N // tm

    def slot_block(k, ahead, i):
        return (k * n_tiles + jnp.minimum(i + ahead, n_tiles - 1),)

    return pl.pallas_call(
        _combine_kernel,
        grid=(n_tiles,),
        in_specs=[pl.BlockSpec((tm,), functools.partial(slot_block, k, ahead), memory_space=pltpu.SMEM)
                  for ahead in (0, 1) for k in range(TOP_K)]
                 + [pl.BlockSpec(memory_space=pl.ANY),
                  pl.BlockSpec((tm, D), lambda i: (i, 0)),
                  pl.BlockSpec((tm, LANES), lambda i: (i, 0)),
                  pl.BlockSpec((1, D), lambda i: (0, 0))],
        out_specs=pl.BlockSpec((tm, D), lambda i: (i, 0)),
        out_shape=jax.ShapeDtypeStruct((N, D), F32),
        scratch_shapes=[pltpu.VMEM((2, TOP_K, tm * ROW_SLAB, LANES), F32), pltpu.SemaphoreType.DMA((2,))],
        compiler_params=pltpu.CompilerParams(
            dimension_semantics=("arbitrary",), vmem_limit_bytes=VMEM_LIMIT),
        name="combine",
    )(slots, slots, slots, slots, ys, x1, rg, final_g)


def _lane_vec(values, offset):
    return jnp.zeros((1, LANES), F32).at[0, offset:offset + values.shape[0]].set(values.astype(F32))


def kernel(x, mix_norm_g, w_in, conv_a_w, conv_a_norm_g, dn_conv_w, dn_a_log, dn_dt_bias, dn_norm_g,
           w_out, ffn_norm_g, router_group_w, router_expert_w, w_gate, w_up, w_down, final_norm_g):
    B, S, D = x.shape
    N = B * S
    depth = w_in.shape[0]
    assert depth == 1, "single-layer block: the final RMSNorm is fused into the layer's combine step"
    group_of = jnp.arange(CONV_A_DIM, dtype=I32) // CONV_A_GROUP_DIM
    gmat = jnp.where(group_of[:, None] == group_of[None, :], 1.0 / CONV_A_GROUP_DIM, 0.0).astype(BF16)
    bm = FFN_BLOCK
    n_blocks = (N * TOP_K) // bm + N_EXPERTS
    for l in range(depth):
        w_in_pad = jnp.pad(w_in[l], ((0, 0), (0, IN_PROJ_PAD - IN_PROJ_DIM))).astype(BF16)
        ya, q, k, v, zg, gcol, grow = _inproj(
            x, mix_norm_g[l][None, :], w_in_pad, conv_a_w[l], conv_a_norm_g[l][None, :], dn_conv_w[l],
            _lane_vec(dn_a_log[l], DN_HEADS), _lane_vec(dn_dt_bias[l], DN_HEADS), gmat)
        u, w, qd, kd, at = _delta_prep(q, k, v, gcol, grow)
        yb = _delta_scan(u, w, qd, kd, at, zg, grow, dn_norm_g[l][None, :])
        rw = jnp.pad(jnp.concatenate([router_group_w[l], router_expert_w[l]], axis=1),
                     ((0, 0), (0, LANES - N_GROUPS - N_EXPERTS)))
        rw_hi = rw.astype(BF16)
        rw = jnp.concatenate([rw_hi, (rw - rw_hi.astype(F32)).astype(BF16)], axis=1)
        x1, h2, ri, rg, cnt = _outproj(ya.reshape(N, CONV_A_DIM), yb.reshape(N, DN_DIM), x.reshape(N, D),
                                       w_out[l].astype(BF16), ffn_norm_g[l][None, :], rw)
        counts = cnt[0, :N_EXPERTS].astype(I32)
        padded = (counts + bm - 1) // bm * bm
        seg_end = jnp.cumsum(padded).astype(I32)
        seg_start = seg_end - padded
        n_used = (seg_end[-1:] // bm)
        slots = _slots(ri, seg_start)
        seg = jnp.concatenate([seg_start + counts, seg_end, n_used])
        xs = _dispatch(seg, slots, h2, n_blocks * bm)
        blk = jnp.concatenate([seg_start // bm, padded // bm, n_used])
        ys = _ffn(blk, xs, w_gate[l], w_up[l], w_down[l])
        x = _combine(slots, ys, x1, rg, final_norm_g[None, :]).reshape(B, S, D)
    return x
```

```python
import functools

import jax
import jax.numpy as jnp
from jax import lax
from jax.experimental import pallas as pl
from jax.experimental.pallas import tpu as pltpu

F32 = jnp.float32
BF16 = jnp.bfloat16
I32 = jnp.int32

D_MODEL = 1024
CHUNK = 64
CONV_A_GROUP_DIM = 64
CONV_A_DIM = 512
CONV_A_WIDTH = 3
DN_HEADS = 4
DN_HEAD_DIM = 128
DN_DIM = 512
DN_CONV_WIDTH = 4
IN_PROJ_DIM = 3 * CONV_A_DIM + 4 * DN_DIM + 2 * DN_HEADS
N_GROUPS = 4
EXPERTS_PER_GROUP = 8
N_EXPERTS = 32
TOP_K = 2
EXPERT_FF = 512
EPS = 1e-6

LANES = 128
HALO = 8
IN_PROJ_PAD = 29 * LANES
GATE_COL = 3 * CONV_A_DIM + 4 * DN_DIM

TM_IN = 512
TM_PREP = 256
SCAN_BATCH = 4
TM_SCAN = 512
TM_OUT = 512
OUT_SPLIT = 4
FFN_BLOCK = 256
TM_SLOTS = 2048
TM_DISPATCH = 2048
TM_COMBINE = 512
FFN_RING = 4
RING_PRIORITY = 1
ISSUE_UNROLL = 4
VMEM_LIMIT = 56 * 1024 * 1024


def _dot(a, b):
    return jnp.dot(a, b, preferred_element_type=F32)


def _dot_nt(a, b):
    return lax.dot_general(a, b, (((1,), (1,)), ((), ())), preferred_element_type=F32)


def _dot_tn(a, b):
    return lax.dot_general(a, b, (((0,), (0,)), ((), ())), preferred_element_type=F32)


def _split_bf16(x, parts):
    out = []
    for _ in range(parts):
        p = x.astype(BF16)
        out.append(p)
        x = x - p.astype(F32)
    return out


def _silu(x):
    return x * jax.nn.sigmoid(x)


ROW_SLAB = D_MODEL // LANES


def _store_rows(ref, val):
    m = val.shape[0]
    for c in range(ROW_SLAB):
        ref[pl.ds(c, m, stride=ROW_SLAB), :] = val[:, c * LANES:(c + 1) * LANES]


def _load_rows(ref, m):
    return jnp.concatenate([ref[pl.ds(c, m, stride=ROW_SLAB), :] for c in range(ROW_SLAB)], axis=-1)


def _slab(ref, row):
    return ref.at[pl.ds(pl.multiple_of(row * ROW_SLAB, ROW_SLAB), ROW_SLAB)]


def _slab_block(ref, block):
    n = FFN_BLOCK * ROW_SLAB
    return ref.at[pl.ds(pl.multiple_of(block * n, n), n)]


def _causal_conv(ext, w):
    taps = w.shape[0]
    acc = None
    for j in range(taps):
        shift = taps - 1 - j
        rows = pltpu.roll(ext, shift, 0)[HALO:] if shift else ext[HALO:]
        term = w[j:j + 1, :] * rows
        acc = term if acc is None else acc + term
    return acc


def _softplus(x):
    return jnp.maximum(x, 0.0) + jnp.log1p(jnp.exp(-jnp.abs(x)))


def _inproj_kernel(x_ref, g_ref, w_ref, caw_ref, cag_ref, dcw_ref, alog_ref, dtb_ref, gmat_ref,
                   ya_ref, q_ref, k_ref, v_ref, zg_ref, gcol_ref, grow_ref,
                   exta_ref, *extq_refs):
    tm = x_ref.shape[1]
    ext_refs = (exta_ref,) + extq_refs

    @pl.when(pl.program_id(1) == 0)
    def _():
        for ext_ref in ext_refs:
            ext_ref[0:HALO, :] = jnp.zeros((HALO, ext_ref.shape[1]), F32)

    x = x_ref[0]
    ms = jnp.mean(x * x, axis=-1, keepdims=True)
    hb = (x * lax.rsqrt(ms + EPS) * g_ref[...]).astype(BF16)

    def proj(c0, width):
        return _dot(hb, w_ref[:, c0:c0 + width])

    base = 3 * CONV_A_DIM

    def qkv_project(i):
        extq_refs[i][HALO:HALO + tm, :] = proj(base + i * DN_DIM, DN_DIM)

    def qkv_finish(i, out_ref):
        cols = slice(i * DN_DIM, (i + 1) * DN_DIM)
        s = _silu(_causal_conv(extq_refs[i][...], dcw_ref[:, cols]))
        if i == 2:
            out_ref[0] = s
        else:
            for h in range(DN_HEADS):
                sh = s[:, h * DN_HEAD_DIM:(h + 1) * DN_HEAD_DIM]
                inv = lax.rsqrt(jnp.sum(sh * sh, axis=-1, keepdims=True) + EPS)
                sh = sh * inv
                if i == 0:
                    sh = sh * (DN_HEAD_DIM ** -0.5)
                out_ref[0, :, h * DN_HEAD_DIM:(h + 1) * DN_HEAD_DIM] = sh

    def mixer_a_finish(a_b):
        y = a_b * _causal_conv(exta_ref[...], caw_ref[...])
        ysq = y * y
        hi = ysq.astype(BF16)
        lo = (ysq - hi.astype(F32)).astype(BF16)
        gmean = _dot(hi, gmat_ref[...]) + _dot(lo, gmat_ref[...])
        ya_ref[0] = (y * lax.rsqrt(gmean + EPS) * cag_ref[...]).astype(ya_ref.dtype)

    qkv_project(0)
    qkv_project(1)
    exta_ref[HALO:HALO + tm, :] = proj(2 * CONV_A_DIM, CONV_A_DIM) * proj(0, CONV_A_DIM)
    qkv_finish(0, q_ref)
    qkv_project(2)
    a_b = proj(CONV_A_DIM, CONV_A_DIM)
    qkv_finish(1, k_ref)
    z = proj(base + 3 * DN_DIM, DN_DIM)
    mixer_a_finish(a_b)
    p = proj(GATE_COL, LANES)
    qkv_finish(2, v_ref)
    zg_ref[0] = _silu(z)
    for ext_ref in ext_refs:
        ext_ref[0:HALO, :] = ext_ref[tm:tm + HALO, :]

    beta = jax.nn.sigmoid(p)
    g = -jnp.exp(alog_ref[...]) * _softplus(p + dtb_ref[...])
    r = lax.broadcasted_iota(I32, (CHUNK, CHUNK), 0)
    c = lax.broadcasted_iota(I32, (CHUNK, CHUNK), 1)
    tri = jnp.where(c <= r, 1.0, 0.0).astype(BF16)
    parts = _split_bf16(g, 3)
    gc = jnp.concatenate(
        [sum(_dot(tri, p[ci * CHUNK:(ci + 1) * CHUNK]) for p in parts) for ci in range(tm // CHUNK)], axis=0)
    lane = lax.broadcasted_iota(I32, (tm, LANES), 1)
    slab = jnp.where(lane < DN_HEADS, beta, gc)
    gcol_ref[0] = slab
    rows = slab.T[0:HALO, :]
    for ci in range(tm // CHUNK):
        grow_ref[0, ci] = rows[:, ci * CHUNK:(ci + 1) * CHUNK]


def _inproj(x, mix_g, w_in_pad, conv_a_w, conv_a_g, dn_conv_w, alog_vec, dtb_vec, gmat):
    B, S, D = x.shape
    tm = TM_IN
    full = lambda shape: pl.BlockSpec(shape, lambda b, t: (0,) * len(shape))
    row = lambda width: pl.BlockSpec((1, tm, width), lambda b, t: (b, t, 0))
    out_shape = (
        jax.ShapeDtypeStruct((B, S, CONV_A_DIM), BF16),
        jax.ShapeDtypeStruct((B, S, DN_DIM), F32),
        jax.ShapeDtypeStruct((B, S, DN_DIM), F32),
        jax.ShapeDtypeStruct((B, S, DN_DIM), F32),
        jax.ShapeDtypeStruct((B, S, DN_DIM), F32),
        jax.ShapeDtypeStruct((B, S, LANES), F32),
        jax.ShapeDtypeStruct((B, S // CHUNK, HALO, CHUNK), F32),
    )
    return pl.pallas_call(
        _inproj_kernel,
        grid=(B, S // tm),
        in_specs=[row(D), full((1, D)), full((D, IN_PROJ_PAD)), full((CONV_A_WIDTH, CONV_A_DIM)),
                  full((1, CONV_A_DIM)), full((DN_CONV_WIDTH, 3 * DN_DIM)), full((1, LANES)),
                  full((1, LANES)), full((CONV_A_DIM, CONV_A_DIM))],
        out_specs=(row(CONV_A_DIM), row(DN_DIM), row(DN_DIM), row(DN_DIM), row(DN_DIM), row(LANES),
                   pl.BlockSpec((1, tm // CHUNK, HALO, CHUNK), lambda b, t: (b, t, 0, 0))),
        out_shape=out_shape,
        scratch_shapes=[pltpu.VMEM((tm + HALO, CONV_A_DIM), F32)] + [pltpu.VMEM((tm + HALO, DN_DIM), F32)] * 3,
        compiler_params=pltpu.CompilerParams(
            dimension_semantics=("arbitrary", "arbitrary"), vmem_limit_bytes=VMEM_LIMIT),
        name="inproj",
    )(x, mix_g, w_in_pad, conv_a_w, conv_a_g, dn_conv_w, alog_vec, dtb_vec, gmat)


def _delta_prep_kernel(q_ref, k_ref, v_ref, gcol_ref, grow_ref,
                       u_ref, w_ref, qd_ref, kd_ref, at_ref):
    tm = q_ref.shape[1]
    ri = lax.broadcasted_iota(I32, (CHUNK, CHUNK), 0)
    ci = lax.broadcasted_iota(I32, (CHUNK, CHUNK), 1)
    causal = ci <= ri
    strict = ci < ri
    eye = jnp.where(ci == ri, 1.0, 0.0).astype(F32)
    lane = lax.broadcasted_iota(I32, (CHUNK, LANES), 1)
    left_half = lane < CHUNK

    def setup(c, h):
        rows = slice(c * CHUNK, (c + 1) * CHUNK)
        cols = slice(h * DN_HEAD_DIM, (h + 1) * DN_HEAD_DIM)
        slab = gcol_ref[0, rows, :]
        beta = jnp.sum(jnp.where(lane == h, slab, 0.0), axis=-1, keepdims=True)
        gcc = jnp.sum(jnp.where(lane == h + DN_HEADS, slab, 0.0), axis=-1, keepdims=True)
        gcr = grow_ref[0, c, h + DN_HEADS:h + DN_HEADS + 1, :]
        diff = gcc - gcr
        decay = jnp.where(causal, jnp.exp(jnp.where(causal, diff, 0.0)), 0.0)
        q = q_ref[0, rows, cols]
        k = k_ref[0, rows, cols]
        v = v_ref[0, rows, cols]
        kb = k * beta
        kq = _dot_nt(jnp.concatenate([kb.astype(BF16), q.astype(BF16)], axis=0), k.astype(BF16))
        L = jnp.where(strict, kq[:CHUNK] * decay, 0.0)
        intra = kq[CHUNK:] * decay
        egc = jnp.exp(gcc)
        gl = gcr[:, CHUNK - 1:CHUNK]
        qd_ref[0, rows, cols] = (q * egc).astype(BF16)
        kd_ref[0, rows, cols] = (k * jnp.exp(gl - gcc)).astype(BF16)
        at_ref[0, rows, h * CHUNK:(h + 1) * CHUNK] = intra.astype(BF16)
        return -L, jnp.concatenate([v * beta, kb * egc], axis=-1).astype(BF16)

    def first_stage(m):
        m16 = m.astype(BF16)
        return jnp.concatenate([eye + m, _dot(m16, m16)], axis=-1)

    def stage(r):
        return _dot(r[:, CHUNK:].astype(BF16), r.astype(BF16)) + jnp.where(left_half, r, 0.0)

    def last_stage(r):
        return r[:, :CHUNK] + _dot(r[:, CHUNK:].astype(BF16), r[:, :CHUNK].astype(BF16))

    def solve(c, h, t, rhs):
        rows = slice(c * CHUNK, (c + 1) * CHUNK)
        cols = slice(h * DN_HEAD_DIM, (h + 1) * DN_HEAD_DIM)
        uw = _dot(t.astype(BF16), rhs)
        u_ref[0, rows, cols] = uw[:, :DN_HEAD_DIM]
        w_ref[0, rows, cols] = uw[:, DN_HEAD_DIM:].astype(BF16)

    chains = [(c, h) for c in range(tm // CHUNK) for h in range(DN_HEADS)]
    ms, rhss = zip(*[setup(c, h) for c, h in chains])
    rs = [first_stage(m) for m in ms]
    for _ in range(4):
        rs = [stage(r) for r in rs]
    ts = [last_stage(r) for r in rs]
    for (c, h), t, rhs in zip(chains, ts, rhss):
        solve(c, h, t, rhs)


def _delta_prep(q, k, v, gcol, grow):
    B, S, _ = q.shape
    tm = TM_PREP
    row = lambda width: pl.BlockSpec((1, tm, width), lambda b, t: (b, t, 0))
    return pl.pallas_call(
        _delta_prep_kernel,
        grid=(B, S // tm),
        in_specs=[row(DN_DIM), row(DN_DIM), row(DN_DIM), row(LANES),
                  pl.BlockSpec((1, tm // CHUNK, HALO, CHUNK), lambda b, t: (b, t, 0, 0))],
        out_specs=(row(DN_DIM), row(DN_DIM), row(DN_DIM), row(DN_DIM), row(DN_HEADS * CHUNK)),
        out_shape=(jax.ShapeDtypeStruct((B, S, DN_DIM), F32),
                   jax.ShapeDtypeStruct((B, S, DN_DIM), BF16),
                   jax.ShapeDtypeStruct((B, S, DN_DIM), BF16),
                   jax.ShapeDtypeStruct((B, S, DN_DIM), BF16),
                   jax.ShapeDtypeStruct((B, S, DN_HEADS * CHUNK), BF16)),
        compiler_params=pltpu.CompilerParams(
            dimension_semantics=("arbitrary", "arbitrary"), vmem_limit_bytes=VMEM_LIMIT),
        name="delta_prep",
    )(q, k, v, gcol, grow)


def _delta_scan_kernel(u_ref, w_ref, qd_ref, kd_ref, at_ref, zg_ref, grow_ref, ng_ref, o_ref, st_ref):
    nb, tm = u_ref.shape[0], u_ref.shape[1]

    @pl.when(pl.program_id(1) == 0)
    def _():
        st_ref[...] = jnp.zeros(st_ref.shape, F32)

    def step(c, carry):
        r0 = pl.multiple_of(c * CHUNK, CHUNK)
        rows = pl.ds(r0, CHUNK)
        chains = [(b, h) for b in range(nb) for h in range(DN_HEADS)]
        cols = lambda h: slice(h * DN_HEAD_DIM, (h + 1) * DN_HEAD_DIM)
        sts = [st_ref[b, h] for b, h in chains]
        st16s = [st.astype(BF16) for st in sts]
        wqs = [_dot(jnp.concatenate([w_ref[b, rows, cols(h)], qd_ref[b, rows, cols(h)]], axis=0), st16)
               for (b, h), st16 in zip(chains, st16s)]
        vn16s = [(u_ref[b, rows, cols(h)] - wq[:CHUNK]).astype(BF16) for (b, h), wq in zip(chains, wqs)]
        upd = [_dot_tn(kd_ref[b, rows, cols(h)], vn16) for (b, h), vn16 in zip(chains, vn16s)]
        avs = [_dot(at_ref[b, rows, h * CHUNK:(h + 1) * CHUNK], vn16) for (b, h), vn16 in zip(chains, vn16s)]
        for (b, h), st, up in zip(chains, sts, upd):
            gl = grow_ref[b, c, h + DN_HEADS:h + DN_HEADS + 1, CHUNK - 1:CHUNK]
            st_ref[b, h] = st * jnp.exp(gl) + up
        for (b, h), wq, av in zip(chains, wqs, avs):
            o = wq[CHUNK:] + av
            on = o * lax.rsqrt(jnp.mean(o * o, axis=-1, keepdims=True) + EPS) * ng_ref[...]
            o_ref[b, rows, cols(h)] = (on * zg_ref[b, rows, cols(h)]).astype(o_ref.dtype)
        return carry

    lax.fori_loop(0, tm // CHUNK, step, 0)


def _delta_scan(u, w, qd, kd, at, zg, grow, norm_g):
    B, S, _ = u.shape
    tm = TM_SCAN
    nb = SCAN_BATCH
    row = lambda width: pl.BlockSpec((nb, tm, width), lambda b, t: (b, t, 0))
    return pl.pallas_call(
        _delta_scan_kernel,
        grid=(B // nb, S // tm),
        in_specs=[row(DN_DIM), row(DN_DIM), row(DN_DIM), row(DN_DIM), row(DN_HEADS * CHUNK), row(DN_DIM),
                  pl.BlockSpec((nb, tm // CHUNK, HALO, CHUNK), lambda b, t: (b, t, 0, 0)),
                  pl.BlockSpec((1, DN_HEAD_DIM), lambda b, t: (0, 0))],
        out_specs=row(DN_DIM),
        out_shape=jax.ShapeDtypeStruct((B, S, DN_DIM), BF16),
        scratch_shapes=[pltpu.VMEM((nb, DN_HEADS, DN_HEAD_DIM, DN_HEAD_DIM), F32)],
        compiler_params=pltpu.CompilerParams(
            dimension_semantics=("arbitrary", "arbitrary"), vmem_limit_bytes=VMEM_LIMIT),
        name="delta_scan",
    )(u, w, qd, kd, at, zg, grow, norm_g)


def _outproj_kernel(ya_ref, yb_ref, x_ref, wo_ref, g_ref, rw_ref,
                    x1_ref, h2_ref, ri_ref, rg_ref, cnt_ref, base_ref):
    tm = x_ref.shape[0]
    sub = tm // OUT_SPLIT
    parts = range(OUT_SPLIT)
    rows_of = lambda s: slice(s * sub, (s + 1) * sub)

    @pl.when(pl.program_id(0) == 0)
    def _():
        base_ref[...] = jnp.zeros(base_ref.shape, F32)

    def project(s):
        rows = rows_of(s)
        return _dot(jnp.concatenate([ya_ref[rows, :], yb_ref[rows, :]], axis=-1), wo_ref[...])

    def normalise(s, y):
        rows = rows_of(s)
        x1 = x_ref[rows, :] + y
        x1_ref[rows, :] = x1
        h = x1 * lax.rsqrt(jnp.mean(x1 * x1, axis=-1, keepdims=True) + EPS) * g_ref[...]
        _store_rows(h2_ref.at[pl.ds(s * sub * ROW_SLAB, sub * ROW_SLAB)], h)
        return h

    def router_logits(h):
        h_hi, h_lo = _split_bf16(h, 2)
        hi_prod = _dot(h_hi, rw_ref[...])
        return hi_prod[:, :LANES] + (hi_prod[:, LANES:] + _dot(h_lo, rw_ref[:, :LANES]))

    lane = lax.broadcasted_iota(I32, (sub, LANES), 1)
    lanef = lane.astype(F32)
    rr = lax.broadcasted_iota(I32, (sub, sub), 0)
    cc = lax.broadcasted_iota(I32, (sub, sub), 1)
    tri = jnp.where(cc < rr, 1.0, 0.0).astype(BF16)

    def route(s, logits):
        rows = rows_of(s)
        neg = jnp.float32(-1e30)
        big = jnp.float32(1e9)
        is_g = lane < N_GROUPS
        gl = jnp.where(is_g, logits, neg)
        gmax = jnp.max(gl, axis=-1, keepdims=True)
        gidx = jnp.min(jnp.where(gl == gmax, lanef, big), axis=-1, keepdims=True)
        gsum = jnp.sum(jnp.where(is_g, jnp.exp(gl - gmax), 0.0), axis=-1, keepdims=True)
        gprob = 1.0 / gsum
        lo = N_GROUPS + EXPERTS_PER_GROUP * gidx
        emask = (lanef >= lo) & (lanef < lo + EXPERTS_PER_GROUP)
        el = jnp.where(emask, logits, neg)
        e1 = jnp.max(el, axis=-1, keepdims=True)
        i1 = jnp.min(jnp.where(el == e1, lanef, big), axis=-1, keepdims=True)
        el2 = jnp.where(lanef == i1, neg, el)
        e2 = jnp.max(el2, axis=-1, keepdims=True)
        i2 = jnp.min(jnp.where(el2 == e2, lanef, big), axis=-1, keepdims=True)
        r = jnp.exp(e2 - e1)
        gate1 = gprob / (1.0 + r)
        gate2 = gprob * r / (1.0 + r)
        id1 = i1 - N_GROUPS
        id2 = i2 - N_GROUPS

        oh1 = jnp.where(lanef == id1, 1.0, 0.0).astype(F32)
        oh2 = jnp.where(lanef == id2, 1.0, 0.0).astype(F32)
        oh = oh1 + oh2
        before = _dot(tri, oh.astype(BF16)) + base_ref[...]
        rank1 = jnp.sum(oh1 * before, axis=-1, keepdims=True)
        rank2 = jnp.sum(oh2 * before, axis=-1, keepdims=True)
        base_ref[...] = base_ref[...] + jnp.sum(oh, axis=0, keepdims=True)

        ri = jnp.where(lane == 0, id1, jnp.where(lane == 1, id2,
                       jnp.where(lane == 2, rank1, jnp.where(lane == 3, rank2, 0.0))))
        ri_ref[rows, :] = ri.astype(I32)
        rg_ref[rows, :] = jnp.where(lane == 0, gate1, jnp.where(lane == 1, gate2, 0.0))

    ys = [project(s) for s in parts]
    hs = [normalise(s, y) for s, y in zip(parts, ys)]
    ls = [router_logits(h) for h in hs]
    for s, logits in zip(parts, ls):
        route(s, logits)
    cnt_ref[...] = base_ref[...]


def _outproj(ya, yb, x, w_out16, ffn_g, rw):
    N, D = x.shape
    tm = TM_OUT
    row = lambda width: pl.BlockSpec((tm, width), lambda i: (i, 0))
    full = lambda shape: pl.BlockSpec(shape, lambda i: (0,) * len(shape))
    return pl.pallas_call(
        _outproj_kernel,
        grid=(N // tm,),
        in_specs=[row(CONV_A_DIM), row(DN_DIM), row(D), full((D, D)), full((1, D)), full((D, 2 * LANES))],
        out_specs=(row(D), pl.BlockSpec((tm * ROW_SLAB, LANES), lambda i: (i, 0)), row(LANES), row(LANES),
                   full((1, LANES))),
        out_shape=(jax.ShapeDtypeStruct((N, D), F32),
                   jax.ShapeDtypeStruct((N * ROW_SLAB, LANES), F32),
                   jax.ShapeDtypeStruct((N, LANES), I32),
                   jax.ShapeDtypeStruct((N, LANES), F32),
                   jax.ShapeDtypeStruct((1, LANES), F32)),
        scratch_shapes=[pltpu.VMEM((1, LANES), F32)],
        compiler_params=pltpu.CompilerParams(
            dimension_semantics=("arbitrary",), vmem_limit_bytes=VMEM_LIMIT),
        name="outproj",
    )(ya, yb, x, w_out16, ffn_g, rw)


def _slots_kernel(ri_ref, start_ref, o_ref):
    rif = ri_ref[...].astype(F32)
    lane = lax.broadcasted_iota(I32, rif.shape, 1)
    lanef = lane.astype(F32)

    def slot(k):
        start = jnp.sum(jnp.where(lanef == rif[:, k:k + 1], start_ref[...], 0.0), axis=-1, keepdims=True)
        return start + rif[:, TOP_K + k:TOP_K + k + 1]

    slab = jnp.where(lane == 0, slot(0), jnp.where(lane == 1, slot(1), 0.0))
    o_ref[...] = slab.T[0:HALO, :].astype(I32)


def _slots(ri, seg_start):
    N = ri.shape[0]
    tm = TM_SLOTS
    out = pl.pallas_call(
        _slots_kernel,
        grid=(N // tm,),
        in_specs=[pl.BlockSpec((tm, LANES), lambda i: (i, 0)), pl.BlockSpec((1, LANES), lambda i: (0, 0))],
        out_specs=pl.BlockSpec((HALO, tm), lambda i: (0, i)),
        out_shape=jax.ShapeDtypeStruct((HALO, N), I32),
        compiler_params=pltpu.CompilerParams(dimension_semantics=("arbitrary",)),
        name="slots",
    )(ri, _lane_vec(seg_start, 0))
    return out[:TOP_K].reshape(-1)


def _dispatch_kernel(seg_ref, slot0_ref, slot1_ref, h_ref, xs_ref, zero_ref, sem, zsem):
    tm = TM_DISPATCH
    bm = FFN_BLOCK
    n_blocks = xs_ref.shape[0] // (bm * ROW_SLAB)

    @pl.when(pl.program_id(0) == 0)
    def _():
        zero_ref[...] = jnp.zeros(zero_ref.shape, F32)

        def pad_copy(d):
            return pltpu.make_async_copy(zero_ref.at[pl.ds(0, ROW_SLAB)], _slab(xs_ref, d), zsem)

        def block_copy(b):
            return pltpu.make_async_copy(zero_ref, _slab_block(xs_ref, b), zsem)

        def each_pad(fn):
            def per_expert(e, carry):
                def per_row(d, c):
                    fn(pad_copy(d))
                    return c
                return lax.fori_loop(seg_ref[e], seg_ref[N_EXPERTS + e], per_row, carry)
            lax.fori_loop(0, N_EXPERTS, per_expert, 0)

            def per_block(b, c):
                fn(block_copy(b))
                return c
            lax.fori_loop(seg_ref[2 * N_EXPERTS], n_blocks, per_block, 0)

        each_pad(lambda cp: cp.start())
        each_pad(lambda cp: cp.wait())

    def issue(t, carry):
        for k, slot_ref in enumerate((slot0_ref, slot1_ref)):
            pltpu.make_async_copy(_slab(h_ref, t), _slab(xs_ref, slot_ref[t]), sem).start(priority=k)
        return carry

    lax.fori_loop(0, tm, issue, 0, unroll=ISSUE_UNROLL)
    for _ in range(TOP_K):
        pltpu.make_async_copy(h_ref, xs_ref.at[pl.ds(0, tm * ROW_SLAB)], sem).wait()


def _dispatch(seg, slots, h2, n_slots):
    N = h2.shape[0] // ROW_SLAB
    tm = TM_DISPATCH
    grid_spec = pltpu.PrefetchScalarGridSpec(
        num_scalar_prefetch=1,
        grid=(N // tm,),
        in_specs=[pl.BlockSpec((tm,), lambda i, s: (i,), memory_space=pltpu.SMEM),
                  pl.BlockSpec((tm,), lambda i, s: (N // tm + i,), memory_space=pltpu.SMEM),
                  pl.BlockSpec((tm * ROW_SLAB, LANES), lambda i, s: (i, 0))],
        out_specs=pl.BlockSpec(memory_space=pl.ANY),
        scratch_shapes=[pltpu.VMEM((FFN_BLOCK * ROW_SLAB, LANES), F32),
                        pltpu.SemaphoreType.DMA(()), pltpu.SemaphoreType.DMA(())],
    )
    return pl.pallas_call(
        _dispatch_kernel,
        grid_spec=grid_spec,
        out_shape=jax.ShapeDtypeStruct((n_slots * ROW_SLAB, LANES), F32),
        compiler_params=pltpu.CompilerParams(dimension_semantics=("arbitrary",)),
        name="dispatch",
    )(seg, slots, slots, h2)


def _ffn_kernel(blk_ref, xs_ref, wg_ref, wu_ref, wd_ref, ys_ref,
                xbuf, ybuf, wg16, wu16, wd16, xsem, ysem, zsem):
    e = pl.program_id(0)
    bm = FFN_BLOCK
    ring = FFN_RING
    n_blocks = ys_ref.shape[0] // (bm * ROW_SLAB)
    first = blk_ref[e]
    count = blk_ref[N_EXPERTS + e]
    n_used = blk_ref[2 * N_EXPERTS]

    def x_copy(g):
        return pltpu.make_async_copy(_slab_block(xs_ref, g), xbuf.at[g % ring], xsem.at[g % ring])

    def y_copy(g):
        return pltpu.make_async_copy(ybuf.at[g % ring], _slab_block(ys_ref, g), ysem.at[g % ring])

    def when_block(g, fn):
        @pl.when((g >= 0) & (g < n_used))
        def _():
            fn(g)

    @pl.when(e == 0)
    def _():
        for g in range(ring - 1):
            when_block(g, lambda g: x_copy(g).start(priority=RING_PRIORITY))

    @pl.when(count > 0)
    def _():
        wg16[...] = wg_ref[0].astype(BF16)
        wu16[...] = wu_ref[0].astype(BF16)
        wd16[...] = wd_ref[0].astype(BF16)

    def block(j, carry):
        g = first + j
        x_copy(g).wait()
        when_block(g + ring - 1, lambda g: x_copy(g).start(priority=RING_PRIORITY))

        x = _load_rows(xbuf.at[g % ring], bm).astype(BF16)
        a = _dot(x, wg16[...])
        b = _dot(x, wu16[...])
        y = _dot((_silu(a) * b).astype(BF16), wd16[...])

        when_block(g - ring, lambda g: y_copy(g).wait())
        _store_rows(ybuf.at[g % ring], y)
        y_copy(g).start(priority=RING_PRIORITY)
        return carry

    lax.fori_loop(0, count, block, 0)

    @pl.when(e == pl.num_programs(0) - 1)
    def _():
        for back in range(ring, 0, -1):
            when_block(n_used - back, lambda g: y_copy(g).wait())
        ybuf[0] = jnp.zeros(ybuf.shape[1:], F32)

        def zero_copy(g):
            return pltpu.make_async_copy(ybuf.at[0], _slab_block(ys_ref, g), zsem)

        def start(g, c):
            zero_copy(g).start()
            return c

        def wait(g, c):
            zero_copy(g).wait()
            return c

        lax.fori_loop(n_used, n_blocks, start, 0)
        lax.fori_loop(n_used, n_blocks, wait, 0)


def _ffn(blk, xs, w_gate, w_up, w_down):
    bm = FFN_BLOCK
    D = D_MODEL
    weights = lambda shape: pl.BlockSpec((1,) + shape, lambda e, blk: (e, 0, 0))
    grid_spec = pltpu.PrefetchScalarGridSpec(
        num_scalar_prefetch=1,
        grid=(N_EXPERTS,),
        in_specs=[pl.BlockSpec(memory_space=pl.ANY),
                  weights((D, EXPERT_FF)), weights((D, EXPERT_FF)), weights((EXPERT_FF, D))],
        out_specs=pl.BlockSpec(memory_space=pl.ANY),
        scratch_shapes=[pltpu.VMEM((FFN_RING, bm * ROW_SLAB, LANES), F32),
                        pltpu.VMEM((FFN_RING, bm * ROW_SLAB, LANES), F32),
                        pltpu.VMEM((D, EXPERT_FF), BF16), pltpu.VMEM((D, EXPERT_FF), BF16),
                        pltpu.VMEM((EXPERT_FF, D), BF16),
                        pltpu.SemaphoreType.DMA((FFN_RING,)), pltpu.SemaphoreType.DMA((FFN_RING,)),
                        pltpu.SemaphoreType.DMA(())],
    )
    return pl.pallas_call(
        _ffn_kernel,
        grid_spec=grid_spec,
        out_shape=jax.ShapeDtypeStruct(xs.shape, F32),
        compiler_params=pltpu.CompilerParams(
            dimension_semantics=("arbitrary",), vmem_limit_bytes=VMEM_LIMIT),
        name="ffn",
    )(blk, xs, w_gate, w_up, w_down)


def _combine_kernel(slot0_ref, slot1_ref, next0_ref, next1_ref, ys_ref, x1_ref, rg_ref, g_ref, o_ref,
                    buf_ref, sems):
    tm = TM_COMBINE
    i = pl.program_id(0)
    half = i % 2

    def gather(slot_refs, s):
        def issue(t, carry):
            for k, slot_ref in enumerate(slot_refs):
                pltpu.make_async_copy(_slab(ys_ref, slot_ref[t]), _slab(buf_ref.at[s, k], t),
                                      sems.at[s]).start(priority=k)
            return carry
        lax.fori_loop(0, tm, issue, 0, unroll=ISSUE_UNROLL)

    @pl.when(i == 0)
    def _():
        gather((slot0_ref, slot1_ref), 0)

    @pl.when(i + 1 < pl.num_programs(0))
    def _():
        gather((next0_ref, next1_ref), 1 - half)

    for k in range(TOP_K):
        pltpu.make_async_copy(ys_ref.at[pl.ds(0, tm * ROW_SLAB)], buf_ref.at[half, k], sems.at[half]).wait()

    rg = rg_ref[...]
    moe = (_load_rows(buf_ref.at[half, 0], tm) * rg[:, 0:1]
           + _load_rows(buf_ref.at[half, 1], tm) * rg[:, 1:2])
    x2 = x1_ref[...] + moe
    o_ref[...] = x2 * lax.rsqrt(jnp.mean(x2 * x2, axis=-1, keepdims=True) + EPS) * g_ref[...]


def _combine(slots, ys, x1, rg, final_g):
    N, D = x1.shape
    tm = TM_COMBINE
    n_tiles = N // tm

    def slot_block(k, ahead, i):
        return (k * n_tiles + jnp.minimum(i + ahead, n_tiles - 1),)

    return pl.pallas_call(
        _combine_kernel,
        grid=(n_tiles,),
        in_specs=[pl.BlockSpec((tm,), functools.partial(slot_block, k, ahead), memory_space=pltpu.SMEM)
                  for ahead in (0, 1) for k in range(TOP_K)]
                 + [pl.BlockSpec(memory_space=pl.ANY),
                  pl.BlockSpec((tm, D), lambda i: (i, 0)),
                  pl.BlockSpec((tm, LANES), lambda i: (i, 0)),
                  pl.BlockSpec((1, D), lambda i: (0, 0))],
        out_specs=pl.BlockSpec((tm, D), lambda i: (i, 0)),
        out_shape=jax.ShapeDtypeStruct((N, D), F32),
        scratch_shapes=[pltpu.VMEM((2, TOP_K, tm * ROW_SLAB, LANES), F32), pltpu.SemaphoreType.DMA((2,))],
        compiler_params=pltpu.CompilerParams(
            dimension_semantics=("arbitrary",), vmem_limit_bytes=VMEM_LIMIT),
        name="combine",
    )(slots, slots, slots, slots, ys, x1, rg, final_g)


def _lane_vec(values, offset):
    return jnp.zeros((1, LANES), F32).at[0, offset:offset + values.shape[0]].set(values.astype(F32))


def kernel(x, mix_norm_g, w_in, conv_a_w, conv_a_norm_g, dn_conv_w, dn_a_log, dn_dt_bias, dn_norm_g,
           w_out, ffn_norm_g, router_group_w, router_expert_w, w_gate, w_up, w_down, final_norm_g):
    B, S, D = x.shape
    N = B * S
    depth = w_in.shape[0]
    assert depth == 1, "single-layer block: the final RMSNorm is fused into the layer's combine step"
    group_of = jnp.arange(CONV_A_DIM, dtype=I32) // CONV_A_GROUP_DIM
    gmat = jnp.where(group_of[:, None] == group_of[None, :], 1.0 / CONV_A_GROUP_DIM, 0.0).astype(BF16)
    bm = FFN_BLOCK
    n_blocks = (N * TOP_K) // bm + N_EXPERTS
    for l in range(depth):
        w_in_pad = jnp.pad(w_in[l], ((0, 0), (0, IN_PROJ_PAD - IN_PROJ_DIM))).astype(BF16)
        ya, q, k, v, zg, gcol, grow = _inproj(
            x, mix_norm_g[l][None, :], w_in_pad, conv_a_w[l], conv_a_norm_g[l][None, :], dn_conv_w[l],
            _lane_vec(dn_a_log[l], DN_HEADS), _lane_vec(dn_dt_bias[l], DN_HEADS), gmat)
        u, w, qd, kd, at = _delta_prep(q, k, v, gcol, grow)
        yb = _delta_scan(u, w, qd, kd, at, zg, grow, dn_norm_g[l][None, :])
        rw = jnp.pad(jnp.concatenate([router_group_w[l], router_expert_w[l]], axis=1),
                     ((0, 0), (0, LANES - N_GROUPS - N_EXPERTS)))
        rw_hi = rw.astype(BF16)
        rw = jnp.concatenate([rw_hi, (rw - rw_hi.astype(F32)).astype(BF16)], axis=1)
        x1, h2, ri, rg, cnt = _outproj(ya.reshape(N, CONV_A_DIM), yb.reshape(N, DN_DIM), x.reshape(N, D),
                                       w_out[l].astype(BF16), ffn_norm_g[l][None, :], rw)
        counts = cnt[0, :N_EXPERTS].astype(I32)
        padded = (counts + bm - 1) // bm * bm
        seg_end = jnp.cumsum(padded).astype(I32)
        seg_start = seg_end - padded
        n_used = (seg_end[-1:] // bm)
        slots = _slots(ri, seg_start)
        seg = jnp.concatenate([seg_start + counts, seg_end, n_used])
        xs = _dispatch(seg, slots, h2, n_blocks * bm)
        blk = jnp.concatenate([seg_start // bm, padded // bm, n_used])
        ys = _ffn(blk, xs, w_gate[l], w_up[l], w_down[l])
        x = _combine(slots, ys, x1, rg, final_norm_g[None, :]).reshape(B, S, D)
    return x
```

```python
import functools

import jax
import jax.numpy as jnp
from jax import lax
from jax.experimental import pallas as pl
from jax.experimental.pallas import tpu as pltpu

F32 = jnp.float32
BF16 = jnp.bfloat16
I32 = jnp.int32

D_MODEL = 1024
CHUNK = 64
CONV_A_GROUP_DIM = 64
CONV_A_DIM = 512
CONV_A_WIDTH = 3
DN_HEADS = 4
DN_HEAD_DIM = 128
DN_DIM = 512
DN_CONV_WIDTH = 4
IN_PROJ_DIM = 3 * CONV_A_DIM + 4 * DN_DIM + 2 * DN_HEADS
N_GROUPS = 4
EXPERTS_PER_GROUP = 8
N_EXPERTS = 32
TOP_K = 2
EXPERT_FF = 512
EPS = 1e-6

LANES = 128
HALO = 8
IN_PROJ_PAD = 29 * LANES
GATE_COL = 3 * CONV_A_DIM + 4 * DN_DIM

TM_IN = 512
TM_PREP = 256
SCAN_BATCH = 4
TM_SCAN = 512
TM_OUT = 1024
OUT_SPLIT = 8
FFN_BLOCK = 256
TM_SLOTS = 2048
TM_DISPATCH = 2048
TM_COMBINE = 512
FFN_RING = 4
RING_PRIORITY = 1
ISSUE_UNROLL = 4
VMEM_LIMIT = 56 * 1024 * 1024


def _dot(a, b):
    return jnp.dot(a, b, preferred_element_type=F32)


def _dot_nt(a, b):
    return lax.dot_general(a, b, (((1,), (1,)), ((), ())), preferred_element_type=F32)


def _dot_tn(a, b):
    return lax.dot_general(a, b, (((0,), (0,)), ((), ())), preferred_element_type=F32)


def _split_bf16(x, parts):
    out = []
    for _ in range(parts):
        p = x.astype(BF16)
        out.append(p)
        x = x - p.astype(F32)
    return out


def _silu(x):
    return x * jax.nn.sigmoid(x)


ROW_SLAB = D_MODEL // LANES


def _store_rows(ref, val):
    m = val.shape[0]
    for c in range(ROW_SLAB):
        ref[pl.ds(c, m, stride=ROW_SLAB), :] = val[:, c * LANES:(c + 1) * LANES]


def _load_rows(ref, m):
    return jnp.concatenate([ref[pl.ds(c, m, stride=ROW_SLAB), :] for c in range(ROW_SLAB)], axis=-1)


def _slab(ref, row):
    return ref.at[pl.ds(pl.multiple_of(row * ROW_SLAB, ROW_SLAB), ROW_SLAB)]


def _slab_block(ref, block):
    n = FFN_BLOCK * ROW_SLAB
    return ref.at[pl.ds(pl.multiple_of(block * n, n), n)]


def _causal_conv(ext, w):
    taps = w.shape[0]
    acc = None
    for j in range(taps):
        shift = taps - 1 - j
        rows = pltpu.roll(ext, shift, 0)[HALO:] if shift else ext[HALO:]
        term = w[j:j + 1, :] * rows
        acc = term if acc is None else acc + term
    return acc


def _softplus(x):
    return jnp.maximum(x, 0.0) + jnp.log1p(jnp.exp(-jnp.abs(x)))


def _inproj_kernel(x_ref, g_ref, w_ref, caw_ref, cag_ref, dcw_ref, alog_ref, dtb_ref, gmat_ref,
                   ya_ref, q_ref, k_ref, v_ref, zg_ref, gcol_ref, grow_ref,
                   exta_ref, *extq_refs):
    tm = x_ref.shape[1]
    ext_refs = (exta_ref,) + extq_refs

    @pl.when(pl.program_id(1) == 0)
    def _():
        for ext_ref in ext_refs:
            ext_ref[0:HALO, :] = jnp.zeros((HALO, ext_ref.shape[1]), F32)

    x = x_ref[0]
    ms = jnp.mean(x * x, axis=-1, keepdims=True)
    hb = (x * lax.rsqrt(ms + EPS) * g_ref[...]).astype(BF16)

    def proj(c0, width):
        return _dot(hb, w_ref[:, c0:c0 + width])

    base = 3 * CONV_A_DIM

    def qkv_project(i):
        extq_refs[i][HALO:HALO + tm, :] = proj(base + i * DN_DIM, DN_DIM)

    def qkv_finish(i, out_ref):
        cols = slice(i * DN_DIM, (i + 1) * DN_DIM)
        s = _silu(_causal_conv(extq_refs[i][...], dcw_ref[:, cols]))
        if i == 2:
            out_ref[0] = s
        else:
            for h in range(DN_HEADS):
                sh = s[:, h * DN_HEAD_DIM:(h + 1) * DN_HEAD_DIM]
                inv = lax.rsqrt(jnp.sum(sh * sh, axis=-1, keepdims=True) + EPS)
                sh = sh * inv
                if i == 0:
                    sh = sh * (DN_HEAD_DIM ** -0.5)
                out_ref[0, :, h * DN_HEAD_DIM:(h + 1) * DN_HEAD_DIM] = sh

    def mixer_a_finish(a_b):
        y = a_b * _causal_conv(exta_ref[...], caw_ref[...])
        ysq = y * y
        hi = ysq.astype(BF16)
        lo = (ysq - hi.astype(F32)).astype(BF16)
        gmean = _dot(hi, gmat_ref[...]) + _dot(lo, gmat_ref[...])
        ya_ref[0] = (y * lax.rsqrt(gmean + EPS) * cag_ref[...]).astype(ya_ref.dtype)

    qkv_project(0)
    qkv_project(1)
    exta_ref[HALO:HALO + tm, :] = proj(2 * CONV_A_DIM, CONV_A_DIM) * proj(0, CONV_A_DIM)
    qkv_finish(0, q_ref)
    qkv_project(2)
    a_b = proj(CONV_A_DIM, CONV_A_DIM)
    qkv_finish(1, k_ref)
    z = proj(base + 3 * DN_DIM, DN_DIM)
    mixer_a_finish(a_b)
    p = proj(GATE_COL, LANES)
    qkv_finish(2, v_ref)
    zg_ref[0] = _silu(z)
    for ext_ref in ext_refs:
        ext_ref[0:HALO, :] = ext_ref[tm:tm + HALO, :]

    beta = jax.nn.sigmoid(p)
    g = -jnp.exp(alog_ref[...]) * _softplus(p + dtb_ref[...])
    r = lax.broadcasted_iota(I32, (CHUNK, CHUNK), 0)
    c = lax.broadcasted_iota(I32, (CHUNK, CHUNK), 1)
    tri = jnp.where(c <= r, 1.0, 0.0).astype(BF16)
    parts = _split_bf16(g, 3)
    gc = jnp.concatenate(
        [sum(_dot(tri, p[ci * CHUNK:(ci + 1) * CHUNK]) for p in parts) for ci in range(tm // CHUNK)], axis=0)
    lane = lax.broadcasted_iota(I32, (tm, LANES), 1)
    slab = jnp.where(lane < DN_HEADS, beta, gc)
    gcol_ref[0] = slab
    rows = slab.T[0:HALO, :]
    for ci in range(tm // CHUNK):
        grow_ref[0, ci] = rows[:, ci * CHUNK:(ci + 1) * CHUNK]


def _inproj(x, mix_g, w_in_pad, conv_a_w, conv_a_g, dn_conv_w, alog_vec, dtb_vec, gmat):
    B, S, D = x.shape
    tm = TM_IN
    full = lambda shape: pl.BlockSpec(shape, lambda b, t: (0,) * len(shape))
    row = lambda width: pl.BlockSpec((1, tm, width), lambda b, t: (b, t, 0))
    out_shape = (
        jax.ShapeDtypeStruct((B, S, CONV_A_DIM), BF16),
        jax.ShapeDtypeStruct((B, S, DN_DIM), F32),
        jax.ShapeDtypeStruct((B, S, DN_DIM), F32),
        jax.ShapeDtypeStruct((B, S, DN_DIM), F32),
        jax.ShapeDtypeStruct((B, S, DN_DIM), F32),
        jax.ShapeDtypeStruct((B, S, LANES), F32),
        jax.ShapeDtypeStruct((B, S // CHUNK, HALO, CHUNK), F32),
    )
    return pl.pallas_call(
        _inproj_kernel,
        grid=(B, S // tm),
        in_specs=[row(D), full((1, D)), full((D, IN_PROJ_PAD)), full((CONV_A_WIDTH, CONV_A_DIM)),
                  full((1, CONV_A_DIM)), full((DN_CONV_WIDTH, 3 * DN_DIM)), full((1, LANES)),
                  full((1, LANES)), full((CONV_A_DIM, CONV_A_DIM))],
        out_specs=(row(CONV_A_DIM), row(DN_DIM), row(DN_DIM), row(DN_DIM), row(DN_DIM), row(LANES),
                   pl.BlockSpec((1, tm // CHUNK, HALO, CHUNK), lambda b, t: (b, t, 0, 0))),
        out_shape=out_shape,
        scratch_shapes=[pltpu.VMEM((tm + HALO, CONV_A_DIM), F32)] + [pltpu.VMEM((tm + HALO, DN_DIM), F32)] * 3,
        compiler_params=pltpu.CompilerParams(
            dimension_semantics=("arbitrary", "arbitrary"), vmem_limit_bytes=VMEM_LIMIT),
        name="inproj",
    )(x, mix_g, w_in_pad, conv_a_w, conv_a_g, dn_conv_w, alog_vec, dtb_vec, gmat)


def _delta_prep_kernel(q_ref, k_ref, v_ref, gcol_ref, grow_ref,
                       u_ref, w_ref, qd_ref, kd_ref, at_ref):
    tm = q_ref.shape[1]
    ri = lax.broadcasted_iota(I32, (CHUNK, CHUNK), 0)
    ci = lax.broadcasted_iota(I32, (CHUNK, CHUNK), 1)
    causal = ci <= ri
    strict = ci < ri
    eye = jnp.where(ci == ri, 1.0, 0.0).astype(F32)
    lane = lax.broadcasted_iota(I32, (CHUNK, LANES), 1)
    left_half = lane < CHUNK

    def setup(c, h):
        rows = slice(c * CHUNK, (c + 1) * CHUNK)
        cols = slice(h * DN_HEAD_DIM, (h + 1) * DN_HEAD_DIM)
        slab = gcol_ref[0, rows, :]
        beta = jnp.sum(jnp.where(lane == h, slab, 0.0), axis=-1, keepdims=True)
        gcc = jnp.sum(jnp.where(lane == h + DN_HEADS, slab, 0.0), axis=-1, keepdims=True)
        gcr = grow_ref[0, c, h + DN_HEADS:h + DN_HEADS + 1, :]
        diff = gcc - gcr
        decay = jnp.where(causal, jnp.exp(jnp.where(causal, diff, 0.0)), 0.0)
        q = q_ref[0, rows, cols]
        k = k_ref[0, rows, cols]
        v = v_ref[0, rows, cols]
        kb = k * beta
        kq = _dot_nt(jnp.concatenate([kb.astype(BF16), q.astype(BF16)], axis=0), k.astype(BF16))
        L = jnp.where(strict, kq[:CHUNK] * decay, 0.0)
        intra = kq[CHUNK:] * decay
        egc = jnp.exp(gcc)
        gl = gcr[:, CHUNK - 1:CHUNK]
        qd_ref[0, rows, cols] = (q * egc).astype(BF16)
        kd_ref[0, rows, cols] = (k * jnp.exp(gl - gcc)).astype(BF16)
        at_ref[0, rows, h * CHUNK:(h + 1) * CHUNK] = intra.astype(BF16)
        return -L, jnp.concatenate([v * beta, kb * egc], axis=-1).astype(BF16)

    def first_stage(m):
        m16 = m.astype(BF16)
        return jnp.concatenate([eye + m, _dot(m16, m16)], axis=-1)

    def stage(r):
        return _dot(r[:, CHUNK:].astype(BF16), r.astype(BF16)) + jnp.where(left_half, r, 0.0)

    def last_stage(r):
        return r[:, :CHUNK] + _dot(r[:, CHUNK:].astype(BF16), r[:, :CHUNK].astype(BF16))

    def solve(c, h, t, rhs):
        rows = slice(c * CHUNK, (c + 1) * CHUNK)
        cols = slice(h * DN_HEAD_DIM, (h + 1) * DN_HEAD_DIM)
        uw = _dot(t.astype(BF16), rhs)
        u_ref[0, rows, cols] = uw[:, :DN_HEAD_DIM]
        w_ref[0, rows, cols] = uw[:, DN_HEAD_DIM:].astype(BF16)

    chains = [(c, h) for c in range(tm // CHUNK) for h in range(DN_HEADS)]
    ms, rhss = zip(*[setup(c, h) for c, h in chains])
    rs = [first_stage(m) for m in ms]
    for _ in range(4):
        rs = [stage(r) for r in rs]
    ts = [last_stage(r) for r in rs]
    for (c, h), t, rhs in zip(chains, ts, rhss):
        solve(c, h, t, rhs)


def _delta_prep(q, k, v, gcol, grow):
    B, S, _ = q.shape
    tm = TM_PREP
    row = lambda width: pl.BlockSpec((1, tm, width), lambda b, t: (b, t, 0))
    return pl.pallas_call(
        _delta_prep_kernel,
        grid=(B, S // tm),
        in_specs=[row(DN_DIM), row(DN_DIM), row(DN_DIM), row(LANES),
                  pl.BlockSpec((1, tm // CHUNK, HALO, CHUNK), lambda b, t: (b, t, 0, 0))],
        out_specs=(row(DN_DIM), row(DN_DIM), row(DN_DIM), row(DN_DIM), row(DN_HEADS * CHUNK)),
        out_shape=(jax.ShapeDtypeStruct((B, S, DN_DIM), F32),
                   jax.ShapeDtypeStruct((B, S, DN_DIM), BF16),
                   jax.ShapeDtypeStruct((B, S, DN_DIM), BF16),
                   jax.ShapeDtypeStruct((B, S, DN_DIM), BF16),
                   jax.ShapeDtypeStruct((B, S, DN_HEADS * CHUNK), BF16)),
        compiler_params=pltpu.CompilerParams(
            dimension_semantics=("arbitrary", "arbitrary"), vmem_limit_bytes=VMEM_LIMIT),
        name="delta_prep",
    )(q, k, v, gcol, grow)


def _delta_scan_kernel(u_ref, w_ref, qd_ref, kd_ref, at_ref, zg_ref, grow_ref, ng_ref, o_ref, st_ref):
    nb, tm = u_ref.shape[0], u_ref.shape[1]

    @pl.when(pl.program_id(1) == 0)
    def _():
        st_ref[...] = jnp.zeros(st_ref.shape, F32)

    def step(c, carry):
        r0 = pl.multiple_of(c * CHUNK, CHUNK)
        rows = pl.ds(r0, CHUNK)
        chains = [(b, h) for b in range(nb) for h in range(DN_HEADS)]
        cols = lambda h: slice(h * DN_HEAD_DIM, (h + 1) * DN_HEAD_DIM)
        sts = [st_ref[b, h] for b, h in chains]
        st16s = [st.astype(BF16) for st in sts]
        wqs = [_dot(jnp.concatenate([w_ref[b, rows, cols(h)], qd_ref[b, rows, cols(h)]], axis=0), st16)
               for (b, h), st16 in zip(chains, st16s)]
        vn16s = [(u_ref[b, rows, cols(h)] - wq[:CHUNK]).astype(BF16) for (b, h), wq in zip(chains, wqs)]
        upd = [_dot_tn(kd_ref[b, rows, cols(h)], vn16) for (b, h), vn16 in zip(chains, vn16s)]
        avs = [_dot(at_ref[b, rows, h * CHUNK:(h + 1) * CHUNK], vn16) for (b, h), vn16 in zip(chains, vn16s)]
        for (b, h), st, up in zip(chains, sts, upd):
            gl = grow_ref[b, c, h + DN_HEADS:h + DN_HEADS + 1, CHUNK - 1:CHUNK]
            st_ref[b, h] = st * jnp.exp(gl) + up
        for (b, h), wq, av in zip(chains, wqs, avs):
            o = wq[CHUNK:] + av
            on = o * lax.rsqrt(jnp.mean(o * o, axis=-1, keepdims=True) + EPS) * ng_ref[...]
            o_ref[b, rows, cols(h)] = (on * zg_ref[b, rows, cols(h)]).astype(o_ref.dtype)
        return carry

    lax.fori_loop(0, tm // CHUNK, step, 0)


def _delta_scan(u, w, qd, kd, at, zg, grow, norm_g):
    B, S, _ = u.shape
    tm = TM_SCAN
    nb = SCAN_BATCH
    row = lambda width: pl.BlockSpec((nb, tm, width), lambda b, t: (b, t, 0))
    return pl.pallas_call(
        _delta_scan_kernel,
        grid=(B // nb, S // tm),
        in_specs=[row(DN_DIM), row(DN_DIM), row(DN_DIM), row(DN_DIM), row(DN_HEADS * CHUNK), row(DN_DIM),
                  pl.BlockSpec((nb, tm // CHUNK, HALO, CHUNK), lambda b, t: (b, t, 0, 0)),
                  pl.BlockSpec((1, DN_HEAD_DIM), lambda b, t: (0, 0))],
        out_specs=row(DN_DIM),
        out_shape=jax.ShapeDtypeStruct((B, S, DN_DIM), BF16),
        scratch_shapes=[pltpu.VMEM((nb, DN_HEADS, DN_HEAD_DIM, DN_HEAD_DIM), F32)],
        compiler_params=pltpu.CompilerParams(
            dimension_semantics=("arbitrary", "arbitrary"), vmem_limit_bytes=VMEM_LIMIT),
        name="delta_scan",
    )(u, w, qd, kd, at, zg, grow, norm_g)


def _outproj_kernel(ya_ref, yb_ref, x_ref, wo_ref, g_ref, rw_ref,
                    x1_ref, h2_ref, ri_ref, rg_ref, cnt_ref, base_ref):
    tm = x_ref.shape[0]
    sub = tm // OUT_SPLIT
    parts = range(OUT_SPLIT)
    rows_of = lambda s: slice(s * sub, (s + 1) * sub)

    @pl.when(pl.program_id(0) == 0)
    def _():
        base_ref[...] = jnp.zeros(base_ref.shape, F32)

    def project(s):
        rows = rows_of(s)
        return _dot(jnp.concatenate([ya_ref[rows, :], yb_ref[rows, :]], axis=-1), wo_ref[...])

    def normalise(s, y):
        rows = rows_of(s)
        x1 = x_ref[rows, :] + y
        x1_ref[rows, :] = x1
        h = x1 * lax.rsqrt(jnp.mean(x1 * x1, axis=-1, keepdims=True) + EPS) * g_ref[...]
        _store_rows(h2_ref.at[pl.ds(s * sub * ROW_SLAB, sub * ROW_SLAB)], h)
        return h

    def router_logits(h):
        h_hi, h_lo = _split_bf16(h, 2)
        hi_prod = _dot(h_hi, rw_ref[...])
        return hi_prod[:, :LANES] + (hi_prod[:, LANES:] + _dot(h_lo, rw_ref[:, :LANES]))

    lane = lax.broadcasted_iota(I32, (sub, LANES), 1)
    lanef = lane.astype(F32)
    rr = lax.broadcasted_iota(I32, (sub, sub), 0)
    cc = lax.broadcasted_iota(I32, (sub, sub), 1)
    tri = jnp.where(cc < rr, 1.0, 0.0).astype(BF16)

    def route(s, logits):
        rows = rows_of(s)
        neg = jnp.float32(-1e30)
        big = jnp.float32(1e9)
        is_g = lane < N_GROUPS
        gl = jnp.where(is_g, logits, neg)
        gmax = jnp.max(gl, axis=-1, keepdims=True)
        gidx = jnp.min(jnp.where(gl == gmax, lanef, big), axis=-1, keepdims=True)
        gsum = jnp.sum(jnp.where(is_g, jnp.exp(gl - gmax), 0.0), axis=-1, keepdims=True)
        gprob = 1.0 / gsum
        lo = N_GROUPS + EXPERTS_PER_GROUP * gidx
        emask = (lanef >= lo) & (lanef < lo + EXPERTS_PER_GROUP)
        el = jnp.where(emask, logits, neg)
        e1 = jnp.max(el, axis=-1, keepdims=True)
        i1 = jnp.min(jnp.where(el == e1, lanef, big), axis=-1, keepdims=True)
        el2 = jnp.where(lanef == i1, neg, el)
        e2 = jnp.max(el2, axis=-1, keepdims=True)
        i2 = jnp.min(jnp.where(el2 == e2, lanef, big), axis=-1, keepdims=True)
        r = jnp.exp(e2 - e1)
        gate1 = gprob / (1.0 + r)
        gate2 = gprob * r / (1.0 + r)
        id1 = i1 - N_GROUPS
        id2 = i2 - N_GROUPS

        oh1 = jnp.where(lanef == id1, 1.0, 0.0).astype(F32)
        oh2 = jnp.where(lanef == id2, 1.0, 0.0).astype(F32)
        oh = oh1 + oh2
        before = _dot(tri, oh.astype(BF16)) + base_ref[...]
        rank1 = jnp.sum(oh1 * before, axis=-1, keepdims=True)
        rank2 = jnp.sum(oh2 * before, axis=-1, keepdims=True)
        base_ref[...] = base_ref[...] + jnp.sum(oh, axis=0, keepdims=True)

        ri = jnp.where(lane == 0, id1, jnp.where(lane == 1, id2,
                       jnp.where(lane == 2, rank1, jnp.where(lane == 3, rank2, 0.0))))
        ri_ref[rows, :] = ri.astype(I32)
        rg_ref[rows, :] = jnp.where(lane == 0, gate1, jnp.where(lane == 1, gate2, 0.0))

    ys = [project(s) for s in parts]
    hs = [normalise(s, y) for s, y in zip(parts, ys)]
    ls = [router_logits(h) for h in hs]
    for s, logits in zip(parts, ls):
        route(s, logits)
    cnt_ref[...] = base_ref[...]


def _outproj(ya, yb, x, w_out16, ffn_g, rw):
    N, D = x.shape
    tm = TM_OUT
    row = lambda width: pl.BlockSpec((tm, width), lambda i: (i, 0))
    full = lambda shape: pl.BlockSpec(shape, lambda i: (0,) * len(shape))
    return pl.pallas_call(
        _outproj_kernel,
        grid=(N // tm,),
        in_specs=[row(CONV_A_DIM), row(DN_DIM), row(D), full((D, D)), full((1, D)), full((D, 2 * LANES))],
        out_specs=(row(D), pl.BlockSpec((tm * ROW_SLAB, LANES), lambda i: (i, 0)), row(LANES), row(LANES),
                   full((1, LANES))),
        out_shape=(jax.ShapeDtypeStruct((N, D), F32),
                   jax.ShapeDtypeStruct((N * ROW_SLAB, LANES), F32),
                   jax.ShapeDtypeStruct((N, LANES), I32),
                   jax.ShapeDtypeStruct((N, LANES), F32),
                   jax.ShapeDtypeStruct((1, LANES), F32)),
        scratch_shapes=[pltpu.VMEM((1, LANES), F32)],
        compiler_params=pltpu.CompilerParams(
            dimension_semantics=("arbitrary",), vmem_limit_bytes=VMEM_LIMIT),
        name="outproj",
    )(ya, yb, x, w_out16, ffn_g, rw)


def _slots_kernel(ri_ref, start_ref, o_ref):
    rif = ri_ref[...].astype(F32)
    lane = lax.broadcasted_iota(I32, rif.shape, 1)
    lanef = lane.astype(F32)

    def slot(k):
        start = jnp.sum(jnp.where(lanef == rif[:, k:k + 1], start_ref[...], 0.0), axis=-1, keepdims=True)
        return start + rif[:, TOP_K + k:TOP_K + k + 1]

    slab = jnp.where(lane == 0, slot(0), jnp.where(lane == 1, slot(1), 0.0))
    o_ref[...] = slab.T[0:HALO, :].astype(I32)


def _slots(ri, seg_start):
    N = ri.shape[0]
    tm = TM_SLOTS
    out = pl.pallas_call(
        _slots_kernel,
        grid=(N // tm,),
        in_specs=[pl.BlockSpec((tm, LANES), lambda i: (i, 0)), pl.BlockSpec((1, LANES), lambda i: (0, 0))],
        out_specs=pl.BlockSpec((HALO, tm), lambda i: (0, i)),
        out_shape=jax.ShapeDtypeStruct((HALO, N), I32),
        compiler_params=pltpu.CompilerParams(dimension_semantics=("arbitrary",)),
        name="slots",
    )(ri, _lane_vec(seg_start, 0))
    return out[:TOP_K].reshape(-1)


def _dispatch_kernel(seg_ref, slot0_ref, slot1_ref, h_ref, xs_ref, zero_ref, sem, zsem):
    tm = TM_DISPATCH
    bm = FFN_BLOCK
    n_blocks = xs_ref.shape[0] // (bm * ROW_SLAB)

    @pl.when(pl.program_id(0) == 0)
    def _():
        zero_ref[...] = jnp.zeros(zero_ref.shape, F32)

        def pad_copy(d):
            return pltpu.make_async_copy(zero_ref.at[pl.ds(0, ROW_SLAB)], _slab(xs_ref, d), zsem)

        def block_copy(b):
            return pltpu.make_async_copy(zero_ref, _slab_block(xs_ref, b), zsem)

        def each_pad(fn):
            def per_expert(e, carry):
                def per_row(d, c):
                    fn(pad_copy(d))
                    return c
                return lax.fori_loop(seg_ref[e], seg_ref[N_EXPERTS + e], per_row, carry)
            lax.fori_loop(0, N_EXPERTS, per_expert, 0)

            def per_block(b, c):
                fn(block_copy(b))
                return c
            lax.fori_loop(seg_ref[2 * N_EXPERTS], n_blocks, per_block, 0)

        each_pad(lambda cp: cp.start())
        each_pad(lambda cp: cp.wait())

    def issue(t, carry):
        for k, slot_ref in enumerate((slot0_ref, slot1_ref)):
            pltpu.make_async_copy(_slab(h_ref, t), _slab(xs_ref, slot_ref[t]), sem).start(priority=k)
        return carry

    lax.fori_loop(0, tm, issue, 0, unroll=ISSUE_UNROLL)
    for _ in range(TOP_K):
        pltpu.make_async_copy(h_ref, xs_ref.at[pl.ds(0, tm * ROW_SLAB)], sem).wait()


def _dispatch(seg, slots, h2, n_slots):
    N = h2.shape[0] // ROW_SLAB
    tm = TM_DISPATCH
    grid_spec = pltpu.PrefetchScalarGridSpec(
        num_scalar_prefetch=1,
        grid=(N // tm,),
        in_specs=[pl.BlockSpec((tm,), lambda i, s: (i,), memory_space=pltpu.SMEM),
                  pl.BlockSpec((tm,), lambda i, s: (N // tm + i,), memory_space=pltpu.SMEM),
                  pl.BlockSpec((tm * ROW_SLAB, LANES), lambda i, s: (i, 0))],
        out_specs=pl.BlockSpec(memory_space=pl.ANY),
        scratch_shapes=[pltpu.VMEM((FFN_BLOCK * ROW_SLAB, LANES), F32),
                        pltpu.SemaphoreType.DMA(()), pltpu.SemaphoreType.DMA(())],
    )
    return pl.pallas_call(
        _dispatch_kernel,
        grid_spec=grid_spec,
        out_shape=jax.ShapeDtypeStruct((n_slots * ROW_SLAB, LANES), F32),
        compiler_params=pltpu.CompilerParams(dimension_semantics=("arbitrary",)),
        name="dispatch",
    )(seg, slots, slots, h2)


def _ffn_kernel(blk_ref, xs_ref, wg_ref, wu_ref, wd_ref, ys_ref,
                xbuf, ybuf, wg16, wu16, wd16, xsem, ysem, zsem):
    e = pl.program_id(0)
    bm = FFN_BLOCK
    ring = FFN_RING
    n_blocks = ys_ref.shape[0] // (bm * ROW_SLAB)
    first = blk_ref[e]
    count = blk_ref[N_EXPERTS + e]
    n_used = blk_ref[2 * N_EXPERTS]

    def x_copy(g):
        return pltpu.make_async_copy(_slab_block(xs_ref, g), xbuf.at[g % ring], xsem.at[g % ring])

    def y_copy(g):
        return pltpu.make_async_copy(ybuf.at[g % ring], _slab_block(ys_ref, g), ysem.at[g % ring])

    def when_block(g, fn):
        @pl.when((g >= 0) & (g < n_used))
        def _():
            fn(g)

    @pl.when(e == 0)
    def _():
        for g in range(ring - 1):
            when_block(g, lambda g: x_copy(g).start(priority=RING_PRIORITY))

    @pl.when(count > 0)
    def _():
        wg16[...] = wg_ref[0].astype(BF16)
        wu16[...] = wu_ref[0].astype(BF16)
        wd16[...] = wd_ref[0].astype(BF16)

    def block(j, carry):
        g = first + j
        x_copy(g).wait()
        when_block(g + ring - 1, lambda g: x_copy(g).start(priority=RING_PRIORITY))

        x = _load_rows(xbuf.at[g % ring], bm).astype(BF16)
        a = _dot(x, wg16[...])
        b = _dot(x, wu16[...])
        y = _dot((_silu(a) * b).astype(BF16), wd16[...])

        when_block(g - ring, lambda g: y_copy(g).wait())
        _store_rows(ybuf.at[g % ring], y)
        y_copy(g).start(priority=RING_PRIORITY)
        return carry

    lax.fori_loop(0, count, block, 0)

    @pl.when(e == pl.num_programs(0) - 1)
    def _():
        for back in range(ring, 0, -1):
            when_block(n_used - back, lambda g: y_copy(g).wait())
        ybuf[0] = jnp.zeros(ybuf.shape[1:], F32)

        def zero_copy(g):
            return pltpu.make_async_copy(ybuf.at[0], _slab_block(ys_ref, g), zsem)

        def start(g, c):
            zero_copy(g).start()
            return c

        def wait(g, c):
            zero_copy(g).wait()
            return c

        lax.fori_loop(n_used, n_blocks, start, 0)
        lax.fori_loop(n_used, n_blocks, wait, 0)


def _ffn(blk, xs, w_gate, w_up, w_down):
    bm = FFN_BLOCK
    D = D_MODEL
    weights = lambda shape: pl.BlockSpec((1,) + shape, lambda e, blk: (e, 0, 0))
    grid_spec = pltpu.PrefetchScalarGridSpec(
        num_scalar_prefetch=1,
        grid=(N_EXPERTS,),
        in_specs=[pl.BlockSpec(memory_space=pl.ANY),
                  weights((D, EXPERT_FF)), weights((D, EXPERT_FF)), weights((EXPERT_FF, D))],
        out_specs=pl.BlockSpec(memory_space=pl.ANY),
        scratch_shapes=[pltpu.VMEM((FFN_RING, bm * ROW_SLAB, LANES), F32),
                        pltpu.VMEM((FFN_RING, bm * ROW_SLAB, LANES), F32),
                        pltpu.VMEM((D, EXPERT_FF), BF16), pltpu.VMEM((D, EXPERT_FF), BF16),
                        pltpu.VMEM((EXPERT_FF, D), BF16),
                        pltpu.SemaphoreType.DMA((FFN_RING,)), pltpu.SemaphoreType.DMA((FFN_RING,)),
                        pltpu.SemaphoreType.DMA(())],
    )
    return pl.pallas_call(
        _ffn_kernel,
        grid_spec=grid_spec,
        out_shape=jax.ShapeDtypeStruct(xs.shape, F32),
        compiler_params=pltpu.CompilerParams(
            dimension_semantics=("arbitrary",), vmem_limit_bytes=VMEM_LIMIT),
        name="ffn",
    )(blk, xs, w_gate, w_up, w_down)


def _combine_kernel(slot0_ref, slot1_ref, next0_ref, next1_ref, ys_ref, x1_ref, rg_ref, g_ref, o_ref,
                    buf_ref, sems):
    tm = TM_COMBINE
    i = pl.program_id(0)
    half = i % 2

    def gather(slot_refs, s):
        def issue(t, carry):
            for k, slot_ref in enumerate(slot_refs):
                pltpu.make_async_copy(_slab(ys_ref, slot_ref[t]), _slab(buf_ref.at[s, k], t),
                                      sems.at[s]).start(priority=k)
            return carry
        lax.fori_loop(0, tm, issue, 0, unroll=ISSUE_UNROLL)

    @pl.when(i == 0)
    def _():
        gather((slot0_ref, slot1_ref), 0)

    @pl.when(i + 1 < pl.num_programs(0))
    def _():
        gather((next0_ref, next1_ref), 1 - half)

    for k in range(TOP_K):
        pltpu.make_async_copy(ys_ref.at[pl.ds(0, tm * ROW_SLAB)], buf_ref.at[half, k], sems.at[half]).wait()

    rg = rg_ref[...]
    moe = (_load_rows(buf_ref.at[half, 0], tm) * rg[:, 0:1]
           + _load_rows(buf_ref.at[half, 1], tm) * rg[:, 1:2])
    x2 = x1_ref[...] + moe
    o_ref[...] = x2 * lax.rsqrt(jnp.mean(x2 * x2, axis=-1, keepdims=True) + EPS) * g_ref[...]


def _combine(slots, ys, x1, rg, final_g):
    N, D = x1.shape
    tm = TM_COMBINE
    n_tiles = N // tm

    def slot_block(k, ahead, i):
        return (k * n_tiles + jnp.minimum(i + ahead, n_tiles - 1),)

    return pl.pallas_call(
        _combine_kernel,
        grid=(n_tiles,),
        in_specs=[pl.BlockSpec((tm,), functools.partial(slot_block, k, ahead), memory_space=pltpu.SMEM)
                  for ahead in (0, 1) for k in range(TOP_K)]
                 + [pl.BlockSpec(memory_space=pl.ANY),
                  pl.BlockSpec((tm, D), lambda i: (i, 0)),
                  pl.BlockSpec((tm, LANES), lambda i: (i, 0)),
                  pl.BlockSpec((1, D), lambda i: (0, 0))],
        out_specs=pl.BlockSpec((tm, D), lambda i: (i, 0)),
        out_shape=jax.ShapeDtypeStruct((N, D), F32),
        scratch_shapes=[pltpu.VMEM((2, TOP_K, tm * ROW_SLAB, LANES), F32), pltpu.SemaphoreType.DMA((2,))],
        compiler_params=pltpu.CompilerParams(
            dimension_semantics=("arbitrary",), vmem_limit_bytes=VMEM_LIMIT),
        name="combine",
    )(slots, slots, slots, slots, ys, x1, rg, final_g)


def _lane_vec(values, offset):
    return jnp.zeros((1, LANES), F32).at[0, offset:offset + values.shape[0]].set(values.astype(F32))


def kernel(x, mix_norm_g, w_in, conv_a_w, conv_a_norm_g, dn_conv_w, dn_a_log, dn_dt_bias, dn_norm_g,
           w_out, ffn_norm_g, router_group_w, router_expert_w, w_gate, w_up, w_down, final_norm_g):
    B, S, D = x.shape
    N = B * S
    depth = w_in.shape[0]
    assert depth == 1, "single-layer block: the final RMSNorm is fused into the layer's combine step"
    group_of = jnp.arange(CONV_A_DIM, dtype=I32) // CONV_A_GROUP_DIM
    gmat = jnp.where(group_of[:, None] == group_of[None, :], 1.0 / CONV_A_GROUP_DIM, 0.0).astype(BF16)
    bm = FFN_BLOCK
    n_blocks = (N * TOP_K) // bm + N_EXPERTS
    for l in range(depth):
        w_in_pad = jnp.pad(w_in[l], ((0, 0), (0, IN_PROJ_PAD - IN_PROJ_DIM))).astype(BF16)
        ya, q, k, v, zg, gcol, grow = _inproj(
            x, mix_norm_g[l][None, :], w_in_pad, conv_a_w[l], conv_a_norm_g[l][None, :], dn_conv_w[l],
            _lane_vec(dn_a_log[l], DN_HEADS), _lane_vec(dn_dt_bias[l], DN_HEADS), gmat)
        u, w, qd, kd, at = _delta_prep(q, k, v, gcol, grow)
        yb = _delta_scan(u, w, qd, kd, at, zg, grow, dn_norm_g[l][None, :])
        rw = jnp.pad(jnp.concatenate([router_group_w[l], router_expert_w[l]], axis=1),
                     ((0, 0), (0, LANES - N_GROUPS - N_EXPERTS)))
        rw_hi = rw.astype(BF16)
        rw = jnp.concatenate([rw_hi, (rw - rw_hi.astype(F32)).astype(BF16)], axis=1)
        x1, h2, ri, rg, cnt = _outproj(ya.reshape(N, CONV_A_DIM), yb.reshape(N, DN_DIM), x.reshape(N, D),
                                       w_out[l].astype(BF16), ffn_norm_g[l][None, :], rw)
        counts = cnt[0, :N_EXPERTS].astype(I32)
        padded = (counts + bm - 1) // bm * bm
        seg_end = jnp.cumsum(padded).astype(I32)
        seg_start = seg_end - padded
        n_used = (seg_end[-1:] // bm)
        slots = _slots(ri, seg_start)
        seg = jnp.concatenate([seg_start + counts, seg_end, n_used])
        xs = _dispatch(seg, slots, h2, n_blocks * bm)
        blk = jnp.concatenate([seg_start // bm, padded // bm, n_used])
        ys = _ffn(blk, xs, w_gate[l], w_up[l], w_down[l])
        x = _combine(slots, ys, x1, rg, final_norm_g[None, :]).reshape(B, S, D)
    return x
```

```python
import functools

import jax
import jax.numpy as jnp
from jax import lax
from jax.experimental import pallas as pl
from jax.experimental.pallas import tpu as pltpu

F32 = jnp.float32
BF16 = jnp.bfloat16
I32 = jnp.int32

D_MODEL = 1024
CHUNK = 64
CONV_A_GROUP_DIM = 64
CONV_A_DIM = 512
CONV_A_WIDTH = 3
DN_HEADS = 4
DN_HEAD_DIM = 128
DN_DIM = 512
DN_CONV_WIDTH = 4
IN_PROJ_DIM = 3 * CONV_A_DIM + 4 * DN_DIM + 2 * DN_HEADS
N_GROUPS = 4
EXPERTS_PER_GROUP = 8
N_EXPERTS = 32
TOP_K = 2
EXPERT_FF = 512
EPS = 1e-6

LANES = 128
HALO = 8
IN_PROJ_PAD = 29 * LANES
GATE_COL = 3 * CONV_A_DIM + 4 * DN_DIM

TM_IN = 512
TM_PREP = 256
SCAN_BATCH = 4
TM_SCAN = 512
TM_OUT = 1024
OUT_SPLIT = 8
FFN_BLOCK = 256
TM_SLOTS = 2048
TM_DISPATCH = 2048
TM_COMBINE = 512
FFN_SPLIT = 2
FFN_RING = 4
RING_PRIORITY = 1
ISSUE_UNROLL = 4
VMEM_LIMIT = 56 * 1024 * 1024


def _dot(a, b):
    return jnp.dot(a, b, preferred_element_type=F32)


def _dot_nt(a, b):
    return lax.dot_general(a, b, (((1,), (1,)), ((), ())), preferred_element_type=F32)


def _dot_tn(a, b):
    return lax.dot_general(a, b, (((0,), (0,)), ((), ())), preferred_element_type=F32)


def _split_bf16(x, parts):
    out = []
    for _ in range(parts):
        p = x.astype(BF16)
        out.append(p)
        x = x - p.astype(F32)
    return out


def _silu(x):
    return x * jax.nn.sigmoid(x)


ROW_SLAB = D_MODEL // LANES


def _store_rows(ref, val):
    m = val.shape[0]
    for c in range(ROW_SLAB):
        ref[pl.ds(c, m, stride=ROW_SLAB), :] = val[:, c * LANES:(c + 1) * LANES]


def _load_rows(ref, m):
    return jnp.concatenate([ref[pl.ds(c, m, stride=ROW_SLAB), :] for c in range(ROW_SLAB)], axis=-1)


def _slab(ref, row):
    return ref.at[pl.ds(pl.multiple_of(row * ROW_SLAB, ROW_SLAB), ROW_SLAB)]


def _slab_block(ref, block):
    n = FFN_BLOCK * ROW_SLAB
    return ref.at[pl.ds(pl.multiple_of(block * n, n), n)]


def _causal_conv(ext, w):
    taps = w.shape[0]
    acc = None
    for j in range(taps):
        shift = taps - 1 - j
        rows = pltpu.roll(ext, shift, 0)[HALO:] if shift else ext[HALO:]
        term = w[j:j + 1, :] * rows
        acc = term if acc is None else acc + term
    return acc


def _softplus(x):
    return jnp.maximum(x, 0.0) + jnp.log1p(jnp.exp(-jnp.abs(x)))


def _inproj_kernel(x_ref, g_ref, w_ref, caw_ref, cag_ref, dcw_ref, alog_ref, dtb_ref, gmat_ref,
                   ya_ref, q_ref, k_ref, v_ref, zg_ref, gcol_ref, grow_ref,
                   exta_ref, *extq_refs):
    tm = x_ref.shape[1]
    ext_refs = (exta_ref,) + extq_refs

    @pl.when(pl.program_id(1) == 0)
    def _():
        for ext_ref in ext_refs:
            ext_ref[0:HALO, :] = jnp.zeros((HALO, ext_ref.shape[1]), F32)

    x = x_ref[0]
    ms = jnp.mean(x * x, axis=-1, keepdims=True)
    hb = (x * lax.rsqrt(ms + EPS) * g_ref[...]).astype(BF16)

    def proj(c0, width):
        return _dot(hb, w_ref[:, c0:c0 + width])

    base = 3 * CONV_A_DIM

    def qkv_project(i):
        extq_refs[i][HALO:HALO + tm, :] = proj(base + i * DN_DIM, DN_DIM)

    def qkv_finish(i, out_ref):
        cols = slice(i * DN_DIM, (i + 1) * DN_DIM)
        s = _silu(_causal_conv(extq_refs[i][...], dcw_ref[:, cols]))
        if i == 2:
            out_ref[0] = s
        else:
            for h in range(DN_HEADS):
                sh = s[:, h * DN_HEAD_DIM:(h + 1) * DN_HEAD_DIM]
                inv = lax.rsqrt(jnp.sum(sh * sh, axis=-1, keepdims=True) + EPS)
                sh = sh * inv
                if i == 0:
                    sh = sh * (DN_HEAD_DIM ** -0.5)
                out_ref[0, :, h * DN_HEAD_DIM:(h + 1) * DN_HEAD_DIM] = sh

    def mixer_a_finish(a_b):
        y = a_b * _causal_conv(exta_ref[...], caw_ref[...])
        ysq = y * y
        hi = ysq.astype(BF16)
        lo = (ysq - hi.astype(F32)).astype(BF16)
        gmean = _dot(hi, gmat_ref[...]) + _dot(lo, gmat_ref[...])
        ya_ref[0] = (y * lax.rsqrt(gmean + EPS) * cag_ref[...]).astype(ya_ref.dtype)

    qkv_project(0)
    qkv_project(1)
    exta_ref[HALO:HALO + tm, :] = proj(2 * CONV_A_DIM, CONV_A_DIM) * proj(0, CONV_A_DIM)
    qkv_finish(0, q_ref)
    qkv_project(2)
    a_b = proj(CONV_A_DIM, CONV_A_DIM)
    qkv_finish(1, k_ref)
    z = proj(base + 3 * DN_DIM, DN_DIM)
    mixer_a_finish(a_b)
    p = proj(GATE_COL, LANES)
    qkv_finish(2, v_ref)
    zg_ref[0] = _silu(z)
    for ext_ref in ext_refs:
        ext_ref[0:HALO, :] = ext_ref[tm:tm + HALO, :]

    beta = jax.nn.sigmoid(p)
    g = -jnp.exp(alog_ref[...]) * _softplus(p + dtb_ref[...])
    r = lax.broadcasted_iota(I32, (CHUNK, CHUNK), 0)
    c = lax.broadcasted_iota(I32, (CHUNK, CHUNK), 1)
    tri = jnp.where(c <= r, 1.0, 0.0).astype(BF16)
    parts = _split_bf16(g, 3)
    gc = jnp.concatenate(
        [sum(_dot(tri, p[ci * CHUNK:(ci + 1) * CHUNK]) for p in parts) for ci in range(tm // CHUNK)], axis=0)
    lane = lax.broadcasted_iota(I32, (tm, LANES), 1)
    slab = jnp.where(lane < DN_HEADS, beta, gc)
    gcol_ref[0] = slab
    rows = slab.T[0:HALO, :]
    for ci in range(tm // CHUNK):
        grow_ref[0, ci] = rows[:, ci * CHUNK:(ci + 1) * CHUNK]


def _inproj(x, mix_g, w_in_pad, conv_a_w, conv_a_g, dn_conv_w, alog_vec, dtb_vec, gmat):
    B, S, D = x.shape
    tm = TM_IN
    full = lambda shape: pl.BlockSpec(shape, lambda b, t: (0,) * len(shape))
    row = lambda width: pl.BlockSpec((1, tm, width), lambda b, t: (b, t, 0))
    out_shape = (
        jax.ShapeDtypeStruct((B, S, CONV_A_DIM), BF16),
        jax.ShapeDtypeStruct((B, S, DN_DIM), F32),
        jax.ShapeDtypeStruct((B, S, DN_DIM), F32),
        jax.ShapeDtypeStruct((B, S, DN_DIM), F32),
        jax.ShapeDtypeStruct((B, S, DN_DIM), F32),
        jax.ShapeDtypeStruct((B, S, LANES), F32),
        jax.ShapeDtypeStruct((B, S // CHUNK, HALO, CHUNK), F32),
    )
    return pl.pallas_call(
        _inproj_kernel,
        grid=(B, S // tm),
        in_specs=[row(D), full((1, D)), full((D, IN_PROJ_PAD)), full((CONV_A_WIDTH, CONV_A_DIM)),
                  full((1, CONV_A_DIM)), full((DN_CONV_WIDTH, 3 * DN_DIM)), full((1, LANES)),
                  full((1, LANES)), full((CONV_A_DIM, CONV_A_DIM))],
        out_specs=(row(CONV_A_DIM), row(DN_DIM), row(DN_DIM), row(DN_DIM), row(DN_DIM), row(LANES),
                   pl.BlockSpec((1, tm // CHUNK, HALO, CHUNK), lambda b, t: (b, t, 0, 0))),
        out_shape=out_shape,
        scratch_shapes=[pltpu.VMEM((tm + HALO, CONV_A_DIM), F32)] + [pltpu.VMEM((tm + HALO, DN_DIM), F32)] * 3,
        compiler_params=pltpu.CompilerParams(
            dimension_semantics=("arbitrary", "arbitrary"), vmem_limit_bytes=VMEM_LIMIT),
        name="inproj",
    )(x, mix_g, w_in_pad, conv_a_w, conv_a_g, dn_conv_w, alog_vec, dtb_vec, gmat)


def _delta_prep_kernel(q_ref, k_ref, v_ref, gcol_ref, grow_ref,
                       u_ref, w_ref, qd_ref, kd_ref, at_ref):
    tm = q_ref.shape[1]
    ri = lax.broadcasted_iota(I32, (CHUNK, CHUNK), 0)
    ci = lax.broadcasted_iota(I32, (CHUNK, CHUNK), 1)
    causal = ci <= ri
    strict = ci < ri
    eye = jnp.where(ci == ri, 1.0, 0.0).astype(F32)
    lane = lax.broadcasted_iota(I32, (CHUNK, LANES), 1)
    left_half = lane < CHUNK

    def setup(c, h):
        rows = slice(c * CHUNK, (c + 1) * CHUNK)
        cols = slice(h * DN_HEAD_DIM, (h + 1) * DN_HEAD_DIM)
        slab = gcol_ref[0, rows, :]
        beta = jnp.sum(jnp.where(lane == h, slab, 0.0), axis=-1, keepdims=True)
        gcc = jnp.sum(jnp.where(lane == h + DN_HEADS, slab, 0.0), axis=-1, keepdims=True)
        gcr = grow_ref[0, c, h + DN_HEADS:h + DN_HEADS + 1, :]
        diff = gcc - gcr
        decay = jnp.where(causal, jnp.exp(jnp.where(causal, diff, 0.0)), 0.0)
        q = q_ref[0, rows, cols]
        k = k_ref[0, rows, cols]
        v = v_ref[0, rows, cols]
        kb = k * beta
        kq = _dot_nt(jnp.concatenate([kb.astype(BF16), q.astype(BF16)], axis=0), k.astype(BF16))
        L = jnp.where(strict, kq[:CHUNK] * decay, 0.0)
        intra = kq[CHUNK:] * decay
        egc = jnp.exp(gcc)
        gl = gcr[:, CHUNK - 1:CHUNK]
        qd_ref[0, rows, cols] = (q * egc).astype(BF16)
        kd_ref[0, rows, cols] = (k * jnp.exp(gl - gcc)).astype(BF16)
        at_ref[0, rows, h * CHUNK:(h + 1) * CHUNK] = intra.astype(BF16)
        return -L, jnp.concatenate([v * beta, kb * egc], axis=-1).astype(BF16)

    def first_stage(m):
        m16 = m.astype(BF16)
        return jnp.concatenate([eye + m, _dot(m16, m16)], axis=-1)

    def stage(r):
        return _dot(r[:, CHUNK:].astype(BF16), r.astype(BF16)) + jnp.where(left_half, r, 0.0)

    def last_stage(r):
        return r[:, :CHUNK] + _dot(r[:, CHUNK:].astype(BF16), r[:, :CHUNK].astype(BF16))

    def solve(c, h, t, rhs):
        rows = slice(c * CHUNK, (c + 1) * CHUNK)
        cols = slice(h * DN_HEAD_DIM, (h + 1) * DN_HEAD_DIM)
        uw = _dot(t.astype(BF16), rhs)
        u_ref[0, rows, cols] = uw[:, :DN_HEAD_DIM]
        w_ref[0, rows, cols] = uw[:, DN_HEAD_DIM:].astype(BF16)

    chains = [(c, h) for c in range(tm // CHUNK) for h in range(DN_HEADS)]
    ms, rhss = zip(*[setup(c, h) for c, h in chains])
    rs = [first_stage(m) for m in ms]
    for _ in range(4):
        rs = [stage(r) for r in rs]
    ts = [last_stage(r) for r in rs]
    for (c, h), t, rhs in zip(chains, ts, rhss):
        solve(c, h, t, rhs)


def _delta_prep(q, k, v, gcol, grow):
    B, S, _ = q.shape
    tm = TM_PREP
    row = lambda width: pl.BlockSpec((1, tm, width), lambda b, t: (b, t, 0))
    return pl.pallas_call(
        _delta_prep_kernel,
        grid=(B, S // tm),
        in_specs=[row(DN_DIM), row(DN_DIM), row(DN_DIM), row(LANES),
                  pl.BlockSpec((1, tm // CHUNK, HALO, CHUNK), lambda b, t: (b, t, 0, 0))],
        out_specs=(row(DN_DIM), row(DN_DIM), row(DN_DIM), row(DN_DIM), row(DN_HEADS * CHUNK)),
        out_shape=(jax.ShapeDtypeStruct((B, S, DN_DIM), F32),
                   jax.ShapeDtypeStruct((B, S, DN_DIM), BF16),
                   jax.ShapeDtypeStruct((B, S, DN_DIM), BF16),
                   jax.ShapeDtypeStruct((B, S, DN_DIM), BF16),
                   jax.ShapeDtypeStruct((B, S, DN_HEADS * CHUNK), BF16)),
        compiler_params=pltpu.CompilerParams(
            dimension_semantics=("arbitrary", "arbitrary"), vmem_limit_bytes=VMEM_LIMIT),
        name="delta_prep",
    )(q, k, v, gcol, grow)


def _delta_scan_kernel(u_ref, w_ref, qd_ref, kd_ref, at_ref, zg_ref, grow_ref, ng_ref, o_ref, st_ref):
    nb, tm = u_ref.shape[0], u_ref.shape[1]

    @pl.when(pl.program_id(1) == 0)
    def _():
        st_ref[...] = jnp.zeros(st_ref.shape, F32)

    def step(c, carry):
        r0 = pl.multiple_of(c * CHUNK, CHUNK)
        rows = pl.ds(r0, CHUNK)
        chains = [(b, h) for b in range(nb) for h in range(DN_HEADS)]
        cols = lambda h: slice(h * DN_HEAD_DIM, (h + 1) * DN_HEAD_DIM)
        sts = [st_ref[b, h] for b, h in chains]
        st16s = [st.astype(BF16) for st in sts]
        wqs = [_dot(jnp.concatenate([w_ref[b, rows, cols(h)], qd_ref[b, rows, cols(h)]], axis=0), st16)
               for (b, h), st16 in zip(chains, st16s)]
        vn16s = [(u_ref[b, rows, cols(h)] - wq[:CHUNK]).astype(BF16) for (b, h), wq in zip(chains, wqs)]
        upd = [_dot_tn(kd_ref[b, rows, cols(h)], vn16) for (b, h), vn16 in zip(chains, vn16s)]
        avs = [_dot(at_ref[b, rows, h * CHUNK:(h + 1) * CHUNK], vn16) for (b, h), vn16 in zip(chains, vn16s)]
        for (b, h), st, up in zip(chains, sts, upd):
            gl = grow_ref[b, c, h + DN_HEADS:h + DN_HEADS + 1, CHUNK - 1:CHUNK]
            st_ref[b, h] = st * jnp.exp(gl) + up
        for (b, h), wq, av in zip(chains, wqs, avs):
            o = wq[CHUNK:] + av
            on = o * lax.rsqrt(jnp.mean(o * o, axis=-1, keepdims=True) + EPS) * ng_ref[...]
            o_ref[b, rows, cols(h)] = (on * zg_ref[b, rows, cols(h)]).astype(o_ref.dtype)
        return carry

    lax.fori_loop(0, tm // CHUNK, step, 0)


def _delta_scan(u, w, qd, kd, at, zg, grow, norm_g):
    B, S, _ = u.shape
    tm = TM_SCAN
    nb = SCAN_BATCH
    row = lambda width: pl.BlockSpec((nb, tm, width), lambda b, t: (b, t, 0))
    return pl.pallas_call(
        _delta_scan_kernel,
        grid=(B // nb, S // tm),
        in_specs=[row(DN_DIM), row(DN_DIM), row(DN_DIM), row(DN_DIM), row(DN_HEADS * CHUNK), row(DN_DIM),
                  pl.BlockSpec((nb, tm // CHUNK, HALO, CHUNK), lambda b, t: (b, t, 0, 0)),
                  pl.BlockSpec((1, DN_HEAD_DIM), lambda b, t: (0, 0))],
        out_specs=row(DN_DIM),
        out_shape=jax.ShapeDtypeStruct((B, S, DN_DIM), BF16),
        scratch_shapes=[pltpu.VMEM((nb, DN_HEADS, DN_HEAD_DIM, DN_HEAD_DIM), F32)],
        compiler_params=pltpu.CompilerParams(
            dimension_semantics=("arbitrary", "arbitrary"), vmem_limit_bytes=VMEM_LIMIT),
        name="delta_scan",
    )(u, w, qd, kd, at, zg, grow, norm_g)


def _outproj_kernel(ya_ref, yb_ref, x_ref, wo_ref, g_ref, rw_ref,
                    x1_ref, h2_ref, ri_ref, rg_ref, cnt_ref, base_ref):
    tm = x_ref.shape[0]
    sub = tm // OUT_SPLIT
    parts = range(OUT_SPLIT)
    rows_of = lambda s: slice(s * sub, (s + 1) * sub)

    @pl.when(pl.program_id(0) == 0)
    def _():
        base_ref[...] = jnp.zeros(base_ref.shape, F32)

    def project(s):
        rows = rows_of(s)
        return _dot(jnp.concatenate([ya_ref[rows, :], yb_ref[rows, :]], axis=-1), wo_ref[...])

    def normalise(s, y):
        rows = rows_of(s)
        x1 = x_ref[rows, :] + y
        x1_ref[rows, :] = x1
        h = x1 * lax.rsqrt(jnp.mean(x1 * x1, axis=-1, keepdims=True) + EPS) * g_ref[...]
        _store_rows(h2_ref.at[pl.ds(s * sub * ROW_SLAB, sub * ROW_SLAB)], h)
        return h

    def router_logits(h):
        h_hi, h_lo = _split_bf16(h, 2)
        hi_prod = _dot(h_hi, rw_ref[...])
        return hi_prod[:, :LANES] + (hi_prod[:, LANES:] + _dot(h_lo, rw_ref[:, :LANES]))

    lane = lax.broadcasted_iota(I32, (sub, LANES), 1)
    lanef = lane.astype(F32)
    rr = lax.broadcasted_iota(I32, (sub, sub), 0)
    cc = lax.broadcasted_iota(I32, (sub, sub), 1)
    tri = jnp.where(cc < rr, 1.0, 0.0).astype(BF16)

    def route(s, logits):
        rows = rows_of(s)
        neg = jnp.float32(-1e30)
        big = jnp.float32(1e9)
        is_g = lane < N_GROUPS
        gl = jnp.where(is_g, logits, neg)
        gmax = jnp.max(gl, axis=-1, keepdims=True)
        gidx = jnp.min(jnp.where(gl == gmax, lanef, big), axis=-1, keepdims=True)
        gsum = jnp.sum(jnp.where(is_g, jnp.exp(gl - gmax), 0.0), axis=-1, keepdims=True)
        gprob = 1.0 / gsum
        lo = N_GROUPS + EXPERTS_PER_GROUP * gidx
        emask = (lanef >= lo) & (lanef < lo + EXPERTS_PER_GROUP)
        el = jnp.where(emask, logits, neg)
        e1 = jnp.max(el, axis=-1, keepdims=True)
        i1 = jnp.min(jnp.where(el == e1, lanef, big), axis=-1, keepdims=True)
        el2 = jnp.where(lanef == i1, neg, el)
        e2 = jnp.max(el2, axis=-1, keepdims=True)
        i2 = jnp.min(jnp.where(el2 == e2, lanef, big), axis=-1, keepdims=True)
        r = jnp.exp(e2 - e1)
        gate1 = gprob / (1.0 + r)
        gate2 = gprob * r / (1.0 + r)
        id1 = i1 - N_GROUPS
        id2 = i2 - N_GROUPS

        oh1 = jnp.where(lanef == id1, 1.0, 0.0).astype(F32)
        oh2 = jnp.where(lanef == id2, 1.0, 0.0).astype(F32)
        oh = oh1 + oh2
        before = _dot(tri, oh.astype(BF16)) + base_ref[...]
        rank1 = jnp.sum(oh1 * before, axis=-1, keepdims=True)
        rank2 = jnp.sum(oh2 * before, axis=-1, keepdims=True)
        base_ref[...] = base_ref[...] + jnp.sum(oh, axis=0, keepdims=True)

        ri = jnp.where(lane == 0, id1, jnp.where(lane == 1, id2,
                       jnp.where(lane == 2, rank1, jnp.where(lane == 3, rank2, 0.0))))
        ri_ref[rows, :] = ri.astype(I32)
        rg_ref[rows, :] = jnp.where(lane == 0, gate1, jnp.where(lane == 1, gate2, 0.0))

    ys = [project(s) for s in parts]
    hs = [normalise(s, y) for s, y in zip(parts, ys)]
    ls = [router_logits(h) for h in hs]
    for s, logits in zip(parts, ls):
        route(s, logits)
    cnt_ref[...] = base_ref[...]


def _outproj(ya, yb, x, w_out16, ffn_g, rw):
    N, D = x.shape
    tm = TM_OUT
    row = lambda width: pl.BlockSpec((tm, width), lambda i: (i, 0))
    full = lambda shape: pl.BlockSpec(shape, lambda i: (0,) * len(shape))
    return pl.pallas_call(
        _outproj_kernel,
        grid=(N // tm,),
        in_specs=[row(CONV_A_DIM), row(DN_DIM), row(D), full((D, D)), full((1, D)), full((D, 2 * LANES))],
        out_specs=(row(D), pl.BlockSpec((tm * ROW_SLAB, LANES), lambda i: (i, 0)), row(LANES), row(LANES),
                   full((1, LANES))),
        out_shape=(jax.ShapeDtypeStruct((N, D), F32),
                   jax.ShapeDtypeStruct((N * ROW_SLAB, LANES), F32),
                   jax.ShapeDtypeStruct((N, LANES), I32),
                   jax.ShapeDtypeStruct((N, LANES), F32),
                   jax.ShapeDtypeStruct((1, LANES), F32)),
        scratch_shapes=[pltpu.VMEM((1, LANES), F32)],
        compiler_params=pltpu.CompilerParams(
            dimension_semantics=("arbitrary",), vmem_limit_bytes=VMEM_LIMIT),
        name="outproj",
    )(ya, yb, x, w_out16, ffn_g, rw)


def _slots_kernel(ri_ref, start_ref, o_ref):
    rif = ri_ref[...].astype(F32)
    lane = lax.broadcasted_iota(I32, rif.shape, 1)
    lanef = lane.astype(F32)

    def slot(k):
        start = jnp.sum(jnp.where(lanef == rif[:, k:k + 1], start_ref[...], 0.0), axis=-1, keepdims=True)
        return start + rif[:, TOP_K + k:TOP_K + k + 1]

    slab = jnp.where(lane == 0, slot(0), jnp.where(lane == 1, slot(1), 0.0))
    o_ref[...] = slab.T[0:HALO, :].astype(I32)


def _slots(ri, seg_start):
    N = ri.shape[0]
    tm = TM_SLOTS
    out = pl.pallas_call(
        _slots_kernel,
        grid=(N // tm,),
        in_specs=[pl.BlockSpec((tm, LANES), lambda i: (i, 0)), pl.BlockSpec((1, LANES), lambda i: (0, 0))],
        out_specs=pl.BlockSpec((HALO, tm), lambda i: (0, i)),
        out_shape=jax.ShapeDtypeStruct((HALO, N), I32),
        compiler_params=pltpu.CompilerParams(dimension_semantics=("arbitrary",)),
        name="slots",
    )(ri, _lane_vec(seg_start, 0))
    return out[:TOP_K].reshape(-1)


def _dispatch_kernel(seg_ref, slot0_ref, slot1_ref, h_ref, xs_ref, zero_ref, sem, zsem):
    tm = TM_DISPATCH
    bm = FFN_BLOCK
    n_blocks = xs_ref.shape[0] // (bm * ROW_SLAB)

    @pl.when(pl.program_id(0) == 0)
    def _():
        zero_ref[...] = jnp.zeros(zero_ref.shape, F32)

        def pad_copy(d):
            return pltpu.make_async_copy(zero_ref.at[pl.ds(0, ROW_SLAB)], _slab(xs_ref, d), zsem)

        def block_copy(b):
            return pltpu.make_async_copy(zero_ref, _slab_block(xs_ref, b), zsem)

        def each_pad(fn):
            def per_expert(e, carry):
                def per_row(d, c):
                    fn(pad_copy(d))
                    return c
                return lax.fori_loop(seg_ref[e], seg_ref[N_EXPERTS + e], per_row, carry)
            lax.fori_loop(0, N_EXPERTS, per_expert, 0)

            def per_block(b, c):
                fn(block_copy(b))
                return c
            lax.fori_loop(seg_ref[2 * N_EXPERTS], n_blocks, per_block, 0)

        each_pad(lambda cp: cp.start())
        each_pad(lambda cp: cp.wait())

    def issue(t, carry):
        for k, slot_ref in enumerate((slot0_ref, slot1_ref)):
            pltpu.make_async_copy(_slab(h_ref, t), _slab(xs_ref, slot_ref[t]), sem).start(priority=k)
        return carry

    lax.fori_loop(0, tm, issue, 0, unroll=ISSUE_UNROLL)
    for _ in range(TOP_K):
        pltpu.make_async_copy(h_ref, xs_ref.at[pl.ds(0, tm * ROW_SLAB)], sem).wait()


def _dispatch(seg, slots, h2, n_slots):
    N = h2.shape[0] // ROW_SLAB
    tm = TM_DISPATCH
    grid_spec = pltpu.PrefetchScalarGridSpec(
        num_scalar_prefetch=1,
        grid=(N // tm,),
        in_specs=[pl.BlockSpec((tm,), lambda i, s: (i,), memory_space=pltpu.SMEM),
                  pl.BlockSpec((tm,), lambda i, s: (N // tm + i,), memory_space=pltpu.SMEM),
                  pl.BlockSpec((tm * ROW_SLAB, LANES), lambda i, s: (i, 0))],
        out_specs=pl.BlockSpec(memory_space=pl.ANY),
        scratch_shapes=[pltpu.VMEM((FFN_BLOCK * ROW_SLAB, LANES), F32),
                        pltpu.SemaphoreType.DMA(()), pltpu.SemaphoreType.DMA(())],
    )
    return pl.pallas_call(
        _dispatch_kernel,
        grid_spec=grid_spec,
        out_shape=jax.ShapeDtypeStruct((n_slots * ROW_SLAB, LANES), F32),
        compiler_params=pltpu.CompilerParams(dimension_semantics=("arbitrary",)),
        name="dispatch",
    )(seg, slots, slots, h2)


def _ffn_kernel(blk_ref, xs_ref, wg_ref, wu_ref, wd_ref, ys_ref,
                xbuf, ybuf, wg16, wu16, wd16, xsem, ysem, zsem):
    e = pl.program_id(0)
    bm = FFN_BLOCK
    ring = FFN_RING
    n_blocks = ys_ref.shape[0] // (bm * ROW_SLAB)
    first = blk_ref[e]
    count = blk_ref[N_EXPERTS + e]
    n_used = blk_ref[2 * N_EXPERTS]

    def x_copy(g):
        return pltpu.make_async_copy(_slab_block(xs_ref, g), xbuf.at[g % ring], xsem.at[g % ring])

    def y_copy(g):
        return pltpu.make_async_copy(ybuf.at[g % ring], _slab_block(ys_ref, g), ysem.at[g % ring])

    def when_block(g, fn):
        @pl.when((g >= 0) & (g < n_used))
        def _():
            fn(g)

    @pl.when(e == 0)
    def _():
        for g in range(ring - 1):
            when_block(g, lambda g: x_copy(g).start(priority=RING_PRIORITY))

    @pl.when(count > 0)
    def _():
        wg16[...] = wg_ref[0].astype(BF16)
        wu16[...] = wu_ref[0].astype(BF16)
        wd16[...] = wd_ref[0].astype(BF16)

    def block(j, carry):
        g = first + j
        x_copy(g).wait()
        when_block(g + ring - 1, lambda g: x_copy(g).start(priority=RING_PRIORITY))

        x = _load_rows(xbuf.at[g % ring], bm).astype(BF16)
        sub = bm // FFN_SPLIT
        x_parts = [x[s * sub:(s + 1) * sub] for s in range(FFN_SPLIT)]
        gates = [_dot(xp, wg16[...]) for xp in x_parts]
        ups = [_dot(xp, wu16[...]) for xp in x_parts]
        acts = [(_silu(a) * b).astype(BF16) for a, b in zip(gates, ups)]
        y = jnp.concatenate([_dot(act, wd16[...]) for act in acts], axis=0)

        when_block(g - ring, lambda g: y_copy(g).wait())
        _store_rows(ybuf.at[g % ring], y)
        y_copy(g).start(priority=RING_PRIORITY)
        return carry

    lax.fori_loop(0, count, block, 0)

    @pl.when(e == pl.num_programs(0) - 1)
    def _():
        for back in range(ring, 0, -1):
            when_block(n_used - back, lambda g: y_copy(g).wait())
        ybuf[0] = jnp.zeros(ybuf.shape[1:], F32)

        def zero_copy(g):
            return pltpu.make_async_copy(ybuf.at[0], _slab_block(ys_ref, g), zsem)

        def start(g, c):
            zero_copy(g).start()
            return c

        def wait(g, c):
            zero_copy(g).wait()
            return c

        lax.fori_loop(n_used, n_blocks, start, 0)
        lax.fori_loop(n_used, n_blocks, wait, 0)


def _ffn(blk, xs, w_gate, w_up, w_down):
    bm = FFN_BLOCK
    D = D_MODEL
    weights = lambda shape: pl.BlockSpec((1,) + shape, lambda e, blk: (e, 0, 0))
    grid_spec = pltpu.PrefetchScalarGridSpec(
        num_scalar_prefetch=1,
        grid=(N_EXPERTS,),
        in_specs=[pl.BlockSpec(memory_space=pl.ANY),
                  weights((D, EXPERT_FF)), weights((D, EXPERT_FF)), weights((EXPERT_FF, D))],
        out_specs=pl.BlockSpec(memory_space=pl.ANY),
        scratch_shapes=[pltpu.VMEM((FFN_RING, bm * ROW_SLAB, LANES), F32),
                        pltpu.VMEM((FFN_RING, bm * ROW_SLAB, LANES), F32),
                        pltpu.VMEM((D, EXPERT_FF), BF16), pltpu.VMEM((D, EXPERT_FF), BF16),
                        pltpu.VMEM((EXPERT_FF, D), BF16),
                        pltpu.SemaphoreType.DMA((FFN_RING,)), pltpu.SemaphoreType.DMA((FFN_RING,)),
                        pltpu.SemaphoreType.DMA(())],
    )
    return pl.pallas_call(
        _ffn_kernel,
        grid_spec=grid_spec,
        out_shape=jax.ShapeDtypeStruct(xs.shape, F32),
        compiler_params=pltpu.CompilerParams(
            dimension_semantics=("arbitrary",), vmem_limit_bytes=VMEM_LIMIT),
        name="ffn",
    )(blk, xs, w_gate, w_up, w_down)


def _combine_kernel(slot0_ref, slot1_ref, next0_ref, next1_ref, ys_ref, x1_ref, rg_ref, g_ref, o_ref,
                    buf_ref, sems):
    tm = TM_COMBINE
    i = pl.program_id(0)
    half = i % 2

    def gather(slot_refs, s):
        def issue(t, carry):
            for k, slot_ref in enumerate(slot_refs):
                pltpu.make_async_copy(_slab(ys_ref, slot_ref[t]), _slab(buf_ref.at[s, k], t),
                                      sems.at[s]).start(priority=k)
            return carry
        lax.fori_loop(0, tm, issue, 0, unroll=ISSUE_UNROLL)

    @pl.when(i == 0)
    def _():
        gather((slot0_ref, slot1_ref), 0)

    @pl.when(i + 1 < pl.num_programs(0))
    def _():
        gather((next0_ref, next1_ref), 1 - half)

    for k in range(TOP_K):
        pltpu.make_async_copy(ys_ref.at[pl.ds(0, tm * ROW_SLAB)], buf_ref.at[half, k], sems.at[half]).wait()

    rg = rg_ref[...]
    moe = (_load_rows(buf_ref.at[half, 0], tm) * rg[:, 0:1]
           + _load_rows(buf_ref.at[half, 1], tm) * rg[:, 1:2])
    x2 = x1_ref[...] + moe
    o_ref[...] = x2 * lax.rsqrt(jnp.mean(x2 * x2, axis=-1, keepdims=True) + EPS) * g_ref[...]


def _combine(slots, ys, x1, rg, final_g):
    N, D = x1.shape
    tm = TM_COMBINE
    n_tiles = N // tm

    def slot_block(k, ahead, i):
        return (k * n_tiles + jnp.minimum(i + ahead, n_tiles - 1),)

    return pl.pallas_call(
        _combine_kernel,
        grid=(n_tiles,),
        in_specs=[pl.BlockSpec((tm,), functools.partial(slot_block, k, ahead), memory_space=pltpu.SMEM)
                  for ahead in (0, 1) for k in range(TOP_K)]
                 + [pl.BlockSpec(memory_space=pl.ANY),
                  pl.BlockSpec((tm, D), lambda i: (i, 0)),
                  pl.BlockSpec((tm, LANES), lambda i: (i, 0)),
                  pl.BlockSpec((1, D), lambda i: (0, 0))],
        out_specs=pl.BlockSpec((tm, D), lambda i: (i, 0)),
        out_shape=jax.ShapeDtypeStruct((N, D), F32),
        scratch_shapes=[pltpu.VMEM((2, TOP_K, tm * ROW_SLAB, LANES), F32), pltpu.SemaphoreType.DMA((2,))],
        compiler_params=pltpu.CompilerParams(
            dimension_semantics=("arbitrary",), vmem_limit_bytes=VMEM_LIMIT),
        name="combine",
    )(slots, slots, slots, slots, ys, x1, rg, final_g)


def _lane_vec(values, offset):
    return jnp.zeros((1, LANES), F32).at[0, offset:offset + values.shape[0]].set(values.astype(F32))


def kernel(x, mix_norm_g, w_in, conv_a_w, conv_a_norm_g, dn_conv_w, dn_a_log, dn_dt_bias, dn_norm_g,
           w_out, ffn_norm_g, router_group_w, router_expert_w, w_gate, w_up, w_down, final_norm_g):
    B, S, D = x.shape
    N = B * S
    depth = w_in.shape[0]
    assert depth == 1, "single-layer block: the final RMSNorm is fused into the layer's combine step"
    group_of = jnp.arange(CONV_A_DIM, dtype=I32) // CONV_A_GROUP_DIM
    gmat = jnp.where(group_of[:, None] == group_of[None, :], 1.0 / CONV_A_GROUP_DIM, 0.0).astype(BF16)
    bm = FFN_BLOCK
    n_blocks = (N * TOP_K) // bm + N_EXPERTS
    for l in range(depth):
        w_in_pad = jnp.pad(w_in[l], ((0, 0), (0, IN_PROJ_PAD - IN_PROJ_DIM))).astype(BF16)
        ya, q, k, v, zg, gcol, grow = _inproj(
            x, mix_norm_g[l][None, :], w_in_pad, conv_a_w[l], conv_a_norm_g[l][None, :], dn_conv_w[l],
            _lane_vec(dn_a_log[l], DN_HEADS), _lane_vec(dn_dt_bias[l], DN_HEADS), gmat)
        u, w, qd, kd, at = _delta_prep(q, k, v, gcol, grow)
        yb = _delta_scan(u, w, qd, kd, at, zg, grow, dn_norm_g[l][None, :])
        rw = jnp.pad(jnp.concatenate([router_group_w[l], router_expert_w[l]], axis=1),
                     ((0, 0), (0, LANES - N_GROUPS - N_EXPERTS)))
        rw_hi = rw.astype(BF16)
        rw = jnp.concatenate([rw_hi, (rw - rw_hi.astype(F32)).astype(BF16)], axis=1)
        x1, h2, ri, rg, cnt = _outproj(ya.reshape(N, CONV_A_DIM), yb.reshape(N, DN_DIM), x.reshape(N, D),
                                       w_out[l].astype(BF16), ffn_norm_g[l][None, :], rw)
        counts = cnt[0, :N_EXPERTS].astype(I32)
        padded = (counts + bm - 1) // bm * bm
        seg_end = jnp.cumsum(padded).astype(I32)
        seg_start = seg_end - padded
        n_used = (seg_end[-1:] // bm)
        slots = _slots(ri, seg_start)
        seg = jnp.concatenate([seg_start + counts, seg_end, n_used])
        xs = _dispatch(seg, slots, h2, n_blocks * bm)
        blk = jnp.concatenate([seg_start // bm, padded // bm, n_used])
        ys = _ffn(blk, xs, w_gate[l], w_up[l], w_down[l])
        x = _combine(slots, ys, x1, rg, final_norm_g[None, :]).reshape(B, S, D)
    return x
```

```python
import functools

import jax
import jax.numpy as jnp
from jax import lax
from jax.experimental import pallas as pl
from jax.experimental.pallas import tpu as pltpu

F32 = jnp.float32
BF16 = jnp.bfloat16
I32 = jnp.int32

D_MODEL = 1024
CHUNK = 64
CONV_A_GROUP_DIM = 64
CONV_A_DIM = 512
CONV_A_WIDTH = 3
DN_HEADS = 4
DN_HEAD_DIM = 128
DN_DIM = 512
DN_CONV_WIDTH = 4
IN_PROJ_DIM = 3 * CONV_A_DIM + 4 * DN_DIM + 2 * DN_HEADS
N_GROUPS = 4
EXPERTS_PER_GROUP = 8
N_EXPERTS = 32
TOP_K = 2
EXPERT_FF = 512
EPS = 1e-6

LANES = 128
HALO = 8
GATE_COL = 3 * CONV_A_DIM + 4 * DN_DIM

TM_IN = 512
TM_PREP = 256
SCAN_BATCH = 4
TM_SCAN = 512
TM_OUT = 1024
OUT_SPLIT = 8
FFN_BLOCK = 256
TM_SLOTS = 2048
TM_DISPATCH = 2048
TM_COMBINE = 512
FFN_SPLIT = 2
FFN_RING = 4
RING_PRIORITY = 1
ISSUE_UNROLL = 4
VMEM_LIMIT = 56 * 1024 * 1024


def _dot(a, b):
    return jnp.dot(a, b, preferred_element_type=F32)


def _dot_nt(a, b):
    return lax.dot_general(a, b, (((1,), (1,)), ((), ())), preferred_element_type=F32)


def _dot_tn(a, b):
    return lax.dot_general(a, b, (((0,), (0,)), ((), ())), preferred_element_type=F32)


def _split_bf16(x, parts):
    out = []
    for _ in range(parts):
        p = x.astype(BF16)
        out.append(p)
        x = x - p.astype(F32)
    return out


def _silu(x):
    return x * jax.nn.sigmoid(x)


ROW_SLAB = D_MODEL // LANES


def _store_rows(ref, val):
    m = val.shape[0]
    for c in range(ROW_SLAB):
        ref[pl.ds(c, m, stride=ROW_SLAB), :] = val[:, c * LANES:(c + 1) * LANES]


def _load_rows(ref, m):
    return jnp.concatenate([ref[pl.ds(c, m, stride=ROW_SLAB), :] for c in range(ROW_SLAB)], axis=-1)


def _slab(ref, row):
    return ref.at[pl.ds(pl.multiple_of(row * ROW_SLAB, ROW_SLAB), ROW_SLAB)]


def _slab_block(ref, block):
    n = FFN_BLOCK * ROW_SLAB
    return ref.at[pl.ds(pl.multiple_of(block * n, n), n)]


def _causal_conv(ext, w):
    taps = w.shape[0]
    acc = None
    for j in range(taps):
        shift = taps - 1 - j
        rows = pltpu.roll(ext, shift, 0)[HALO:] if shift else ext[HALO:]
        term = w[j:j + 1, :] * rows
        acc = term if acc is None else acc + term
    return acc


def _softplus(x):
    return jnp.maximum(x, 0.0) + jnp.log1p(jnp.exp(-jnp.abs(x)))


def _inproj_kernel(x_ref, g_ref, w_ref, caw_ref, cag_ref, dcw_ref, alog_ref, dtb_ref, gmat_ref,
                   ya_ref, q_ref, k_ref, v_ref, zg_ref, gcol_ref, grow_ref,
                   exta_ref, *extq_refs):
    tm = x_ref.shape[1]
    ext_refs = (exta_ref,) + extq_refs

    @pl.when(pl.program_id(1) == 0)
    def _():
        for ext_ref in ext_refs:
            ext_ref[0:HALO, :] = jnp.zeros((HALO, ext_ref.shape[1]), F32)

    x = x_ref[0]
    ms = jnp.mean(x * x, axis=-1, keepdims=True)
    hb = (x * lax.rsqrt(ms + EPS) * g_ref[...]).astype(BF16)

    def proj(c0, width):
        return _dot(hb, w_ref[:, c0:c0 + width])

    base = 3 * CONV_A_DIM

    def qkv_project(i):
        extq_refs[i][HALO:HALO + tm, :] = proj(base + i * DN_DIM, DN_DIM)

    def qkv_finish(i, out_ref):
        cols = slice(i * DN_DIM, (i + 1) * DN_DIM)
        s = _silu(_causal_conv(extq_refs[i][...], dcw_ref[:, cols]))
        if i == 2:
            out_ref[0] = s
        else:
            for h in range(DN_HEADS):
                sh = s[:, h * DN_HEAD_DIM:(h + 1) * DN_HEAD_DIM]
                inv = lax.rsqrt(jnp.sum(sh * sh, axis=-1, keepdims=True) + EPS)
                sh = sh * inv
                if i == 0:
                    sh = sh * (DN_HEAD_DIM ** -0.5)
                out_ref[0, :, h * DN_HEAD_DIM:(h + 1) * DN_HEAD_DIM] = sh

    def mixer_a_finish(a_b):
        y = a_b * _causal_conv(exta_ref[...], caw_ref[...])
        ysq = y * y
        hi = ysq.astype(BF16)
        lo = (ysq - hi.astype(F32)).astype(BF16)
        gmean = _dot(hi, gmat_ref[...]) + _dot(lo, gmat_ref[...])
        ya_ref[0] = (y * lax.rsqrt(gmean + EPS) * cag_ref[...]).astype(ya_ref.dtype)

    qkv_project(0)
    qkv_project(1)
    exta_ref[HALO:HALO + tm, :] = proj(2 * CONV_A_DIM, CONV_A_DIM) * proj(0, CONV_A_DIM)
    qkv_finish(0, q_ref)
    qkv_project(2)
    a_b = proj(CONV_A_DIM, CONV_A_DIM)
    qkv_finish(1, k_ref)
    z = proj(base + 3 * DN_DIM, DN_DIM)
    mixer_a_finish(a_b)
    n_gate = IN_PROJ_DIM - GATE_COL
    p = jnp.concatenate([proj(GATE_COL, n_gate), jnp.zeros((tm, LANES - n_gate), F32)], axis=-1)
    qkv_finish(2, v_ref)
    zg_ref[0] = _silu(z)
    for ext_ref in ext_refs:
        ext_ref[0:HALO, :] = ext_ref[tm:tm + HALO, :]

    beta = jax.nn.sigmoid(p)
    g = -jnp.exp(alog_ref[...]) * _softplus(p + dtb_ref[...])
    r = lax.broadcasted_iota(I32, (CHUNK, CHUNK), 0)
    c = lax.broadcasted_iota(I32, (CHUNK, CHUNK), 1)
    tri = jnp.where(c <= r, 1.0, 0.0).astype(BF16)
    parts = _split_bf16(g, 3)
    gc = jnp.concatenate(
        [sum(_dot(tri, p[ci * CHUNK:(ci + 1) * CHUNK]) for p in parts) for ci in range(tm // CHUNK)], axis=0)
    lane = lax.broadcasted_iota(I32, (tm, LANES), 1)
    slab = jnp.where(lane < DN_HEADS, beta, gc)
    gcol_ref[0] = slab
    rows = slab.T[0:HALO, :]
    for ci in range(tm // CHUNK):
        grow_ref[0, ci] = rows[:, ci * CHUNK:(ci + 1) * CHUNK]


def _inproj(x, mix_g, w_in16, conv_a_w, conv_a_g, dn_conv_w, alog_vec, dtb_vec, gmat):
    B, S, D = x.shape
    tm = TM_IN
    full = lambda shape: pl.BlockSpec(shape, lambda b, t: (0,) * len(shape))
    row = lambda width: pl.BlockSpec((1, tm, width), lambda b, t: (b, t, 0))
    out_shape = (
        jax.ShapeDtypeStruct((B, S, CONV_A_DIM), BF16),
        jax.ShapeDtypeStruct((B, S, DN_DIM), F32),
        jax.ShapeDtypeStruct((B, S, DN_DIM), F32),
        jax.ShapeDtypeStruct((B, S, DN_DIM), F32),
        jax.ShapeDtypeStruct((B, S, DN_DIM), F32),
        jax.ShapeDtypeStruct((B, S, LANES), F32),
        jax.ShapeDtypeStruct((B, S // CHUNK, HALO, CHUNK), F32),
    )
    return pl.pallas_call(
        _inproj_kernel,
        grid=(B, S // tm),
        in_specs=[row(D), full((1, D)), full((D, IN_PROJ_DIM)), full((CONV_A_WIDTH, CONV_A_DIM)),
                  full((1, CONV_A_DIM)), full((DN_CONV_WIDTH, 3 * DN_DIM)), full((1, LANES)),
                  full((1, LANES)), full((CONV_A_DIM, CONV_A_DIM))],
        out_specs=(row(CONV_A_DIM), row(DN_DIM), row(DN_DIM), row(DN_DIM), row(DN_DIM), row(LANES),
                   pl.BlockSpec((1, tm // CHUNK, HALO, CHUNK), lambda b, t: (b, t, 0, 0))),
        out_shape=out_shape,
        scratch_shapes=[pltpu.VMEM((tm + HALO, CONV_A_DIM), F32)] + [pltpu.VMEM((tm + HALO, DN_DIM), F32)] * 3,
        compiler_params=pltpu.CompilerParams(
            dimension_semantics=("arbitrary", "arbitrary"), vmem_limit_bytes=VMEM_LIMIT),
        name="inproj",
    )(x, mix_g, w_in16, conv_a_w, conv_a_g, dn_conv_w, alog_vec, dtb_vec, gmat)


def _delta_prep_kernel(q_ref, k_ref, v_ref, gcol_ref, grow_ref,
                       u_ref, w_ref, qd_ref, kd_ref, at_ref):
    tm = q_ref.shape[1]
    ri = lax.broadcasted_iota(I32, (CHUNK, CHUNK), 0)
    ci = lax.broadcasted_iota(I32, (CHUNK, CHUNK), 1)
    causal = ci <= ri
    strict = ci < ri
    eye = jnp.where(ci == ri, 1.0, 0.0).astype(F32)
    lane = lax.broadcasted_iota(I32, (CHUNK, LANES), 1)
    left_half = lane < CHUNK

    def setup(c, h):
        rows = slice(c * CHUNK, (c + 1) * CHUNK)
        cols = slice(h * DN_HEAD_DIM, (h + 1) * DN_HEAD_DIM)
        slab = gcol_ref[0, rows, :]
        beta = jnp.sum(jnp.where(lane == h, slab, 0.0), axis=-1, keepdims=True)
        gcc = jnp.sum(jnp.where(lane == h + DN_HEADS, slab, 0.0), axis=-1, keepdims=True)
        gcr = grow_ref[0, c, h + DN_HEADS:h + DN_HEADS + 1, :]
        diff = gcc - gcr
        decay = jnp.where(causal, jnp.exp(jnp.where(causal, diff, 0.0)), 0.0)
        q = q_ref[0, rows, cols]
        k = k_ref[0, rows, cols]
        v = v_ref[0, rows, cols]
        kb = k * beta
        kq = _dot_nt(jnp.concatenate([kb.astype(BF16), q.astype(BF16)], axis=0), k.astype(BF16))
        L = jnp.where(strict, kq[:CHUNK] * decay, 0.0)
        intra = kq[CHUNK:] * decay
        egc = jnp.exp(gcc)
        gl = gcr[:, CHUNK - 1:CHUNK]
        qd_ref[0, rows, cols] = (q * egc).astype(BF16)
        kd_ref[0, rows, cols] = (k * jnp.exp(gl - gcc)).astype(BF16)
        at_ref[0, rows, h * CHUNK:(h + 1) * CHUNK] = intra.astype(BF16)
        return -L, jnp.concatenate([v * beta, kb * egc], axis=-1).astype(BF16)

    def first_stage(m):
        m16 = m.astype(BF16)
        return jnp.concatenate([eye + m, _dot(m16, m16)], axis=-1)

    def stage(r):
        return _dot(r[:, CHUNK:].astype(BF16), r.astype(BF16)) + jnp.where(left_half, r, 0.0)

    def last_stage(r):
        return r[:, :CHUNK] + _dot(r[:, CHUNK:].astype(BF16), r[:, :CHUNK].astype(BF16))

    def solve(c, h, t, rhs):
        rows = slice(c * CHUNK, (c + 1) * CHUNK)
        cols = slice(h * DN_HEAD_DIM, (h + 1) * DN_HEAD_DIM)
        uw = _dot(t.astype(BF16), rhs)
        u_ref[0, rows, cols] = uw[:, :DN_HEAD_DIM]
        w_ref[0, rows, cols] = uw[:, DN_HEAD_DIM:].astype(BF16)

    chains = [(c, h) for c in range(tm // CHUNK) for h in range(DN_HEADS)]
    ms, rhss = zip(*[setup(c, h) for c, h in chains])
    rs = [first_stage(m) for m in ms]
    for _ in range(4):
        rs = [stage(r) for r in rs]
    ts = [last_stage(r) for r in rs]
    for (c, h), t, rhs in zip(chains, ts, rhss):
        solve(c, h, t, rhs)


def _delta_prep(q, k, v, gcol, grow):
    B, S, _ = q.shape
    tm = TM_PREP
    row = lambda width: pl.BlockSpec((1, tm, width), lambda b, t: (b, t, 0))
    return pl.pallas_call(
        _delta_prep_kernel,
        grid=(B, S // tm),
        in_specs=[row(DN_DIM), row(DN_DIM), row(DN_DIM), row(LANES),
                  pl.BlockSpec((1, tm // CHUNK, HALO, CHUNK), lambda b, t: (b, t, 0, 0))],
        out_specs=(row(DN_DIM), row(DN_DIM), row(DN_DIM), row(DN_DIM), row(DN_HEADS * CHUNK)),
        out_shape=(jax.ShapeDtypeStruct((B, S, DN_DIM), F32),
                   jax.ShapeDtypeStruct((B, S, DN_DIM), BF16),
                   jax.ShapeDtypeStruct((B, S, DN_DIM), BF16),
                   jax.ShapeDtypeStruct((B, S, DN_DIM), BF16),
                   jax.ShapeDtypeStruct((B, S, DN_HEADS * CHUNK), BF16)),
        compiler_params=pltpu.CompilerParams(
            dimension_semantics=("arbitrary", "arbitrary"), vmem_limit_bytes=VMEM_LIMIT),
        name="delta_prep",
    )(q, k, v, gcol, grow)


def _delta_scan_kernel(u_ref, w_ref, qd_ref, kd_ref, at_ref, zg_ref, grow_ref, ng_ref, o_ref, st_ref):
    nb, tm = u_ref.shape[0], u_ref.shape[1]

    @pl.when(pl.program_id(1) == 0)
    def _():
        st_ref[...] = jnp.zeros(st_ref.shape, F32)

    def step(c, carry):
        r0 = pl.multiple_of(c * CHUNK, CHUNK)
        rows = pl.ds(r0, CHUNK)
        chains = [(b, h) for b in range(nb) for h in range(DN_HEADS)]
        cols = lambda h: slice(h * DN_HEAD_DIM, (h + 1) * DN_HEAD_DIM)
        sts = [st_ref[b, h] for b, h in chains]
        st16s = [st.astype(BF16) for st in sts]
        wqs = [_dot(jnp.concatenate([w_ref[b, rows, cols(h)], qd_ref[b, rows, cols(h)]], axis=0), st16)
               for (b, h), st16 in zip(chains, st16s)]
        vn16s = [(u_ref[b, rows, cols(h)] - wq[:CHUNK]).astype(BF16) for (b, h), wq in zip(chains, wqs)]
        upd = [_dot_tn(kd_ref[b, rows, cols(h)], vn16) for (b, h), vn16 in zip(chains, vn16s)]
        avs = [_dot(at_ref[b, rows, h * CHUNK:(h + 1) * CHUNK], vn16) for (b, h), vn16 in zip(chains, vn16s)]
        for (b, h), st, up in zip(chains, sts, upd):
            gl = grow_ref[b, c, h + DN_HEADS:h + DN_HEADS + 1, CHUNK - 1:CHUNK]
            st_ref[b, h] = st * jnp.exp(gl) + up
        for (b, h), wq, av in zip(chains, wqs, avs):
            o = wq[CHUNK:] + av
            on = o * lax.rsqrt(jnp.mean(o * o, axis=-1, keepdims=True) + EPS) * ng_ref[...]
            o_ref[b, rows, cols(h)] = (on * zg_ref[b, rows, cols(h)]).astype(o_ref.dtype)
        return carry

    lax.fori_loop(0, tm // CHUNK, step, 0)


def _delta_scan(u, w, qd, kd, at, zg, grow, norm_g):
    B, S, _ = u.shape
    tm = TM_SCAN
    nb = SCAN_BATCH
    row = lambda width: pl.BlockSpec((nb, tm, width), lambda b, t: (b, t, 0))
    return pl.pallas_call(
        _delta_scan_kernel,
        grid=(B // nb, S // tm),
        in_specs=[row(DN_DIM), row(DN_DIM), row(DN_DIM), row(DN_DIM), row(DN_HEADS * CHUNK), row(DN_DIM),
                  pl.BlockSpec((nb, tm // CHUNK, HALO, CHUNK), lambda b, t: (b, t, 0, 0)),
                  pl.BlockSpec((1, DN_HEAD_DIM), lambda b, t: (0, 0))],
        out_specs=row(DN_DIM),
        out_shape=jax.ShapeDtypeStruct((B, S, DN_DIM), BF16),
        scratch_shapes=[pltpu.VMEM((nb, DN_HEADS, DN_HEAD_DIM, DN_HEAD_DIM), F32)],
        compiler_params=pltpu.CompilerParams(
            dimension_semantics=("arbitrary", "arbitrary"), vmem_limit_bytes=VMEM_LIMIT),
        name="delta_scan",
    )(u, w, qd, kd, at, zg, grow, norm_g)


def _outproj_kernel(ya_ref, yb_ref, x_ref, wo_ref, g_ref, rw_ref,
                    x1_ref, h2_ref, ri_ref, rg_ref, cnt_ref, base_ref):
    tm = x_ref.shape[0]
    sub = tm // OUT_SPLIT
    parts = range(OUT_SPLIT)
    rows_of = lambda s: slice(s * sub, (s + 1) * sub)

    @pl.when(pl.program_id(0) == 0)
    def _():
        base_ref[...] = jnp.zeros(base_ref.shape, F32)

    def project(s):
        rows = rows_of(s)
        return _dot(jnp.concatenate([ya_ref[rows, :], yb_ref[rows, :]], axis=-1), wo_ref[...])

    def normalise(s, y):
        rows = rows_of(s)
        x1 = x_ref[rows, :] + y
        x1_ref[rows, :] = x1
        h = x1 * lax.rsqrt(jnp.mean(x1 * x1, axis=-1, keepdims=True) + EPS) * g_ref[...]
        _store_rows(h2_ref.at[pl.ds(s * sub * ROW_SLAB, sub * ROW_SLAB)], h)
        return h

    def router_logits(h):
        h_hi, h_lo = _split_bf16(h, 2)
        hi_prod = _dot(h_hi, rw_ref[...])
        return hi_prod[:, :LANES] + (hi_prod[:, LANES:] + _dot(h_lo, rw_ref[:, :LANES]))

    lane = lax.broadcasted_iota(I32, (sub, LANES), 1)
    lanef = lane.astype(F32)
    rr = lax.broadcasted_iota(I32, (sub, sub), 0)
    cc = lax.broadcasted_iota(I32, (sub, sub), 1)
    tri = jnp.where(cc < rr, 1.0, 0.0).astype(BF16)

    def route(s, logits):
        rows = rows_of(s)
        neg = jnp.float32(-1e30)
        big = jnp.float32(1e9)
        is_g = lane < N_GROUPS
        gl = jnp.where(is_g, logits, neg)
        gmax = jnp.max(gl, axis=-1, keepdims=True)
        gidx = jnp.min(jnp.where(gl == gmax, lanef, big), axis=-1, keepdims=True)
        gsum = jnp.sum(jnp.where(is_g, jnp.exp(gl - gmax), 0.0), axis=-1, keepdims=True)
        gprob = 1.0 / gsum
        lo = N_GROUPS + EXPERTS_PER_GROUP * gidx
        emask = (lanef >= lo) & (lanef < lo + EXPERTS_PER_GROUP)
        el = jnp.where(emask, logits, neg)
        e1 = jnp.max(el, axis=-1, keepdims=True)
        i1 = jnp.min(jnp.where(el == e1, lanef, big), axis=-1, keepdims=True)
        el2 = jnp.where(lanef == i1, neg, el)
        e2 = jnp.max(el2, axis=-1, keepdims=True)
        i2 = jnp.min(jnp.where(el2 == e2, lanef, big), axis=-1, keepdims=True)
        r = jnp.exp(e2 - e1)
        gate1 = gprob / (1.0 + r)
        gate2 = gprob * r / (1.0 + r)
        id1 = i1 - N_GROUPS
        id2 = i2 - N_GROUPS

        oh1 = jnp.where(lanef == id1, 1.0, 0.0).astype(F32)
        oh2 = jnp.where(lanef == id2, 1.0, 0.0).astype(F32)
        oh = oh1 + oh2
        before = _dot(tri, oh.astype(BF16)) + base_ref[...]
        rank1 = jnp.sum(oh1 * before, axis=-1, keepdims=True)
        rank2 = jnp.sum(oh2 * before, axis=-1, keepdims=True)
        base_ref[...] = base_ref[...] + jnp.sum(oh, axis=0, keepdims=True)

        ri = jnp.where(lane == 0, id1, jnp.where(lane == 1, id2,
                       jnp.where(lane == 2, rank1, jnp.where(lane == 3, rank2, 0.0))))
        ri_ref[rows, :] = ri.astype(I32)
        rg_ref[rows, :] = jnp.where(lane == 0, gate1, jnp.where(lane == 1, gate2, 0.0))

    ys = [project(s) for s in parts]
    hs = [normalise(s, y) for s, y in zip(parts, ys)]
    ls = [router_logits(h) for h in hs]
    for s, logits in zip(parts, ls):
        route(s, logits)
    cnt_ref[...] = base_ref[...]


def _outproj(ya, yb, x, w_out16, ffn_g, rw):
    N, D = x.shape
    tm = TM_OUT
    row = lambda width: pl.BlockSpec((tm, width), lambda i: (i, 0))
    full = lambda shape: pl.BlockSpec(shape, lambda i: (0,) * len(shape))
    return pl.pallas_call(
        _outproj_kernel,
        grid=(N // tm,),
        in_specs=[row(CONV_A_DIM), row(DN_DIM), row(D), full((D, D)), full((1, D)), full((D, 2 * LANES))],
        out_specs=(row(D), pl.BlockSpec((tm * ROW_SLAB, LANES), lambda i: (i, 0)), row(LANES), row(LANES),
                   full((1, LANES))),
        out_shape=(jax.ShapeDtypeStruct((N, D), F32),
                   jax.ShapeDtypeStruct((N * ROW_SLAB, LANES), F32),
                   jax.ShapeDtypeStruct((N, LANES), I32),
                   jax.ShapeDtypeStruct((N, LANES), F32),
                   jax.ShapeDtypeStruct((1, LANES), F32)),
        scratch_shapes=[pltpu.VMEM((1, LANES), F32)],
        compiler_params=pltpu.CompilerParams(
            dimension_semantics=("arbitrary",), vmem_limit_bytes=VMEM_LIMIT),
        name="outproj",
    )(ya, yb, x, w_out16, ffn_g, rw)


def _slots_kernel(ri_ref, start_ref, o_ref):
    rif = ri_ref[...].astype(F32)
    lane = lax.broadcasted_iota(I32, rif.shape, 1)
    lanef = lane.astype(F32)

    def slot(k):
        start = jnp.sum(jnp.where(lanef == rif[:, k:k + 1], start_ref[...], 0.0), axis=-1, keepdims=True)
        return start + rif[:, TOP_K + k:TOP_K + k + 1]

    slab = jnp.where(lane == 0, slot(0), jnp.where(lane == 1, slot(1), 0.0))
    o_ref[...] = slab.T[0:HALO, :].astype(I32)


def _slots(ri, seg_start):
    N = ri.shape[0]
    tm = TM_SLOTS
    out = pl.pallas_call(
        _slots_kernel,
        grid=(N // tm,),
        in_specs=[pl.BlockSpec((tm, LANES), lambda i: (i, 0)), pl.BlockSpec((1, LANES), lambda i: (0, 0))],
        out_specs=pl.BlockSpec((HALO, tm), lambda i: (0, i)),
        out_shape=jax.ShapeDtypeStruct((HALO, N), I32),
        compiler_params=pltpu.CompilerParams(dimension_semantics=("arbitrary",)),
        name="slots",
    )(ri, _lane_vec(seg_start, 0))
    return out[:TOP_K].reshape(-1)


def _dispatch_kernel(seg_ref, slot0_ref, slot1_ref, h_ref, xs_ref, zero_ref, sem, zsem):
    tm = TM_DISPATCH
    bm = FFN_BLOCK
    n_blocks = xs_ref.shape[0] // (bm * ROW_SLAB)

    @pl.when(pl.program_id(0) == 0)
    def _():
        zero_ref[...] = jnp.zeros(zero_ref.shape, F32)

        def pad_copy(d):
            return pltpu.make_async_copy(zero_ref.at[pl.ds(0, ROW_SLAB)], _slab(xs_ref, d), zsem)

        def block_copy(b):
            return pltpu.make_async_copy(zero_ref, _slab_block(xs_ref, b), zsem)

        def each_pad(fn):
            def per_expert(e, carry):
                def per_row(d, c):
                    fn(pad_copy(d))
                    return c
                return lax.fori_loop(seg_ref[e], seg_ref[N_EXPERTS + e], per_row, carry)
            lax.fori_loop(0, N_EXPERTS, per_expert, 0)

            def per_block(b, c):
                fn(block_copy(b))
                return c
            lax.fori_loop(seg_ref[2 * N_EXPERTS], n_blocks, per_block, 0)

        each_pad(lambda cp: cp.start())
        each_pad(lambda cp: cp.wait())

    def issue(t, carry):
        for k, slot_ref in enumerate((slot0_ref, slot1_ref)):
            pltpu.make_async_copy(_slab(h_ref, t), _slab(xs_ref, slot_ref[t]), sem).start(priority=k)
        return carry

    lax.fori_loop(0, tm, issue, 0, unroll=ISSUE_UNROLL)
    for _ in range(TOP_K):
        pltpu.make_async_copy(h_ref, xs_ref.at[pl.ds(0, tm * ROW_SLAB)], sem).wait()


def _dispatch(seg, slots, h2, n_slots):
    N = h2.shape[0] // ROW_SLAB
    tm = TM_DISPATCH
    grid_spec = pltpu.PrefetchScalarGridSpec(
        num_scalar_prefetch=1,
        grid=(N // tm,),
        in_specs=[pl.BlockSpec((tm,), lambda i, s: (i,), memory_space=pltpu.SMEM),
                  pl.BlockSpec((tm,), lambda i, s: (N // tm + i,), memory_space=pltpu.SMEM),
                  pl.BlockSpec((tm * ROW_SLAB, LANES), lambda i, s: (i, 0))],
        out_specs=pl.BlockSpec(memory_space=pl.ANY),
        scratch_shapes=[pltpu.VMEM((FFN_BLOCK * ROW_SLAB, LANES), F32),
                        pltpu.SemaphoreType.DMA(()), pltpu.SemaphoreType.DMA(())],
    )
    return pl.pallas_call(
        _dispatch_kernel,
        grid_spec=grid_spec,
        out_shape=jax.ShapeDtypeStruct((n_slots * ROW_SLAB, LANES), F32),
        compiler_params=pltpu.CompilerParams(dimension_semantics=("arbitrary",)),
        name="dispatch",
    )(seg, slots, slots, h2)


def _ffn_kernel(blk_ref, xs_ref, wg_ref, wu_ref, wd_ref, ys_ref,
                xbuf, ybuf, wg16, wu16, wd16, xsem, ysem, zsem):
    e = pl.program_id(0)
    bm = FFN_BLOCK
    ring = FFN_RING
    n_blocks = ys_ref.shape[0] // (bm * ROW_SLAB)
    first = blk_ref[e]
    count = blk_ref[N_EXPERTS + e]
    n_used = blk_ref[2 * N_EXPERTS]

    def x_copy(g):
        return pltpu.make_async_copy(_slab_block(xs_ref, g), xbuf.at[g % ring], xsem.at[g % ring])

    def y_copy(g):
        return pltpu.make_async_copy(ybuf.at[g % ring], _slab_block(ys_ref, g), ysem.at[g % ring])

    def when_block(g, fn):
        @pl.when((g >= 0) & (g < n_used))
        def _():
            fn(g)

    @pl.when(e == 0)
    def _():
        for g in range(ring - 1):
            when_block(g, lambda g: x_copy(g).start(priority=RING_PRIORITY))

    @pl.when(count > 0)
    def _():
        wg16[...] = wg_ref[0].astype(BF16)
        wu16[...] = wu_ref[0].astype(BF16)
        wd16[...] = wd_ref[0].astype(BF16)

    def block(j, carry):
        g = first + j
        x_copy(g).wait()
        when_block(g + ring - 1, lambda g: x_copy(g).start(priority=RING_PRIORITY))

        x = _load_rows(xbuf.at[g % ring], bm).astype(BF16)
        sub = bm // FFN_SPLIT
        x_parts = [x[s * sub:(s + 1) * sub] for s in range(FFN_SPLIT)]
        gates = [_dot(xp, wg16[...]) for xp in x_parts]
        ups = [_dot(xp, wu16[...]) for xp in x_parts]
        acts = [(_silu(a) * b).astype(BF16) for a, b in zip(gates, ups)]
        y = jnp.concatenate([_dot(act, wd16[...]) for act in acts], axis=0)

        when_block(g - ring, lambda g: y_copy(g).wait())
        _store_rows(ybuf.at[g % ring], y)
        y_copy(g).start(priority=RING_PRIORITY)
        return carry

    lax.fori_loop(0, count, block, 0)

    @pl.when(e == pl.num_programs(0) - 1)
    def _():
        for back in range(ring, 0, -1):
            when_block(n_used - back, lambda g: y_copy(g).wait())
        ybuf[0] = jnp.zeros(ybuf.shape[1:], F32)

        def zero_copy(g):
            return pltpu.make_async_copy(ybuf.at[0], _slab_block(ys_ref, g), zsem)

        def start(g, c):
            zero_copy(g).start()
            return c

        def wait(g, c):
            zero_copy(g).wait()
            return c

        lax.fori_loop(n_used, n_blocks, start, 0)
        lax.fori_loop(n_used, n_blocks, wait, 0)


def _ffn(blk, xs, w_gate, w_up, w_down):
    bm = FFN_BLOCK
    D = D_MODEL
    weights = lambda shape: pl.BlockSpec((1,) + shape, lambda e, blk: (e, 0, 0))
    grid_spec = pltpu.PrefetchScalarGridSpec(
        num_scalar_prefetch=1,
        grid=(N_EXPERTS,),
        in_specs=[pl.BlockSpec(memory_space=pl.ANY),
                  weights((D, EXPERT_FF)), weights((D, EXPERT_FF)), weights((EXPERT_FF, D))],
        out_specs=pl.BlockSpec(memory_space=pl.ANY),
        scratch_shapes=[pltpu.VMEM((FFN_RING, bm * ROW_SLAB, LANES), F32),
                        pltpu.VMEM((FFN_RING, bm * ROW_SLAB, LANES), F32),
                        pltpu.VMEM((D, EXPERT_FF), BF16), pltpu.VMEM((D, EXPERT_FF), BF16),
                        pltpu.VMEM((EXPERT_FF, D), BF16),
                        pltpu.SemaphoreType.DMA((FFN_RING,)), pltpu.SemaphoreType.DMA((FFN_RING,)),
                        pltpu.SemaphoreType.DMA(())],
    )
    return pl.pallas_call(
        _ffn_kernel,
        grid_spec=grid_spec,
        out_shape=jax.ShapeDtypeStruct(xs.shape, F32),
        compiler_params=pltpu.CompilerParams(
            dimension_semantics=("arbitrary",), vmem_limit_bytes=VMEM_LIMIT),
        name="ffn",
    )(blk, xs, w_gate, w_up, w_down)


def _combine_kernel(slot0_ref, slot1_ref, next0_ref, next1_ref, ys_ref, x1_ref, rg_ref, g_ref, o_ref,
                    buf_ref, sems):
    tm = TM_COMBINE
    i = pl.program_id(0)
    half = i % 2

    def gather(slot_refs, s):
        def issue(t, carry):
            for k, slot_ref in enumerate(slot_refs):
                pltpu.make_async_copy(_slab(ys_ref, slot_ref[t]), _slab(buf_ref.at[s, k], t),
                                      sems.at[s]).start(priority=k)
            return carry
        lax.fori_loop(0, tm, issue, 0, unroll=ISSUE_UNROLL)

    @pl.when(i == 0)
    def _():
        gather((slot0_ref, slot1_ref), 0)

    @pl.when(i + 1 < pl.num_programs(0))
    def _():
        gather((next0_ref, next1_ref), 1 - half)

    for k in range(TOP_K):
        pltpu.make_async_copy(ys_ref.at[pl.ds(0, tm * ROW_SLAB)], buf_ref.at[half, k], sems.at[half]).wait()

    rg = rg_ref[...]
    moe = (_load_rows(buf_ref.at[half, 0], tm) * rg[:, 0:1]
           + _load_rows(buf_ref.at[half, 1], tm) * rg[:, 1:2])
    x2 = x1_ref[...] + moe
    o_ref[...] = x2 * lax.rsqrt(jnp.mean(x2 * x2, axis=-1, keepdims=True) + EPS) * g_ref[...]


def _combine(slots, ys, x1, rg, final_g):
    N, D = x1.shape
    tm = TM_COMBINE
    n_tiles = N // tm

    def slot_block(k, ahead, i):
        return (k * n_tiles + jnp.minimum(i + ahead, n_tiles - 1),)

    return pl.pallas_call(
        _combine_kernel,
        grid=(n_tiles,),
        in_specs=[pl.BlockSpec((tm,), functools.partial(slot_block, k, ahead), memory_space=pltpu.SMEM)
                  for ahead in (0, 1) for k in range(TOP_K)]
                 + [pl.BlockSpec(memory_space=pl.ANY),
                  pl.BlockSpec((tm, D), lambda i: (i, 0)),
                  pl.BlockSpec((tm, LANES), lambda i: (i, 0)),
                  pl.BlockSpec((1, D), lambda i: (0, 0))],
        out_specs=pl.BlockSpec((tm, D), lambda i: (i, 0)),
        out_shape=jax.ShapeDtypeStruct((N, D), F32),
        scratch_shapes=[pltpu.VMEM((2, TOP_K, tm * ROW_SLAB, LANES), F32), pltpu.SemaphoreType.DMA((2,))],
        compiler_params=pltpu.CompilerParams(
            dimension_semantics=("arbitrary",), vmem_limit_bytes=VMEM_LIMIT),
        name="combine",
    )(slots, slots, slots, slots, ys, x1, rg, final_g)


def _lane_vec(values, offset):
    return jnp.zeros((1, LANES), F32).at[0, offset:offset + values.shape[0]].set(values.astype(F32))


def kernel(x, mix_norm_g, w_in, conv_a_w, conv_a_norm_g, dn_conv_w, dn_a_log, dn_dt_bias, dn_norm_g,
           w_out, ffn_norm_g, router_group_w, router_expert_w, w_gate, w_up, w_down, final_norm_g):
    B, S, D = x.shape
    N = B * S
    depth = w_in.shape[0]
    assert depth == 1, "single-layer block: the final RMSNorm is fused into the layer's combine step"
    group_of = jnp.arange(CONV_A_DIM, dtype=I32) // CONV_A_GROUP_DIM
    gmat = jnp.where(group_of[:, None] == group_of[None, :], 1.0 / CONV_A_GROUP_DIM, 0.0).astype(BF16)
    bm = FFN_BLOCK
    n_blocks = (N * TOP_K) // bm + N_EXPERTS
    for l in range(depth):
        ya, q, k, v, zg, gcol, grow = _inproj(
            x, mix_norm_g[l][None, :], w_in[l].astype(BF16), conv_a_w[l], conv_a_norm_g[l][None, :], dn_conv_w[l],
            _lane_vec(dn_a_log[l], DN_HEADS), _lane_vec(dn_dt_bias[l], DN_HEADS), gmat)
        u, w, qd, kd, at = _delta_prep(q, k, v, gcol, grow)
        yb = _delta_scan(u, w, qd, kd, at, zg, grow, dn_norm_g[l][None, :])
        rw = jnp.pad(jnp.concatenate([router_group_w[l], router_expert_w[l]], axis=1),
                     ((0, 0), (0, LANES - N_GROUPS - N_EXPERTS)))
        rw_hi = rw.astype(BF16)
        rw = jnp.concatenate([rw_hi, (rw - rw_hi.astype(F32)).astype(BF16)], axis=1)
        x1, h2, ri, rg, cnt = _outproj(ya.reshape(N, CONV_A_DIM), yb.reshape(N, DN_DIM), x.reshape(N, D),
                                       w_out[l].astype(BF16), ffn_norm_g[l][None, :], rw)
        counts = cnt[0, :N_EXPERTS].astype(I32)
        padded = (counts + bm - 1) // bm * bm
        seg_end = jnp.cumsum(padded).astype(I32)
        seg_start = seg_end - padded
        n_used = (seg_end[-1:] // bm)
        slots = _slots(ri, seg_start)
        seg = jnp.concatenate([seg_start + counts, seg_end, n_used])
        xs = _dispatch(seg, slots, h2, n_blocks * bm)
        blk = jnp.concatenate([seg_start // bm, padded // bm, n_used])
        ys = _ffn(blk, xs, w_gate[l], w_up[l], w_down[l])
        x = _combine(slots, ys, x1, rg, final_norm_g[None, :]).reshape(B, S, D)
    return x
```

```python
import functools

import jax
import jax.numpy as jnp
from jax import lax
from jax.experimental import pallas as pl
from jax.experimental.pallas import tpu as pltpu

F32 = jnp.float32
BF16 = jnp.bfloat16
I32 = jnp.int32

D_MODEL = 1024
CHUNK = 64
CONV_A_GROUP_DIM = 64
CONV_A_DIM = 512
CONV_A_WIDTH = 3
DN_HEADS = 4
DN_HEAD_DIM = 128
DN_DIM = 512
DN_CONV_WIDTH = 4
IN_PROJ_DIM = 3 * CONV_A_DIM + 4 * DN_DIM + 2 * DN_HEADS
N_GROUPS = 4
EXPERTS_PER_GROUP = 8
N_EXPERTS = 32
TOP_K = 2
EXPERT_FF = 512
EPS = 1e-6

LANES = 128
HALO = 8
GATE_COL = 3 * CONV_A_DIM + 4 * DN_DIM

TM_IN = 512
TM_PREP = 256
SCAN_BATCH = 8
TM_SCAN = 256
TM_OUT = 1024
OUT_SPLIT = 8
FFN_BLOCK = 256
TM_SLOTS = 2048
TM_DISPATCH = 2048
TM_COMBINE = 512
FFN_SPLIT = 2
FFN_RING = 4
RING_PRIORITY = 1
ISSUE_UNROLL = 4
VMEM_LIMIT = 56 * 1024 * 1024


def _dot(a, b):
    return jnp.dot(a, b, preferred_element_type=F32)


def _dot_nt(a, b):
    return lax.dot_general(a, b, (((1,), (1,)), ((), ())), preferred_element_type=F32)


def _dot_tn(a, b):
    return lax.dot_general(a, b, (((0,), (0,)), ((), ())), preferred_element_type=F32)


def _split_bf16(x, parts):
    out = []
    for _ in range(parts):
        p = x.astype(BF16)
        out.append(p)
        x = x - p.astype(F32)
    return out


def _silu(x):
    return x * jax.nn.sigmoid(x)


ROW_SLAB = D_MODEL // LANES


def _store_rows(ref, val):
    m = val.shape[0]
    for c in range(ROW_SLAB):
        ref[pl.ds(c, m, stride=ROW_SLAB), :] = val[:, c * LANES:(c + 1) * LANES]


def _load_rows(ref, m):
    return jnp.concatenate([ref[pl.ds(c, m, stride=ROW_SLAB), :] for c in range(ROW_SLAB)], axis=-1)


def _slab(ref, row):
    return ref.at[pl.ds(pl.multiple_of(row * ROW_SLAB, ROW_SLAB), ROW_SLAB)]


def _slab_block(ref, block):
    n = FFN_BLOCK * ROW_SLAB
    return ref.at[pl.ds(pl.multiple_of(block * n, n), n)]


def _causal_conv(ext, w):
    taps = w.shape[0]
    acc = None
    for j in range(taps):
        shift = taps - 1 - j
        rows = pltpu.roll(ext, shift, 0)[HALO:] if shift else ext[HALO:]
        term = w[j:j + 1, :] * rows
        acc = term if acc is None else acc + term
    return acc


def _softplus(x):
    return jnp.maximum(x, 0.0) + jnp.log1p(jnp.exp(-jnp.abs(x)))


def _inproj_kernel(x_ref, g_ref, w_ref, caw_ref, cag_ref, dcw_ref, alog_ref, dtb_ref, gmat_ref,
                   ya_ref, q_ref, k_ref, v_ref, zg_ref, gcol_ref, grow_ref,
                   exta_ref, *extq_refs):
    tm = x_ref.shape[1]
    ext_refs = (exta_ref,) + extq_refs

    @pl.when(pl.program_id(1) == 0)
    def _():
        for ext_ref in ext_refs:
            ext_ref[0:HALO, :] = jnp.zeros((HALO, ext_ref.shape[1]), F32)

    x = x_ref[0]
    ms = jnp.mean(x * x, axis=-1, keepdims=True)
    hb = (x * lax.rsqrt(ms + EPS) * g_ref[...]).astype(BF16)

    def proj(c0, width):
        return _dot(hb, w_ref[:, c0:c0 + width])

    base = 3 * CONV_A_DIM

    def qkv_project(i):
        extq_refs[i][HALO:HALO + tm, :] = proj(base + i * DN_DIM, DN_DIM)

    def qkv_finish(i, out_ref):
        cols = slice(i * DN_DIM, (i + 1) * DN_DIM)
        s = _silu(_causal_conv(extq_refs[i][...], dcw_ref[:, cols]))
        if i == 2:
            out_ref[0] = s
        else:
            for h in range(DN_HEADS):
                sh = s[:, h * DN_HEAD_DIM:(h + 1) * DN_HEAD_DIM]
                inv = lax.rsqrt(jnp.sum(sh * sh, axis=-1, keepdims=True) + EPS)
                sh = sh * inv
                if i == 0:
                    sh = sh * (DN_HEAD_DIM ** -0.5)
                out_ref[0, :, h * DN_HEAD_DIM:(h + 1) * DN_HEAD_DIM] = sh

    def mixer_a_finish(a_b):
        y = a_b * _causal_conv(exta_ref[...], caw_ref[...])
        ysq = y * y
        hi = ysq.astype(BF16)
        lo = (ysq - hi.astype(F32)).astype(BF16)
        gmean = _dot(hi, gmat_ref[...]) + _dot(lo, gmat_ref[...])
        ya_ref[0] = (y * lax.rsqrt(gmean + EPS) * cag_ref[...]).astype(ya_ref.dtype)

    qkv_project(0)
    qkv_project(1)
    exta_ref[HALO:HALO + tm, :] = proj(2 * CONV_A_DIM, CONV_A_DIM) * proj(0, CONV_A_DIM)
    qkv_finish(0, q_ref)
    qkv_project(2)
    a_b = proj(CONV_A_DIM, CONV_A_DIM)
    qkv_finish(1, k_ref)
    z = proj(base + 3 * DN_DIM, DN_DIM)
    mixer_a_finish(a_b)
    n_gate = IN_PROJ_DIM - GATE_COL
    p = jnp.concatenate([proj(GATE_COL, n_gate), jnp.zeros((tm, LANES - n_gate), F32)], axis=-1)
    qkv_finish(2, v_ref)
    zg_ref[0] = _silu(z)
    for ext_ref in ext_refs:
        ext_ref[0:HALO, :] = ext_ref[tm:tm + HALO, :]

    beta = jax.nn.sigmoid(p)
    g = -jnp.exp(alog_ref[...]) * _softplus(p + dtb_ref[...])
    r = lax.broadcasted_iota(I32, (CHUNK, CHUNK), 0)
    c = lax.broadcasted_iota(I32, (CHUNK, CHUNK), 1)
    tri = jnp.where(c <= r, 1.0, 0.0).astype(BF16)
    parts = _split_bf16(g, 3)
    gc = jnp.concatenate(
        [sum(_dot(tri, p[ci * CHUNK:(ci + 1) * CHUNK]) for p in parts) for ci in range(tm // CHUNK)], axis=0)
    lane = lax.broadcasted_iota(I32, (tm, LANES), 1)
    slab = jnp.where(lane < DN_HEADS, beta, gc)
    gcol_ref[0] = slab
    rows = slab.T[0:HALO, :]
    for ci in range(tm // CHUNK):
        grow_ref[0, ci] = rows[:, ci * CHUNK:(ci + 1) * CHUNK]


def _inproj(x, mix_g, w_in16, conv_a_w, conv_a_g, dn_conv_w, alog_vec, dtb_vec, gmat):
    B, S, D = x.shape
    tm = TM_IN
    full = lambda shape: pl.BlockSpec(shape, lambda b, t: (0,) * len(shape))
    row = lambda width: pl.BlockSpec((1, tm, width), lambda b, t: (b, t, 0))
    out_shape = (
        jax.ShapeDtypeStruct((B, S, CONV_A_DIM), BF16),
        jax.ShapeDtypeStruct((B, S, DN_DIM), F32),
        jax.ShapeDtypeStruct((B, S, DN_DIM), F32),
        jax.ShapeDtypeStruct((B, S, DN_DIM), F32),
        jax.ShapeDtypeStruct((B, S, DN_DIM), F32),
        jax.ShapeDtypeStruct((B, S, LANES), F32),
        jax.ShapeDtypeStruct((B, S // CHUNK, HALO, CHUNK), F32),
    )
    return pl.pallas_call(
        _inproj_kernel,
        grid=(B, S // tm),
        in_specs=[row(D), full((1, D)), full((D, IN_PROJ_DIM)), full((CONV_A_WIDTH, CONV_A_DIM)),
                  full((1, CONV_A_DIM)), full((DN_CONV_WIDTH, 3 * DN_DIM)), full((1, LANES)),
                  full((1, LANES)), full((CONV_A_DIM, CONV_A_DIM))],
        out_specs=(row(CONV_A_DIM), row(DN_DIM), row(DN_DIM), row(DN_DIM), row(DN_DIM), row(LANES),
                   pl.BlockSpec((1, tm // CHUNK, HALO, CHUNK), lambda b, t: (b, t, 0, 0))),
        out_shape=out_shape,
        scratch_shapes=[pltpu.VMEM((tm + HALO, CONV_A_DIM), F32)] + [pltpu.VMEM((tm + HALO, DN_DIM), F32)] * 3,
        compiler_params=pltpu.CompilerParams(
            dimension_semantics=("arbitrary", "arbitrary"), vmem_limit_bytes=VMEM_LIMIT),
        name="inproj",
    )(x, mix_g, w_in16, conv_a_w, conv_a_g, dn_conv_w, alog_vec, dtb_vec, gmat)


def _delta_prep_kernel(q_ref, k_ref, v_ref, gcol_ref, grow_ref,
                       u_ref, w_ref, qd_ref, kd_ref, at_ref):
    tm = q_ref.shape[1]
    ri = lax.broadcasted_iota(I32, (CHUNK, CHUNK), 0)
    ci = lax.broadcasted_iota(I32, (CHUNK, CHUNK), 1)
    causal = ci <= ri
    strict = ci < ri
    eye = jnp.where(ci == ri, 1.0, 0.0).astype(F32)
    lane = lax.broadcasted_iota(I32, (CHUNK, LANES), 1)
    left_half = lane < CHUNK

    def setup(c, h):
        rows = slice(c * CHUNK, (c + 1) * CHUNK)
        cols = slice(h * DN_HEAD_DIM, (h + 1) * DN_HEAD_DIM)
        slab = gcol_ref[0, rows, :]
        beta = jnp.sum(jnp.where(lane == h, slab, 0.0), axis=-1, keepdims=True)
        gcc = jnp.sum(jnp.where(lane == h + DN_HEADS, slab, 0.0), axis=-1, keepdims=True)
        gcr = grow_ref[0, c, h + DN_HEADS:h + DN_HEADS + 1, :]
        diff = gcc - gcr
        decay = jnp.where(causal, jnp.exp(jnp.where(causal, diff, 0.0)), 0.0)
        q = q_ref[0, rows, cols]
        k = k_ref[0, rows, cols]
        v = v_ref[0, rows, cols]
        kb = k * beta
        kq = _dot_nt(jnp.concatenate([kb.astype(BF16), q.astype(BF16)], axis=0), k.astype(BF16))
        L = jnp.where(strict, kq[:CHUNK] * decay, 0.0)
        intra = kq[CHUNK:] * decay
        egc = jnp.exp(gcc)
        gl = gcr[:, CHUNK - 1:CHUNK]
        qd_ref[0, rows, cols] = (q * egc).astype(BF16)
        kd_ref[0, rows, cols] = (k * jnp.exp(gl - gcc)).astype(BF16)
        at_ref[0, rows, h * CHUNK:(h + 1) * CHUNK] = intra.astype(BF16)
        return -L, jnp.concatenate([v * beta, kb * egc], axis=-1).astype(BF16)

    def first_stage(m):
        m16 = m.astype(BF16)
        return jnp.concatenate([eye + m, _dot(m16, m16)], axis=-1)

    def stage(r):
        return _dot(r[:, CHUNK:].astype(BF16), r.astype(BF16)) + jnp.where(left_half, r, 0.0)

    def last_stage(r):
        return r[:, :CHUNK] + _dot(r[:, CHUNK:].astype(BF16), r[:, :CHUNK].astype(BF16))

    def solve(c, h, t, rhs):
        rows = slice(c * CHUNK, (c + 1) * CHUNK)
        cols = slice(h * DN_HEAD_DIM, (h + 1) * DN_HEAD_DIM)
        uw = _dot(t.astype(BF16), rhs)
        u_ref[0, rows, cols] = uw[:, :DN_HEAD_DIM]
        w_ref[0, rows, cols] = uw[:, DN_HEAD_DIM:].astype(BF16)

    chains = [(c, h) for c in range(tm // CHUNK) for h in range(DN_HEADS)]
    ms, rhss = zip(*[setup(c, h) for c, h in chains])
    rs = [first_stage(m) for m in ms]
    for _ in range(4):
        rs = [stage(r) for r in rs]
    ts = [last_stage(r) for r in rs]
    for (c, h), t, rhs in zip(chains, ts, rhss):
        solve(c, h, t, rhs)


def _delta_prep(q, k, v, gcol, grow):
    B, S, _ = q.shape
    tm = TM_PREP
    row = lambda width: pl.BlockSpec((1, tm, width), lambda b, t: (b, t, 0))
    return pl.pallas_call(
        _delta_prep_kernel,
        grid=(B, S // tm),
        in_specs=[row(DN_DIM), row(DN_DIM), row(DN_DIM), row(LANES),
                  pl.BlockSpec((1, tm // CHUNK, HALO, CHUNK), lambda b, t: (b, t, 0, 0))],
        out_specs=(row(DN_DIM), row(DN_DIM), row(DN_DIM), row(DN_DIM), row(DN_HEADS * CHUNK)),
        out_shape=(jax.ShapeDtypeStruct((B, S, DN_DIM), F32),
                   jax.ShapeDtypeStruct((B, S, DN_DIM), BF16),
                   jax.ShapeDtypeStruct((B, S, DN_DIM), BF16),
                   jax.ShapeDtypeStruct((B, S, DN_DIM), BF16),
                   jax.ShapeDtypeStruct((B, S, DN_HEADS * CHUNK), BF16)),
        compiler_params=pltpu.CompilerParams(
            dimension_semantics=("arbitrary", "arbitrary"), vmem_limit_bytes=VMEM_LIMIT),
        name="delta_prep",
    )(q, k, v, gcol, grow)


def _delta_scan_kernel(u_ref, w_ref, qd_ref, kd_ref, at_ref, zg_ref, grow_ref, ng_ref, o_ref, st_ref):
    nb, tm = u_ref.shape[0], u_ref.shape[1]

    @pl.when(pl.program_id(1) == 0)
    def _():
        st_ref[...] = jnp.zeros(st_ref.shape, F32)

    def step(c, carry):
        r0 = pl.multiple_of(c * CHUNK, CHUNK)
        rows = pl.ds(r0, CHUNK)
        chains = [(b, h) for b in range(nb) for h in range(DN_HEADS)]
        cols = lambda h: slice(h * DN_HEAD_DIM, (h + 1) * DN_HEAD_DIM)
        sts = [st_ref[b, h] for b, h in chains]
        st16s = [st.astype(BF16) for st in sts]
        wqs = [_dot(jnp.concatenate([w_ref[b, rows, cols(h)], qd_ref[b, rows, cols(h)]], axis=0), st16)
               for (b, h), st16 in zip(chains, st16s)]
        vn16s = [(u_ref[b, rows, cols(h)] - wq[:CHUNK]).astype(BF16) for (b, h), wq in zip(chains, wqs)]
        upd = [_dot_tn(kd_ref[b, rows, cols(h)], vn16) for (b, h), vn16 in zip(chains, vn16s)]
        avs = [_dot(at_ref[b, rows, h * CHUNK:(h + 1) * CHUNK], vn16) for (b, h), vn16 in zip(chains, vn16s)]
        for (b, h), st, up in zip(chains, sts, upd):
            gl = grow_ref[b, c, h + DN_HEADS:h + DN_HEADS + 1, CHUNK - 1:CHUNK]
            st_ref[b, h] = st * jnp.exp(gl) + up
        for (b, h), wq, av in zip(chains, wqs, avs):
            o = wq[CHUNK:] + av
            on = o * lax.rsqrt(jnp.mean(o * o, axis=-1, keepdims=True) + EPS) * ng_ref[...]
            o_ref[b, rows, cols(h)] = (on * zg_ref[b, rows, cols(h)]).astype(o_ref.dtype)
        return carry

    lax.fori_loop(0, tm // CHUNK, step, 0)


def _delta_scan(u, w, qd, kd, at, zg, grow, norm_g):
    B, S, _ = u.shape
    tm = TM_SCAN
    nb = SCAN_BATCH
    row = lambda width: pl.BlockSpec((nb, tm, width), lambda b, t: (b, t, 0))
    return pl.pallas_call(
        _delta_scan_kernel,
        grid=(B // nb, S // tm),
        in_specs=[row(DN_DIM), row(DN_DIM), row(DN_DIM), row(DN_DIM), row(DN_HEADS * CHUNK), row(DN_DIM),
                  pl.BlockSpec((nb, tm // CHUNK, HALO, CHUNK), lambda b, t: (b, t, 0, 0)),
                  pl.BlockSpec((1, DN_HEAD_DIM), lambda b, t: (0, 0))],
        out_specs=row(DN_DIM),
        out_shape=jax.ShapeDtypeStruct((B, S, DN_DIM), BF16),
        scratch_shapes=[pltpu.VMEM((nb, DN_HEADS, DN_HEAD_DIM, DN_HEAD_DIM), F32)],
        compiler_params=pltpu.CompilerParams(
            dimension_semantics=("arbitrary", "arbitrary"), vmem_limit_bytes=VMEM_LIMIT),
        name="delta_scan",
    )(u, w, qd, kd, at, zg, grow, norm_g)


def _outproj_kernel(ya_ref, yb_ref, x_ref, wo_ref, g_ref, rw_ref,
                    x1_ref, h2_ref, ri_ref, rg_ref, cnt_ref, base_ref):
    tm = x_ref.shape[0]
    sub = tm // OUT_SPLIT
    parts = range(OUT_SPLIT)
    rows_of = lambda s: slice(s * sub, (s + 1) * sub)

    @pl.when(pl.program_id(0) == 0)
    def _():
        base_ref[...] = jnp.zeros(base_ref.shape, F32)

    def project(s):
        rows = rows_of(s)
        return _dot(jnp.concatenate([ya_ref[rows, :], yb_ref[rows, :]], axis=-1), wo_ref[...])

    def normalise(s, y):
        rows = rows_of(s)
        x1 = x_ref[rows, :] + y
        x1_ref[rows, :] = x1
        h = x1 * lax.rsqrt(jnp.mean(x1 * x1, axis=-1, keepdims=True) + EPS) * g_ref[...]
        _store_rows(h2_ref.at[pl.ds(s * sub * ROW_SLAB, sub * ROW_SLAB)], h)
        return h

    def router_logits(h):
        h_hi, h_lo = _split_bf16(h, 2)
        hi_prod = _dot(h_hi, rw_ref[...])
        return hi_prod[:, :LANES] + (hi_prod[:, LANES:] + _dot(h_lo, rw_ref[:, :LANES]))

    lane = lax.broadcasted_iota(I32, (sub, LANES), 1)
    lanef = lane.astype(F32)
    rr = lax.broadcasted_iota(I32, (sub, sub), 0)
    cc = lax.broadcasted_iota(I32, (sub, sub), 1)
    tri = jnp.where(cc < rr, 1.0, 0.0).astype(BF16)

    def route(s, logits):
        rows = rows_of(s)
        neg = jnp.float32(-1e30)
        big = jnp.float32(1e9)
        is_g = lane < N_GROUPS
        gl = jnp.where(is_g, logits, neg)
        gmax = jnp.max(gl, axis=-1, keepdims=True)
        gidx = jnp.min(jnp.where(gl == gmax, lanef, big), axis=-1, keepdims=True)
        gsum = jnp.sum(jnp.where(is_g, jnp.exp(gl - gmax), 0.0), axis=-1, keepdims=True)
        gprob = 1.0 / gsum
        lo = N_GROUPS + EXPERTS_PER_GROUP * gidx
        emask = (lanef >= lo) & (lanef < lo + EXPERTS_PER_GROUP)
        el = jnp.where(emask, logits, neg)
        e1 = jnp.max(el, axis=-1, keepdims=True)
        i1 = jnp.min(jnp.where(el == e1, lanef, big), axis=-1, keepdims=True)
        el2 = jnp.where(lanef == i1, neg, el)
        e2 = jnp.max(el2, axis=-1, keepdims=True)
        i2 = jnp.min(jnp.where(el2 == e2, lanef, big), axis=-1, keepdims=True)
        r = jnp.exp(e2 - e1)
        gate1 = gprob / (1.0 + r)
        gate2 = gprob * r / (1.0 + r)
        id1 = i1 - N_GROUPS
        id2 = i2 - N_GROUPS

        oh1 = jnp.where(lanef == id1, 1.0, 0.0).astype(F32)
        oh2 = jnp.where(lanef == id2, 1.0, 0.0).astype(F32)
        oh = oh1 + oh2
        before = _dot(tri, oh.astype(BF16)) + base_ref[...]
        rank1 = jnp.sum(oh1 * before, axis=-1, keepdims=True)
        rank2 = jnp.sum(oh2 * before, axis=-1, keepdims=True)
        base_ref[...] = base_ref[...] + jnp.sum(oh, axis=0, keepdims=True)

        ri = jnp.where(lane == 0, id1, jnp.where(lane == 1, id2,
                       jnp.where(lane == 2, rank1, jnp.where(lane == 3, rank2, 0.0))))
        ri_ref[rows, :] = ri.astype(I32)
        rg_ref[rows, :] = jnp.where(lane == 0, gate1, jnp.where(lane == 1, gate2, 0.0))

    ys = [project(s) for s in parts]
    hs = [normalise(s, y) for s, y in zip(parts, ys)]
    ls = [router_logits(h) for h in hs]
    for s, logits in zip(parts, ls):
        route(s, logits)
    cnt_ref[...] = base_ref[...]


def _outproj(ya, yb, x, w_out16, ffn_g, rw):
    N, D = x.shape
    tm = TM_OUT
    row = lambda width: pl.BlockSpec((tm, width), lambda i: (i, 0))
    full = lambda shape: pl.BlockSpec(shape, lambda i: (0,) * len(shape))
    return pl.pallas_call(
        _outproj_kernel,
        grid=(N // tm,),
        in_specs=[row(CONV_A_DIM), row(DN_DIM), row(D), full((D, D)), full((1, D)), full((D, 2 * LANES))],
        out_specs=(row(D), pl.BlockSpec((tm * ROW_SLAB, LANES), lambda i: (i, 0)), row(LANES), row(LANES),
                   full((1, LANES))),
        out_shape=(jax.ShapeDtypeStruct((N, D), F32),
                   jax.ShapeDtypeStruct((N * ROW_SLAB, LANES), F32),
                   jax.ShapeDtypeStruct((N, LANES), I32),
                   jax.ShapeDtypeStruct((N, LANES), F32),
                   jax.ShapeDtypeStruct((1, LANES), F32)),
        scratch_shapes=[pltpu.VMEM((1, LANES), F32)],
        compiler_params=pltpu.CompilerParams(
            dimension_semantics=("arbitrary",), vmem_limit_bytes=VMEM_LIMIT),
        name="outproj",
    )(ya, yb, x, w_out16, ffn_g, rw)


def _slots_kernel(ri_ref, start_ref, o_ref):
    rif = ri_ref[...].astype(F32)
    lane = lax.broadcasted_iota(I32, rif.shape, 1)
    lanef = lane.astype(F32)

    def slot(k):
        start = jnp.sum(jnp.where(lanef == rif[:, k:k + 1], start_ref[...], 0.0), axis=-1, keepdims=True)
        return start + rif[:, TOP_K + k:TOP_K + k + 1]

    slab = jnp.where(lane == 0, slot(0), jnp.where(lane == 1, slot(1), 0.0))
    o_ref[...] = slab.T[0:HALO, :].astype(I32)


def _slots(ri, seg_start):
    N = ri.shape[0]
    tm = TM_SLOTS
    out = pl.pallas_call(
        _slots_kernel,
        grid=(N // tm,),
        in_specs=[pl.BlockSpec((tm, LANES), lambda i: (i, 0)), pl.BlockSpec((1, LANES), lambda i: (0, 0))],
        out_specs=pl.BlockSpec((HALO, tm), lambda i: (0, i)),
        out_shape=jax.ShapeDtypeStruct((HALO, N), I32),
        compiler_params=pltpu.CompilerParams(dimension_semantics=("arbitrary",)),
        name="slots",
    )(ri, _lane_vec(seg_start, 0))
    return out[:TOP_K].reshape(-1)


def _dispatch_kernel(seg_ref, slot0_ref, slot1_ref, h_ref, xs_ref, zero_ref, sem, zsem):
    tm = TM_DISPATCH
    bm = FFN_BLOCK
    n_blocks = xs_ref.shape[0] // (bm * ROW_SLAB)

    @pl.when(pl.program_id(0) == 0)
    def _():
        zero_ref[...] = jnp.zeros(zero_ref.shape, F32)

        def run_copy(first_row, n_rows):
            start = pl.multiple_of(first_row * ROW_SLAB, ROW_SLAB)
            return pltpu.make_async_copy(zero_ref.at[pl.ds(0, n_rows * ROW_SLAB)],
                                         xs_ref.at[pl.ds(start, n_rows * ROW_SLAB)], zsem)

        def block_copy(b):
            return pltpu.make_async_copy(zero_ref, _slab_block(xs_ref, b), zsem)

        def each_pad(fn):
            def per_expert(e, carry):
                row = seg_ref[e]
                n_pad = seg_ref[N_EXPERTS + e] - row
                run = bm // 2
                while run >= 1:
                    @pl.when((n_pad & run) != 0)
                    def _(row=row, run=run):
                        fn(run_copy(row, run))
                    row = row + (n_pad & run)
                    run //= 2
                return carry
            lax.fori_loop(0, N_EXPERTS, per_expert, 0)

            def per_block(b, c):
                fn(block_copy(b))
                return c
            lax.fori_loop(seg_ref[2 * N_EXPERTS], n_blocks, per_block, 0)

        each_pad(lambda cp: cp.start())
        each_pad(lambda cp: cp.wait())

    def issue(t, carry):
        for k, slot_ref in enumerate((slot0_ref, slot1_ref)):
            pltpu.make_async_copy(_slab(h_ref, t), _slab(xs_ref, slot_ref[t]), sem).start(priority=k)
        return carry

    lax.fori_loop(0, tm, issue, 0, unroll=ISSUE_UNROLL)
    for _ in range(TOP_K):
        pltpu.make_async_copy(h_ref, xs_ref.at[pl.ds(0, tm * ROW_SLAB)], sem).wait()


def _dispatch(seg, slots, h2, n_slots):
    N = h2.shape[0] // ROW_SLAB
    tm = TM_DISPATCH
    grid_spec = pltpu.PrefetchScalarGridSpec(
        num_scalar_prefetch=1,
        grid=(N // tm,),
        in_specs=[pl.BlockSpec((tm,), lambda i, s: (i,), memory_space=pltpu.SMEM),
                  pl.BlockSpec((tm,), lambda i, s: (N // tm + i,), memory_space=pltpu.SMEM),
                  pl.BlockSpec((tm * ROW_SLAB, LANES), lambda i, s: (i, 0))],
        out_specs=pl.BlockSpec(memory_space=pl.ANY),
        scratch_shapes=[pltpu.VMEM((FFN_BLOCK * ROW_SLAB, LANES), F32),
                        pltpu.SemaphoreType.DMA(()), pltpu.SemaphoreType.DMA(())],
    )
    return pl.pallas_call(
        _dispatch_kernel,
        grid_spec=grid_spec,
        out_shape=jax.ShapeDtypeStruct((n_slots * ROW_SLAB, LANES), F32),
        compiler_params=pltpu.CompilerParams(dimension_semantics=("arbitrary",)),
        name="dispatch",
    )(seg, slots, slots, h2)


def _ffn_kernel(blk_ref, xs_ref, wg_ref, wu_ref, wd_ref, ys_ref,
                xbuf, ybuf, wg16, wu16, wd16, xsem, ysem, zsem):
    e = pl.program_id(0)
    bm = FFN_BLOCK
    ring = FFN_RING
    n_blocks = ys_ref.shape[0] // (bm * ROW_SLAB)
    first = blk_ref[e]
    count = blk_ref[N_EXPERTS + e]
    n_used = blk_ref[2 * N_EXPERTS]

    def x_copy(g):
        return pltpu.make_async_copy(_slab_block(xs_ref, g), xbuf.at[g % ring], xsem.at[g % ring])

    def y_copy(g):
        return pltpu.make_async_copy(ybuf.at[g % ring], _slab_block(ys_ref, g), ysem.at[g % ring])

    def when_block(g, fn):
        @pl.when((g >= 0) & (g < n_used))
        def _():
            fn(g)

    @pl.when(e == 0)
    def _():
        for g in range(ring - 1):
            when_block(g, lambda g: x_copy(g).start(priority=RING_PRIORITY))

    @pl.when(count > 0)
    def _():
        wg16[...] = wg_ref[0].astype(BF16)
        wu16[...] = wu_ref[0].astype(BF16)
        wd16[...] = wd_ref[0].astype(BF16)

    def block(j, carry):
        g = first + j
        x_copy(g).wait()
        when_block(g + ring - 1, lambda g: x_copy(g).start(priority=RING_PRIORITY))

        x = _load_rows(xbuf.at[g % ring], bm).astype(BF16)
        sub = bm // FFN_SPLIT
        x_parts = [x[s * sub:(s + 1) * sub] for s in range(FFN_SPLIT)]
        gates = [_dot(xp, wg16[...]) for xp in x_parts]
        ups = [_dot(xp, wu16[...]) for xp in x_parts]
        acts = [(_silu(a) * b).astype(BF16) for a, b in zip(gates, ups)]
        y = jnp.concatenate([_dot(act, wd16[...]) for act in acts], axis=0)

        when_block(g - ring, lambda g: y_copy(g).wait())
        _store_rows(ybuf.at[g % ring], y)
        y_copy(g).start(priority=RING_PRIORITY)
        return carry

    lax.fori_loop(0, count, block, 0)

    @pl.when(e == pl.num_programs(0) - 1)
    def _():
        for back in range(ring, 0, -1):
            when_block(n_used - back, lambda g: y_copy(g).wait())
        ybuf[0] = jnp.zeros(ybuf.shape[1:], F32)

        def zero_copy(g):
            return pltpu.make_async_copy(ybuf.at[0], _slab_block(ys_ref, g), zsem)

        def start(g, c):
            zero_copy(g).start()
            return c

        def wait(g, c):
            zero_copy(g).wait()
            return c

        lax.fori_loop(n_used, n_blocks, start, 0)
        lax.fori_loop(n_used, n_blocks, wait, 0)


def _ffn(blk, xs, w_gate, w_up, w_down):
    bm = FFN_BLOCK
    D = D_MODEL
    weights = lambda shape: pl.BlockSpec((1,) + shape, lambda e, blk: (e, 0, 0))
    grid_spec = pltpu.PrefetchScalarGridSpec(
        num_scalar_prefetch=1,
        grid=(N_EXPERTS,),
        in_specs=[pl.BlockSpec(memory_space=pl.ANY),
                  weights((D, EXPERT_FF)), weights((D, EXPERT_FF)), weights((EXPERT_FF, D))],
        out_specs=pl.BlockSpec(memory_space=pl.ANY),
        scratch_shapes=[pltpu.VMEM((FFN_RING, bm * ROW_SLAB, LANES), F32),
                        pltpu.VMEM((FFN_RING, bm * ROW_SLAB, LANES), F32),
                        pltpu.VMEM((D, EXPERT_FF), BF16), pltpu.VMEM((D, EXPERT_FF), BF16),
                        pltpu.VMEM((EXPERT_FF, D), BF16),
                        pltpu.SemaphoreType.DMA((FFN_RING,)), pltpu.SemaphoreType.DMA((FFN_RING,)),
                        pltpu.SemaphoreType.DMA(())],
    )
    return pl.pallas_call(
        _ffn_kernel,
        grid_spec=grid_spec,
        out_shape=jax.ShapeDtypeStruct(xs.shape, F32),
        compiler_params=pltpu.CompilerParams(
            dimension_semantics=("arbitrary",), vmem_limit_bytes=VMEM_LIMIT),
        name="ffn",
    )(blk, xs, w_gate, w_up, w_down)


def _combine_kernel(slot0_ref, slot1_ref, next0_ref, next1_ref, ys_ref, x1_ref, rg_ref, g_ref, o_ref,
                    buf_ref, sems):
    tm = TM_COMBINE
    i = pl.program_id(0)
    half = i % 2

    def gather(slot_refs, s):
        def issue(t, carry):
            for k, slot_ref in enumerate(slot_refs):
                pltpu.make_async_copy(_slab(ys_ref, slot_ref[t]), _slab(buf_ref.at[s, k], t),
                                      sems.at[s]).start(priority=k)
            return carry
        lax.fori_loop(0, tm, issue, 0, unroll=ISSUE_UNROLL)

    @pl.when(i == 0)
    def _():
        gather((slot0_ref, slot1_ref), 0)

    @pl.when(i + 1 < pl.num_programs(0))
    def _():
        gather((next0_ref, next1_ref), 1 - half)

    for k in range(TOP_K):
        pltpu.make_async_copy(ys_ref.at[pl.ds(0, tm * ROW_SLAB)], buf_ref.at[half, k], sems.at[half]).wait()

    rg = rg_ref[...]
    moe = (_load_rows(buf_ref.at[half, 0], tm) * rg[:, 0:1]
           + _load_rows(buf_ref.at[half, 1], tm) * rg[:, 1:2])
    x2 = x1_ref[...] + moe
    o_ref[...] = x2 * lax.rsqrt(jnp.mean(x2 * x2, axis=-1, keepdims=True) + EPS) * g_ref[...]


def _combine(slots, ys, x1, rg, final_g):
    N, D = x1.shape
    tm = TM_COMBINE
    n_tiles = N // tm

    def slot_block(k, ahead, i):
        return (k * n_tiles + jnp.minimum(i + ahead, n_tiles - 1),)

    return pl.pallas_call(
        _combine_kernel,
        grid=(n_tiles,),
        in_specs=[pl.BlockSpec((tm,), functools.partial(slot_block, k, ahead), memory_space=pltpu.SMEM)
                  for ahead in (0, 1) for k in range(TOP_K)]
                 + [pl.BlockSpec(memory_space=pl.ANY),
                  pl.BlockSpec((tm, D), lambda i: (i, 0)),
                  pl.BlockSpec((tm, LANES), lambda i: (i, 0)),
                  pl.BlockSpec((1, D), lambda i: (0, 0))],
        out_specs=pl.BlockSpec((tm, D), lambda i: (i, 0)),
        out_shape=jax.ShapeDtypeStruct((N, D), F32),
        scratch_shapes=[pltpu.VMEM((2, TOP_K, tm * ROW_SLAB, LANES), F32), pltpu.SemaphoreType.DMA((2,))],
        compiler_params=pltpu.CompilerParams(
            dimension_semantics=("arbitrary",), vmem_limit_bytes=VMEM_LIMIT),
        name="combine",
    )(slots, slots, slots, slots, ys, x1, rg, final_g)


def _lane_vec(values, offset):
    return jnp.zeros((1, LANES), F32).at[0, offset:offset + values.shape[0]].set(values.astype(F32))


def kernel(x, mix_norm_g, w_in, conv_a_w, conv_a_norm_g, dn_conv_w, dn_a_log, dn_dt_bias, dn_norm_g,
           w_out, ffn_norm_g, router_group_w, router_expert_w, w_gate, w_up, w_down, final_norm_g):
    B, S, D = x.shape
    N = B * S
    depth = w_in.shape[0]
    assert depth == 1, "single-layer block: the final RMSNorm is fused into the layer's combine step"
    group_of = jnp.arange(CONV_A_DIM, dtype=I32) // CONV_A_GROUP_DIM
    gmat = jnp.where(group_of[:, None] == group_of[None, :], 1.0 / CONV_A_GROUP_DIM, 0.0).astype(BF16)
    bm = FFN_BLOCK
    n_blocks = (N * TOP_K) // bm + N_EXPERTS
    for l in range(depth):
        ya, q, k, v, zg, gcol, grow = _inproj(
            x, mix_norm_g[l][None, :], w_in[l].astype(BF16), conv_a_w[l], conv_a_norm_g[l][None, :], dn_conv_w[l],
            _lane_vec(dn_a_log[l], DN_HEADS), _lane_vec(dn_dt_bias[l], DN_HEADS), gmat)
        u, w, qd, kd, at = _delta_prep(q, k, v, gcol, grow)
        yb = _delta_scan(u, w, qd, kd, at, zg, grow, dn_norm_g[l][None, :])
        rw = jnp.pad(jnp.concatenate([router_group_w[l], router_expert_w[l]], axis=1),
                     ((0, 0), (0, LANES - N_GROUPS - N_EXPERTS)))
        rw_hi = rw.astype(BF16)
        rw = jnp.concatenate([rw_hi, (rw - rw_hi.astype(F32)).astype(BF16)], axis=1)
        x1, h2, ri, rg, cnt = _outproj(ya.reshape(N, CONV_A_DIM), yb.reshape(N, DN_DIM), x.reshape(N, D),
                                       w_out[l].astype(BF16), ffn_norm_g[l][None, :], rw)
        counts = cnt[0, :N_EXPERTS].astype(I32)
        padded = (counts + bm - 1) // bm * bm
        seg_end = jnp.cumsum(padded).astype(I32)
        seg_start = seg_end - padded
        n_used = (seg_end[-1:] // bm)
        slots = _slots(ri, seg_start)
        seg = jnp.concatenate([seg_start + counts, seg_end, n_used])
        xs = _dispatch(seg, slots, h2, n_blocks * bm)
        blk = jnp.concatenate([seg_start // bm, padded // bm, n_used])
        ys = _ffn(blk, xs, w_gate[l], w_up[l], w_down[l])
        x = _combine(slots, ys, x1, rg, final_norm_g[None, :]).reshape(B, S, D)
    return x
```

```python
import functools

import jax
import jax.numpy as jnp
from jax import lax
from jax.experimental import pallas as pl
from jax.experimental.pallas import tpu as pltpu

F32 = jnp.float32
BF16 = jnp.bfloat16
I32 = jnp.int32

D_MODEL = 1024
CHUNK = 64
CONV_A_GROUP_DIM = 64
CONV_A_DIM = 512
CONV_A_WIDTH = 3
DN_HEADS = 4
DN_HEAD_DIM = 128
DN_DIM = 512
DN_CONV_WIDTH = 4
IN_PROJ_DIM = 3 * CONV_A_DIM + 4 * DN_DIM + 2 * DN_HEADS
N_GROUPS = 4
EXPERTS_PER_GROUP = 8
N_EXPERTS = 32
TOP_K = 2
EXPERT_FF = 512
EPS = 1e-6

LANES = 128
HALO = 8
GATE_COL = 3 * CONV_A_DIM + 4 * DN_DIM

TM_IN = 512
TM_PREP = 256
SCAN_BATCH = 8
TM_SCAN = 256
TM_OUT = 1024
OUT_SPLIT = 8
FFN_BLOCK = 256
TM_SLOTS = 2048
TM_DISPATCH = 2048
TM_COMBINE = 512
FFN_SPLIT = 2
FFN_RING = 4
RING_PRIORITY = 1
ISSUE_UNROLL = 4
VMEM_LIMIT = 56 * 1024 * 1024


def _dot(a, b):
    return jnp.dot(a, b, preferred_element_type=F32)


def _dot_nt(a, b):
    return lax.dot_general(a, b, (((1,), (1,)), ((), ())), preferred_element_type=F32)


def _dot_tn(a, b):
    return lax.dot_general(a, b, (((0,), (0,)), ((), ())), preferred_element_type=F32)


def _split_bf16(x, parts):
    out = []
    for _ in range(parts):
        p = x.astype(BF16)
        out.append(p)
        x = x - p.astype(F32)
    return out


def _silu(x):
    return x * jax.nn.sigmoid(x)


ROW_SLAB = D_MODEL // LANES


def _store_rows(ref, val):
    m = val.shape[0]
    for c in range(ROW_SLAB):
        ref[pl.ds(c, m, stride=ROW_SLAB), :] = val[:, c * LANES:(c + 1) * LANES]


def _load_rows(ref, m):
    return jnp.concatenate([ref[pl.ds(c, m, stride=ROW_SLAB), :] for c in range(ROW_SLAB)], axis=-1)


def _slab(ref, row):
    return ref.at[pl.ds(pl.multiple_of(row * ROW_SLAB, ROW_SLAB), ROW_SLAB)]


def _slab_block(ref, block):
    n = FFN_BLOCK * ROW_SLAB
    return ref.at[pl.ds(pl.multiple_of(block * n, n), n)]


def _causal_conv(ext, w):
    taps = w.shape[0]
    acc = None
    for j in range(taps):
        shift = taps - 1 - j
        rows = pltpu.roll(ext, shift, 0)[HALO:] if shift else ext[HALO:]
        term = w[j:j + 1, :] * rows
        acc = term if acc is None else acc + term
    return acc


def _softplus(x):
    return jnp.maximum(x, 0.0) + jnp.log1p(jnp.exp(-jnp.abs(x)))


def _inproj_kernel(x_ref, g_ref, w_ref, caw_ref, cag_ref, dcw_ref, alog_ref, dtb_ref, gmat_ref,
                   ya_ref, q_ref, k_ref, v_ref, zg_ref, gcol_ref, grow_ref,
                   exta_ref, *extq_refs):
    tm = x_ref.shape[1]
    ext_refs = (exta_ref,) + extq_refs

    @pl.when(pl.program_id(1) == 0)
    def _():
        for ext_ref in ext_refs:
            ext_ref[0:HALO, :] = jnp.zeros((HALO, ext_ref.shape[1]), F32)

    x = x_ref[0]
    ms = jnp.mean(x * x, axis=-1, keepdims=True)
    hb = (x * lax.rsqrt(ms + EPS) * g_ref[...]).astype(BF16)

    def proj(c0, width):
        return _dot(hb, w_ref[:, c0:c0 + width])

    base = 3 * CONV_A_DIM

    def qkv_project(i):
        extq_refs[i][HALO:HALO + tm, :] = proj(base + i * DN_DIM, DN_DIM)

    def qkv_finish(i, out_ref):
        cols = slice(i * DN_DIM, (i + 1) * DN_DIM)
        s = _silu(_causal_conv(extq_refs[i][...], dcw_ref[:, cols]))
        if i == 2:
            out_ref[0] = s
        else:
            for h in range(DN_HEADS):
                sh = s[:, h * DN_HEAD_DIM:(h + 1) * DN_HEAD_DIM]
                inv = lax.rsqrt(jnp.sum(sh * sh, axis=-1, keepdims=True) + EPS)
                sh = sh * inv
                if i == 0:
                    sh = sh * (DN_HEAD_DIM ** -0.5)
                out_ref[0, :, h * DN_HEAD_DIM:(h + 1) * DN_HEAD_DIM] = sh

    def mixer_a_finish(a_b):
        y = a_b * _causal_conv(exta_ref[...], caw_ref[...])
        ysq = y * y
        hi = ysq.astype(BF16)
        lo = (ysq - hi.astype(F32)).astype(BF16)
        gmean = _dot(hi, gmat_ref[...]) + _dot(lo, gmat_ref[...])
        ya_ref[0] = (y * lax.rsqrt(gmean + EPS) * cag_ref[...]).astype(ya_ref.dtype)

    qkv_project(0)
    qkv_project(1)
    exta_ref[HALO:HALO + tm, :] = proj(2 * CONV_A_DIM, CONV_A_DIM) * proj(0, CONV_A_DIM)
    qkv_finish(0, q_ref)
    qkv_project(2)
    a_b = proj(CONV_A_DIM, CONV_A_DIM)
    qkv_finish(1, k_ref)
    z = proj(base + 3 * DN_DIM, DN_DIM)
    mixer_a_finish(a_b)
    n_gate = IN_PROJ_DIM - GATE_COL
    p = jnp.concatenate([proj(GATE_COL, n_gate), jnp.zeros((tm, LANES - n_gate), F32)], axis=-1)
    qkv_finish(2, v_ref)
    zg_ref[0] = _silu(z)
    for ext_ref in ext_refs:
        ext_ref[0:HALO, :] = ext_ref[tm:tm + HALO, :]

    beta = jax.nn.sigmoid(p)
    g = -jnp.exp(alog_ref[...]) * _softplus(p + dtb_ref[...])
    r = lax.broadcasted_iota(I32, (CHUNK, CHUNK), 0)
    c = lax.broadcasted_iota(I32, (CHUNK, CHUNK), 1)
    tri = jnp.where(c <= r, 1.0, 0.0).astype(BF16)
    parts = _split_bf16(g, 3)
    gc = jnp.concatenate(
        [sum(_dot(tri, p[ci * CHUNK:(ci + 1) * CHUNK]) for p in parts) for ci in range(tm // CHUNK)], axis=0)
    lane = lax.broadcasted_iota(I32, (tm, LANES), 1)
    slab = jnp.where(lane < DN_HEADS, beta, gc)
    gcol_ref[0] = slab
    rows = slab.T[0:HALO, :]
    for ci in range(tm // CHUNK):
        grow_ref[0, ci] = rows[:, ci * CHUNK:(ci + 1) * CHUNK]


def _inproj(x, mix_g, w_in16, conv_a_w, conv_a_g, dn_conv_w, alog_vec, dtb_vec, gmat):
    B, S, D = x.shape
    tm = TM_IN
    full = lambda shape: pl.BlockSpec(shape, lambda b, t: (0,) * len(shape))
    row = lambda width: pl.BlockSpec((1, tm, width), lambda b, t: (b, t, 0))
    out_shape = (
        jax.ShapeDtypeStruct((B, S, CONV_A_DIM), BF16),
        jax.ShapeDtypeStruct((B, S, DN_DIM), F32),
        jax.ShapeDtypeStruct((B, S, DN_DIM), F32),
        jax.ShapeDtypeStruct((B, S, DN_DIM), F32),
        jax.ShapeDtypeStruct((B, S, DN_DIM), F32),
        jax.ShapeDtypeStruct((B, S, LANES), F32),
        jax.ShapeDtypeStruct((B, S // CHUNK, HALO, CHUNK), F32),
    )
    return pl.pallas_call(
        _inproj_kernel,
        grid=(B, S // tm),
        in_specs=[row(D), full((1, D)), full((D, IN_PROJ_DIM)), full((CONV_A_WIDTH, CONV_A_DIM)),
                  full((1, CONV_A_DIM)), full((DN_CONV_WIDTH, 3 * DN_DIM)), full((1, LANES)),
                  full((1, LANES)), full((CONV_A_DIM, CONV_A_DIM))],
        out_specs=(row(CONV_A_DIM), row(DN_DIM), row(DN_DIM), row(DN_DIM), row(DN_DIM), row(LANES),
                   pl.BlockSpec((1, tm // CHUNK, HALO, CHUNK), lambda b, t: (b, t, 0, 0))),
        out_shape=out_shape,
        scratch_shapes=[pltpu.VMEM((tm + HALO, CONV_A_DIM), F32)] + [pltpu.VMEM((tm + HALO, DN_DIM), F32)] * 3,
        compiler_params=pltpu.CompilerParams(
            dimension_semantics=("arbitrary", "arbitrary"), vmem_limit_bytes=VMEM_LIMIT),
        name="inproj",
    )(x, mix_g, w_in16, conv_a_w, conv_a_g, dn_conv_w, alog_vec, dtb_vec, gmat)


def _delta_prep_kernel(q_ref, k_ref, v_ref, gcol_ref, grow_ref,
                       u_ref, w_ref, qd_ref, kd_ref, at_ref):
    tm = q_ref.shape[1]
    ri = lax.broadcasted_iota(I32, (CHUNK, CHUNK), 0)
    ci = lax.broadcasted_iota(I32, (CHUNK, CHUNK), 1)
    causal = ci <= ri
    strict = ci < ri
    eye = jnp.where(ci == ri, 1.0, 0.0).astype(F32)
    lane = lax.broadcasted_iota(I32, (CHUNK, LANES), 1)
    left_half = lane < CHUNK

    def setup(c, h):
        rows = slice(c * CHUNK, (c + 1) * CHUNK)
        cols = slice(h * DN_HEAD_DIM, (h + 1) * DN_HEAD_DIM)
        slab = gcol_ref[0, rows, :]
        beta = jnp.sum(jnp.where(lane == h, slab, 0.0), axis=-1, keepdims=True)
        gcc = jnp.sum(jnp.where(lane == h + DN_HEADS, slab, 0.0), axis=-1, keepdims=True)
        gcr = grow_ref[0, c, h + DN_HEADS:h + DN_HEADS + 1, :]
        diff = gcc - gcr
        decay = jnp.where(causal, jnp.exp(jnp.where(causal, diff, 0.0)), 0.0)
        q = q_ref[0, rows, cols]
        k = k_ref[0, rows, cols]
        v = v_ref[0, rows, cols]
        kb = k * beta
        kq = _dot_nt(jnp.concatenate([kb.astype(BF16), q.astype(BF16)], axis=0), k.astype(BF16))
        L = jnp.where(strict, kq[:CHUNK] * decay, 0.0)
        intra = kq[CHUNK:] * decay
        egc = jnp.exp(gcc)
        gl = gcr[:, CHUNK - 1:CHUNK]
        qd_ref[0, rows, cols] = (q * egc).astype(BF16)
        kd_ref[0, rows, cols] = (k * jnp.exp(gl - gcc)).astype(BF16)
        at_ref[0, rows, h * CHUNK:(h + 1) * CHUNK] = intra.astype(BF16)
        return -L, jnp.concatenate([v * beta, kb * egc], axis=-1).astype(BF16)

    def first_stage(m):
        m16 = m.astype(BF16)
        return jnp.concatenate([_dot(m16, m16), eye + m], axis=-1)

    def stage(r):
        return _dot(r[:, :CHUNK].astype(BF16), r.astype(BF16)) + jnp.where(left_half, 0.0, r)

    def last_stage(r):
        return (stage(r))[:, CHUNK:]

    def solve(c, h, t, rhs):
        rows = slice(c * CHUNK, (c + 1) * CHUNK)
        cols = slice(h * DN_HEAD_DIM, (h + 1) * DN_HEAD_DIM)
        uw = _dot(t.astype(BF16), rhs)
        u_ref[0, rows, cols] = uw[:, :DN_HEAD_DIM]
        w_ref[0, rows, cols] = uw[:, DN_HEAD_DIM:].astype(BF16)

    chains = [(c, h) for c in range(tm // CHUNK) for h in range(DN_HEADS)]
    ms, rhss = zip(*[setup(c, h) for c, h in chains])
    rs = [first_stage(m) for m in ms]
    for _ in range(4):
        rs = [stage(r) for r in rs]
    ts = [last_stage(r) for r in rs]
    for (c, h), t, rhs in zip(chains, ts, rhss):
        solve(c, h, t, rhs)


def _delta_prep(q, k, v, gcol, grow):
    B, S, _ = q.shape
    tm = TM_PREP
    row = lambda width: pl.BlockSpec((1, tm, width), lambda b, t: (b, t, 0))
    return pl.pallas_call(
        _delta_prep_kernel,
        grid=(B, S // tm),
        in_specs=[row(DN_DIM), row(DN_DIM), row(DN_DIM), row(LANES),
                  pl.BlockSpec((1, tm // CHUNK, HALO, CHUNK), lambda b, t: (b, t, 0, 0))],
        out_specs=(row(DN_DIM), row(DN_DIM), row(DN_DIM), row(DN_DIM), row(DN_HEADS * CHUNK)),
        out_shape=(jax.ShapeDtypeStruct((B, S, DN_DIM), F32),
                   jax.ShapeDtypeStruct((B, S, DN_DIM), BF16),
                   jax.ShapeDtypeStruct((B, S, DN_DIM), BF16),
                   jax.ShapeDtypeStruct((B, S, DN_DIM), BF16),
                   jax.ShapeDtypeStruct((B, S, DN_HEADS * CHUNK), BF16)),
        compiler_params=pltpu.CompilerParams(
            dimension_semantics=("arbitrary", "arbitrary"), vmem_limit_bytes=VMEM_LIMIT),
        name="delta_prep",
    )(q, k, v, gcol, grow)


def _delta_scan_kernel(u_ref, w_ref, qd_ref, kd_ref, at_ref, zg_ref, grow_ref, ng_ref, o_ref, st_ref):
    nb, tm = u_ref.shape[0], u_ref.shape[1]

    @pl.when(pl.program_id(1) == 0)
    def _():
        st_ref[...] = jnp.zeros(st_ref.shape, F32)

    def step(c, carry):
        r0 = pl.multiple_of(c * CHUNK, CHUNK)
        rows = pl.ds(r0, CHUNK)
        chains = [(b, h) for b in range(nb) for h in range(DN_HEADS)]
        cols = lambda h: slice(h * DN_HEAD_DIM, (h + 1) * DN_HEAD_DIM)
        sts = [st_ref[b, h] for b, h in chains]
        st16s = [st.astype(BF16) for st in sts]
        wqs = [_dot(jnp.concatenate([w_ref[b, rows, cols(h)], qd_ref[b, rows, cols(h)]], axis=0), st16)
               for (b, h), st16 in zip(chains, st16s)]
        vn16s = [(u_ref[b, rows, cols(h)] - wq[:CHUNK]).astype(BF16) for (b, h), wq in zip(chains, wqs)]
        upd = [_dot_tn(kd_ref[b, rows, cols(h)], vn16) for (b, h), vn16 in zip(chains, vn16s)]
        avs = [_dot(at_ref[b, rows, h * CHUNK:(h + 1) * CHUNK], vn16) for (b, h), vn16 in zip(chains, vn16s)]
        for (b, h), st, up in zip(chains, sts, upd):
            gl = grow_ref[b, c, h + DN_HEADS:h + DN_HEADS + 1, CHUNK - 1:CHUNK]
            st_ref[b, h] = st * jnp.exp(gl) + up
        for (b, h), wq, av in zip(chains, wqs, avs):
            o = wq[CHUNK:] + av
            on = o * lax.rsqrt(jnp.mean(o * o, axis=-1, keepdims=True) + EPS) * ng_ref[...]
            o_ref[b, rows, cols(h)] = (on * zg_ref[b, rows, cols(h)]).astype(o_ref.dtype)
        return carry

    lax.fori_loop(0, tm // CHUNK, step, 0)


def _delta_scan(u, w, qd, kd, at, zg, grow, norm_g):
    B, S, _ = u.shape
    tm = TM_SCAN
    nb = SCAN_BATCH
    row = lambda width: pl.BlockSpec((nb, tm, width), lambda b, t: (b, t, 0))
    return pl.pallas_call(
        _delta_scan_kernel,
        grid=(B // nb, S // tm),
        in_specs=[row(DN_DIM), row(DN_DIM), row(DN_DIM), row(DN_DIM), row(DN_HEADS * CHUNK), row(DN_DIM),
                  pl.BlockSpec((nb, tm // CHUNK, HALO, CHUNK), lambda b, t: (b, t, 0, 0)),
                  pl.BlockSpec((1, DN_HEAD_DIM), lambda b, t: (0, 0))],
        out_specs=row(DN_DIM),
        out_shape=jax.ShapeDtypeStruct((B, S, DN_DIM), BF16),
        scratch_shapes=[pltpu.VMEM((nb, DN_HEADS, DN_HEAD_DIM, DN_HEAD_DIM), F32)],
        compiler_params=pltpu.CompilerParams(
            dimension_semantics=("arbitrary", "arbitrary"), vmem_limit_bytes=VMEM_LIMIT),
        name="delta_scan",
    )(u, w, qd, kd, at, zg, grow, norm_g)


def _outproj_kernel(ya_ref, yb_ref, x_ref, wo_ref, g_ref, rw_ref,
                    x1_ref, h2_ref, ri_ref, rg_ref, cnt_ref, base_ref):
    tm = x_ref.shape[0]
    sub = tm // OUT_SPLIT
    parts = range(OUT_SPLIT)
    rows_of = lambda s: slice(s * sub, (s + 1) * sub)

    @pl.when(pl.program_id(0) == 0)
    def _():
        base_ref[...] = jnp.zeros(base_ref.shape, F32)

    def project(s):
        rows = rows_of(s)
        return _dot(jnp.concatenate([ya_ref[rows, :], yb_ref[rows, :]], axis=-1), wo_ref[...])

    def normalise(s, y):
        rows = rows_of(s)
        x1 = x_ref[rows, :] + y
        x1_ref[rows, :] = x1
        h = x1 * lax.rsqrt(jnp.mean(x1 * x1, axis=-1, keepdims=True) + EPS) * g_ref[...]
        _store_rows(h2_ref.at[pl.ds(s * sub * ROW_SLAB, sub * ROW_SLAB)], h)
        return h

    def router_logits(h):
        h_hi, h_lo = _split_bf16(h, 2)
        hi_prod = _dot(h_hi, rw_ref[...])
        return hi_prod[:, :LANES] + (hi_prod[:, LANES:] + _dot(h_lo, rw_ref[:, :LANES]))

    lane = lax.broadcasted_iota(I32, (sub, LANES), 1)
    lanef = lane.astype(F32)
    rr = lax.broadcasted_iota(I32, (sub, sub), 0)
    cc = lax.broadcasted_iota(I32, (sub, sub), 1)
    tri = jnp.where(cc < rr, 1.0, 0.0).astype(BF16)

    def route(s, logits):
        rows = rows_of(s)
        neg = jnp.float32(-1e30)
        big = jnp.float32(1e9)
        is_g = lane < N_GROUPS
        gl = jnp.where(is_g, logits, neg)
        gmax = jnp.max(gl, axis=-1, keepdims=True)
        gidx = jnp.min(jnp.where(gl == gmax, lanef, big), axis=-1, keepdims=True)
        gsum = jnp.sum(jnp.where(is_g, jnp.exp(gl - gmax), 0.0), axis=-1, keepdims=True)
        gprob = 1.0 / gsum
        lo = N_GROUPS + EXPERTS_PER_GROUP * gidx
        emask = (lanef >= lo) & (lanef < lo + EXPERTS_PER_GROUP)
        el = jnp.where(emask, logits, neg)
        e1 = jnp.max(el, axis=-1, keepdims=True)
        i1 = jnp.min(jnp.where(el == e1, lanef, big), axis=-1, keepdims=True)
        el2 = jnp.where(lanef == i1, neg, el)
        e2 = jnp.max(el2, axis=-1, keepdims=True)
        i2 = jnp.min(jnp.where(el2 == e2, lanef, big), axis=-1, keepdims=True)
        r = jnp.exp(e2 - e1)
        gate1 = gprob / (1.0 + r)
        gate2 = gprob * r / (1.0 + r)
        id1 = i1 - N_GROUPS
        id2 = i2 - N_GROUPS

        oh1 = jnp.where(lanef == id1, 1.0, 0.0).astype(F32)
        oh2 = jnp.where(lanef == id2, 1.0, 0.0).astype(F32)
        oh = oh1 + oh2
        before = _dot(tri, oh.astype(BF16)) + base_ref[...]
        rank1 = jnp.sum(oh1 * before, axis=-1, keepdims=True)
        rank2 = jnp.sum(oh2 * before, axis=-1, keepdims=True)
        base_ref[...] = base_ref[...] + jnp.sum(oh, axis=0, keepdims=True)

        ri = jnp.where(lane == 0, id1, jnp.where(lane == 1, id2,
                       jnp.where(lane == 2, rank1, jnp.where(lane == 3, rank2, 0.0))))
        ri_ref[rows, :] = ri.astype(I32)
        rg_ref[rows, :] = jnp.where(lane == 0, gate1, jnp.where(lane == 1, gate2, 0.0))

    ys = [project(s) for s in parts]
    hs = [normalise(s, y) for s, y in zip(parts, ys)]
    ls = [router_logits(h) for h in hs]
    for s, logits in zip(parts, ls):
        route(s, logits)
    cnt_ref[...] = base_ref[...]


def _outproj(ya, yb, x, w_out16, ffn_g, rw):
    N, D = x.shape
    tm = TM_OUT
    row = lambda width: pl.BlockSpec((tm, width), lambda i: (i, 0))
    full = lambda shape: pl.BlockSpec(shape, lambda i: (0,) * len(shape))
    return pl.pallas_call(
        _outproj_kernel,
        grid=(N // tm,),
        in_specs=[row(CONV_A_DIM), row(DN_DIM), row(D), full((D, D)), full((1, D)), full((D, 2 * LANES))],
        out_specs=(row(D), pl.BlockSpec((tm * ROW_SLAB, LANES), lambda i: (i, 0)), row(LANES), row(LANES),
                   full((1, LANES))),
        out_shape=(jax.ShapeDtypeStruct((N, D), F32),
                   jax.ShapeDtypeStruct((N * ROW_SLAB, LANES), F32),
                   jax.ShapeDtypeStruct((N, LANES), I32),
                   jax.ShapeDtypeStruct((N, LANES), F32),
                   jax.ShapeDtypeStruct((1, LANES), F32)),
        scratch_shapes=[pltpu.VMEM((1, LANES), F32)],
        compiler_params=pltpu.CompilerParams(
            dimension_semantics=("arbitrary",), vmem_limit_bytes=VMEM_LIMIT),
        name="outproj",
    )(ya, yb, x, w_out16, ffn_g, rw)


def _slots_kernel(ri_ref, start_ref, o_ref):
    rif = ri_ref[...].astype(F32)
    lane = lax.broadcasted_iota(I32, rif.shape, 1)
    lanef = lane.astype(F32)

    def slot(k):
        start = jnp.sum(jnp.where(lanef == rif[:, k:k + 1], start_ref[...], 0.0), axis=-1, keepdims=True)
        return start + rif[:, TOP_K + k:TOP_K + k + 1]

    slab = jnp.where(lane == 0, slot(0), jnp.where(lane == 1, slot(1), 0.0))
    o_ref[...] = slab.T[0:HALO, :].astype(I32)


def _slots(ri, seg_start):
    N = ri.shape[0]
    tm = TM_SLOTS
    out = pl.pallas_call(
        _slots_kernel,
        grid=(N // tm,),
        in_specs=[pl.BlockSpec((tm, LANES), lambda i: (i, 0)), pl.BlockSpec((1, LANES), lambda i: (0, 0))],
        out_specs=pl.BlockSpec((HALO, tm), lambda i: (0, i)),
        out_shape=jax.ShapeDtypeStruct((HALO, N), I32),
        compiler_params=pltpu.CompilerParams(dimension_semantics=("arbitrary",)),
        name="slots",
    )(ri, _lane_vec(seg_start, 0))
    return out[:TOP_K].reshape(-1)


def _dispatch_kernel(seg_ref, slot0_ref, slot1_ref, h_ref, xs_ref, zero_ref, sem, zsem):
    tm = TM_DISPATCH
    bm = FFN_BLOCK
    n_blocks = xs_ref.shape[0] // (bm * ROW_SLAB)

    @pl.when(pl.program_id(0) == 0)
    def _():
        zero_ref[...] = jnp.zeros(zero_ref.shape, F32)

        def run_copy(first_row, n_rows):
            start = pl.multiple_of(first_row * ROW_SLAB, ROW_SLAB)
            return pltpu.make_async_copy(zero_ref.at[pl.ds(0, n_rows * ROW_SLAB)],
                                         xs_ref.at[pl.ds(start, n_rows * ROW_SLAB)], zsem)

        def block_copy(b):
            return pltpu.make_async_copy(zero_ref, _slab_block(xs_ref, b), zsem)

        def each_pad(fn):
            def per_expert(e, carry):
                row = seg_ref[e]
                n_pad = seg_ref[N_EXPERTS + e] - row
                run = bm // 2
                while run >= 1:
                    @pl.when((n_pad & run) != 0)
                    def _(row=row, run=run):
                        fn(run_copy(row, run))
                    row = row + (n_pad & run)
                    run //= 2
                return carry
            lax.fori_loop(0, N_EXPERTS, per_expert, 0)

            def per_block(b, c):
                fn(block_copy(b))
                return c
            lax.fori_loop(seg_ref[2 * N_EXPERTS], n_blocks, per_block, 0)

        each_pad(lambda cp: cp.start())
        each_pad(lambda cp: cp.wait())

    def issue(t, carry):
        for k, slot_ref in enumerate((slot0_ref, slot1_ref)):
            pltpu.make_async_copy(_slab(h_ref, t), _slab(xs_ref, slot_ref[t]), sem).start(priority=k)
        return carry

    lax.fori_loop(0, tm, issue, 0, unroll=ISSUE_UNROLL)
    for _ in range(TOP_K):
        pltpu.make_async_copy(h_ref, xs_ref.at[pl.ds(0, tm * ROW_SLAB)], sem).wait()


def _dispatch(seg, slots, h2, n_slots):
    N = h2.shape[0] // ROW_SLAB
    tm = TM_DISPATCH
    grid_spec = pltpu.PrefetchScalarGridSpec(
        num_scalar_prefetch=1,
        grid=(N // tm,),
        in_specs=[pl.BlockSpec((tm,), lambda i, s: (i,), memory_space=pltpu.SMEM),
                  pl.BlockSpec((tm,), lambda i, s: (N // tm + i,), memory_space=pltpu.SMEM),
                  pl.BlockSpec((tm * ROW_SLAB, LANES), lambda i, s: (i, 0))],
        out_specs=pl.BlockSpec(memory_space=pl.ANY),
        scratch_shapes=[pltpu.VMEM((FFN_BLOCK * ROW_SLAB, LANES), F32),
                        pltpu.SemaphoreType.DMA(()), pltpu.SemaphoreType.DMA(())],
    )
    return pl.pallas_call(
        _dispatch_kernel,
        grid_spec=grid_spec,
        out_shape=jax.ShapeDtypeStruct((n_slots * ROW_SLAB, LANES), F32),
        compiler_params=pltpu.CompilerParams(dimension_semantics=("arbitrary",)),
        name="dispatch",
    )(seg, slots, slots, h2)


def _ffn_kernel(blk_ref, xs_ref, wg_ref, wu_ref, wd_ref, ys_ref,
                xbuf, ybuf, wg16, wu16, wd16, xsem, ysem, zsem):
    e = pl.program_id(0)
    bm = FFN_BLOCK
    ring = FFN_RING
    n_blocks = ys_ref.shape[0] // (bm * ROW_SLAB)
    first = blk_ref[e]
    count = blk_ref[N_EXPERTS + e]
    n_used = blk_ref[2 * N_EXPERTS]

    def x_copy(g):
        return pltpu.make_async_copy(_slab_block(xs_ref, g), xbuf.at[g % ring], xsem.at[g % ring])

    def y_copy(g):
        return pltpu.make_async_copy(ybuf.at[g % ring], _slab_block(ys_ref, g), ysem.at[g % ring])

    def when_block(g, fn):
        @pl.when((g >= 0) & (g < n_used))
        def _():
            fn(g)

    @pl.when(e == 0)
    def _():
        for g in range(ring - 1):
            when_block(g, lambda g: x_copy(g).start(priority=RING_PRIORITY))

    @pl.when(count > 0)
    def _():
        wg16[...] = wg_ref[0].astype(BF16)
        wu16[...] = wu_ref[0].astype(BF16)
        wd16[...] = wd_ref[0].astype(BF16)

    def block(j, carry):
        g = first + j
        x_copy(g).wait()
        when_block(g + ring - 1, lambda g: x_copy(g).start(priority=RING_PRIORITY))

        x = _load_rows(xbuf.at[g % ring], bm).astype(BF16)
        sub = bm // FFN_SPLIT
        x_parts = [x[s * sub:(s + 1) * sub] for s in range(FFN_SPLIT)]
        gates = [_dot(xp, wg16[...]) for xp in x_parts]
        ups = [_dot(xp, wu16[...]) for xp in x_parts]
        acts = [(_silu(a) * b).astype(BF16) for a, b in zip(gates, ups)]
        y = jnp.concatenate([_dot(act, wd16[...]) for act in acts], axis=0)

        when_block(g - ring, lambda g: y_copy(g).wait())
        _store_rows(ybuf.at[g % ring], y)
        y_copy(g).start(priority=RING_PRIORITY)
        return carry

    lax.fori_loop(0, count, block, 0)

    @pl.when(e == pl.num_programs(0) - 1)
    def _():
        for back in range(ring, 0, -1):
            when_block(n_used - back, lambda g: y_copy(g).wait())
        ybuf[0] = jnp.zeros(ybuf.shape[1:], F32)

        def zero_copy(g):
            return pltpu.make_async_copy(ybuf.at[0], _slab_block(ys_ref, g), zsem)

        def start(g, c):
            zero_copy(g).start()
            return c

        def wait(g, c):
            zero_copy(g).wait()
            return c

        lax.fori_loop(n_used, n_blocks, start, 0)
        lax.fori_loop(n_used, n_blocks, wait, 0)


def _ffn(blk, xs, w_gate, w_up, w_down):
    bm = FFN_BLOCK
    D = D_MODEL
    weights = lambda shape: pl.BlockSpec((1,) + shape, lambda e, blk: (e, 0, 0))
    grid_spec = pltpu.PrefetchScalarGridSpec(
        num_scalar_prefetch=1,
        grid=(N_EXPERTS,),
        in_specs=[pl.BlockSpec(memory_space=pl.ANY),
                  weights((D, EXPERT_FF)), weights((D, EXPERT_FF)), weights((EXPERT_FF, D))],
        out_specs=pl.BlockSpec(memory_space=pl.ANY),
        scratch_shapes=[pltpu.VMEM((FFN_RING, bm * ROW_SLAB, LANES), F32),
                        pltpu.VMEM((FFN_RING, bm * ROW_SLAB, LANES), F32),
                        pltpu.VMEM((D, EXPERT_FF), BF16), pltpu.VMEM((D, EXPERT_FF), BF16),
                        pltpu.VMEM((EXPERT_FF, D), BF16),
                        pltpu.SemaphoreType.DMA((FFN_RING,)), pltpu.SemaphoreType.DMA((FFN_RING,)),
                        pltpu.SemaphoreType.DMA(())],
    )
    return pl.pallas_call(
        _ffn_kernel,
        grid_spec=grid_spec,
        out_shape=jax.ShapeDtypeStruct(xs.shape, F32),
        compiler_params=pltpu.CompilerParams(
            dimension_semantics=("arbitrary",), vmem_limit_bytes=VMEM_LIMIT),
        name="ffn",
    )(blk, xs, w_gate, w_up, w_down)


def _combine_kernel(slot0_ref, slot1_ref, next0_ref, next1_ref, ys_ref, x1_ref, rg_ref, g_ref, o_ref,
                    buf_ref, sems):
    tm = TM_COMBINE
    i = pl.program_id(0)
    half = i % 2

    def gather(slot_refs, s):
        def issue(t, carry):
            for k, slot_ref in enumerate(slot_refs):
                pltpu.make_async_copy(_slab(ys_ref, slot_ref[t]), _slab(buf_ref.at[s, k], t),
                                      sems.at[s]).start(priority=k)
            return carry
        lax.fori_loop(0, tm, issue, 0, unroll=ISSUE_UNROLL)

    @pl.when(i == 0)
    def _():
        gather((slot0_ref, slot1_ref), 0)

    @pl.when(i + 1 < pl.num_programs(0))
    def _():
        gather((next0_ref, next1_ref), 1 - half)

    for k in range(TOP_K):
        pltpu.make_async_copy(ys_ref.at[pl.ds(0, tm * ROW_SLAB)], buf_ref.at[half, k], sems.at[half]).wait()

    rg = rg_ref[...]
    moe = (_load_rows(buf_ref.at[half, 0], tm) * rg[:, 0:1]
           + _load_rows(buf_ref.at[half, 1], tm) * rg[:, 1:2])
    x2 = x1_ref[...] + moe
    o_ref[...] = x2 * lax.rsqrt(jnp.mean(x2 * x2, axis=-1, keepdims=True) + EPS) * g_ref[...]


def _combine(slots, ys, x1, rg, final_g):
    N, D = x1.shape
    tm = TM_COMBINE
    n_tiles = N // tm

    def slot_block(k, ahead, i):
        return (k * n_tiles + jnp.minimum(i + ahead, n_tiles - 1),)

    return pl.pallas_call(
        _combine_kernel,
        grid=(n_tiles,),
        in_specs=[pl.BlockSpec((tm,), functools.partial(slot_block, k, ahead), memory_space=pltpu.SMEM)
                  for ahead in (0, 1) for k in range(TOP_K)]
                 + [pl.BlockSpec(memory_space=pl.ANY),
                  pl.BlockSpec((tm, D), lambda i: (i, 0)),
                  pl.BlockSpec((tm, LANES), lambda i: (i, 0)),
                  pl.BlockSpec((1, D), lambda i: (0, 0))],
        out_specs=pl.BlockSpec((tm, D), lambda i: (i, 0)),
        out_shape=jax.ShapeDtypeStruct((N, D), F32),
        scratch_shapes=[pltpu.VMEM((2, TOP_K, tm * ROW_SLAB, LANES), F32), pltpu.SemaphoreType.DMA((2,))],
        compiler_params=pltpu.CompilerParams(
            dimension_semantics=("arbitrary",), vmem_limit_bytes=VMEM_LIMIT),
        name="combine",
    )(slots, slots, slots, slots, ys, x1, rg, final_g)


def _lane_vec(values, offset):
    return jnp.zeros((1, LANES), F32).at[0, offset:offset + values.shape[0]].set(values.astype(F32))


def kernel(x, mix_norm_g, w_in, conv_a_w, conv_a_norm_g, dn_conv_w, dn_a_log, dn_dt_bias, dn_norm_g,
           w_out, ffn_norm_g, router_group_w, router_expert_w, w_gate, w_up, w_down, final_norm_g):
    B, S, D = x.shape
    N = B * S
    depth = w_in.shape[0]
    assert depth == 1, "single-layer block: the final RMSNorm is fused into the layer's combine step"
    group_of = jnp.arange(CONV_A_DIM, dtype=I32) // CONV_A_GROUP_DIM
    gmat = jnp.where(group_of[:, None] == group_of[None, :], 1.0 / CONV_A_GROUP_DIM, 0.0).astype(BF16)
    bm = FFN_BLOCK
    n_blocks = (N * TOP_K) // bm + N_EXPERTS
    for l in range(depth):
        ya, q, k, v, zg, gcol, grow = _inproj(
            x, mix_norm_g[l][None, :], w_in[l].astype(BF16), conv_a_w[l], conv_a_norm_g[l][None, :], dn_conv_w[l],
            _lane_vec(dn_a_log[l], DN_HEADS), _lane_vec(dn_dt_bias[l], DN_HEADS), gmat)
        u, w, qd, kd, at = _delta_prep(q, k, v, gcol, grow)
        yb = _delta_scan(u, w, qd, kd, at, zg, grow, dn_norm_g[l][None, :])
        rw = jnp.pad(jnp.concatenate([router_group_w[l], router_expert_w[l]], axis=1),
                     ((0, 0), (0, LANES - N_GROUPS - N_EXPERTS)))
        rw_hi = rw.astype(BF16)
        rw = jnp.concatenate([rw_hi, (rw - rw_hi.astype(F32)).astype(BF16)], axis=1)
        x1, h2, ri, rg, cnt = _outproj(ya.reshape(N, CONV_A_DIM), yb.reshape(N, DN_DIM), x.reshape(N, D),
                                       w_out[l].astype(BF16), ffn_norm_g[l][None, :], rw)
        counts = cnt[0, :N_EXPERTS].astype(I32)
        padded = (counts + bm - 1) // bm * bm
        seg_end = jnp.cumsum(padded).astype(I32)
        seg_start = seg_end - padded
        n_used = (seg_end[-1:] // bm)
        slots = _slots(ri, seg_start)
        seg = jnp.concatenate([seg_start + counts, seg_end, n_used])
        xs = _dispatch(seg, slots, h2, n_blocks * bm)
        blk = jnp.concatenate([seg_start // bm, padded // bm, n_used])
        ys = _ffn(blk, xs, w_gate[l], w_up[l], w_down[l])
        x = _combine(slots, ys, x1, rg, final_norm_g[None, :]).reshape(B, S, D)
    return x
```

```python
import functools

import jax
import jax.numpy as jnp
from jax import lax
from jax.experimental import pallas as pl
from jax.experimental.pallas import tpu as pltpu

F32 = jnp.float32
BF16 = jnp.bfloat16
I32 = jnp.int32

D_MODEL = 1024
CHUNK = 64
CONV_A_GROUP_DIM = 64
CONV_A_DIM = 512
CONV_A_WIDTH = 3
DN_HEADS = 4
DN_HEAD_DIM = 128
DN_DIM = 512
DN_CONV_WIDTH = 4
IN_PROJ_DIM = 3 * CONV_A_DIM + 4 * DN_DIM + 2 * DN_HEADS
N_GROUPS = 4
EXPERTS_PER_GROUP = 8
N_EXPERTS = 32
TOP_K = 2
EXPERT_FF = 512
EPS = 1e-6

LANES = 128
HALO = 8
GATE_COL = 3 * CONV_A_DIM + 4 * DN_DIM

TM_IN = 512
TM_PREP = 256
SCAN_BATCH = 8
TM_SCAN = 256
TM_OUT = 1024
OUT_SPLIT = 8
FFN_BLOCK = 256
TM_SLOTS = 2048
TM_DISPATCH = 2048
TM_COMBINE = 512
FFN_SPLIT = 2
FFN_RING = 4
RING_PRIORITY = 1
ISSUE_UNROLL = 4
VMEM_LIMIT = 56 * 1024 * 1024


def _dot(a, b):
    return jnp.dot(a, b, preferred_element_type=F32)


def _dot_nt(a, b):
    return lax.dot_general(a, b, (((1,), (1,)), ((), ())), preferred_element_type=F32)


def _dot_tn(a, b):
    return lax.dot_general(a, b, (((0,), (0,)), ((), ())), preferred_element_type=F32)


def _split_bf16(x, parts):
    out = []
    for _ in range(parts):
        p = x.astype(BF16)
        out.append(p)
        x = x - p.astype(F32)
    return out


def _silu(x):
    return x * jax.nn.sigmoid(x)


ROW_SLAB = D_MODEL // LANES


def _store_rows(ref, val):
    m = val.shape[0]
    for c in range(ROW_SLAB):
        ref[pl.ds(c, m, stride=ROW_SLAB), :] = val[:, c * LANES:(c + 1) * LANES]


def _load_rows(ref, m):
    return jnp.concatenate([ref[pl.ds(c, m, stride=ROW_SLAB), :] for c in range(ROW_SLAB)], axis=-1)


def _slab(ref, row):
    return ref.at[pl.ds(pl.multiple_of(row * ROW_SLAB, ROW_SLAB), ROW_SLAB)]


def _slab_block(ref, block):
    n = FFN_BLOCK * ROW_SLAB
    return ref.at[pl.ds(pl.multiple_of(block * n, n), n)]


def _causal_conv(ext, w):
    taps = w.shape[0]
    delayed = pltpu.roll(ext, 1, 0)
    if taps == 4:
        near = w[3:4, :] * ext[HALO:] + w[2:3, :] * delayed[HALO:]
        far = w[1:2, :] * ext + w[0:1, :] * delayed
        return near + pltpu.roll(far, 2, 0)[HALO:]
    assert taps == 3
    acc = w[0:1, :] * pltpu.roll(ext, 2, 0)[HALO:]
    acc = acc + w[1:2, :] * delayed[HALO:]
    return acc + w[2:3, :] * ext[HALO:]


def _softplus(x):
    return jnp.maximum(x, 0.0) + jnp.log1p(jnp.exp(-jnp.abs(x)))


def _inproj_kernel(x_ref, g_ref, w_ref, caw_ref, cag_ref, dcw_ref, alog_ref, dtb_ref, gmat_ref,
                   ya_ref, q_ref, k_ref, v_ref, zg_ref, gcol_ref, grow_ref,
                   exta_ref, *extq_refs):
    tm = x_ref.shape[1]
    ext_refs = (exta_ref,) + extq_refs

    @pl.when(pl.program_id(1) == 0)
    def _():
        for ext_ref in ext_refs:
            ext_ref[0:HALO, :] = jnp.zeros((HALO, ext_ref.shape[1]), F32)

    x = x_ref[0]
    ms = jnp.mean(x * x, axis=-1, keepdims=True)
    hb = (x * lax.rsqrt(ms + EPS) * g_ref[...]).astype(BF16)

    def proj(c0, width):
        return _dot(hb, w_ref[:, c0:c0 + width])

    base = 3 * CONV_A_DIM

    def qkv_project(i):
        extq_refs[i][HALO:HALO + tm, :] = proj(base + i * DN_DIM, DN_DIM)

    def qkv_finish(i, out_ref):
        cols = slice(i * DN_DIM, (i + 1) * DN_DIM)
        s = _silu(_causal_conv(extq_refs[i][...], dcw_ref[:, cols]))
        if i == 2:
            out_ref[0] = s
        else:
            for h in range(DN_HEADS):
                sh = s[:, h * DN_HEAD_DIM:(h + 1) * DN_HEAD_DIM]
                inv = lax.rsqrt(jnp.sum(sh * sh, axis=-1, keepdims=True) + EPS)
                sh = sh * inv
                if i == 0:
                    sh = sh * (DN_HEAD_DIM ** -0.5)
                out_ref[0, :, h * DN_HEAD_DIM:(h + 1) * DN_HEAD_DIM] = sh

    def mixer_a_finish(a_b):
        y = a_b * _causal_conv(exta_ref[...], caw_ref[...])
        ysq = y * y
        hi = ysq.astype(BF16)
        lo = (ysq - hi.astype(F32)).astype(BF16)
        gmean = _dot(hi, gmat_ref[...]) + _dot(lo, gmat_ref[...])
        ya_ref[0] = (y * lax.rsqrt(gmean + EPS) * cag_ref[...]).astype(ya_ref.dtype)

    qkv_project(0)
    qkv_project(1)
    exta_ref[HALO:HALO + tm, :] = proj(2 * CONV_A_DIM, CONV_A_DIM) * proj(0, CONV_A_DIM)
    qkv_finish(0, q_ref)
    qkv_project(2)
    a_b = proj(CONV_A_DIM, CONV_A_DIM)
    qkv_finish(1, k_ref)
    z = proj(base + 3 * DN_DIM, DN_DIM)
    mixer_a_finish(a_b)
    n_gate = IN_PROJ_DIM - GATE_COL
    p = jnp.concatenate([proj(GATE_COL, n_gate), jnp.zeros((tm, LANES - n_gate), F32)], axis=-1)
    qkv_finish(2, v_ref)
    zg_ref[0] = _silu(z)
    for ext_ref in ext_refs:
        ext_ref[0:HALO, :] = ext_ref[tm:tm + HALO, :]

    beta = jax.nn.sigmoid(p)
    g = -jnp.exp(alog_ref[...]) * _softplus(p + dtb_ref[...])
    r = lax.broadcasted_iota(I32, (CHUNK, CHUNK), 0)
    c = lax.broadcasted_iota(I32, (CHUNK, CHUNK), 1)
    tri = jnp.where(c <= r, 1.0, 0.0).astype(BF16)
    parts = _split_bf16(g, 3)
    gc = jnp.concatenate(
        [sum(_dot(tri, p[ci * CHUNK:(ci + 1) * CHUNK]) for p in parts) for ci in range(tm // CHUNK)], axis=0)
    lane = lax.broadcasted_iota(I32, (tm, LANES), 1)
    slab = jnp.where(lane < DN_HEADS, beta, gc)
    gcol_ref[0] = slab
    rows = slab.T[0:HALO, :]
    for ci in range(tm // CHUNK):
        grow_ref[0, ci] = rows[:, ci * CHUNK:(ci + 1) * CHUNK]


def _inproj(x, mix_g, w_in16, conv_a_w, conv_a_g, dn_conv_w, alog_vec, dtb_vec, gmat):
    B, S, D = x.shape
    tm = TM_IN
    full = lambda shape: pl.BlockSpec(shape, lambda b, t: (0,) * len(shape))
    row = lambda width: pl.BlockSpec((1, tm, width), lambda b, t: (b, t, 0))
    out_shape = (
        jax.ShapeDtypeStruct((B, S, CONV_A_DIM), BF16),
        jax.ShapeDtypeStruct((B, S, DN_DIM), F32),
        jax.ShapeDtypeStruct((B, S, DN_DIM), F32),
        jax.ShapeDtypeStruct((B, S, DN_DIM), F32),
        jax.ShapeDtypeStruct((B, S, DN_DIM), F32),
        jax.ShapeDtypeStruct((B, S, LANES), F32),
        jax.ShapeDtypeStruct((B, S // CHUNK, HALO, CHUNK), F32),
    )
    return pl.pallas_call(
        _inproj_kernel,
        grid=(B, S // tm),
        in_specs=[row(D), full((1, D)), full((D, IN_PROJ_DIM)), full((CONV_A_WIDTH, CONV_A_DIM)),
                  full((1, CONV_A_DIM)), full((DN_CONV_WIDTH, 3 * DN_DIM)), full((1, LANES)),
                  full((1, LANES)), full((CONV_A_DIM, CONV_A_DIM))],
        out_specs=(row(CONV_A_DIM), row(DN_DIM), row(DN_DIM), row(DN_DIM), row(DN_DIM), row(LANES),
                   pl.BlockSpec((1, tm // CHUNK, HALO, CHUNK), lambda b, t: (b, t, 0, 0))),
        out_shape=out_shape,
        scratch_shapes=[pltpu.VMEM((tm + HALO, CONV_A_DIM), F32)] + [pltpu.VMEM((tm + HALO, DN_DIM), F32)] * 3,
        compiler_params=pltpu.CompilerParams(
            dimension_semantics=("arbitrary", "arbitrary"), vmem_limit_bytes=VMEM_LIMIT),
        name="inproj",
    )(x, mix_g, w_in16, conv_a_w, conv_a_g, dn_conv_w, alog_vec, dtb_vec, gmat)


def _delta_prep_kernel(q_ref, k_ref, v_ref, gcol_ref, grow_ref,
                       u_ref, w_ref, qd_ref, kd_ref, at_ref):
    tm = q_ref.shape[1]
    ri = lax.broadcasted_iota(I32, (CHUNK, CHUNK), 0)
    ci = lax.broadcasted_iota(I32, (CHUNK, CHUNK), 1)
    causal = ci <= ri
    strict = ci < ri
    eye = jnp.where(ci == ri, 1.0, 0.0).astype(F32)
    lane = lax.broadcasted_iota(I32, (CHUNK, LANES), 1)
    left_half = lane < CHUNK

    def setup(c, h):
        rows = slice(c * CHUNK, (c + 1) * CHUNK)
        cols = slice(h * DN_HEAD_DIM, (h + 1) * DN_HEAD_DIM)
        slab = gcol_ref[0, rows, :]
        beta = jnp.sum(jnp.where(lane == h, slab, 0.0), axis=-1, keepdims=True)
        gcc = jnp.sum(jnp.where(lane == h + DN_HEADS, slab, 0.0), axis=-1, keepdims=True)
        gcr = grow_ref[0, c, h + DN_HEADS:h + DN_HEADS + 1, :]
        diff = gcc - gcr
        decay = jnp.where(causal, jnp.exp(jnp.where(causal, diff, 0.0)), 0.0)
        q = q_ref[0, rows, cols]
        k = k_ref[0, rows, cols]
        v = v_ref[0, rows, cols]
        kb = k * beta
        kq = _dot_nt(jnp.concatenate([kb.astype(BF16), q.astype(BF16)], axis=0), k.astype(BF16))
        L = jnp.where(strict, kq[:CHUNK] * decay, 0.0)
        intra = kq[CHUNK:] * decay
        egc = jnp.exp(gcc)
        gl = gcr[:, CHUNK - 1:CHUNK]
        qd_ref[0, rows, cols] = (q * egc).astype(BF16)
        kd_ref[0, rows, cols] = (k * jnp.exp(gl - gcc)).astype(BF16)
        at_ref[0, rows, h * CHUNK:(h + 1) * CHUNK] = intra.astype(BF16)
        return -L, jnp.concatenate([v * beta, kb * egc], axis=-1).astype(BF16)

    def first_stage(m):
        m16 = m.astype(BF16)
        return jnp.concatenate([_dot(m16, m16), eye + m], axis=-1)

    def stage(r):
        return _dot(r[:, :CHUNK].astype(BF16), r.astype(BF16)) + jnp.where(left_half, 0.0, r)

    def last_stage(r):
        return (stage(r))[:, CHUNK:]

    def solve(c, h, t, rhs):
        rows = slice(c * CHUNK, (c + 1) * CHUNK)
        cols = slice(h * DN_HEAD_DIM, (h + 1) * DN_HEAD_DIM)
        uw = _dot(t.astype(BF16), rhs)
        u_ref[0, rows, cols] = uw[:, :DN_HEAD_DIM]
        w_ref[0, rows, cols] = uw[:, DN_HEAD_DIM:].astype(BF16)

    chains = [(c, h) for c in range(tm // CHUNK) for h in range(DN_HEADS)]
    ms, rhss = zip(*[setup(c, h) for c, h in chains])
    rs = [first_stage(m) for m in ms]
    for _ in range(4):
        rs = [stage(r) for r in rs]
    ts = [last_stage(r) for r in rs]
    for (c, h), t, rhs in zip(chains, ts, rhss):
        solve(c, h, t, rhs)


def _delta_prep(q, k, v, gcol, grow):
    B, S, _ = q.shape
    tm = TM_PREP
    row = lambda width: pl.BlockSpec((1, tm, width), lambda b, t: (b, t, 0))
    return pl.pallas_call(
        _delta_prep_kernel,
        grid=(B, S // tm),
        in_specs=[row(DN_DIM), row(DN_DIM), row(DN_DIM), row(LANES),
                  pl.BlockSpec((1, tm // CHUNK, HALO, CHUNK), lambda b, t: (b, t, 0, 0))],
        out_specs=(row(DN_DIM), row(DN_DIM), row(DN_DIM), row(DN_DIM), row(DN_HEADS * CHUNK)),
        out_shape=(jax.ShapeDtypeStruct((B, S, DN_DIM), F32),
                   jax.ShapeDtypeStruct((B, S, DN_DIM), BF16),
                   jax.ShapeDtypeStruct((B, S, DN_DIM), BF16),
                   jax.ShapeDtypeStruct((B, S, DN_DIM), BF16),
                   jax.ShapeDtypeStruct((B, S, DN_HEADS * CHUNK), BF16)),
        compiler_params=pltpu.CompilerParams(
            dimension_semantics=("arbitrary", "arbitrary"), vmem_limit_bytes=VMEM_LIMIT),
        name="delta_prep",
    )(q, k, v, gcol, grow)


def _delta_scan_kernel(u_ref, w_ref, qd_ref, kd_ref, at_ref, zg_ref, grow_ref, ng_ref, o_ref, st_ref):
    nb, tm = u_ref.shape[0], u_ref.shape[1]

    @pl.when(pl.program_id(1) == 0)
    def _():
        st_ref[...] = jnp.zeros(st_ref.shape, F32)

    def step(c, carry):
        r0 = pl.multiple_of(c * CHUNK, CHUNK)
        rows = pl.ds(r0, CHUNK)
        chains = [(b, h) for b in range(nb) for h in range(DN_HEADS)]
        cols = lambda h: slice(h * DN_HEAD_DIM, (h + 1) * DN_HEAD_DIM)
        sts = [st_ref[b, h] for b, h in chains]
        st16s = [st.astype(BF16) for st in sts]
        wqs = [_dot(jnp.concatenate([w_ref[b, rows, cols(h)], qd_ref[b, rows, cols(h)]], axis=0), st16)
               for (b, h), st16 in zip(chains, st16s)]
        vn16s = [(u_ref[b, rows, cols(h)] - wq[:CHUNK]).astype(BF16) for (b, h), wq in zip(chains, wqs)]
        upd = [_dot_tn(kd_ref[b, rows, cols(h)], vn16) for (b, h), vn16 in zip(chains, vn16s)]
        avs = [_dot(at_ref[b, rows, h * CHUNK:(h + 1) * CHUNK], vn16) for (b, h), vn16 in zip(chains, vn16s)]
        for (b, h), st, up in zip(chains, sts, upd):
            gl = grow_ref[b, c, h + DN_HEADS:h + DN_HEADS + 1, CHUNK - 1:CHUNK]
            st_ref[b, h] = st * jnp.exp(gl) + up
        for (b, h), wq, av in zip(chains, wqs, avs):
            o = wq[CHUNK:] + av
            on = o * lax.rsqrt(jnp.mean(o * o, axis=-1, keepdims=True) + EPS) * ng_ref[...]
            o_ref[b, rows, cols(h)] = (on * zg_ref[b, rows, cols(h)]).astype(o_ref.dtype)
        return carry

    lax.fori_loop(0, tm // CHUNK, step, 0)


def _delta_scan(u, w, qd, kd, at, zg, grow, norm_g):
    B, S, _ = u.shape
    tm = TM_SCAN
    nb = SCAN_BATCH
    row = lambda width: pl.BlockSpec((nb, tm, width), lambda b, t: (b, t, 0))
    return pl.pallas_call(
        _delta_scan_kernel,
        grid=(B // nb, S // tm),
        in_specs=[row(DN_DIM), row(DN_DIM), row(DN_DIM), row(DN_DIM), row(DN_HEADS * CHUNK), row(DN_DIM),
                  pl.BlockSpec((nb, tm // CHUNK, HALO, CHUNK), lambda b, t: (b, t, 0, 0)),
                  pl.BlockSpec((1, DN_HEAD_DIM), lambda b, t: (0, 0))],
        out_specs=row(DN_DIM),
        out_shape=jax.ShapeDtypeStruct((B, S, DN_DIM), BF16),
        scratch_shapes=[pltpu.VMEM((nb, DN_HEADS, DN_HEAD_DIM, DN_HEAD_DIM), F32)],
        compiler_params=pltpu.CompilerParams(
            dimension_semantics=("arbitrary", "arbitrary"), vmem_limit_bytes=VMEM_LIMIT),
        name="delta_scan",
    )(u, w, qd, kd, at, zg, grow, norm_g)


def _outproj_kernel(ya_ref, yb_ref, x_ref, wo_ref, g_ref, rw_ref,
                    x1_ref, h2_ref, ri_ref, rg_ref, cnt_ref, base_ref):
    tm = x_ref.shape[0]
    sub = tm // OUT_SPLIT
    parts = range(OUT_SPLIT)
    rows_of = lambda s: slice(s * sub, (s + 1) * sub)

    @pl.when(pl.program_id(0) == 0)
    def _():
        base_ref[...] = jnp.zeros(base_ref.shape, F32)

    def project(s):
        rows = rows_of(s)
        return _dot(jnp.concatenate([ya_ref[rows, :], yb_ref[rows, :]], axis=-1), wo_ref[...])

    def normalise(s, y):
        rows = rows_of(s)
        x1 = x_ref[rows, :] + y
        x1_ref[rows, :] = x1
        h = x1 * lax.rsqrt(jnp.mean(x1 * x1, axis=-1, keepdims=True) + EPS) * g_ref[...]
        _store_rows(h2_ref.at[pl.ds(s * sub * ROW_SLAB, sub * ROW_SLAB)], h)
        return h

    def router_logits(h):
        h_hi, h_lo = _split_bf16(h, 2)
        hi_prod = _dot(h_hi, rw_ref[...])
        return hi_prod[:, :LANES] + (hi_prod[:, LANES:] + _dot(h_lo, rw_ref[:, :LANES]))

    lane = lax.broadcasted_iota(I32, (sub, LANES), 1)
    lanef = lane.astype(F32)
    rr = lax.broadcasted_iota(I32, (sub, sub), 0)
    cc = lax.broadcasted_iota(I32, (sub, sub), 1)
    tri = jnp.where(cc < rr, 1.0, 0.0).astype(BF16)

    def route(s, logits):
        rows = rows_of(s)
        neg = jnp.float32(-1e30)
        big = jnp.float32(1e9)
        is_g = lane < N_GROUPS
        gl = jnp.where(is_g, logits, neg)
        gmax = jnp.max(gl, axis=-1, keepdims=True)
        gidx = jnp.min(jnp.where(gl == gmax, lanef, big), axis=-1, keepdims=True)
        gsum = jnp.sum(jnp.where(is_g, jnp.exp(gl - gmax), 0.0), axis=-1, keepdims=True)
        gprob = 1.0 / gsum
        lo = N_GROUPS + EXPERTS_PER_GROUP * gidx
        emask = (lanef >= lo) & (lanef < lo + EXPERTS_PER_GROUP)
        el = jnp.where(emask, logits, neg)
        e1 = jnp.max(el, axis=-1, keepdims=True)
        i1 = jnp.min(jnp.where(el == e1, lanef, big), axis=-1, keepdims=True)
        el2 = jnp.where(lanef == i1, neg, el)
        e2 = jnp.max(el2, axis=-1, keepdims=True)
        i2 = jnp.min(jnp.where(el2 == e2, lanef, big), axis=-1, keepdims=True)
        r = jnp.exp(e2 - e1)
        gate1 = gprob / (1.0 + r)
        gate2 = gprob * r / (1.0 + r)
        id1 = i1 - N_GROUPS
        id2 = i2 - N_GROUPS

        oh1 = jnp.where(lanef == id1, 1.0, 0.0).astype(F32)
        oh2 = jnp.where(lanef == id2, 1.0, 0.0).astype(F32)
        oh = oh1 + oh2
        before = _dot(tri, oh.astype(BF16)) + base_ref[...]
        rank1 = jnp.sum(oh1 * before, axis=-1, keepdims=True)
        rank2 = jnp.sum(oh2 * before, axis=-1, keepdims=True)
        base_ref[...] = base_ref[...] + jnp.sum(oh, axis=0, keepdims=True)

        ri = jnp.where(lane == 0, id1, jnp.where(lane == 1, id2,
                       jnp.where(lane == 2, rank1, jnp.where(lane == 3, rank2, 0.0))))
        ri_ref[rows, :] = ri.astype(I32)
        rg_ref[rows, :] = jnp.where(lane == 0, gate1, jnp.where(lane == 1, gate2, 0.0))

    ys = [project(s) for s in parts]
    hs = [normalise(s, y) for s, y in zip(parts, ys)]
    ls = [router_logits(h) for h in hs]
    for s, logits in zip(parts, ls):
        route(s, logits)
    cnt_ref[...] = base_ref[...]


def _outproj(ya, yb, x, w_out16, ffn_g, rw):
    N, D = x.shape
    tm = TM_OUT
    row = lambda width: pl.BlockSpec((tm, width), lambda i: (i, 0))
    full = lambda shape: pl.BlockSpec(shape, lambda i: (0,) * len(shape))
    return pl.pallas_call(
        _outproj_kernel,
        grid=(N // tm,),
        in_specs=[row(CONV_A_DIM), row(DN_DIM), row(D), full((D, D)), full((1, D)), full((D, 2 * LANES))],
        out_specs=(row(D), pl.BlockSpec((tm * ROW_SLAB, LANES), lambda i: (i, 0)), row(LANES), row(LANES),
                   full((1, LANES))),
        out_shape=(jax.ShapeDtypeStruct((N, D), F32),
                   jax.ShapeDtypeStruct((N * ROW_SLAB, LANES), F32),
                   jax.ShapeDtypeStruct((N, LANES), I32),
                   jax.ShapeDtypeStruct((N, LANES), F32),
                   jax.ShapeDtypeStruct((1, LANES), F32)),
        scratch_shapes=[pltpu.VMEM((1, LANES), F32)],
        compiler_params=pltpu.CompilerParams(
            dimension_semantics=("arbitrary",), vmem_limit_bytes=VMEM_LIMIT),
        name="outproj",
    )(ya, yb, x, w_out16, ffn_g, rw)


def _slots_kernel(ri_ref, start_ref, o_ref):
    rif = ri_ref[...].astype(F32)
    lane = lax.broadcasted_iota(I32, rif.shape, 1)
    lanef = lane.astype(F32)

    def slot(k):
        start = jnp.sum(jnp.where(lanef == rif[:, k:k + 1], start_ref[...], 0.0), axis=-1, keepdims=True)
        return start + rif[:, TOP_K + k:TOP_K + k + 1]

    slab = jnp.where(lane == 0, slot(0), jnp.where(lane == 1, slot(1), 0.0))
    o_ref[...] = slab.T[0:HALO, :].astype(I32)


def _slots(ri, seg_start):
    N = ri.shape[0]
    tm = TM_SLOTS
    out = pl.pallas_call(
        _slots_kernel,
        grid=(N // tm,),
        in_specs=[pl.BlockSpec((tm, LANES), lambda i: (i, 0)), pl.BlockSpec((1, LANES), lambda i: (0, 0))],
        out_specs=pl.BlockSpec((HALO, tm), lambda i: (0, i)),
        out_shape=jax.ShapeDtypeStruct((HALO, N), I32),
        compiler_params=pltpu.CompilerParams(dimension_semantics=("arbitrary",)),
        name="slots",
    )(ri, _lane_vec(seg_start, 0))
    return out[:TOP_K].reshape(-1)


def _dispatch_kernel(seg_ref, slot0_ref, slot1_ref, h_ref, xs_ref, zero_ref, sem, zsem):
    tm = TM_DISPATCH
    bm = FFN_BLOCK
    n_blocks = xs_ref.shape[0] // (bm * ROW_SLAB)

    @pl.when(pl.program_id(0) == 0)
    def _():
        zero_ref[...] = jnp.zeros(zero_ref.shape, F32)

        def run_copy(first_row, n_rows):
            start = pl.multiple_of(first_row * ROW_SLAB, ROW_SLAB)
            return pltpu.make_async_copy(zero_ref.at[pl.ds(0, n_rows * ROW_SLAB)],
                                         xs_ref.at[pl.ds(start, n_rows * ROW_SLAB)], zsem)

        def block_copy(b):
            return pltpu.make_async_copy(zero_ref, _slab_block(xs_ref, b), zsem)

        def each_pad(fn):
            def per_expert(e, carry):
                row = seg_ref[e]
                n_pad = seg_ref[N_EXPERTS + e] - row
                run = bm // 2
                while run >= 1:
                    @pl.when((n_pad & run) != 0)
                    def _(row=row, run=run):
                        fn(run_copy(row, run))
                    row = row + (n_pad & run)
                    run //= 2
                return carry
            lax.fori_loop(0, N_EXPERTS, per_expert, 0)

            def per_block(b, c):
                fn(block_copy(b))
                return c
            lax.fori_loop(seg_ref[2 * N_EXPERTS], n_blocks, per_block, 0)

        each_pad(lambda cp: cp.start())
        each_pad(lambda cp: cp.wait())

    def issue(t, carry):
        for k, slot_ref in enumerate((slot0_ref, slot1_ref)):
            pltpu.make_async_copy(_slab(h_ref, t), _slab(xs_ref, slot_ref[t]), sem).start(priority=k)
        return carry

    lax.fori_loop(0, tm, issue, 0, unroll=ISSUE_UNROLL)
    for _ in range(TOP_K):
        pltpu.make_async_copy(h_ref, xs_ref.at[pl.ds(0, tm * ROW_SLAB)], sem).wait()


def _dispatch(seg, slots, h2, n_slots):
    N = h2.shape[0] // ROW_SLAB
    tm = TM_DISPATCH
    grid_spec = pltpu.PrefetchScalarGridSpec(
        num_scalar_prefetch=1,
        grid=(N // tm,),
        in_specs=[pl.BlockSpec((tm,), lambda i, s: (i,), memory_space=pltpu.SMEM),
                  pl.BlockSpec((tm,), lambda i, s: (N // tm + i,), memory_space=pltpu.SMEM),
                  pl.BlockSpec((tm * ROW_SLAB, LANES), lambda i, s: (i, 0))],
        out_specs=pl.BlockSpec(memory_space=pl.ANY),
        scratch_shapes=[pltpu.VMEM((FFN_BLOCK * ROW_SLAB, LANES), F32),
                        pltpu.SemaphoreType.DMA(()), pltpu.SemaphoreType.DMA(())],
    )
    return pl.pallas_call(
        _dispatch_kernel,
        grid_spec=grid_spec,
        out_shape=jax.ShapeDtypeStruct((n_slots * ROW_SLAB, LANES), F32),
        compiler_params=pltpu.CompilerParams(dimension_semantics=("arbitrary",)),
        name="dispatch",
    )(seg, slots, slots, h2)


def _ffn_kernel(blk_ref, xs_ref, wg_ref, wu_ref, wd_ref, ys_ref,
                xbuf, ybuf, wg16, wu16, wd16, xsem, ysem, zsem):
    e = pl.program_id(0)
    bm = FFN_BLOCK
    ring = FFN_RING
    n_blocks = ys_ref.shape[0] // (bm * ROW_SLAB)
    first = blk_ref[e]
    count = blk_ref[N_EXPERTS + e]
    n_used = blk_ref[2 * N_EXPERTS]

    def x_copy(g):
        return pltpu.make_async_copy(_slab_block(xs_ref, g), xbuf.at[g % ring], xsem.at[g % ring])

    def y_copy(g):
        return pltpu.make_async_copy(ybuf.at[g % ring], _slab_block(ys_ref, g), ysem.at[g % ring])

    def when_block(g, fn):
        @pl.when((g >= 0) & (g < n_used))
        def _():
            fn(g)

    @pl.when(e == 0)
    def _():
        for g in range(ring - 1):
            when_block(g, lambda g: x_copy(g).start(priority=RING_PRIORITY))

    @pl.when(count > 0)
    def _():
        wg16[...] = wg_ref[0].astype(BF16)
        wu16[...] = wu_ref[0].astype(BF16)
        wd16[...] = wd_ref[0].astype(BF16)

    def block(j, carry):
        g = first + j
        x_copy(g).wait()
        when_block(g + ring - 1, lambda g: x_copy(g).start(priority=RING_PRIORITY))

        x = _load_rows(xbuf.at[g % ring], bm).astype(BF16)
        sub = bm // FFN_SPLIT
        x_parts = [x[s * sub:(s + 1) * sub] for s in range(FFN_SPLIT)]
        gates = [_dot(xp, wg16[...]) for xp in x_parts]
        ups = [_dot(xp, wu16[...]) for xp in x_parts]
        acts = [(_silu(a) * b).astype(BF16) for a, b in zip(gates, ups)]
        y = jnp.concatenate([_dot(act, wd16[...]) for act in acts], axis=0)

        when_block(g - ring, lambda g: y_copy(g).wait())
        _store_rows(ybuf.at[g % ring], y)
        y_copy(g).start(priority=RING_PRIORITY)
        return carry

    lax.fori_loop(0, count, block, 0)

    @pl.when(e == pl.num_programs(0) - 1)
    def _():
        for back in range(ring, 0, -1):
            when_block(n_used - back, lambda g: y_copy(g).wait())
        ybuf[0] = jnp.zeros(ybuf.shape[1:], F32)

        def zero_copy(g):
            return pltpu.make_async_copy(ybuf.at[0], _slab_block(ys_ref, g), zsem)

        def start(g, c):
            zero_copy(g).start()
            return c

        def wait(g, c):
            zero_copy(g).wait()
            return c

        lax.fori_loop(n_used, n_blocks, start, 0)
        lax.fori_loop(n_used, n_blocks, wait, 0)


def _ffn(blk, xs, w_gate, w_up, w_down):
    bm = FFN_BLOCK
    D = D_MODEL
    weights = lambda shape: pl.BlockSpec((1,) + shape, lambda e, blk: (e, 0, 0))
    grid_spec = pltpu.PrefetchScalarGridSpec(
        num_scalar_prefetch=1,
        grid=(N_EXPERTS,),
        in_specs=[pl.BlockSpec(memory_space=pl.ANY),
                  weights((D, EXPERT_FF)), weights((D, EXPERT_FF)), weights((EXPERT_FF, D))],
        out_specs=pl.BlockSpec(memory_space=pl.ANY),
        scratch_shapes=[pltpu.VMEM((FFN_RING, bm * ROW_SLAB, LANES), F32),
                        pltpu.VMEM((FFN_RING, bm * ROW_SLAB, LANES), F32),
                        pltpu.VMEM((D, EXPERT_FF), BF16), pltpu.VMEM((D, EXPERT_FF), BF16),
                        pltpu.VMEM((EXPERT_FF, D), BF16),
                        pltpu.SemaphoreType.DMA((FFN_RING,)), pltpu.SemaphoreType.DMA((FFN_RING,)),
                        pltpu.SemaphoreType.DMA(())],
    )
    return pl.pallas_call(
        _ffn_kernel,
        grid_spec=grid_spec,
        out_shape=jax.ShapeDtypeStruct(xs.shape, F32),
        compiler_params=pltpu.CompilerParams(
            dimension_semantics=("arbitrary",), vmem_limit_bytes=VMEM_LIMIT),
        name="ffn",
    )(blk, xs, w_gate, w_up, w_down)


def _combine_kernel(slot0_ref, slot1_ref, next0_ref, next1_ref, ys_ref, x1_ref, rg_ref, g_ref, o_ref,
                    buf_ref, sems):
    tm = TM_COMBINE
    i = pl.program_id(0)
    half = i % 2

    def gather(slot_refs, s):
        def issue(t, carry):
            for k, slot_ref in enumerate(slot_refs):
                pltpu.make_async_copy(_slab(ys_ref, slot_ref[t]), _slab(buf_ref.at[s, k], t),
                                      sems.at[s]).start(priority=k)
            return carry
        lax.fori_loop(0, tm, issue, 0, unroll=ISSUE_UNROLL)

    @pl.when(i == 0)
    def _():
        gather((slot0_ref, slot1_ref), 0)

    @pl.when(i + 1 < pl.num_programs(0))
    def _():
        gather((next0_ref, next1_ref), 1 - half)

    for k in range(TOP_K):
        pltpu.make_async_copy(ys_ref.at[pl.ds(0, tm * ROW_SLAB)], buf_ref.at[half, k], sems.at[half]).wait()

    rg = rg_ref[...]
    moe = (_load_rows(buf_ref.at[half, 0], tm) * rg[:, 0:1]
           + _load_rows(buf_ref.at[half, 1], tm) * rg[:, 1:2])
    x2 = x1_ref[...] + moe
    o_ref[...] = x2 * lax.rsqrt(jnp.mean(x2 * x2, axis=-1, keepdims=True) + EPS) * g_ref[...]


def _combine(slots, ys, x1, rg, final_g):
    N, D = x1.shape
    tm = TM_COMBINE
    n_tiles = N // tm

    def slot_block(k, ahead, i):
        return (k * n_tiles + jnp.minimum(i + ahead, n_tiles - 1),)

    return pl.pallas_call(
        _combine_kernel,
        grid=(n_tiles,),
        in_specs=[pl.BlockSpec((tm,), functools.partial(slot_block, k, ahead), memory_space=pltpu.SMEM)
                  for ahead in (0, 1) for k in range(TOP_K)]
                 + [pl.BlockSpec(memory_space=pl.ANY),
                  pl.BlockSpec((tm, D), lambda i: (i, 0)),
                  pl.BlockSpec((tm, LANES), lambda i: (i, 0)),
                  pl.BlockSpec((1, D), lambda i: (0, 0))],
        out_specs=pl.BlockSpec((tm, D), lambda i: (i, 0)),
        out_shape=jax.ShapeDtypeStruct((N, D), F32),
        scratch_shapes=[pltpu.VMEM((2, TOP_K, tm * ROW_SLAB, LANES), F32), pltpu.SemaphoreType.DMA((2,))],
        compiler_params=pltpu.CompilerParams(
            dimension_semantics=("arbitrary",), vmem_limit_bytes=VMEM_LIMIT),
        name="combine",
    )(slots, slots, slots, slots, ys, x1, rg, final_g)


def _lane_vec(values, offset):
    return jnp.zeros((1, LANES), F32).at[0, offset:offset + values.shape[0]].set(values.astype(F32))


def kernel(x, mix_norm_g, w_in, conv_a_w, conv_a_norm_g, dn_conv_w, dn_a_log, dn_dt_bias, dn_norm_g,
           w_out, ffn_norm_g, router_group_w, router_expert_w, w_gate, w_up, w_down, final_norm_g):
    B, S, D = x.shape
    N = B * S
    depth = w_in.shape[0]
    assert depth == 1, "single-layer block: the final RMSNorm is fused into the layer's combine step"
    group_of = jnp.arange(CONV_A_DIM, dtype=I32) // CONV_A_GROUP_DIM
    gmat = jnp.where(group_of[:, None] == group_of[None, :], 1.0 / CONV_A_GROUP_DIM, 0.0).astype(BF16)
    bm = FFN_BLOCK
    n_blocks = (N * TOP_K) // bm + N_EXPERTS
    for l in range(depth):
        ya, q, k, v, zg, gcol, grow = _inproj(
            x, mix_norm_g[l][None, :], w_in[l].astype(BF16), conv_a_w[l], conv_a_norm_g[l][None, :], dn_conv_w[l],
            _lane_vec(dn_a_log[l], DN_HEADS), _lane_vec(dn_dt_bias[l], DN_HEADS), gmat)
        u, w, qd, kd, at = _delta_prep(q, k, v, gcol, grow)
        yb = _delta_scan(u, w, qd, kd, at, zg, grow, dn_norm_g[l][None, :])
        rw = jnp.pad(jnp.concatenate([router_group_w[l], router_expert_w[l]], axis=1),
                     ((0, 0), (0, LANES - N_GROUPS - N_EXPERTS)))
        rw_hi = rw.astype(BF16)
        rw = jnp.concatenate([rw_hi, (rw - rw_hi.astype(F32)).astype(BF16)], axis=1)
        x1, h2, ri, rg, cnt = _outproj(ya.reshape(N, CONV_A_DIM), yb.reshape(N, DN_DIM), x.reshape(N, D),
                                       w_out[l].astype(BF16), ffn_norm_g[l][None, :], rw)
        counts = cnt[0, :N_EXPERTS].astype(I32)
        padded = (counts + bm - 1) // bm * bm
        seg_end = jnp.cumsum(padded).astype(I32)
        seg_start = seg_end - padded
        n_used = (seg_end[-1:] // bm)
        slots = _slots(ri, seg_start)
        seg = jnp.concatenate([seg_start + counts, seg_end, n_used])
        xs = _dispatch(seg, slots, h2, n_blocks * bm)
        blk = jnp.concatenate([seg_start // bm, padded // bm, n_used])
        ys = _ffn(blk, xs, w_gate[l], w_up[l], w_down[l])
        x = _combine(slots, ys, x1, rg, final_norm_g[None, :]).reshape(B, S, D)
    return x
```

```python
import functools

import jax
import jax.numpy as jnp
from jax import lax
from jax.experimental import pallas as pl
from jax.experimental.pallas import tpu as pltpu

F32 = jnp.float32
BF16 = jnp.bfloat16
I32 = jnp.int32

D_MODEL = 1024
CHUNK = 64
CONV_A_GROUP_DIM = 64
CONV_A_DIM = 512
CONV_A_WIDTH = 3
DN_HEADS = 4
DN_HEAD_DIM = 128
DN_DIM = 512
DN_CONV_WIDTH = 4
IN_PROJ_DIM = 3 * CONV_A_DIM + 4 * DN_DIM + 2 * DN_HEADS
N_GROUPS = 4
EXPERTS_PER_GROUP = 8
N_EXPERTS = 32
TOP_K = 2
EXPERT_FF = 512
EPS = 1e-6

LANES = 128
HALO = 8
GATE_COL = 3 * CONV_A_DIM + 4 * DN_DIM

TM_IN = 512
TM_PREP = 256
SCAN_BATCH = 8
TM_SCAN = 256
TM_OUT = 1024
OUT_SPLIT = 8
FFN_BLOCK = 256
TM_SLOTS = 2048
TM_DISPATCH = 2048
TM_COMBINE = 512
FFN_SPLIT = 2
FFN_RING = 4
RING_PRIORITY = 1
ISSUE_UNROLL = 4
VMEM_LIMIT = 56 * 1024 * 1024


def _dot(a, b):
    return jnp.dot(a, b, preferred_element_type=F32)


def _dot_nt(a, b):
    return lax.dot_general(a, b, (((1,), (1,)), ((), ())), preferred_element_type=F32)


def _dot_tn(a, b):
    return lax.dot_general(a, b, (((0,), (0,)), ((), ())), preferred_element_type=F32)


def _split_bf16(x, parts):
    out = []
    for _ in range(parts):
        p = x.astype(BF16)
        out.append(p)
        x = x - p.astype(F32)
    return out


def _silu(x):
    return x * jax.nn.sigmoid(x)


ROW_SLAB = D_MODEL // LANES


def _store_rows(ref, val):
    m = val.shape[0]
    for c in range(ROW_SLAB):
        ref[pl.ds(c, m, stride=ROW_SLAB), :] = val[:, c * LANES:(c + 1) * LANES]


def _load_rows(ref, m):
    return jnp.concatenate([ref[pl.ds(c, m, stride=ROW_SLAB), :] for c in range(ROW_SLAB)], axis=-1)


def _slab(ref, row):
    return ref.at[pl.ds(pl.multiple_of(row * ROW_SLAB, ROW_SLAB), ROW_SLAB)]


def _slab_block(ref, block):
    n = FFN_BLOCK * ROW_SLAB
    return ref.at[pl.ds(pl.multiple_of(block * n, n), n)]


def _causal_conv(ext, w):
    taps = w.shape[0]
    delayed = pltpu.roll(ext, 1, 0)
    if taps == 4:
        near = w[3:4, :] * ext[HALO:] + w[2:3, :] * delayed[HALO:]
        far = w[1:2, :] * ext + w[0:1, :] * delayed
        return near + pltpu.roll(far, 2, 0)[HALO:]
    assert taps == 3
    acc = w[0:1, :] * pltpu.roll(ext, 2, 0)[HALO:]
    acc = acc + w[1:2, :] * delayed[HALO:]
    return acc + w[2:3, :] * ext[HALO:]


def _softplus(x):
    return jnp.maximum(x, 0.0) + jnp.log1p(jnp.exp(-jnp.abs(x)))


def _inproj_kernel(x_ref, g_ref, w_ref, caw_ref, cag_ref, dcw_ref, alog_ref, dtb_ref, gmat_ref,
                   ya_ref, q_ref, k_ref, v_ref, zg_ref, gcol_ref, grow_ref,
                   exta_ref, *extq_refs):
    tm = x_ref.shape[1]
    ext_refs = (exta_ref,) + extq_refs

    @pl.when(pl.program_id(1) == 0)
    def _():
        for ext_ref in ext_refs:
            ext_ref[0:HALO, :] = jnp.zeros((HALO, ext_ref.shape[1]), F32)

    x = x_ref[0]
    ms = jnp.mean(x * x, axis=-1, keepdims=True)
    hb = (x * lax.rsqrt(ms + EPS) * g_ref[...]).astype(BF16)

    def proj(c0, width):
        return _dot(hb, w_ref[:, c0:c0 + width])

    base = 3 * CONV_A_DIM

    def qkv_project(i):
        extq_refs[i][HALO:HALO + tm, :] = proj(base + i * DN_DIM, DN_DIM)

    def qkv_finish(i, out_ref):
        cols = slice(i * DN_DIM, (i + 1) * DN_DIM)
        s = _silu(_causal_conv(extq_refs[i][...], dcw_ref[:, cols]))
        if i == 2:
            out_ref[0] = s
        else:
            for h in range(DN_HEADS):
                sh = s[:, h * DN_HEAD_DIM:(h + 1) * DN_HEAD_DIM]
                inv = lax.rsqrt(jnp.sum(sh * sh, axis=-1, keepdims=True) + EPS)
                sh = sh * inv
                if i == 0:
                    sh = sh * (DN_HEAD_DIM ** -0.5)
                out_ref[0, :, h * DN_HEAD_DIM:(h + 1) * DN_HEAD_DIM] = sh

    def mixer_a_finish(a_b):
        y = a_b * _causal_conv(exta_ref[...], caw_ref[...])
        ysq = y * y
        hi = ysq.astype(BF16)
        lo = (ysq - hi.astype(F32)).astype(BF16)
        gmean = _dot(hi, gmat_ref[...]) + _dot(lo, gmat_ref[...])
        ya_ref[0] = (y * lax.rsqrt(gmean + EPS) * cag_ref[...]).astype(ya_ref.dtype)

    qkv_project(0)
    qkv_project(1)
    exta_ref[HALO:HALO + tm, :] = proj(2 * CONV_A_DIM, CONV_A_DIM) * proj(0, CONV_A_DIM)
    qkv_finish(0, q_ref)
    qkv_project(2)
    a_b = proj(CONV_A_DIM, CONV_A_DIM)
    qkv_finish(1, k_ref)
    z = proj(base + 3 * DN_DIM, DN_DIM)
    mixer_a_finish(a_b)
    n_gate = IN_PROJ_DIM - GATE_COL
    p = jnp.concatenate([proj(GATE_COL, n_gate), jnp.zeros((tm, LANES - n_gate), F32)], axis=-1)
    qkv_finish(2, v_ref)
    zg_ref[0] = _silu(z)
    for ext_ref in ext_refs:
        ext_ref[0:HALO, :] = ext_ref[tm:tm + HALO, :]

    beta = jax.nn.sigmoid(p)
    g = -jnp.exp(alog_ref[...]) * _softplus(p + dtb_ref[...])
    r = lax.broadcasted_iota(I32, (CHUNK, CHUNK), 0)
    c = lax.broadcasted_iota(I32, (CHUNK, CHUNK), 1)
    tri = jnp.where(c <= r, 1.0, 0.0).astype(BF16)
    parts = _split_bf16(g, 3)
    gc = jnp.concatenate(
        [sum(_dot(tri, p[ci * CHUNK:(ci + 1) * CHUNK]) for p in parts) for ci in range(tm // CHUNK)], axis=0)
    lane = lax.broadcasted_iota(I32, (tm, LANES), 1)
    slab = jnp.where(lane < DN_HEADS, beta, gc)
    gcol_ref[0] = slab
    rows = slab.T[0:HALO, :]
    for ci in range(tm // CHUNK):
        grow_ref[0, ci] = rows[:, ci * CHUNK:(ci + 1) * CHUNK]


def _inproj(x, mix_g, w_in16, conv_a_w, conv_a_g, dn_conv_w, alog_vec, dtb_vec, gmat):
    B, S, D = x.shape
    tm = TM_IN
    full = lambda shape: pl.BlockSpec(shape, lambda b, t: (0,) * len(shape))
    row = lambda width: pl.BlockSpec((1, tm, width), lambda b, t: (b, t, 0))
    out_shape = (
        jax.ShapeDtypeStruct((B, S, CONV_A_DIM), BF16),
        jax.ShapeDtypeStruct((B, S, DN_DIM), F32),
        jax.ShapeDtypeStruct((B, S, DN_DIM), F32),
        jax.ShapeDtypeStruct((B, S, DN_DIM), F32),
        jax.ShapeDtypeStruct((B, S, DN_DIM), F32),
        jax.ShapeDtypeStruct((B, S, LANES), F32),
        jax.ShapeDtypeStruct((B, S // CHUNK, HALO, CHUNK), F32),
    )
    return pl.pallas_call(
        _inproj_kernel,
        grid=(B, S // tm),
        in_specs=[row(D), full((1, D)), full((D, IN_PROJ_DIM)), full((CONV_A_WIDTH, CONV_A_DIM)),
                  full((1, CONV_A_DIM)), full((DN_CONV_WIDTH, 3 * DN_DIM)), full((1, LANES)),
                  full((1, LANES)), full((CONV_A_DIM, CONV_A_DIM))],
        out_specs=(row(CONV_A_DIM), row(DN_DIM), row(DN_DIM), row(DN_DIM), row(DN_DIM), row(LANES),
                   pl.BlockSpec((1, tm // CHUNK, HALO, CHUNK), lambda b, t: (b, t, 0, 0))),
        out_shape=out_shape,
        scratch_shapes=[pltpu.VMEM((tm + HALO, CONV_A_DIM), F32)] + [pltpu.VMEM((tm + HALO, DN_DIM), F32)] * 3,
        compiler_params=pltpu.CompilerParams(
            dimension_semantics=("arbitrary", "arbitrary"), vmem_limit_bytes=VMEM_LIMIT),
        name="inproj",
    )(x, mix_g, w_in16, conv_a_w, conv_a_g, dn_conv_w, alog_vec, dtb_vec, gmat)


def _delta_prep_kernel(q_ref, k_ref, v_ref, gcol_ref, grow_ref,
                       u_ref, w_ref, qd_ref, kd_ref, at_ref):
    tm = q_ref.shape[1]
    ri = lax.broadcasted_iota(I32, (CHUNK, CHUNK), 0)
    ci = lax.broadcasted_iota(I32, (CHUNK, CHUNK), 1)
    causal = ci <= ri
    strict = ci < ri
    eye = jnp.where(ci == ri, 1.0, 0.0).astype(F32)
    lane = lax.broadcasted_iota(I32, (CHUNK, LANES), 1)
    left_half = lane < CHUNK

    def setup(c, h):
        rows = slice(c * CHUNK, (c + 1) * CHUNK)
        cols = slice(h * DN_HEAD_DIM, (h + 1) * DN_HEAD_DIM)
        slab = gcol_ref[0, rows, :]
        beta = jnp.sum(jnp.where(lane == h, slab, 0.0), axis=-1, keepdims=True)
        gcc = jnp.sum(jnp.where(lane == h + DN_HEADS, slab, 0.0), axis=-1, keepdims=True)
        gcr = grow_ref[0, c, h + DN_HEADS:h + DN_HEADS + 1, :]
        diff = gcc - gcr
        decay = jnp.where(causal, jnp.exp(jnp.where(causal, diff, 0.0)), 0.0)
        q = q_ref[0, rows, cols]
        k = k_ref[0, rows, cols]
        v = v_ref[0, rows, cols]
        kb = k * beta
        kq = _dot_nt(jnp.concatenate([kb.astype(BF16), q.astype(BF16)], axis=0), k.astype(BF16))
        L = jnp.where(strict, kq[:CHUNK] * decay, 0.0)
        intra = kq[CHUNK:] * decay
        egc = jnp.exp(gcc)
        gl = gcr[:, CHUNK - 1:CHUNK]
        qd_ref[0, rows, cols] = (q * egc).astype(BF16)
        kd_ref[0, rows, cols] = (k * jnp.exp(gl - gcc)).astype(BF16)
        at_ref[0, rows, h * CHUNK:(h + 1) * CHUNK] = intra.astype(BF16)
        return -L, jnp.concatenate([v * beta, kb * egc], axis=-1).astype(BF16)

    def first_stage(m):
        m16 = m.astype(BF16)
        return jnp.concatenate([_dot(m16, m16), eye + m], axis=-1)

    def stage(r):
        return _dot(r[:, :CHUNK].astype(BF16), r.astype(BF16)) + jnp.where(left_half, 0.0, r)

    def last_stage(r):
        return (stage(r))[:, CHUNK:]

    def solve(c, h, t, rhs):
        rows = slice(c * CHUNK, (c + 1) * CHUNK)
        cols = slice(h * DN_HEAD_DIM, (h + 1) * DN_HEAD_DIM)
        uw = _dot(t.astype(BF16), rhs)
        u_ref[0, rows, cols] = uw[:, :DN_HEAD_DIM]
        w_ref[0, rows, cols] = uw[:, DN_HEAD_DIM:].astype(BF16)

    chains = [(c, h) for c in range(tm // CHUNK) for h in range(DN_HEADS)]
    ms, rhss = zip(*[setup(c, h) for c, h in chains])
    rs = [first_stage(m) for m in ms]
    for _ in range(4):
        rs = [stage(r) for r in rs]
    ts = [last_stage(r) for r in rs]
    for (c, h), t, rhs in zip(chains, ts, rhss):
        solve(c, h, t, rhs)


def _delta_prep(q, k, v, gcol, grow):
    B, S, _ = q.shape
    tm = TM_PREP
    row = lambda width: pl.BlockSpec((1, tm, width), lambda b, t: (b, t, 0))
    return pl.pallas_call(
        _delta_prep_kernel,
        grid=(B, S // tm),
        in_specs=[row(DN_DIM), row(DN_DIM), row(DN_DIM), row(LANES),
                  pl.BlockSpec((1, tm // CHUNK, HALO, CHUNK), lambda b, t: (b, t, 0, 0))],
        out_specs=(row(DN_DIM), row(DN_DIM), row(DN_DIM), row(DN_DIM), row(DN_HEADS * CHUNK)),
        out_shape=(jax.ShapeDtypeStruct((B, S, DN_DIM), F32),
                   jax.ShapeDtypeStruct((B, S, DN_DIM), BF16),
                   jax.ShapeDtypeStruct((B, S, DN_DIM), BF16),
                   jax.ShapeDtypeStruct((B, S, DN_DIM), BF16),
                   jax.ShapeDtypeStruct((B, S, DN_HEADS * CHUNK), BF16)),
        compiler_params=pltpu.CompilerParams(
            dimension_semantics=("arbitrary", "arbitrary"), vmem_limit_bytes=VMEM_LIMIT),
        name="delta_prep",
    )(q, k, v, gcol, grow)


def _delta_scan_kernel(u_ref, w_ref, qd_ref, kd_ref, at_ref, zg_ref, grow_ref, ng_ref, o_ref, st_ref):
    nb, tm = u_ref.shape[0], u_ref.shape[1]

    @pl.when(pl.program_id(1) == 0)
    def _():
        st_ref[...] = jnp.zeros(st_ref.shape, F32)

    def step(c, carry):
        r0 = pl.multiple_of(c * CHUNK, CHUNK)
        rows = pl.ds(r0, CHUNK)
        chains = [(b, h) for b in range(nb) for h in range(DN_HEADS)]
        cols = lambda h: slice(h * DN_HEAD_DIM, (h + 1) * DN_HEAD_DIM)
        sts = [st_ref[b, h] for b, h in chains]
        st16s = [st.astype(BF16) for st in sts]
        wqs = [_dot(jnp.concatenate([w_ref[b, rows, cols(h)], qd_ref[b, rows, cols(h)]], axis=0), st16)
               for (b, h), st16 in zip(chains, st16s)]
        vn16s = [(u_ref[b, rows, cols(h)] - wq[:CHUNK]).astype(BF16) for (b, h), wq in zip(chains, wqs)]
        upd = [_dot_tn(kd_ref[b, rows, cols(h)], vn16) for (b, h), vn16 in zip(chains, vn16s)]
        avs = [_dot(at_ref[b, rows, h * CHUNK:(h + 1) * CHUNK], vn16) for (b, h), vn16 in zip(chains, vn16s)]
        for (b, h), st, up in zip(chains, sts, upd):
            gl = grow_ref[b, c, h + DN_HEADS:h + DN_HEADS + 1, CHUNK - 1:CHUNK]
            st_ref[b, h] = st * jnp.exp(gl) + up
        for (b, h), wq, av in zip(chains, wqs, avs):
            o = wq[CHUNK:] + av
            on = o * lax.rsqrt(jnp.mean(o * o, axis=-1, keepdims=True) + EPS) * ng_ref[...]
            o_ref[b, rows, cols(h)] = (on * zg_ref[b, rows, cols(h)]).astype(o_ref.dtype)
        return carry

    lax.fori_loop(0, tm // CHUNK, step, 0)


def _delta_scan(u, w, qd, kd, at, zg, grow, norm_g):
    B, S, _ = u.shape
    tm = TM_SCAN
    nb = SCAN_BATCH
    row = lambda width: pl.BlockSpec((nb, tm, width), lambda b, t: (b, t, 0))
    return pl.pallas_call(
        _delta_scan_kernel,
        grid=(B // nb, S // tm),
        in_specs=[row(DN_DIM), row(DN_DIM), row(DN_DIM), row(DN_DIM), row(DN_HEADS * CHUNK), row(DN_DIM),
                  pl.BlockSpec((nb, tm // CHUNK, HALO, CHUNK), lambda b, t: (b, t, 0, 0)),
                  pl.BlockSpec((1, DN_HEAD_DIM), lambda b, t: (0, 0))],
        out_specs=row(DN_DIM),
        out_shape=jax.ShapeDtypeStruct((B, S, DN_DIM), BF16),
        scratch_shapes=[pltpu.VMEM((nb, DN_HEADS, DN_HEAD_DIM, DN_HEAD_DIM), F32)],
        compiler_params=pltpu.CompilerParams(
            dimension_semantics=("arbitrary", "arbitrary"), vmem_limit_bytes=VMEM_LIMIT),
        name="delta_scan",
    )(u, w, qd, kd, at, zg, grow, norm_g)


def _outproj_kernel(ya_ref, yb_ref, x_ref, wo_ref, g_ref, rw_ref,
                    x1_ref, h2_ref, ri_ref, rg_ref, cnt_ref, base_ref):
    tm = x_ref.shape[0]
    sub = tm // OUT_SPLIT
    parts = range(OUT_SPLIT)
    rows_of = lambda s: slice(s * sub, (s + 1) * sub)

    @pl.when(pl.program_id(0) == 0)
    def _():
        base_ref[...] = jnp.zeros(base_ref.shape, F32)

    def project(s):
        rows = rows_of(s)
        return _dot(jnp.concatenate([ya_ref[rows, :], yb_ref[rows, :]], axis=-1), wo_ref[...])

    def normalise(s, y):
        rows = rows_of(s)
        x1 = x_ref[rows, :] + y
        x1_ref[rows, :] = x1
        h = x1 * lax.rsqrt(jnp.mean(x1 * x1, axis=-1, keepdims=True) + EPS) * g_ref[...]
        _store_rows(h2_ref.at[pl.ds(s * sub * ROW_SLAB, sub * ROW_SLAB)], h)
        return h

    def router_logits(h):
        h_hi, h_lo = _split_bf16(h, 2)
        hi_prod = _dot(h_hi, rw_ref[...])
        return hi_prod[:, :LANES] + (hi_prod[:, LANES:] + _dot(h_lo, rw_ref[:, :LANES]))

    lane = lax.broadcasted_iota(I32, (sub, LANES), 1)
    lanef = lane.astype(F32)
    rr = lax.broadcasted_iota(I32, (sub, sub), 0)
    cc = lax.broadcasted_iota(I32, (sub, sub), 1)
    tri = jnp.where(cc < rr, 1.0, 0.0).astype(BF16)

    def route(s, logits):
        rows = rows_of(s)
        neg = jnp.float32(-1e30)
        big = jnp.float32(1e9)
        is_g = lane < N_GROUPS
        gl = jnp.where(is_g, logits, neg)
        gmax = jnp.max(gl, axis=-1, keepdims=True)
        gidx = jnp.min(jnp.where(gl == gmax, lanef, big), axis=-1, keepdims=True)
        gsum = jnp.sum(jnp.where(is_g, jnp.exp(gl - gmax), 0.0), axis=-1, keepdims=True)
        gprob = 1.0 / gsum
        lo = N_GROUPS + EXPERTS_PER_GROUP * gidx
        emask = (lanef >= lo) & (lanef < lo + EXPERTS_PER_GROUP)
        el = jnp.where(emask, logits, neg)
        e1 = jnp.max(el, axis=-1, keepdims=True)
        i1 = jnp.min(jnp.where(el == e1, lanef, big), axis=-1, keepdims=True)
        el2 = jnp.where(lanef == i1, neg, el)
        e2 = jnp.max(el2, axis=-1, keepdims=True)
        i2 = jnp.min(jnp.where(el2 == e2, lanef, big), axis=-1, keepdims=True)
        r = jnp.exp(e2 - e1)
        gate1 = gprob / (1.0 + r)
        gate2 = gprob * r / (1.0 + r)
        id1 = i1 - N_GROUPS
        id2 = i2 - N_GROUPS

        oh1 = jnp.where(lanef == id1, 1.0, 0.0).astype(F32)
        oh2 = jnp.where(lanef == id2, 1.0, 0.0).astype(F32)
        oh = oh1 + oh2
        before = _dot(tri, oh.astype(BF16)) + base_ref[...]
        rank1 = jnp.sum(oh1 * before, axis=-1, keepdims=True)
        rank2 = jnp.sum(oh2 * before, axis=-1, keepdims=True)
        base_ref[...] = base_ref[...] + jnp.sum(oh, axis=0, keepdims=True)

        ri = jnp.where(lane == 0, rank1, jnp.where(lane == 1, rank2,
                       jnp.where(lane == 2, id1, jnp.where(lane == 3, id2, 0.0))))
        ri_ref[rows, :] = ri.astype(I32)
        rg_ref[rows, :] = jnp.where(lane == 0, gate1, jnp.where(lane == 1, gate2, 0.0))

    ys = [project(s) for s in parts]
    hs = [normalise(s, y) for s, y in zip(parts, ys)]
    ls = [router_logits(h) for h in hs]
    for s, logits in zip(parts, ls):
        route(s, logits)
    cnt_ref[...] = base_ref[...]


def _outproj(ya, yb, x, w_out16, ffn_g, rw):
    N, D = x.shape
    tm = TM_OUT
    row = lambda width: pl.BlockSpec((tm, width), lambda i: (i, 0))
    full = lambda shape: pl.BlockSpec(shape, lambda i: (0,) * len(shape))
    return pl.pallas_call(
        _outproj_kernel,
        grid=(N // tm,),
        in_specs=[row(CONV_A_DIM), row(DN_DIM), row(D), full((D, D)), full((1, D)), full((D, 2 * LANES))],
        out_specs=(row(D), pl.BlockSpec((tm * ROW_SLAB, LANES), lambda i: (i, 0)), row(LANES), row(LANES),
                   full((1, LANES))),
        out_shape=(jax.ShapeDtypeStruct((N, D), F32),
                   jax.ShapeDtypeStruct((N * ROW_SLAB, LANES), F32),
                   jax.ShapeDtypeStruct((N, LANES), I32),
                   jax.ShapeDtypeStruct((N, LANES), F32),
                   jax.ShapeDtypeStruct((1, LANES), F32)),
        scratch_shapes=[pltpu.VMEM((1, LANES), F32)],
        compiler_params=pltpu.CompilerParams(
            dimension_semantics=("arbitrary",), vmem_limit_bytes=VMEM_LIMIT),
        name="outproj",
    )(ya, yb, x, w_out16, ffn_g, rw)


def _slots_kernel(ri_ref, sel_ref, tab_ref, o_ref):
    rif = ri_ref[...].astype(F32)
    lane = lax.broadcasted_iota(I32, rif.shape, 1)
    lanef = lane.astype(F32)
    ids = _dot(rif.astype(BF16), sel_ref[...])
    starts = [_dot(jnp.where(lanef == ids[:, k * LANES:(k + 1) * LANES], 1.0, 0.0).astype(BF16), tab_ref[...])
              for k in range(TOP_K)]
    slab = FFN_BLOCK * jnp.where(lane == 0, starts[0], starts[1]) + rif
    o_ref[...] = slab.T[0:HALO, :].astype(I32)


def _slots(ri, seg_start):
    N = ri.shape[0]
    tm = TM_SLOTS
    lane = jnp.arange(LANES, dtype=I32)
    sel = jnp.concatenate([(lane[:, None] == TOP_K + k) & (lane[None, :] >= 0) for k in range(TOP_K)], axis=1)
    tab = jnp.zeros((LANES, LANES), F32).at[:N_EXPERTS, :].set((seg_start // FFN_BLOCK)[:, None].astype(F32))
    out = pl.pallas_call(
        _slots_kernel,
        grid=(N // tm,),
        in_specs=[pl.BlockSpec((tm, LANES), lambda i: (i, 0)),
                  pl.BlockSpec((LANES, TOP_K * LANES), lambda i: (0, 0)),
                  pl.BlockSpec((LANES, LANES), lambda i: (0, 0))],
        out_specs=pl.BlockSpec((HALO, tm), lambda i: (0, i)),
        out_shape=jax.ShapeDtypeStruct((HALO, N), I32),
        compiler_params=pltpu.CompilerParams(dimension_semantics=("arbitrary",)),
        name="slots",
    )(ri, sel.astype(BF16), tab.astype(BF16))
    return out[:TOP_K].reshape(-1)


def _dispatch_kernel(seg_ref, slot0_ref, slot1_ref, h_ref, xs_ref, zero_ref, sem, zsem):
    tm = TM_DISPATCH
    bm = FFN_BLOCK
    n_blocks = xs_ref.shape[0] // (bm * ROW_SLAB)

    @pl.when(pl.program_id(0) == 0)
    def _():
        zero_ref[...] = jnp.zeros(zero_ref.shape, F32)

        def run_copy(first_row, n_rows):
            start = pl.multiple_of(first_row * ROW_SLAB, ROW_SLAB)
            return pltpu.make_async_copy(zero_ref.at[pl.ds(0, n_rows * ROW_SLAB)],
                                         xs_ref.at[pl.ds(start, n_rows * ROW_SLAB)], zsem)

        def block_copy(b):
            return pltpu.make_async_copy(zero_ref, _slab_block(xs_ref, b), zsem)

        def each_pad(fn):
            def per_expert(e, carry):
                row = seg_ref[e]
                n_pad = seg_ref[N_EXPERTS + e] - row
                run = bm // 2
                while run >= 1:
                    @pl.when((n_pad & run) != 0)
                    def _(row=row, run=run):
                        fn(run_copy(row, run))
                    row = row + (n_pad & run)
                    run //= 2
                return carry
            lax.fori_loop(0, N_EXPERTS, per_expert, 0)

            def per_block(b, c):
                fn(block_copy(b))
                return c
            lax.fori_loop(seg_ref[2 * N_EXPERTS], n_blocks, per_block, 0)

        each_pad(lambda cp: cp.start())
        each_pad(lambda cp: cp.wait())

    def issue(t, carry):
        for k, slot_ref in enumerate((slot0_ref, slot1_ref)):
            pltpu.make_async_copy(_slab(h_ref, t), _slab(xs_ref, slot_ref[t]), sem).start(priority=k)
        return carry

    lax.fori_loop(0, tm, issue, 0, unroll=ISSUE_UNROLL)
    for _ in range(TOP_K):
        pltpu.make_async_copy(h_ref, xs_ref.at[pl.ds(0, tm * ROW_SLAB)], sem).wait()


def _dispatch(seg, slots, h2, n_slots):
    N = h2.shape[0] // ROW_SLAB
    tm = TM_DISPATCH
    grid_spec = pltpu.PrefetchScalarGridSpec(
        num_scalar_prefetch=1,
        grid=(N // tm,),
        in_specs=[pl.BlockSpec((tm,), lambda i, s: (i,), memory_space=pltpu.SMEM),
                  pl.BlockSpec((tm,), lambda i, s: (N // tm + i,), memory_space=pltpu.SMEM),
                  pl.BlockSpec((tm * ROW_SLAB, LANES), lambda i, s: (i, 0))],
        out_specs=pl.BlockSpec(memory_space=pl.ANY),
        scratch_shapes=[pltpu.VMEM((FFN_BLOCK * ROW_SLAB, LANES), F32),
                        pltpu.SemaphoreType.DMA(()), pltpu.SemaphoreType.DMA(())],
    )
    return pl.pallas_call(
        _dispatch_kernel,
        grid_spec=grid_spec,
        out_shape=jax.ShapeDtypeStruct((n_slots * ROW_SLAB, LANES), F32),
        compiler_params=pltpu.CompilerParams(dimension_semantics=("arbitrary",)),
        name="dispatch",
    )(seg, slots, slots, h2)


def _ffn_kernel(blk_ref, xs_ref, wg_ref, wu_ref, wd_ref, ys_ref,
                xbuf, ybuf, wg16, wu16, wd16, xsem, ysem, zsem):
    e = pl.program_id(0)
    bm = FFN_BLOCK
    ring = FFN_RING
    n_blocks = ys_ref.shape[0] // (bm * ROW_SLAB)
    first = blk_ref[e]
    count = blk_ref[N_EXPERTS + e]
    n_used = blk_ref[2 * N_EXPERTS]

    def x_copy(g):
        return pltpu.make_async_copy(_slab_block(xs_ref, g), xbuf.at[g % ring], xsem.at[g % ring])

    def y_copy(g):
        return pltpu.make_async_copy(ybuf.at[g % ring], _slab_block(ys_ref, g), ysem.at[g % ring])

    def when_block(g, fn):
        @pl.when((g >= 0) & (g < n_used))
        def _():
            fn(g)

    @pl.when(e == 0)
    def _():
        for g in range(ring - 1):
            when_block(g, lambda g: x_copy(g).start(priority=RING_PRIORITY))

    @pl.when(count > 0)
    def _():
        wg16[...] = wg_ref[0].astype(BF16)
        wu16[...] = wu_ref[0].astype(BF16)
        wd16[...] = wd_ref[0].astype(BF16)

    def block(j, carry):
        g = first + j
        x_copy(g).wait()
        when_block(g + ring - 1, lambda g: x_copy(g).start(priority=RING_PRIORITY))

        x = _load_rows(xbuf.at[g % ring], bm).astype(BF16)
        sub = bm // FFN_SPLIT
        x_parts = [x[s * sub:(s + 1) * sub] for s in range(FFN_SPLIT)]
        gates = [_dot(xp, wg16[...]) for xp in x_parts]
        ups = [_dot(xp, wu16[...]) for xp in x_parts]
        acts = [(_silu(a) * b).astype(BF16) for a, b in zip(gates, ups)]
        y = jnp.concatenate([_dot(act, wd16[...]) for act in acts], axis=0)

        when_block(g - ring, lambda g: y_copy(g).wait())
        _store_rows(ybuf.at[g % ring], y)
        y_copy(g).start(priority=RING_PRIORITY)
        return carry

    lax.fori_loop(0, count, block, 0)

    @pl.when(e == pl.num_programs(0) - 1)
    def _():
        for back in range(ring, 0, -1):
            when_block(n_used - back, lambda g: y_copy(g).wait())
        ybuf[0] = jnp.zeros(ybuf.shape[1:], F32)

        def zero_copy(g):
            return pltpu.make_async_copy(ybuf.at[0], _slab_block(ys_ref, g), zsem)

        def start(g, c):
            zero_copy(g).start()
            return c

        def wait(g, c):
            zero_copy(g).wait()
            return c

        lax.fori_loop(n_used, n_blocks, start, 0)
        lax.fori_loop(n_used, n_blocks, wait, 0)


def _ffn(blk, xs, w_gate, w_up, w_down):
    bm = FFN_BLOCK
    D = D_MODEL
    weights = lambda shape: pl.BlockSpec((1,) + shape, lambda e, blk: (e, 0, 0))
    grid_spec = pltpu.PrefetchScalarGridSpec(
        num_scalar_prefetch=1,
        grid=(N_EXPERTS,),
        in_specs=[pl.BlockSpec(memory_space=pl.ANY),
                  weights((D, EXPERT_FF)), weights((D, EXPERT_FF)), weights((EXPERT_FF, D))],
        out_specs=pl.BlockSpec(memory_space=pl.ANY),
        scratch_shapes=[pltpu.VMEM((FFN_RING, bm * ROW_SLAB, LANES), F32),
                        pltpu.VMEM((FFN_RING, bm * ROW_SLAB, LANES), F32),
                        pltpu.VMEM((D, EXPERT_FF), BF16), pltpu.VMEM((D, EXPERT_FF), BF16),
                        pltpu.VMEM((EXPERT_FF, D), BF16),
                        pltpu.SemaphoreType.DMA((FFN_RING,)), pltpu.SemaphoreType.DMA((FFN_RING,)),
                        pltpu.SemaphoreType.DMA(())],
    )
    return pl.pallas_call(
        _ffn_kernel,
        grid_spec=grid_spec,
        out_shape=jax.ShapeDtypeStruct(xs.shape, F32),
        compiler_params=pltpu.CompilerParams(
            dimension_semantics=("arbitrary",), vmem_limit_bytes=VMEM_LIMIT),
        name="ffn",
    )(blk, xs, w_gate, w_up, w_down)


def _combine_kernel(slot0_ref, slot1_ref, next0_ref, next1_ref, ys_ref, x1_ref, rg_ref, g_ref, o_ref,
                    buf_ref, sems):
    tm = TM_COMBINE
    i = pl.program_id(0)
    half = i % 2

    def gather(slot_refs, s):
        def issue(t, carry):
            for k, slot_ref in enumerate(slot_refs):
                pltpu.make_async_copy(_slab(ys_ref, slot_ref[t]), _slab(buf_ref.at[s, k], t),
                                      sems.at[s]).start(priority=k)
            return carry
        lax.fori_loop(0, tm, issue, 0, unroll=ISSUE_UNROLL)

    @pl.when(i == 0)
    def _():
        gather((slot0_ref, slot1_ref), 0)

    @pl.when(i + 1 < pl.num_programs(0))
    def _():
        gather((next0_ref, next1_ref), 1 - half)

    for k in range(TOP_K):
        pltpu.make_async_copy(ys_ref.at[pl.ds(0, tm * ROW_SLAB)], buf_ref.at[half, k], sems.at[half]).wait()

    rg = rg_ref[...]
    moe = (_load_rows(buf_ref.at[half, 0], tm) * rg[:, 0:1]
           + _load_rows(buf_ref.at[half, 1], tm) * rg[:, 1:2])
    x2 = x1_ref[...] + moe
    o_ref[...] = x2 * lax.rsqrt(jnp.mean(x2 * x2, axis=-1, keepdims=True) + EPS) * g_ref[...]


def _combine(slots, ys, x1, rg, final_g):
    N, D = x1.shape
    tm = TM_COMBINE
    n_tiles = N // tm

    def slot_block(k, ahead, i):
        return (k * n_tiles + jnp.minimum(i + ahead, n_tiles - 1),)

    return pl.pallas_call(
        _combine_kernel,
        grid=(n_tiles,),
        in_specs=[pl.BlockSpec((tm,), functools.partial(slot_block, k, ahead), memory_space=pltpu.SMEM)
                  for ahead in (0, 1) for k in range(TOP_K)]
                 + [pl.BlockSpec(memory_space=pl.ANY),
                  pl.BlockSpec((tm, D), lambda i: (i, 0)),
                  pl.BlockSpec((tm, LANES), lambda i: (i, 0)),
                  pl.BlockSpec((1, D), lambda i: (0, 0))],
        out_specs=pl.BlockSpec((tm, D), lambda i: (i, 0)),
        out_shape=jax.ShapeDtypeStruct((N, D), F32),
        scratch_shapes=[pltpu.VMEM((2, TOP_K, tm * ROW_SLAB, LANES), F32), pltpu.SemaphoreType.DMA((2,))],
        compiler_params=pltpu.CompilerParams(
            dimension_semantics=("arbitrary",), vmem_limit_bytes=VMEM_LIMIT),
        name="combine",
    )(slots, slots, slots, slots, ys, x1, rg, final_g)


def _lane_vec(values, offset):
    return jnp.zeros((1, LANES), F32).at[0, offset:offset + values.shape[0]].set(values.astype(F32))


def kernel(x, mix_norm_g, w_in, conv_a_w, conv_a_norm_g, dn_conv_w, dn_a_log, dn_dt_bias, dn_norm_g,
           w_out, ffn_norm_g, router_group_w, router_expert_w, w_gate, w_up, w_down, final_norm_g):
    B, S, D = x.shape
    N = B * S
    depth = w_in.shape[0]
    assert depth == 1, "single-layer block: the final RMSNorm is fused into the layer's combine step"
    group_of = jnp.arange(CONV_A_DIM, dtype=I32) // CONV_A_GROUP_DIM
    gmat = jnp.where(group_of[:, None] == group_of[None, :], 1.0 / CONV_A_GROUP_DIM, 0.0).astype(BF16)
    bm = FFN_BLOCK
    n_blocks = (N * TOP_K) // bm + N_EXPERTS
    for l in range(depth):
        ya, q, k, v, zg, gcol, grow = _inproj(
            x, mix_norm_g[l][None, :], w_in[l].astype(BF16), conv_a_w[l], conv_a_norm_g[l][None, :], dn_conv_w[l],
            _lane_vec(dn_a_log[l], DN_HEADS), _lane_vec(dn_dt_bias[l], DN_HEADS), gmat)
        u, w, qd, kd, at = _delta_prep(q, k, v, gcol, grow)
        yb = _delta_scan(u, w, qd, kd, at, zg, grow, dn_norm_g[l][None, :])
        rw = jnp.pad(jnp.concatenate([router_group_w[l], router_expert_w[l]], axis=1),
                     ((0, 0), (0, LANES - N_GROUPS - N_EXPERTS)))
        rw_hi = rw.astype(BF16)
        rw = jnp.concatenate([rw_hi, (rw - rw_hi.astype(F32)).astype(BF16)], axis=1)
        x1, h2, ri, rg, cnt = _outproj(ya.reshape(N, CONV_A_DIM), yb.reshape(N, DN_DIM), x.reshape(N, D),
                                       w_out[l].astype(BF16), ffn_norm_g[l][None, :], rw)
        counts = cnt[0, :N_EXPERTS].astype(I32)
        padded = (counts + bm - 1) // bm * bm
        seg_end = jnp.cumsum(padded).astype(I32)
        seg_start = seg_end - padded
        n_used = (seg_end[-1:] // bm)
        slots = _slots(ri, seg_start)
        seg = jnp.concatenate([seg_start + counts, seg_end, n_used])
        xs = _dispatch(seg, slots, h2, n_blocks * bm)
        blk = jnp.concatenate([seg_start // bm, padded // bm, n_used])
        ys = _ffn(blk, xs, w_gate[l], w_up[l], w_down[l])
        x = _combine(slots, ys, x1, rg, final_norm_g[None, :]).reshape(B, S, D)
    return x
```

```python
import functools

import jax
import jax.numpy as jnp
from jax import lax
from jax.experimental import pallas as pl
from jax.experimental.pallas import tpu as pltpu

F32 = jnp.float32
BF16 = jnp.bfloat16
I32 = jnp.int32

D_MODEL = 1024
CHUNK = 64
CONV_A_GROUP_DIM = 64
CONV_A_DIM = 512
CONV_A_WIDTH = 3
DN_HEADS = 4
DN_HEAD_DIM = 128
DN_DIM = 512
DN_CONV_WIDTH = 4
IN_PROJ_DIM = 3 * CONV_A_DIM + 4 * DN_DIM + 2 * DN_HEADS
N_GROUPS = 4
EXPERTS_PER_GROUP = 8
N_EXPERTS = 32
TOP_K = 2
EXPERT_FF = 512
EPS = 1e-6

LANES = 128
HALO = 8
GATE_COL = 3 * CONV_A_DIM + 4 * DN_DIM

TM_IN = 512
TM_PREP = 256
SCAN_BATCH = 8
TM_SCAN = 256
TM_OUT = 1024
OUT_SPLIT = 8
FFN_BLOCK = 256
TM_SLOTS = 2048
TM_DISPATCH = 2048
TM_COMBINE = 512
FFN_SPLIT = 2
FFN_RING = 4
RING_PRIORITY = 1
ISSUE_UNROLL = 8
VMEM_LIMIT = 56 * 1024 * 1024


def _dot(a, b):
    return jnp.dot(a, b, preferred_element_type=F32)


def _dot_nt(a, b):
    return lax.dot_general(a, b, (((1,), (1,)), ((), ())), preferred_element_type=F32)


def _dot_tn(a, b):
    return lax.dot_general(a, b, (((0,), (0,)), ((), ())), preferred_element_type=F32)


def _split_bf16(x, parts):
    out = []
    for _ in range(parts):
        p = x.astype(BF16)
        out.append(p)
        x = x - p.astype(F32)
    return out


def _silu(x):
    return x * jax.nn.sigmoid(x)


ROW_SLAB = D_MODEL // LANES


def _store_rows(ref, val):
    m = val.shape[0]
    for c in range(ROW_SLAB):
        ref[pl.ds(c, m, stride=ROW_SLAB), :] = val[:, c * LANES:(c + 1) * LANES]


def _load_rows(ref, m):
    return jnp.concatenate([ref[pl.ds(c, m, stride=ROW_SLAB), :] for c in range(ROW_SLAB)], axis=-1)


def _slab(ref, row):
    return ref.at[pl.ds(pl.multiple_of(row * ROW_SLAB, ROW_SLAB), ROW_SLAB)]


def _slab_block(ref, block):
    n = FFN_BLOCK * ROW_SLAB
    return ref.at[pl.ds(pl.multiple_of(block * n, n), n)]


def _causal_conv(ext, w):
    taps = w.shape[0]
    delayed = pltpu.roll(ext, 1, 0)
    if taps == 4:
        near = w[3:4, :] * ext[HALO:] + w[2:3, :] * delayed[HALO:]
        far = w[1:2, :] * ext + w[0:1, :] * delayed
        return near + pltpu.roll(far, 2, 0)[HALO:]
    assert taps == 3
    acc = w[0:1, :] * pltpu.roll(ext, 2, 0)[HALO:]
    acc = acc + w[1:2, :] * delayed[HALO:]
    return acc + w[2:3, :] * ext[HALO:]


def _softplus(x):
    return jnp.maximum(x, 0.0) + jnp.log1p(jnp.exp(-jnp.abs(x)))


def _inproj_kernel(x_ref, g_ref, w_ref, caw_ref, cag_ref, dcw_ref, alog_ref, dtb_ref, gmat_ref,
                   ya_ref, q_ref, k_ref, v_ref, zg_ref, gcol_ref, grow_ref,
                   exta_ref, *extq_refs):
    tm = x_ref.shape[1]
    ext_refs = (exta_ref,) + extq_refs

    @pl.when(pl.program_id(1) == 0)
    def _():
        for ext_ref in ext_refs:
            ext_ref[0:HALO, :] = jnp.zeros((HALO, ext_ref.shape[1]), F32)

    x = x_ref[0]
    ms = jnp.mean(x * x, axis=-1, keepdims=True)
    hb = (x * lax.rsqrt(ms + EPS) * g_ref[...]).astype(BF16)

    def proj(c0, width):
        return _dot(hb, w_ref[:, c0:c0 + width])

    base = 3 * CONV_A_DIM

    def qkv_project(i):
        extq_refs[i][HALO:HALO + tm, :] = proj(base + i * DN_DIM, DN_DIM)

    def qkv_finish(i, out_ref):
        cols = slice(i * DN_DIM, (i + 1) * DN_DIM)
        s = _silu(_causal_conv(extq_refs[i][...], dcw_ref[:, cols]))
        if i == 2:
            out_ref[0] = s
        else:
            for h in range(DN_HEADS):
                sh = s[:, h * DN_HEAD_DIM:(h + 1) * DN_HEAD_DIM]
                inv = lax.rsqrt(jnp.sum(sh * sh, axis=-1, keepdims=True) + EPS)
                sh = sh * inv
                if i == 0:
                    sh = sh * (DN_HEAD_DIM ** -0.5)
                out_ref[0, :, h * DN_HEAD_DIM:(h + 1) * DN_HEAD_DIM] = sh

    def mixer_a_finish(a_b):
        y = a_b * _causal_conv(exta_ref[...], caw_ref[...])
        ysq = y * y
        hi = ysq.astype(BF16)
        lo = (ysq - hi.astype(F32)).astype(BF16)
        gmean = _dot(hi, gmat_ref[...]) + _dot(lo, gmat_ref[...])
        ya_ref[0] = (y * lax.rsqrt(gmean + EPS) * cag_ref[...]).astype(ya_ref.dtype)

    qkv_project(0)
    qkv_project(1)
    exta_ref[HALO:HALO + tm, :] = proj(2 * CONV_A_DIM, CONV_A_DIM) * proj(0, CONV_A_DIM)
    qkv_finish(0, q_ref)
    qkv_project(2)
    a_b = proj(CONV_A_DIM, CONV_A_DIM)
    qkv_finish(1, k_ref)
    z = proj(base + 3 * DN_DIM, DN_DIM)
    mixer_a_finish(a_b)
    n_gate = IN_PROJ_DIM - GATE_COL
    p = jnp.concatenate([proj(GATE_COL, n_gate), jnp.zeros((tm, LANES - n_gate), F32)], axis=-1)
    qkv_finish(2, v_ref)
    zg_ref[0] = _silu(z)
    for ext_ref in ext_refs:
        ext_ref[0:HALO, :] = ext_ref[tm:tm + HALO, :]

    beta = jax.nn.sigmoid(p)
    g = -jnp.exp(alog_ref[...]) * _softplus(p + dtb_ref[...])
    r = lax.broadcasted_iota(I32, (CHUNK, CHUNK), 0)
    c = lax.broadcasted_iota(I32, (CHUNK, CHUNK), 1)
    tri = jnp.where(c <= r, 1.0, 0.0).astype(BF16)
    parts = _split_bf16(g, 3)
    gc = jnp.concatenate(
        [sum(_dot(tri, p[ci * CHUNK:(ci + 1) * CHUNK]) for p in parts) for ci in range(tm // CHUNK)], axis=0)
    lane = lax.broadcasted_iota(I32, (tm, LANES), 1)
    slab = jnp.where(lane < DN_HEADS, beta, gc)
    gcol_ref[0] = slab
    rows = slab.T[0:HALO, :]
    for ci in range(tm // CHUNK):
        grow_ref[0, ci] = rows[:, ci * CHUNK:(ci + 1) * CHUNK]


def _inproj(x, mix_g, w_in16, conv_a_w, conv_a_g, dn_conv_w, alog_vec, dtb_vec, gmat):
    B, S, D = x.shape
    tm = TM_IN
    full = lambda shape: pl.BlockSpec(shape, lambda b, t: (0,) * len(shape))
    row = lambda width: pl.BlockSpec((1, tm, width), lambda b, t: (b, t, 0))
    out_shape = (
        jax.ShapeDtypeStruct((B, S, CONV_A_DIM), BF16),
        jax.ShapeDtypeStruct((B, S, DN_DIM), F32),
        jax.ShapeDtypeStruct((B, S, DN_DIM), F32),
        jax.ShapeDtypeStruct((B, S, DN_DIM), F32),
        jax.ShapeDtypeStruct((B, S, DN_DIM), F32),
        jax.ShapeDtypeStruct((B, S, LANES), F32),
        jax.ShapeDtypeStruct((B, S // CHUNK, HALO, CHUNK), F32),
    )
    return pl.pallas_call(
        _inproj_kernel,
        grid=(B, S // tm),
        in_specs=[row(D), full((1, D)), full((D, IN_PROJ_DIM)), full((CONV_A_WIDTH, CONV_A_DIM)),
                  full((1, CONV_A_DIM)), full((DN_CONV_WIDTH, 3 * DN_DIM)), full((1, LANES)),
                  full((1, LANES)), full((CONV_A_DIM, CONV_A_DIM))],
        out_specs=(row(CONV_A_DIM), row(DN_DIM), row(DN_DIM), row(DN_DIM), row(DN_DIM), row(LANES),
                   pl.BlockSpec((1, tm // CHUNK, HALO, CHUNK), lambda b, t: (b, t, 0, 0))),
        out_shape=out_shape,
        scratch_shapes=[pltpu.VMEM((tm + HALO, CONV_A_DIM), F32)] + [pltpu.VMEM((tm + HALO, DN_DIM), F32)] * 3,
        compiler_params=pltpu.CompilerParams(
            dimension_semantics=("arbitrary", "arbitrary"), vmem_limit_bytes=VMEM_LIMIT),
        name="inproj",
    )(x, mix_g, w_in16, conv_a_w, conv_a_g, dn_conv_w, alog_vec, dtb_vec, gmat)


def _delta_prep_kernel(q_ref, k_ref, v_ref, gcol_ref, grow_ref,
                       u_ref, w_ref, qd_ref, kd_ref, at_ref):
    tm = q_ref.shape[1]
    ri = lax.broadcasted_iota(I32, (CHUNK, CHUNK), 0)
    ci = lax.broadcasted_iota(I32, (CHUNK, CHUNK), 1)
    causal = ci <= ri
    strict = ci < ri
    eye = jnp.where(ci == ri, 1.0, 0.0).astype(F32)
    lane = lax.broadcasted_iota(I32, (CHUNK, LANES), 1)
    left_half = lane < CHUNK

    def setup(c, h):
        rows = slice(c * CHUNK, (c + 1) * CHUNK)
        cols = slice(h * DN_HEAD_DIM, (h + 1) * DN_HEAD_DIM)
        slab = gcol_ref[0, rows, :]
        beta = jnp.sum(jnp.where(lane == h, slab, 0.0), axis=-1, keepdims=True)
        gcc = jnp.sum(jnp.where(lane == h + DN_HEADS, slab, 0.0), axis=-1, keepdims=True)
        gcr = grow_ref[0, c, h + DN_HEADS:h + DN_HEADS + 1, :]
        diff = gcc - gcr
        decay = jnp.where(causal, jnp.exp(jnp.where(causal, diff, 0.0)), 0.0)
        q = q_ref[0, rows, cols]
        k = k_ref[0, rows, cols]
        v = v_ref[0, rows, cols]
        kb = k * beta
        kq = _dot_nt(jnp.concatenate([kb.astype(BF16), q.astype(BF16)], axis=0), k.astype(BF16))
        L = jnp.where(strict, kq[:CHUNK] * decay, 0.0)
        intra = kq[CHUNK:] * decay
        egc = jnp.exp(gcc)
        gl = gcr[:, CHUNK - 1:CHUNK]
        qd_ref[0, rows, cols] = (q * egc).astype(BF16)
        kd_ref[0, rows, cols] = (k * jnp.exp(gl - gcc)).astype(BF16)
        at_ref[0, rows, h * CHUNK:(h + 1) * CHUNK] = intra.astype(BF16)
        return -L, jnp.concatenate([v * beta, kb * egc], axis=-1).astype(BF16)

    def first_stage(m):
        m16 = m.astype(BF16)
        return jnp.concatenate([_dot(m16, m16), eye + m], axis=-1)

    def stage(r):
        return _dot(r[:, :CHUNK].astype(BF16), r.astype(BF16)) + jnp.where(left_half, 0.0, r)

    def last_stage(r):
        return (stage(r))[:, CHUNK:]

    def solve(c, h, t, rhs):
        rows = slice(c * CHUNK, (c + 1) * CHUNK)
        cols = slice(h * DN_HEAD_DIM, (h + 1) * DN_HEAD_DIM)
        uw = _dot(t.astype(BF16), rhs)
        u_ref[0, rows, cols] = uw[:, :DN_HEAD_DIM]
        w_ref[0, rows, cols] = uw[:, DN_HEAD_DIM:].astype(BF16)

    chains = [(c, h) for c in range(tm // CHUNK) for h in range(DN_HEADS)]
    ms, rhss = zip(*[setup(c, h) for c, h in chains])
    rs = [first_stage(m) for m in ms]
    for _ in range(4):
        rs = [stage(r) for r in rs]
    ts = [last_stage(r) for r in rs]
    for (c, h), t, rhs in zip(chains, ts, rhss):
        solve(c, h, t, rhs)


def _delta_prep(q, k, v, gcol, grow):
    B, S, _ = q.shape
    tm = TM_PREP
    row = lambda width: pl.BlockSpec((1, tm, width), lambda b, t: (b, t, 0))
    return pl.pallas_call(
        _delta_prep_kernel,
        grid=(B, S // tm),
        in_specs=[row(DN_DIM), row(DN_DIM), row(DN_DIM), row(LANES),
                  pl.BlockSpec((1, tm // CHUNK, HALO, CHUNK), lambda b, t: (b, t, 0, 0))],
        out_specs=(row(DN_DIM), row(DN_DIM), row(DN_DIM), row(DN_DIM), row(DN_HEADS * CHUNK)),
        out_shape=(jax.ShapeDtypeStruct((B, S, DN_DIM), F32),
                   jax.ShapeDtypeStruct((B, S, DN_DIM), BF16),
                   jax.ShapeDtypeStruct((B, S, DN_DIM), BF16),
                   jax.ShapeDtypeStruct((B, S, DN_DIM), BF16),
                   jax.ShapeDtypeStruct((B, S, DN_HEADS * CHUNK), BF16)),
        compiler_params=pltpu.CompilerParams(
            dimension_semantics=("arbitrary", "arbitrary"), vmem_limit_bytes=VMEM_LIMIT),
        name="delta_prep",
    )(q, k, v, gcol, grow)


def _delta_scan_kernel(u_ref, w_ref, qd_ref, kd_ref, at_ref, zg_ref, grow_ref, ng_ref, o_ref, st_ref):
    nb, tm = u_ref.shape[0], u_ref.shape[1]

    @pl.when(pl.program_id(1) == 0)
    def _():
        st_ref[...] = jnp.zeros(st_ref.shape, F32)

    def step(c, carry):
        r0 = pl.multiple_of(c * CHUNK, CHUNK)
        rows = pl.ds(r0, CHUNK)
        chains = [(b, h) for b in range(nb) for h in range(DN_HEADS)]
        cols = lambda h: slice(h * DN_HEAD_DIM, (h + 1) * DN_HEAD_DIM)
        sts = [st_ref[b, h] for b, h in chains]
        st16s = [st.astype(BF16) for st in sts]
        wqs = [_dot(jnp.concatenate([w_ref[b, rows, cols(h)], qd_ref[b, rows, cols(h)]], axis=0), st16)
               for (b, h), st16 in zip(chains, st16s)]
        vn16s = [(u_ref[b, rows, cols(h)] - wq[:CHUNK]).astype(BF16) for (b, h), wq in zip(chains, wqs)]
        upd = [_dot_tn(kd_ref[b, rows, cols(h)], vn16) for (b, h), vn16 in zip(chains, vn16s)]
        avs = [_dot(at_ref[b, rows, h * CHUNK:(h + 1) * CHUNK], vn16) for (b, h), vn16 in zip(chains, vn16s)]
        for (b, h), st, up in zip(chains, sts, upd):
            gl = grow_ref[b, c, h + DN_HEADS:h + DN_HEADS + 1, CHUNK - 1:CHUNK]
            st_ref[b, h] = st * jnp.exp(gl) + up
        for (b, h), wq, av in zip(chains, wqs, avs):
            o = wq[CHUNK:] + av
            on = o * lax.rsqrt(jnp.mean(o * o, axis=-1, keepdims=True) + EPS) * ng_ref[...]
            o_ref[b, rows, cols(h)] = (on * zg_ref[b, rows, cols(h)]).astype(o_ref.dtype)
        return carry

    lax.fori_loop(0, tm // CHUNK, step, 0)


def _delta_scan(u, w, qd, kd, at, zg, grow, norm_g):
    B, S, _ = u.shape
    tm = TM_SCAN
    nb = SCAN_BATCH
    row = lambda width: pl.BlockSpec((nb, tm, width), lambda b, t: (b, t, 0))
    return pl.pallas_call(
        _delta_scan_kernel,
        grid=(B // nb, S // tm),
        in_specs=[row(DN_DIM), row(DN_DIM), row(DN_DIM), row(DN_DIM), row(DN_HEADS * CHUNK), row(DN_DIM),
                  pl.BlockSpec((nb, tm // CHUNK, HALO, CHUNK), lambda b, t: (b, t, 0, 0)),
                  pl.BlockSpec((1, DN_HEAD_DIM), lambda b, t: (0, 0))],
        out_specs=row(DN_DIM),
        out_shape=jax.ShapeDtypeStruct((B, S, DN_DIM), BF16),
        scratch_shapes=[pltpu.VMEM((nb, DN_HEADS, DN_HEAD_DIM, DN_HEAD_DIM), F32)],
        compiler_params=pltpu.CompilerParams(
            dimension_semantics=("arbitrary", "arbitrary"), vmem_limit_bytes=VMEM_LIMIT),
        name="delta_scan",
    )(u, w, qd, kd, at, zg, grow, norm_g)


def _outproj_kernel(ya_ref, yb_ref, x_ref, wo_ref, g_ref, rw_ref,
                    x1_ref, h2_ref, ri_ref, rg_ref, cnt_ref, base_ref):
    tm = x_ref.shape[0]
    sub = tm // OUT_SPLIT
    parts = range(OUT_SPLIT)
    rows_of = lambda s: slice(s * sub, (s + 1) * sub)

    @pl.when(pl.program_id(0) == 0)
    def _():
        base_ref[...] = jnp.zeros(base_ref.shape, F32)

    def project(s):
        rows = rows_of(s)
        return _dot(jnp.concatenate([ya_ref[rows, :], yb_ref[rows, :]], axis=-1), wo_ref[...])

    def normalise(s, y):
        rows = rows_of(s)
        x1 = x_ref[rows, :] + y
        x1_ref[rows, :] = x1
        h = x1 * lax.rsqrt(jnp.mean(x1 * x1, axis=-1, keepdims=True) + EPS) * g_ref[...]
        _store_rows(h2_ref.at[pl.ds(s * sub * ROW_SLAB, sub * ROW_SLAB)], h)
        return h

    def router_logits(h):
        h_hi, h_lo = _split_bf16(h, 2)
        hi_prod = _dot(h_hi, rw_ref[...])
        return hi_prod[:, :LANES] + (hi_prod[:, LANES:] + _dot(h_lo, rw_ref[:, :LANES]))

    lane = lax.broadcasted_iota(I32, (sub, LANES), 1)
    lanef = lane.astype(F32)
    rr = lax.broadcasted_iota(I32, (sub, sub), 0)
    cc = lax.broadcasted_iota(I32, (sub, sub), 1)
    tri = jnp.where(cc < rr, 1.0, 0.0).astype(BF16)

    def route(s, logits):
        rows = rows_of(s)
        neg = jnp.float32(-1e30)
        big = jnp.float32(1e9)
        is_g = lane < N_GROUPS
        gl = jnp.where(is_g, logits, neg)
        gmax = jnp.max(gl, axis=-1, keepdims=True)
        gidx = jnp.min(jnp.where(gl == gmax, lanef, big), axis=-1, keepdims=True)
        gsum = jnp.sum(jnp.where(is_g, jnp.exp(gl - gmax), 0.0), axis=-1, keepdims=True)
        gprob = 1.0 / gsum
        lo = N_GROUPS + EXPERTS_PER_GROUP * gidx
        emask = (lanef >= lo) & (lanef < lo + EXPERTS_PER_GROUP)
        el = jnp.where(emask, logits, neg)
        e1 = jnp.max(el, axis=-1, keepdims=True)
        i1 = jnp.min(jnp.where(el == e1, lanef, big), axis=-1, keepdims=True)
        el2 = jnp.where(lanef == i1, neg, el)
        e2 = jnp.max(el2, axis=-1, keepdims=True)
        i2 = jnp.min(jnp.where(el2 == e2, lanef, big), axis=-1, keepdims=True)
        r = jnp.exp(e2 - e1)
        gate1 = gprob / (1.0 + r)
        gate2 = gprob * r / (1.0 + r)
        id1 = i1 - N_GROUPS
        id2 = i2 - N_GROUPS

        oh1 = jnp.where(lanef == id1, 1.0, 0.0).astype(F32)
        oh2 = jnp.where(lanef == id2, 1.0, 0.0).astype(F32)
        oh = oh1 + oh2
        before = _dot(tri, oh.astype(BF16)) + base_ref[...]
        rank1 = jnp.sum(oh1 * before, axis=-1, keepdims=True)
        rank2 = jnp.sum(oh2 * before, axis=-1, keepdims=True)
        base_ref[...] = base_ref[...] + jnp.sum(oh, axis=0, keepdims=True)

        ri = jnp.where(lane == 0, rank1, jnp.where(lane == 1, rank2,
                       jnp.where(lane == 2, id1, jnp.where(lane == 3, id2, 0.0))))
        ri_ref[rows, :] = ri.astype(I32)
        rg_ref[rows, :] = jnp.where(lane == 0, gate1, jnp.where(lane == 1, gate2, 0.0))

    ys = [project(s) for s in parts]
    hs = [normalise(s, y) for s, y in zip(parts, ys)]
    ls = [router_logits(h) for h in hs]
    for s, logits in zip(parts, ls):
        route(s, logits)
    cnt_ref[...] = base_ref[...]


def _outproj(ya, yb, x, w_out16, ffn_g, rw):
    N, D = x.shape
    tm = TM_OUT
    row = lambda width: pl.BlockSpec((tm, width), lambda i: (i, 0))
    full = lambda shape: pl.BlockSpec(shape, lambda i: (0,) * len(shape))
    return pl.pallas_call(
        _outproj_kernel,
        grid=(N // tm,),
        in_specs=[row(CONV_A_DIM), row(DN_DIM), row(D), full((D, D)), full((1, D)), full((D, 2 * LANES))],
        out_specs=(row(D), pl.BlockSpec((tm * ROW_SLAB, LANES), lambda i: (i, 0)), row(LANES), row(LANES),
                   full((1, LANES))),
        out_shape=(jax.ShapeDtypeStruct((N, D), F32),
                   jax.ShapeDtypeStruct((N * ROW_SLAB, LANES), F32),
                   jax.ShapeDtypeStruct((N, LANES), I32),
                   jax.ShapeDtypeStruct((N, LANES), F32),
                   jax.ShapeDtypeStruct((1, LANES), F32)),
        scratch_shapes=[pltpu.VMEM((1, LANES), F32)],
        compiler_params=pltpu.CompilerParams(
            dimension_semantics=("arbitrary",), vmem_limit_bytes=VMEM_LIMIT),
        name="outproj",
    )(ya, yb, x, w_out16, ffn_g, rw)


def _slots_kernel(ri_ref, sel_ref, tab_ref, o_ref):
    rif = ri_ref[...].astype(F32)
    lane = lax.broadcasted_iota(I32, rif.shape, 1)
    lanef = lane.astype(F32)
    ids = _dot(rif.astype(BF16), sel_ref[...])
    starts = [_dot(jnp.where(lanef == ids[:, k * LANES:(k + 1) * LANES], 1.0, 0.0).astype(BF16), tab_ref[...])
              for k in range(TOP_K)]
    slab = FFN_BLOCK * jnp.where(lane == 0, starts[0], starts[1]) + rif
    o_ref[...] = slab.T[0:HALO, :].astype(I32)


def _slots(ri, seg_start):
    N = ri.shape[0]
    tm = TM_SLOTS
    lane = jnp.arange(LANES, dtype=I32)
    sel = jnp.concatenate([(lane[:, None] == TOP_K + k) & (lane[None, :] >= 0) for k in range(TOP_K)], axis=1)
    tab = jnp.zeros((LANES, LANES), F32).at[:N_EXPERTS, :].set((seg_start // FFN_BLOCK)[:, None].astype(F32))
    out = pl.pallas_call(
        _slots_kernel,
        grid=(N // tm,),
        in_specs=[pl.BlockSpec((tm, LANES), lambda i: (i, 0)),
                  pl.BlockSpec((LANES, TOP_K * LANES), lambda i: (0, 0)),
                  pl.BlockSpec((LANES, LANES), lambda i: (0, 0))],
        out_specs=pl.BlockSpec((HALO, tm), lambda i: (0, i)),
        out_shape=jax.ShapeDtypeStruct((HALO, N), I32),
        compiler_params=pltpu.CompilerParams(dimension_semantics=("arbitrary",)),
        name="slots",
    )(ri, sel.astype(BF16), tab.astype(BF16))
    return out[:TOP_K].reshape(-1)


def _dispatch_kernel(seg_ref, slot0_ref, slot1_ref, h_ref, xs_ref, zero_ref, sem, zsem):
    tm = TM_DISPATCH
    bm = FFN_BLOCK
    n_blocks = xs_ref.shape[0] // (bm * ROW_SLAB)

    @pl.when(pl.program_id(0) == 0)
    def _():
        zero_ref[...] = jnp.zeros(zero_ref.shape, F32)

        def run_copy(first_row, n_rows):
            start = pl.multiple_of(first_row * ROW_SLAB, ROW_SLAB)
            return pltpu.make_async_copy(zero_ref.at[pl.ds(0, n_rows * ROW_SLAB)],
                                         xs_ref.at[pl.ds(start, n_rows * ROW_SLAB)], zsem)

        def block_copy(b):
            return pltpu.make_async_copy(zero_ref, _slab_block(xs_ref, b), zsem)

        def each_pad(fn):
            def per_expert(e, carry):
                row = seg_ref[e]
                n_pad = seg_ref[N_EXPERTS + e] - row
                run = bm // 2
                while run >= 1:
                    @pl.when((n_pad & run) != 0)
                    def _(row=row, run=run):
                        fn(run_copy(row, run))
                    row = row + (n_pad & run)
                    run //= 2
                return carry
            lax.fori_loop(0, N_EXPERTS, per_expert, 0)

            def per_block(b, c):
                fn(block_copy(b))
                return c
            lax.fori_loop(seg_ref[2 * N_EXPERTS], n_blocks, per_block, 0)

        each_pad(lambda cp: cp.start())
        each_pad(lambda cp: cp.wait())

    def issue(t, carry):
        for k, slot_ref in enumerate((slot0_ref, slot1_ref)):
            pltpu.make_async_copy(_slab(h_ref, t), _slab(xs_ref, slot_ref[t]), sem).start(priority=k)
        return carry

    lax.fori_loop(0, tm, issue, 0, unroll=ISSUE_UNROLL)
    for _ in range(TOP_K):
        pltpu.make_async_copy(h_ref, xs_ref.at[pl.ds(0, tm * ROW_SLAB)], sem).wait()


def _dispatch(seg, slots, h2, n_slots):
    N = h2.shape[0] // ROW_SLAB
    tm = TM_DISPATCH
    grid_spec = pltpu.PrefetchScalarGridSpec(
        num_scalar_prefetch=1,
        grid=(N // tm,),
        in_specs=[pl.BlockSpec((tm,), lambda i, s: (i,), memory_space=pltpu.SMEM),
                  pl.BlockSpec((tm,), lambda i, s: (N // tm + i,), memory_space=pltpu.SMEM),
                  pl.BlockSpec((tm * ROW_SLAB, LANES), lambda i, s: (i, 0))],
        out_specs=pl.BlockSpec(memory_space=pl.ANY),
        scratch_shapes=[pltpu.VMEM((FFN_BLOCK * ROW_SLAB, LANES), F32),
                        pltpu.SemaphoreType.DMA(()), pltpu.SemaphoreType.DMA(())],
    )
    return pl.pallas_call(
        _dispatch_kernel,
        grid_spec=grid_spec,
        out_shape=jax.ShapeDtypeStruct((n_slots * ROW_SLAB, LANES), F32),
        compiler_params=pltpu.CompilerParams(dimension_semantics=("arbitrary",)),
        name="dispatch",
    )(seg, slots, slots, h2)


def _ffn_kernel(blk_ref, xs_ref, wg_ref, wu_ref, wd_ref, ys_ref,
                xbuf, ybuf, wg16, wu16, wd16, xsem, ysem, zsem):
    e = pl.program_id(0)
    bm = FFN_BLOCK
    ring = FFN_RING
    n_blocks = ys_ref.shape[0] // (bm * ROW_SLAB)
    first = blk_ref[e]
    count = blk_ref[N_EXPERTS + e]
    n_used = blk_ref[2 * N_EXPERTS]

    def x_copy(g):
        return pltpu.make_async_copy(_slab_block(xs_ref, g), xbuf.at[g % ring], xsem.at[g % ring])

    def y_copy(g):
        return pltpu.make_async_copy(ybuf.at[g % ring], _slab_block(ys_ref, g), ysem.at[g % ring])

    def when_block(g, fn):
        @pl.when((g >= 0) & (g < n_used))
        def _():
            fn(g)

    @pl.when(e == 0)
    def _():
        for g in range(ring - 1):
            when_block(g, lambda g: x_copy(g).start(priority=RING_PRIORITY))

    @pl.when(count > 0)
    def _():
        wg16[...] = wg_ref[0].astype(BF16)
        wu16[...] = wu_ref[0].astype(BF16)
        wd16[...] = wd_ref[0].astype(BF16)

    def block(j, carry):
        g = first + j
        x_copy(g).wait()
        when_block(g + ring - 1, lambda g: x_copy(g).start(priority=RING_PRIORITY))

        x = _load_rows(xbuf.at[g % ring], bm).astype(BF16)
        sub = bm // FFN_SPLIT
        x_parts = [x[s * sub:(s + 1) * sub] for s in range(FFN_SPLIT)]
        gates = [_dot(xp, wg16[...]) for xp in x_parts]
        ups = [_dot(xp, wu16[...]) for xp in x_parts]
        acts = [(_silu(a) * b).astype(BF16) for a, b in zip(gates, ups)]
        y = jnp.concatenate([_dot(act, wd16[...]) for act in acts], axis=0)

        when_block(g - ring, lambda g: y_copy(g).wait())
        _store_rows(ybuf.at[g % ring], y)
        y_copy(g).start(priority=RING_PRIORITY)
        return carry

    lax.fori_loop(0, count, block, 0)

    @pl.when(e == pl.num_programs(0) - 1)
    def _():
        for back in range(ring, 0, -1):
            when_block(n_used - back, lambda g: y_copy(g).wait())
        ybuf[0] = jnp.zeros(ybuf.shape[1:], F32)

        def zero_copy(g):
            return pltpu.make_async_copy(ybuf.at[0], _slab_block(ys_ref, g), zsem)

        def start(g, c):
            zero_copy(g).start()
            return c

        def wait(g, c):
            zero_copy(g).wait()
            return c

        lax.fori_loop(n_used, n_blocks, start, 0)
        lax.fori_loop(n_used, n_blocks, wait, 0)


def _ffn(blk, xs, w_gate, w_up, w_down):
    bm = FFN_BLOCK
    D = D_MODEL
    weights = lambda shape: pl.BlockSpec((1,) + shape, lambda e, blk: (e, 0, 0))
    grid_spec = pltpu.PrefetchScalarGridSpec(
        num_scalar_prefetch=1,
        grid=(N_EXPERTS,),
        in_specs=[pl.BlockSpec(memory_space=pl.ANY),
                  weights((D, EXPERT_FF)), weights((D, EXPERT_FF)), weights((EXPERT_FF, D))],
        out_specs=pl.BlockSpec(memory_space=pl.ANY),
        scratch_shapes=[pltpu.VMEM((FFN_RING, bm * ROW_SLAB, LANES), F32),
                        pltpu.VMEM((FFN_RING, bm * ROW_SLAB, LANES), F32),
                        pltpu.VMEM((D, EXPERT_FF), BF16), pltpu.VMEM((D, EXPERT_FF), BF16),
                        pltpu.VMEM((EXPERT_FF, D), BF16),
                        pltpu.SemaphoreType.DMA((FFN_RING,)), pltpu.SemaphoreType.DMA((FFN_RING,)),
                        pltpu.SemaphoreType.DMA(())],
    )
    return pl.pallas_call(
        _ffn_kernel,
        grid_spec=grid_spec,
        out_shape=jax.ShapeDtypeStruct(xs.shape, F32),
        compiler_params=pltpu.CompilerParams(
            dimension_semantics=("arbitrary",), vmem_limit_bytes=VMEM_LIMIT),
        name="ffn",
    )(blk, xs, w_gate, w_up, w_down)


def _combine_kernel(slot0_ref, slot1_ref, next0_ref, next1_ref, ys_ref, x1_ref, rg_ref, g_ref, o_ref,
                    buf_ref, sems):
    tm = TM_COMBINE
    i = pl.program_id(0)
    half = i % 2

    def gather(slot_refs, s):
        def issue(t, carry):
            for k, slot_ref in enumerate(slot_refs):
                pltpu.make_async_copy(_slab(ys_ref, slot_ref[t]), _slab(buf_ref.at[s, k], t),
                                      sems.at[s]).start(priority=k)
            return carry
        lax.fori_loop(0, tm, issue, 0, unroll=ISSUE_UNROLL)

    @pl.when(i == 0)
    def _():
        gather((slot0_ref, slot1_ref), 0)

    @pl.when(i + 1 < pl.num_programs(0))
    def _():
        gather((next0_ref, next1_ref), 1 - half)

    for k in range(TOP_K):
        pltpu.make_async_copy(ys_ref.at[pl.ds(0, tm * ROW_SLAB)], buf_ref.at[half, k], sems.at[half]).wait()

    rg = rg_ref[...]
    moe = (_load_rows(buf_ref.at[half, 0], tm) * rg[:, 0:1]
           + _load_rows(buf_ref.at[half, 1], tm) * rg[:, 1:2])
    x2 = x1_ref[...] + moe
    o_ref[...] = x2 * lax.rsqrt(jnp.mean(x2 * x2, axis=-1, keepdims=True) + EPS) * g_ref[...]


def _combine(slots, ys, x1, rg, final_g):
    N, D = x1.shape
    tm = TM_COMBINE
    n_tiles = N // tm

    def slot_block(k, ahead, i):
        return (k * n_tiles + jnp.minimum(i + ahead, n_tiles - 1),)

    return pl.pallas_call(
        _combine_kernel,
        grid=(n_tiles,),
        in_specs=[pl.BlockSpec((tm,), functools.partial(slot_block, k, ahead), memory_space=pltpu.SMEM)
                  for ahead in (0, 1) for k in range(TOP_K)]
                 + [pl.BlockSpec(memory_space=pl.ANY),
                  pl.BlockSpec((tm, D), lambda i: (i, 0)),
                  pl.BlockSpec((tm, LANES), lambda i: (i, 0)),
                  pl.BlockSpec((1, D), lambda i: (0, 0))],
        out_specs=pl.BlockSpec((tm, D), lambda i: (i, 0)),
        out_shape=jax.ShapeDtypeStruct((N, D), F32),
        scratch_shapes=[pltpu.VMEM((2, TOP_K, tm * ROW_SLAB, LANES), F32), pltpu.SemaphoreType.DMA((2,))],
        compiler_params=pltpu.CompilerParams(
            dimension_semantics=("arbitrary",), vmem_limit_bytes=VMEM_LIMIT),
        name="combine",
    )(slots, slots, slots, slots, ys, x1, rg, final_g)


def _lane_vec(values, offset):
    return jnp.zeros((1, LANES), F32).at[0, offset:offset + values.shape[0]].set(values.astype(F32))


def kernel(x, mix_norm_g, w_in, conv_a_w, conv_a_norm_g, dn_conv_w, dn_a_log, dn_dt_bias, dn_norm_g,
           w_out, ffn_norm_g, router_group_w, router_expert_w, w_gate, w_up, w_down, final_norm_g):
    B, S, D = x.shape
    N = B * S
    depth = w_in.shape[0]
    assert depth == 1, "single-layer block: the final RMSNorm is fused into the layer's combine step"
    group_of = jnp.arange(CONV_A_DIM, dtype=I32) // CONV_A_GROUP_DIM
    gmat = jnp.where(group_of[:, None] == group_of[None, :], 1.0 / CONV_A_GROUP_DIM, 0.0).astype(BF16)
    bm = FFN_BLOCK
    n_blocks = (N * TOP_K) // bm + N_EXPERTS
    for l in range(depth):
        ya, q, k, v, zg, gcol, grow = _inproj(
            x, mix_norm_g[l][None, :], w_in[l].astype(BF16), conv_a_w[l], conv_a_norm_g[l][None, :], dn_conv_w[l],
            _lane_vec(dn_a_log[l], DN_HEADS), _lane_vec(dn_dt_bias[l], DN_HEADS), gmat)
        u, w, qd, kd, at = _delta_prep(q, k, v, gcol, grow)
        yb = _delta_scan(u, w, qd, kd, at, zg, grow, dn_norm_g[l][None, :])
        rw = jnp.pad(jnp.concatenate([router_group_w[l], router_expert_w[l]], axis=1),
                     ((0, 0), (0, LANES - N_GROUPS - N_EXPERTS)))
        rw_hi = rw.astype(BF16)
        rw = jnp.concatenate([rw_hi, (rw - rw_hi.astype(F32)).astype(BF16)], axis=1)
        x1, h2, ri, rg, cnt = _outproj(ya.reshape(N, CONV_A_DIM), yb.reshape(N, DN_DIM), x.reshape(N, D),
                                       w_out[l].astype(BF16), ffn_norm_g[l][None, :], rw)
        counts = cnt[0, :N_EXPERTS].astype(I32)
        padded = (counts + bm - 1) // bm * bm
        seg_end = jnp.cumsum(padded).astype(I32)
        seg_start = seg_end - padded
        n_used = (seg_end[-1:] // bm)
        slots = _slots(ri, seg_start)
        seg = jnp.concatenate([seg_start + counts, seg_end, n_used])
        xs = _dispatch(seg, slots, h2, n_blocks * bm)
        blk = jnp.concatenate([seg_start // bm, padded // bm, n_used])
        ys = _ffn(blk, xs, w_gate[l], w_up[l], w_down[l])
        x = _combine(slots, ys, x1, rg, final_norm_g[None, :]).reshape(B, S, D)
    return x
```

```python
import functools

import jax
import jax.numpy as jnp
from jax import lax
from jax.experimental import pallas as pl
from jax.experimental.pallas import tpu as pltpu

F32 = jnp.float32
BF16 = jnp.bfloat16
I32 = jnp.int32

D_MODEL = 1024
CHUNK = 64
CONV_A_GROUP_DIM = 64
CONV_A_DIM = 512
CONV_A_WIDTH = 3
DN_HEADS = 4
DN_HEAD_DIM = 128
DN_DIM = 512
DN_CONV_WIDTH = 4
IN_PROJ_DIM = 3 * CONV_A_DIM + 4 * DN_DIM + 2 * DN_HEADS
N_GROUPS = 4
EXPERTS_PER_GROUP = 8
N_EXPERTS = 32
TOP_K = 2
EXPERT_FF = 512
EPS = 1e-6

LANES = 128
HALO = 8
GATE_COL = 3 * CONV_A_DIM + 4 * DN_DIM

TM_IN = 512
TM_PREP = 256
SCAN_BATCH = 8
TM_SCAN = 256
TM_OUT = 1024
OUT_SPLIT = 8
FFN_BLOCK = 256
TM_SLOTS = 2048
TM_DISPATCH = 2048
TM_COMBINE = 512
FFN_SPLIT = 2
FFN_RING = 4
RING_PRIORITY = 1
ISSUE_UNROLL = 8
VMEM_LIMIT = 56 * 1024 * 1024


def _dot(a, b):
    return jnp.dot(a, b, preferred_element_type=F32)


def _dot_nt(a, b):
    return lax.dot_general(a, b, (((1,), (1,)), ((), ())), preferred_element_type=F32)


def _dot_tn(a, b):
    return lax.dot_general(a, b, (((0,), (0,)), ((), ())), preferred_element_type=F32)


def _split_bf16(x, parts):
    out = []
    for _ in range(parts):
        p = x.astype(BF16)
        out.append(p)
        x = x - p.astype(F32)
    return out


def _silu(x):
    return x * jax.nn.sigmoid(x)


ROW_SLAB = D_MODEL // LANES


def _store_rows(ref, val):
    m = val.shape[0]
    for c in range(ROW_SLAB):
        ref[pl.ds(c, m, stride=ROW_SLAB), :] = val[:, c * LANES:(c + 1) * LANES]


def _load_rows(ref, m):
    return jnp.concatenate([ref[pl.ds(c, m, stride=ROW_SLAB), :] for c in range(ROW_SLAB)], axis=-1)


def _slab(ref, row):
    return ref.at[pl.ds(pl.multiple_of(row * ROW_SLAB, ROW_SLAB), ROW_SLAB)]


def _slab_block(ref, block):
    n = FFN_BLOCK * ROW_SLAB
    return ref.at[pl.ds(pl.multiple_of(block * n, n), n)]


def _causal_conv(ext, w):
    taps = w.shape[0]
    delayed = pltpu.roll(ext, 1, 0)
    if taps == 4:
        near = w[3:4, :] * ext[HALO:] + w[2:3, :] * delayed[HALO:]
        far = w[1:2, :] * ext + w[0:1, :] * delayed
        return near + pltpu.roll(far, 2, 0)[HALO:]
    assert taps == 3
    acc = w[0:1, :] * pltpu.roll(ext, 2, 0)[HALO:]
    acc = acc + w[1:2, :] * delayed[HALO:]
    return acc + w[2:3, :] * ext[HALO:]


def _softplus(x):
    return jnp.maximum(x, 0.0) + jnp.log1p(jnp.exp(-jnp.abs(x)))


def _inproj_kernel(x_ref, g_ref, w_ref, caw_ref, cag_ref, dcw_ref, alog_ref, dtb_ref, gmat_ref,
                   ya_ref, q_ref, k_ref, v_ref, zg_ref, gcol_ref, grow_ref,
                   exta_ref, *extq_refs):
    tm = x_ref.shape[1]
    ext_refs = (exta_ref,) + extq_refs

    @pl.when(pl.program_id(1) == 0)
    def _():
        for ext_ref in ext_refs:
            ext_ref[0:HALO, :] = jnp.zeros((HALO, ext_ref.shape[1]), F32)

    x = x_ref[0]
    ms = jnp.mean(x * x, axis=-1, keepdims=True)
    hb = (x * lax.rsqrt(ms + EPS) * g_ref[...]).astype(BF16)

    def proj(c0, width):
        return _dot(hb, w_ref[:, c0:c0 + width])

    base = 3 * CONV_A_DIM

    def qkv_project(i):
        extq_refs[i][HALO:HALO + tm, :] = proj(base + i * DN_DIM, DN_DIM)

    def qkv_finish(i, out_ref):
        cols = slice(i * DN_DIM, (i + 1) * DN_DIM)
        s = _silu(_causal_conv(extq_refs[i][...], dcw_ref[:, cols]))
        if i == 2:
            out_ref[0] = s
        else:
            for h in range(DN_HEADS):
                sh = s[:, h * DN_HEAD_DIM:(h + 1) * DN_HEAD_DIM]
                inv = lax.rsqrt(jnp.sum(sh * sh, axis=-1, keepdims=True) + EPS)
                sh = sh * inv
                if i == 0:
                    sh = sh * (DN_HEAD_DIM ** -0.5)
                out_ref[0, :, h * DN_HEAD_DIM:(h + 1) * DN_HEAD_DIM] = sh

    def mixer_a_finish(a_b):
        y = a_b * _causal_conv(exta_ref[...], caw_ref[...])
        ysq = y * y
        hi = ysq.astype(BF16)
        lo = (ysq - hi.astype(F32)).astype(BF16)
        gmean = _dot(hi, gmat_ref[...]) + _dot(lo, gmat_ref[...])
        ya_ref[0] = (y * lax.rsqrt(gmean + EPS) * cag_ref[...]).astype(ya_ref.dtype)

    qkv_project(0)
    qkv_project(1)
    exta_ref[HALO:HALO + tm, :] = proj(2 * CONV_A_DIM, CONV_A_DIM) * proj(0, CONV_A_DIM)
    qkv_finish(0, q_ref)
    qkv_project(2)
    a_b = proj(CONV_A_DIM, CONV_A_DIM)
    qkv_finish(1, k_ref)
    z = proj(base + 3 * DN_DIM, DN_DIM)
    mixer_a_finish(a_b)
    n_gate = IN_PROJ_DIM - GATE_COL
    p = jnp.concatenate([proj(GATE_COL, n_gate), jnp.zeros((tm, LANES - n_gate), F32)], axis=-1)
    qkv_finish(2, v_ref)
    zg_ref[0] = _silu(z)
    for ext_ref in ext_refs:
        ext_ref[0:HALO, :] = ext_ref[tm:tm + HALO, :]

    beta = jax.nn.sigmoid(p)
    g = -jnp.exp(alog_ref[...]) * _softplus(p + dtb_ref[...])
    r = lax.broadcasted_iota(I32, (CHUNK, CHUNK), 0)
    c = lax.broadcasted_iota(I32, (CHUNK, CHUNK), 1)
    tri = jnp.where(c <= r, 1.0, 0.0).astype(BF16)
    parts = _split_bf16(g, 3)
    gc = jnp.concatenate(
        [sum(_dot(tri, p[ci * CHUNK:(ci + 1) * CHUNK]) for p in parts) for ci in range(tm // CHUNK)], axis=0)
    lane = lax.broadcasted_iota(I32, (tm, LANES), 1)
    slab = jnp.where(lane < DN_HEADS, beta, gc)
    gcol_ref[0] = slab
    rows = slab.T[0:HALO, :]
    for ci in range(tm // CHUNK):
        grow_ref[0, ci] = rows[:, ci * CHUNK:(ci + 1) * CHUNK]


def _inproj(x, mix_g, w_in16, conv_a_w, conv_a_g, dn_conv_w, alog_vec, dtb_vec, gmat):
    B, S, D = x.shape
    tm = TM_IN
    full = lambda shape: pl.BlockSpec(shape, lambda b, t: (0,) * len(shape))
    row = lambda width: pl.BlockSpec((1, tm, width), lambda b, t: (b, t, 0))
    out_shape = (
        jax.ShapeDtypeStruct((B, S, CONV_A_DIM), BF16),
        jax.ShapeDtypeStruct((B, S, DN_DIM), F32),
        jax.ShapeDtypeStruct((B, S, DN_DIM), F32),
        jax.ShapeDtypeStruct((B, S, DN_DIM), F32),
        jax.ShapeDtypeStruct((B, S, DN_DIM), F32),
        jax.ShapeDtypeStruct((B, S, LANES), F32),
        jax.ShapeDtypeStruct((B, S // CHUNK, HALO, CHUNK), F32),
    )
    return pl.pallas_call(
        _inproj_kernel,
        grid=(B, S // tm),
        in_specs=[row(D), full((1, D)), full((D, IN_PROJ_DIM)), full((CONV_A_WIDTH, CONV_A_DIM)),
                  full((1, CONV_A_DIM)), full((DN_CONV_WIDTH, 3 * DN_DIM)), full((1, LANES)),
                  full((1, LANES)), full((CONV_A_DIM, CONV_A_DIM))],
        out_specs=(row(CONV_A_DIM), row(DN_DIM), row(DN_DIM), row(DN_DIM), row(DN_DIM), row(LANES),
                   pl.BlockSpec((1, tm // CHUNK, HALO, CHUNK), lambda b, t: (b, t, 0, 0))),
        out_shape=out_shape,
        scratch_shapes=[pltpu.VMEM((tm + HALO, CONV_A_DIM), F32)] + [pltpu.VMEM((tm + HALO, DN_DIM), F32)] * 3,
        compiler_params=pltpu.CompilerParams(
            dimension_semantics=("arbitrary", "arbitrary"), vmem_limit_bytes=VMEM_LIMIT),
        name="inproj",
    )(x, mix_g, w_in16, conv_a_w, conv_a_g, dn_conv_w, alog_vec, dtb_vec, gmat)


def _delta_prep_kernel(q_ref, k_ref, v_ref, gcol_ref, grow_ref,
                       u_ref, w_ref, qd_ref, kd_ref, at_ref):
    tm = q_ref.shape[1]
    ri = lax.broadcasted_iota(I32, (CHUNK, CHUNK), 0)
    ci = lax.broadcasted_iota(I32, (CHUNK, CHUNK), 1)
    causal = ci <= ri
    strict = ci < ri
    eye = jnp.where(ci == ri, 1.0, 0.0).astype(F32)
    lane = lax.broadcasted_iota(I32, (CHUNK, LANES), 1)
    left_half = lane < CHUNK

    def setup(c, h):
        rows = slice(c * CHUNK, (c + 1) * CHUNK)
        cols = slice(h * DN_HEAD_DIM, (h + 1) * DN_HEAD_DIM)
        slab = gcol_ref[0, rows, :]
        beta = jnp.sum(jnp.where(lane == h, slab, 0.0), axis=-1, keepdims=True)
        gcc = jnp.sum(jnp.where(lane == h + DN_HEADS, slab, 0.0), axis=-1, keepdims=True)
        gcr = grow_ref[0, c, h + DN_HEADS:h + DN_HEADS + 1, :]
        diff = gcc - gcr
        decay = jnp.where(causal, jnp.exp(jnp.where(causal, diff, 0.0)), 0.0)
        q = q_ref[0, rows, cols]
        k = k_ref[0, rows, cols]
        v = v_ref[0, rows, cols]
        kb = k * beta
        kq = _dot_nt(jnp.concatenate([kb.astype(BF16), q.astype(BF16)], axis=0), k.astype(BF16))
        L = jnp.where(strict, kq[:CHUNK] * decay, 0.0)
        intra = kq[CHUNK:] * decay
        egc = jnp.exp(gcc)
        gl = gcr[:, CHUNK - 1:CHUNK]
        qd_ref[0, rows, cols] = (q * egc).astype(BF16)
        kd_ref[0, rows, cols] = (k * jnp.exp(gl - gcc)).astype(BF16)
        at_ref[0, rows, h * CHUNK:(h + 1) * CHUNK] = intra.astype(BF16)
        return -L, jnp.concatenate([v * beta, kb * egc], axis=-1).astype(BF16)

    def first_stage(m):
        m16 = m.astype(BF16)
        return jnp.concatenate([_dot(m16, m16), eye + m], axis=-1)

    def stage(r):
        return _dot(r[:, :CHUNK].astype(BF16), r.astype(BF16)) + jnp.where(left_half, 0.0, r)

    def last_stage(r):
        return (stage(r))[:, CHUNK:]

    def solve(c, h, t, rhs):
        rows = slice(c * CHUNK, (c + 1) * CHUNK)
        cols = slice(h * DN_HEAD_DIM, (h + 1) * DN_HEAD_DIM)
        uw = _dot(t.astype(BF16), rhs)
        u_ref[0, rows, cols] = uw[:, :DN_HEAD_DIM]
        w_ref[0, rows, cols] = uw[:, DN_HEAD_DIM:].astype(BF16)

    chains = [(c, h) for c in range(tm // CHUNK) for h in range(DN_HEADS)]
    ms, rhss = zip(*[setup(c, h) for c, h in chains])
    rs = [first_stage(m) for m in ms]
    for _ in range(4):
        rs = [stage(r) for r in rs]
    ts = [last_stage(r) for r in rs]
    for (c, h), t, rhs in zip(chains, ts, rhss):
        solve(c, h, t, rhs)


def _delta_prep(q, k, v, gcol, grow):
    B, S, _ = q.shape
    tm = TM_PREP
    row = lambda width: pl.BlockSpec((1, tm, width), lambda b, t: (b, t, 0))
    return pl.pallas_call(
        _delta_prep_kernel,
        grid=(B, S // tm),
        in_specs=[row(DN_DIM), row(DN_DIM), row(DN_DIM), row(LANES),
                  pl.BlockSpec((1, tm // CHUNK, HALO, CHUNK), lambda b, t: (b, t, 0, 0))],
        out_specs=(row(DN_DIM), row(DN_DIM), row(DN_DIM), row(DN_DIM), row(DN_HEADS * CHUNK)),
        out_shape=(jax.ShapeDtypeStruct((B, S, DN_DIM), F32),
                   jax.ShapeDtypeStruct((B, S, DN_DIM), BF16),
                   jax.ShapeDtypeStruct((B, S, DN_DIM), BF16),
                   jax.ShapeDtypeStruct((B, S, DN_DIM), BF16),
                   jax.ShapeDtypeStruct((B, S, DN_HEADS * CHUNK), BF16)),
        compiler_params=pltpu.CompilerParams(
            dimension_semantics=("arbitrary", "arbitrary"), vmem_limit_bytes=VMEM_LIMIT),
        name="delta_prep",
    )(q, k, v, gcol, grow)


def _delta_scan_kernel(u_ref, w_ref, qd_ref, kd_ref, at_ref, zg_ref, grow_ref, ng_ref, o_ref, st_ref):
    nb, tm = u_ref.shape[0], u_ref.shape[1]

    @pl.when(pl.program_id(1) == 0)
    def _():
        st_ref[...] = jnp.zeros(st_ref.shape, F32)

    def step(c, carry):
        r0 = pl.multiple_of(c * CHUNK, CHUNK)
        rows = pl.ds(r0, CHUNK)
        chains = [(b, h) for b in range(nb) for h in range(DN_HEADS)]
        cols = lambda h: slice(h * DN_HEAD_DIM, (h + 1) * DN_HEAD_DIM)
        sts = [st_ref[b, h] for b, h in chains]
        st16s = [st.astype(BF16) for st in sts]
        wqs = [_dot(jnp.concatenate([w_ref[b, rows, cols(h)], qd_ref[b, rows, cols(h)]], axis=0), st16)
               for (b, h), st16 in zip(chains, st16s)]
        vn16s = [(u_ref[b, rows, cols(h)] - wq[:CHUNK]).astype(BF16) for (b, h), wq in zip(chains, wqs)]
        upd = [_dot_tn(kd_ref[b, rows, cols(h)], vn16) for (b, h), vn16 in zip(chains, vn16s)]
        avs = [_dot(at_ref[b, rows, h * CHUNK:(h + 1) * CHUNK], vn16) for (b, h), vn16 in zip(chains, vn16s)]
        for (b, h), st, up in zip(chains, sts, upd):
            gl = grow_ref[b, c, h + DN_HEADS:h + DN_HEADS + 1, CHUNK - 1:CHUNK]
            st_ref[b, h] = st * jnp.exp(gl) + up
        for (b, h), wq, av in zip(chains, wqs, avs):
            o = wq[CHUNK:] + av
            on = o * lax.rsqrt(jnp.mean(o * o, axis=-1, keepdims=True) + EPS) * ng_ref[...]
            o_ref[b, rows, cols(h)] = (on * zg_ref[b, rows, cols(h)]).astype(o_ref.dtype)
        return carry

    lax.fori_loop(0, tm // CHUNK, step, 0)


def _delta_scan(u, w, qd, kd, at, zg, grow, norm_g):
    B, S, _ = u.shape
    tm = TM_SCAN
    nb = SCAN_BATCH
    row = lambda width: pl.BlockSpec((nb, tm, width), lambda b, t: (b, t, 0))
    return pl.pallas_call(
        _delta_scan_kernel,
        grid=(B // nb, S // tm),
        in_specs=[row(DN_DIM), row(DN_DIM), row(DN_DIM), row(DN_DIM), row(DN_HEADS * CHUNK), row(DN_DIM),
                  pl.BlockSpec((nb, tm // CHUNK, HALO, CHUNK), lambda b, t: (b, t, 0, 0)),
                  pl.BlockSpec((1, DN_HEAD_DIM), lambda b, t: (0, 0))],
        out_specs=row(DN_DIM),
        out_shape=jax.ShapeDtypeStruct((B, S, DN_DIM), BF16),
        scratch_shapes=[pltpu.VMEM((nb, DN_HEADS, DN_HEAD_DIM, DN_HEAD_DIM), F32)],
        compiler_params=pltpu.CompilerParams(
            dimension_semantics=("arbitrary", "arbitrary"), vmem_limit_bytes=VMEM_LIMIT),
        name="delta_scan",
    )(u, w, qd, kd, at, zg, grow, norm_g)


def _outproj_kernel(ya_ref, yb_ref, x_ref, wo_ref, g_ref, rw_ref,
                    x1_ref, h2_ref, ri_ref, rg_ref, cnt_ref, base_ref):
    tm = x_ref.shape[0]
    sub = tm // OUT_SPLIT
    parts = range(OUT_SPLIT)
    rows_of = lambda s: slice(s * sub, (s + 1) * sub)

    @pl.when(pl.program_id(0) == 0)
    def _():
        base_ref[...] = jnp.zeros(base_ref.shape, F32)

    def project(s):
        rows = rows_of(s)
        return _dot(jnp.concatenate([ya_ref[rows, :], yb_ref[rows, :]], axis=-1), wo_ref[...])

    def normalise(s, y):
        rows = rows_of(s)
        x1 = x_ref[rows, :] + y
        x1_ref[rows, :] = x1
        h = x1 * lax.rsqrt(jnp.mean(x1 * x1, axis=-1, keepdims=True) + EPS) * g_ref[...]
        _store_rows(h2_ref.at[pl.ds(s * sub * ROW_SLAB, sub * ROW_SLAB)], h)
        return h

    def router_logits(h):
        h_hi, h_lo = _split_bf16(h, 2)
        hi_prod = _dot(h_hi, rw_ref[...])
        return hi_prod[:, :LANES] + (hi_prod[:, LANES:] + _dot(h_lo, rw_ref[:, :LANES]))

    lane = lax.broadcasted_iota(I32, (sub, LANES), 1)
    lanef = lane.astype(F32)
    rr = lax.broadcasted_iota(I32, (sub, sub), 0)
    cc = lax.broadcasted_iota(I32, (sub, sub), 1)
    tri = jnp.where(cc < rr, 1.0, 0.0).astype(BF16)

    def route(s, logits):
        rows = rows_of(s)
        neg = jnp.float32(-1e30)
        big = jnp.float32(1e9)
        is_g = lane < N_GROUPS
        gl = jnp.where(is_g, logits, neg)
        gmax = jnp.max(gl, axis=-1, keepdims=True)
        gidx = jnp.min(jnp.where(gl == gmax, lanef, big), axis=-1, keepdims=True)
        gsum = jnp.sum(jnp.where(is_g, jnp.exp(gl - gmax), 0.0), axis=-1, keepdims=True)
        gprob = 1.0 / gsum
        lo = N_GROUPS + EXPERTS_PER_GROUP * gidx
        emask = (lanef >= lo) & (lanef < lo + EXPERTS_PER_GROUP)
        el = jnp.where(emask, logits, neg)
        e1 = jnp.max(el, axis=-1, keepdims=True)
        i1 = jnp.min(jnp.where(el == e1, lanef, big), axis=-1, keepdims=True)
        el2 = jnp.where(lanef == i1, neg, el)
        e2 = jnp.max(el2, axis=-1, keepdims=True)
        i2 = jnp.min(jnp.where(el2 == e2, lanef, big), axis=-1, keepdims=True)
        r = jnp.exp(e2 - e1)
        gate1 = gprob / (1.0 + r)
        gate2 = gprob * r / (1.0 + r)
        id1 = i1 - N_GROUPS
        id2 = i2 - N_GROUPS

        oh1 = jnp.where(lanef == id1, 1.0, 0.0).astype(F32)
        oh2 = jnp.where(lanef == id2, 1.0, 0.0).astype(F32)
        oh = oh1 + oh2
        before = _dot(tri, oh.astype(BF16)) + base_ref[...]
        rank1 = jnp.sum(oh1 * before, axis=-1, keepdims=True)
        rank2 = jnp.sum(oh2 * before, axis=-1, keepdims=True)
        base_ref[...] = base_ref[...] + jnp.sum(oh, axis=0, keepdims=True)

        ri = jnp.where(lane == 0, rank1, jnp.where(lane == 1, rank2,
                       jnp.where(lane == 2, id1, jnp.where(lane == 3, id2, 0.0))))
        ri_ref[rows, :] = ri.astype(I32)
        rg_ref[rows, :] = jnp.where(lane == 0, gate1, jnp.where(lane == 1, gate2, 0.0))

    ys = [project(s) for s in parts]
    hs = [normalise(s, y) for s, y in zip(parts, ys)]
    ls = [router_logits(h) for h in hs]
    for s, logits in zip(parts, ls):
        route(s, logits)
    cnt_ref[...] = base_ref[...]


def _outproj(ya, yb, x, w_out16, ffn_g, rw):
    N, D = x.shape
    tm = TM_OUT
    row = lambda width: pl.BlockSpec((tm, width), lambda i: (i, 0))
    full = lambda shape: pl.BlockSpec(shape, lambda i: (0,) * len(shape))
    return pl.pallas_call(
        _outproj_kernel,
        grid=(N // tm,),
        in_specs=[row(CONV_A_DIM), row(DN_DIM), row(D), full((D, D)), full((1, D)), full((D, 2 * LANES))],
        out_specs=(row(D), pl.BlockSpec((tm * ROW_SLAB, LANES), lambda i: (i, 0)), row(LANES), row(LANES),
                   full((1, LANES))),
        out_shape=(jax.ShapeDtypeStruct((N, D), F32),
                   jax.ShapeDtypeStruct((N * ROW_SLAB, LANES), F32),
                   jax.ShapeDtypeStruct((N, LANES), I32),
                   jax.ShapeDtypeStruct((N, LANES), F32),
                   jax.ShapeDtypeStruct((1, LANES), F32)),
        scratch_shapes=[pltpu.VMEM((1, LANES), F32)],
        compiler_params=pltpu.CompilerParams(
            dimension_semantics=("arbitrary",), vmem_limit_bytes=VMEM_LIMIT),
        name="outproj",
    )(ya, yb, x, w_out16, ffn_g, rw)


def _slots_kernel(ri_ref, sel_ref, tab_ref, o_ref):
    rif = ri_ref[...].astype(F32)
    lane = lax.broadcasted_iota(I32, rif.shape, 1)
    lanef = lane.astype(F32)
    ids = _dot(rif.astype(BF16), sel_ref[...])
    starts = [_dot(jnp.where(lanef == ids[:, k * LANES:(k + 1) * LANES], 1.0, 0.0).astype(BF16), tab_ref[...])
              for k in range(TOP_K)]
    slab = FFN_BLOCK * jnp.where(lane == 0, starts[0], starts[1]) + rif
    o_ref[...] = slab.T[0:HALO, :].astype(I32)


def _slots(ri, seg_start):
    N = ri.shape[0]
    tm = TM_SLOTS
    lane = jnp.arange(LANES, dtype=I32)
    sel = jnp.concatenate([(lane[:, None] == TOP_K + k) & (lane[None, :] >= 0) for k in range(TOP_K)], axis=1)
    tab = jnp.zeros((LANES, LANES), F32).at[:N_EXPERTS, :].set((seg_start // FFN_BLOCK)[:, None].astype(F32))
    out = pl.pallas_call(
        _slots_kernel,
        grid=(N // tm,),
        in_specs=[pl.BlockSpec((tm, LANES), lambda i: (i, 0)),
                  pl.BlockSpec((LANES, TOP_K * LANES), lambda i: (0, 0)),
                  pl.BlockSpec((LANES, LANES), lambda i: (0, 0))],
        out_specs=pl.BlockSpec((HALO, tm), lambda i: (0, i)),
        out_shape=jax.ShapeDtypeStruct((HALO, N), I32),
        compiler_params=pltpu.CompilerParams(dimension_semantics=("arbitrary",)),
        name="slots",
    )(ri, sel.astype(BF16), tab.astype(BF16))
    return out[:TOP_K].reshape(-1)


def _dispatch_kernel(seg_ref, slot0_ref, slot1_ref, h_ref, xs_ref, zero_ref, sem, zsem):
    tm = TM_DISPATCH
    bm = FFN_BLOCK
    n_blocks = xs_ref.shape[0] // (bm * ROW_SLAB)

    @pl.when(pl.program_id(0) == 0)
    def _():
        zero_ref[...] = jnp.zeros(zero_ref.shape, F32)

        def run_copy(first_row, n_rows):
            start = pl.multiple_of(first_row * ROW_SLAB, ROW_SLAB)
            return pltpu.make_async_copy(zero_ref.at[pl.ds(0, n_rows * ROW_SLAB)],
                                         xs_ref.at[pl.ds(start, n_rows * ROW_SLAB)], zsem)

        def block_copy(b):
            return pltpu.make_async_copy(zero_ref, _slab_block(xs_ref, b), zsem)

        def each_pad(fn):
            def per_expert(e, carry):
                row = seg_ref[e]
                n_pad = seg_ref[N_EXPERTS + e] - row
                run = bm // 2
                while run >= 1:
                    @pl.when((n_pad & run) != 0)
                    def _(row=row, run=run):
                        fn(run_copy(row, run))
                    row = row + (n_pad & run)
                    run //= 2
                return carry
            lax.fori_loop(0, N_EXPERTS, per_expert, 0)

            def per_block(b, c):
                fn(block_copy(b))
                return c
            lax.fori_loop(seg_ref[2 * N_EXPERTS], n_blocks, per_block, 0)

        each_pad(lambda cp: cp.start())
        each_pad(lambda cp: cp.wait())

    def issue(t, carry):
        for k, slot_ref in enumerate((slot0_ref, slot1_ref)):
            pltpu.make_async_copy(_slab(h_ref, t), _slab(xs_ref, slot_ref[t]), sem).start(priority=k)
        return carry

    lax.fori_loop(0, tm, issue, 0, unroll=ISSUE_UNROLL)
    for _ in range(TOP_K):
        pltpu.make_async_copy(h_ref, xs_ref.at[pl.ds(0, tm * ROW_SLAB)], sem).wait()


def _dispatch(seg, slots, h2, n_slots):
    N = h2.shape[0] // ROW_SLAB
    tm = TM_DISPATCH
    grid_spec = pltpu.PrefetchScalarGridSpec(
        num_scalar_prefetch=1,
        grid=(N // tm,),
        in_specs=[pl.BlockSpec((tm,), lambda i, s: (i,), memory_space=pltpu.SMEM),
                  pl.BlockSpec((tm,), lambda i, s: (N // tm + i,), memory_space=pltpu.SMEM),
                  pl.BlockSpec((tm * ROW_SLAB, LANES), lambda i, s: (i, 0))],
        out_specs=pl.BlockSpec(memory_space=pl.ANY),
        scratch_shapes=[pltpu.VMEM((FFN_BLOCK * ROW_SLAB, LANES), F32),
                        pltpu.SemaphoreType.DMA(()), pltpu.SemaphoreType.DMA(())],
    )
    return pl.pallas_call(
        _dispatch_kernel,
        grid_spec=grid_spec,
        out_shape=jax.ShapeDtypeStruct((n_slots * ROW_SLAB, LANES), F32),
        compiler_params=pltpu.CompilerParams(dimension_semantics=("arbitrary",)),
        name="dispatch",
    )(seg, slots, slots, h2)


def _ffn_kernel(blk_ref, xs_ref, wg_ref, wu_ref, wd_ref, ys_ref,
                xbuf, ybuf, wg16, wu16, wd16, xsem, ysem, zsem):
    e = pl.program_id(0)
    bm = FFN_BLOCK
    ring = FFN_RING
    n_blocks = ys_ref.shape[0] // (bm * ROW_SLAB)
    first = blk_ref[e]
    count = blk_ref[N_EXPERTS + e]
    n_used = blk_ref[2 * N_EXPERTS]

    def x_copy(g):
        return pltpu.make_async_copy(_slab_block(xs_ref, g), xbuf.at[g % ring], xsem.at[g % ring])

    def y_copy(g):
        return pltpu.make_async_copy(ybuf.at[g % ring], _slab_block(ys_ref, g), ysem.at[g % ring])

    def when_block(g, fn):
        @pl.when((g >= 0) & (g < n_used))
        def _():
            fn(g)

    @pl.when(e == 0)
    def _():
        for g in range(ring - 1):
            when_block(g, lambda g: x_copy(g).start(priority=RING_PRIORITY))

    @pl.when(count > 0)
    def _():
        wg16[...] = wg_ref[0].astype(BF16)
        wu16[...] = wu_ref[0].astype(BF16)
        wd16[...] = wd_ref[0].astype(BF16)

    def block(j, carry):
        g = first + j
        x_copy(g).wait()
        when_block(g + ring - 1, lambda g: x_copy(g).start(priority=RING_PRIORITY))
        when_block(g - ring, lambda g: y_copy(g).wait())

        x = _load_rows(xbuf.at[g % ring], bm).astype(BF16)
        sub = bm // FFN_SPLIT
        x_parts = [x[s * sub:(s + 1) * sub] for s in range(FFN_SPLIT)]
        gates = [_dot(xp, wg16[...]) for xp in x_parts]
        ups = [_dot(xp, wu16[...]) for xp in x_parts]
        acts = [(_silu(a) * b).astype(BF16) for a, b in zip(gates, ups)]
        y = jnp.concatenate([_dot(act, wd16[...]) for act in acts], axis=0)
        _store_rows(ybuf.at[g % ring], y)
        y_copy(g).start(priority=RING_PRIORITY)
        return carry

    lax.fori_loop(0, count, block, 0)

    @pl.when(e == pl.num_programs(0) - 1)
    def _():
        for back in range(ring, 0, -1):
            when_block(n_used - back, lambda g: y_copy(g).wait())
        ybuf[0] = jnp.zeros(ybuf.shape[1:], F32)

        def zero_copy(g):
            return pltpu.make_async_copy(ybuf.at[0], _slab_block(ys_ref, g), zsem)

        def start(g, c):
            zero_copy(g).start()
            return c

        def wait(g, c):
            zero_copy(g).wait()
            return c

        lax.fori_loop(n_used, n_blocks, start, 0)
        lax.fori_loop(n_used, n_blocks, wait, 0)


def _ffn(blk, xs, w_gate, w_up, w_down):
    bm = FFN_BLOCK
    D = D_MODEL
    weights = lambda shape: pl.BlockSpec((1,) + shape, lambda e, blk: (e, 0, 0))
    grid_spec = pltpu.PrefetchScalarGridSpec(
        num_scalar_prefetch=1,
        grid=(N_EXPERTS,),
        in_specs=[pl.BlockSpec(memory_space=pl.ANY),
                  weights((D, EXPERT_FF)), weights((D, EXPERT_FF)), weights((EXPERT_FF, D))],
        out_specs=pl.BlockSpec(memory_space=pl.ANY),
        scratch_shapes=[pltpu.VMEM((FFN_RING, bm * ROW_SLAB, LANES), F32),
                        pltpu.VMEM((FFN_RING, bm * ROW_SLAB, LANES), F32),
                        pltpu.VMEM((D, EXPERT_FF), BF16), pltpu.VMEM((D, EXPERT_FF), BF16),
                        pltpu.VMEM((EXPERT_FF, D), BF16),
                        pltpu.SemaphoreType.DMA((FFN_RING,)), pltpu.SemaphoreType.DMA((FFN_RING,)),
                        pltpu.SemaphoreType.DMA(())],
    )
    return pl.pallas_call(
        _ffn_kernel,
        grid_spec=grid_spec,
        out_shape=jax.ShapeDtypeStruct(xs.shape, F32),
        compiler_params=pltpu.CompilerParams(
            dimension_semantics=("arbitrary",), vmem_limit_bytes=VMEM_LIMIT),
        name="ffn",
    )(blk, xs, w_gate, w_up, w_down)


def _combine_kernel(slot0_ref, slot1_ref, next0_ref, next1_ref, ys_ref, x1_ref, rg_ref, g_ref, o_ref,
                    buf_ref, sems):
    tm = TM_COMBINE
    i = pl.program_id(0)
    half = i % 2

    def gather(slot_refs, s):
        def issue(t, carry):
            for k, slot_ref in enumerate(slot_refs):
                pltpu.make_async_copy(_slab(ys_ref, slot_ref[t]), _slab(buf_ref.at[s, k], t),
                                      sems.at[s]).start(priority=k)
            return carry
        lax.fori_loop(0, tm, issue, 0, unroll=ISSUE_UNROLL)

    @pl.when(i == 0)
    def _():
        gather((slot0_ref, slot1_ref), 0)

    @pl.when(i + 1 < pl.num_programs(0))
    def _():
        gather((next0_ref, next1_ref), 1 - half)

    for k in range(TOP_K):
        pltpu.make_async_copy(ys_ref.at[pl.ds(0, tm * ROW_SLAB)], buf_ref.at[half, k], sems.at[half]).wait()

    rg = rg_ref[...]
    moe = (_load_rows(buf_ref.at[half, 0], tm) * rg[:, 0:1]
           + _load_rows(buf_ref.at[half, 1], tm) * rg[:, 1:2])
    x2 = x1_ref[...] + moe
    o_ref[...] = x2 * lax.rsqrt(jnp.mean(x2 * x2, axis=-1, keepdims=True) + EPS) * g_ref[...]


def _combine(slots, ys, x1, rg, final_g):
    N, D = x1.shape
    tm = TM_COMBINE
    n_tiles = N // tm

    def slot_block(k, ahead, i):
        return (k * n_tiles + jnp.minimum(i + ahead, n_tiles - 1),)

    return pl.pallas_call(
        _combine_kernel,
        grid=(n_tiles,),
        in_specs=[pl.BlockSpec((tm,), functools.partial(slot_block, k, ahead), memory_space=pltpu.SMEM)
                  for ahead in (0, 1) for k in range(TOP_K)]
                 + [pl.BlockSpec(memory_space=pl.ANY),
                  pl.BlockSpec((tm, D), lambda i: (i, 0)),
                  pl.BlockSpec((tm, LANES), lambda i: (i, 0)),
                  pl.BlockSpec((1, D), lambda i: (0, 0))],
        out_specs=pl.BlockSpec((tm, D), lambda i: (i, 0)),
        out_shape=jax.ShapeDtypeStruct((N, D), F32),
        scratch_shapes=[pltpu.VMEM((2, TOP_K, tm * ROW_SLAB, LANES), F32), pltpu.SemaphoreType.DMA((2,))],
        compiler_params=pltpu.CompilerParams(
            dimension_semantics=("arbitrary",), vmem_limit_bytes=VMEM_LIMIT),
        name="combine",
    )(slots, slots, slots, slots, ys, x1, rg, final_g)


def _lane_vec(values, offset):
    return jnp.zeros((1, LANES), F32).at[0, offset:offset + values.shape[0]].set(values.astype(F32))


def kernel(x, mix_norm_g, w_in, conv_a_w, conv_a_norm_g, dn_conv_w, dn_a_log, dn_dt_bias, dn_norm_g,
           w_out, ffn_norm_g, router_group_w, router_expert_w, w_gate, w_up, w_down, final_norm_g):
    B, S, D = x.shape
    N = B * S
    depth = w_in.shape[0]
    assert depth == 1, "single-layer block: the final RMSNorm is fused into the layer's combine step"
    group_of = jnp.arange(CONV_A_DIM, dtype=I32) // CONV_A_GROUP_DIM
    gmat = jnp.where(group_of[:, None] == group_of[None, :], 1.0 / CONV_A_GROUP_DIM, 0.0).astype(BF16)
    bm = FFN_BLOCK
    n_blocks = (N * TOP_K) // bm + N_EXPERTS
    for l in range(depth):
        ya, q, k, v, zg, gcol, grow = _inproj(
            x, mix_norm_g[l][None, :], w_in[l].astype(BF16), conv_a_w[l], conv_a_norm_g[l][None, :], dn_conv_w[l],
            _lane_vec(dn_a_log[l], DN_HEADS), _lane_vec(dn_dt_bias[l], DN_HEADS), gmat)
        u, w, qd, kd, at = _delta_prep(q, k, v, gcol, grow)
        yb = _delta_scan(u, w, qd, kd, at, zg, grow, dn_norm_g[l][None, :])
        rw = jnp.pad(jnp.concatenate([router_group_w[l], router_expert_w[l]], axis=1),
                     ((0, 0), (0, LANES - N_GROUPS - N_EXPERTS)))
        rw_hi = rw.astype(BF16)
        rw = jnp.concatenate([rw_hi, (rw - rw_hi.astype(F32)).astype(BF16)], axis=1)
        x1, h2, ri, rg, cnt = _outproj(ya.reshape(N, CONV_A_DIM), yb.reshape(N, DN_DIM), x.reshape(N, D),
                                       w_out[l].astype(BF16), ffn_norm_g[l][None, :], rw)
        counts = cnt[0, :N_EXPERTS].astype(I32)
        padded = (counts + bm - 1) // bm * bm
        seg_end = jnp.cumsum(padded).astype(I32)
        seg_start = seg_end - padded
        n_used = (seg_end[-1:] // bm)
        slots = _slots(ri, seg_start)
        seg = jnp.concatenate([seg_start + counts, seg_end, n_used])
        xs = _dispatch(seg, slots, h2, n_blocks * bm)
        blk = jnp.concatenate([seg_start // bm, padded // bm, n_used])
        ys = _ffn(blk, xs, w_gate[l], w_up[l], w_down[l])
        x = _combine(slots, ys, x1, rg, final_norm_g[None, :]).reshape(B, S, D)
    return x
```

```python
import functools

import jax
import jax.numpy as jnp
from jax import lax
from jax.experimental import pallas as pl
from jax.experimental.pallas import tpu as pltpu

F32 = jnp.float32
BF16 = jnp.bfloat16
I32 = jnp.int32

D_MODEL = 1024
CHUNK = 64
CONV_A_GROUP_DIM = 64
CONV_A_DIM = 512
CONV_A_WIDTH = 3
DN_HEADS = 4
DN_HEAD_DIM = 128
DN_DIM = 512
DN_CONV_WIDTH = 4
IN_PROJ_DIM = 3 * CONV_A_DIM + 4 * DN_DIM + 2 * DN_HEADS
N_GROUPS = 4
EXPERTS_PER_GROUP = 8
N_EXPERTS = 32
TOP_K = 2
EXPERT_FF = 512
EPS = 1e-6

LANES = 128
HALO = 8
GATE_COL = 3 * CONV_A_DIM + 4 * DN_DIM

TM_IN = 512
TM_PREP = 256
SCAN_BATCH = 8
TM_SCAN = 256
TM_OUT = 1024
OUT_SPLIT = 8
FFN_BLOCK = 256
TM_SLOTS = 2048
TM_DISPATCH = 2048
TM_COMBINE = 512
FFN_SPLIT = 2
FFN_RING = 4
RING_PRIORITY = 1
ISSUE_UNROLL = 8
VMEM_LIMIT = 56 * 1024 * 1024


def _dot(a, b):
    return jnp.dot(a, b, preferred_element_type=F32)


def _dot_nt(a, b):
    return lax.dot_general(a, b, (((1,), (1,)), ((), ())), preferred_element_type=F32)


def _dot_tn(a, b):
    return lax.dot_general(a, b, (((0,), (0,)), ((), ())), preferred_element_type=F32)


def _split_bf16(x, parts):
    out = []
    for _ in range(parts):
        p = x.astype(BF16)
        out.append(p)
        x = x - p.astype(F32)
    return out


def _silu(x):
    return x * jax.nn.sigmoid(x)


ROW_SLAB = D_MODEL // LANES


def _store_rows(ref, val):
    m = val.shape[0]
    for c in range(ROW_SLAB):
        ref[pl.ds(c, m, stride=ROW_SLAB), :] = val[:, c * LANES:(c + 1) * LANES]


def _load_rows(ref, m):
    return jnp.concatenate([ref[pl.ds(c, m, stride=ROW_SLAB), :] for c in range(ROW_SLAB)], axis=-1)


def _slab(ref, row):
    return ref.at[pl.ds(pl.multiple_of(row * ROW_SLAB, ROW_SLAB), ROW_SLAB)]


def _slab_block(ref, block):
    n = FFN_BLOCK * ROW_SLAB
    return ref.at[pl.ds(pl.multiple_of(block * n, n), n)]


def _causal_conv(ext, w):
    taps = w.shape[0]
    delayed = pltpu.roll(ext, 1, 0)
    if taps == 4:
        near = w[3:4, :] * ext[HALO:] + w[2:3, :] * delayed[HALO:]
        far = w[1:2, :] * ext + w[0:1, :] * delayed
        return near + pltpu.roll(far, 2, 0)[HALO:]
    assert taps == 3
    acc = w[0:1, :] * pltpu.roll(ext, 2, 0)[HALO:]
    acc = acc + w[1:2, :] * delayed[HALO:]
    return acc + w[2:3, :] * ext[HALO:]


def _softplus(x):
    return jnp.maximum(x, 0.0) + jnp.log1p(jnp.exp(-jnp.abs(x)))


def _inproj_kernel(x_ref, g_ref, w_ref, caw_ref, cag_ref, dcw_ref, alog_ref, dtb_ref, gmat_ref,
                   ya_ref, q_ref, k_ref, v_ref, zg_ref, gcol_ref, grow_ref,
                   exta_ref, *extq_refs):
    tm = x_ref.shape[1]
    ext_refs = (exta_ref,) + extq_refs

    @pl.when(pl.program_id(1) == 0)
    def _():
        for ext_ref in ext_refs:
            ext_ref[0:HALO, :] = jnp.zeros((HALO, ext_ref.shape[1]), F32)

    x = x_ref[0]
    ms = jnp.mean(x * x, axis=-1, keepdims=True)
    hb = (x * lax.rsqrt(ms + EPS) * g_ref[...]).astype(BF16)

    def proj(c0, width):
        return _dot(hb, w_ref[:, c0:c0 + width])

    base = 3 * CONV_A_DIM

    def qkv_project(i):
        extq_refs[i][HALO:HALO + tm, :] = proj(base + i * DN_DIM, DN_DIM)

    def qkv_finish(i, out_ref):
        for h in range(DN_HEADS):
            hc = slice(h * DN_HEAD_DIM, (h + 1) * DN_HEAD_DIM)
            wc = slice(i * DN_DIM + h * DN_HEAD_DIM, i * DN_DIM + (h + 1) * DN_HEAD_DIM)
            sh = _silu(_causal_conv(extq_refs[i][:, hc], dcw_ref[:, wc]))
            if i != 2:
                inv = lax.rsqrt(jnp.sum(sh * sh, axis=-1, keepdims=True) + EPS)
                sh = sh * inv
                if i == 0:
                    sh = sh * (DN_HEAD_DIM ** -0.5)
            out_ref[0, :, hc] = sh

    def mixer_a_finish(a_b):
        for s in range(CONV_A_DIM // LANES):
            sc = slice(s * LANES, (s + 1) * LANES)
            y = a_b[:, sc] * _causal_conv(exta_ref[:, sc], caw_ref[:, sc])
            ysq = y * y
            hi = ysq.astype(BF16)
            lo = (ysq - hi.astype(F32)).astype(BF16)
            gblk = gmat_ref[sc, sc]
            gmean = _dot(hi, gblk) + _dot(lo, gblk)
            ya_ref[0, :, sc] = (y * lax.rsqrt(gmean + EPS) * cag_ref[:, sc]).astype(ya_ref.dtype)

    qkv_project(0)
    qkv_project(1)
    exta_ref[HALO:HALO + tm, :] = proj(2 * CONV_A_DIM, CONV_A_DIM) * proj(0, CONV_A_DIM)
    qkv_finish(0, q_ref)
    qkv_project(2)
    a_b = proj(CONV_A_DIM, CONV_A_DIM)
    qkv_finish(1, k_ref)
    z = proj(base + 3 * DN_DIM, DN_DIM)
    mixer_a_finish(a_b)
    n_gate = IN_PROJ_DIM - GATE_COL
    p = jnp.concatenate([proj(GATE_COL, n_gate), jnp.zeros((tm, LANES - n_gate), F32)], axis=-1)
    qkv_finish(2, v_ref)
    zg_ref[0] = _silu(z)
    for ext_ref in ext_refs:
        ext_ref[0:HALO, :] = ext_ref[tm:tm + HALO, :]

    beta = jax.nn.sigmoid(p)
    g = -jnp.exp(alog_ref[...]) * _softplus(p + dtb_ref[...])
    r = lax.broadcasted_iota(I32, (CHUNK, CHUNK), 0)
    c = lax.broadcasted_iota(I32, (CHUNK, CHUNK), 1)
    tri = jnp.where(c <= r, 1.0, 0.0).astype(BF16)
    parts = _split_bf16(g, 3)
    gc = jnp.concatenate(
        [sum(_dot(tri, p[ci * CHUNK:(ci + 1) * CHUNK]) for p in parts) for ci in range(tm // CHUNK)], axis=0)
    lane = lax.broadcasted_iota(I32, (tm, LANES), 1)
    slab = jnp.where(lane < DN_HEADS, beta, gc)
    gcol_ref[0] = slab
    rows = slab.T[0:HALO, :]
    for ci in range(tm // CHUNK):
        grow_ref[0, ci] = rows[:, ci * CHUNK:(ci + 1) * CHUNK]


def _inproj(x, mix_g, w_in16, conv_a_w, conv_a_g, dn_conv_w, alog_vec, dtb_vec, gmat):
    B, S, D = x.shape
    tm = TM_IN
    full = lambda shape: pl.BlockSpec(shape, lambda b, t: (0,) * len(shape))
    row = lambda width: pl.BlockSpec((1, tm, width), lambda b, t: (b, t, 0))
    out_shape = (
        jax.ShapeDtypeStruct((B, S, CONV_A_DIM), BF16),
        jax.ShapeDtypeStruct((B, S, DN_DIM), F32),
        jax.ShapeDtypeStruct((B, S, DN_DIM), F32),
        jax.ShapeDtypeStruct((B, S, DN_DIM), F32),
        jax.ShapeDtypeStruct((B, S, DN_DIM), F32),
        jax.ShapeDtypeStruct((B, S, LANES), F32),
        jax.ShapeDtypeStruct((B, S // CHUNK, HALO, CHUNK), F32),
    )
    return pl.pallas_call(
        _inproj_kernel,
        grid=(B, S // tm),
        in_specs=[row(D), full((1, D)), full((D, IN_PROJ_DIM)), full((CONV_A_WIDTH, CONV_A_DIM)),
                  full((1, CONV_A_DIM)), full((DN_CONV_WIDTH, 3 * DN_DIM)), full((1, LANES)),
                  full((1, LANES)), full((CONV_A_DIM, CONV_A_DIM))],
        out_specs=(row(CONV_A_DIM), row(DN_DIM), row(DN_DIM), row(DN_DIM), row(DN_DIM), row(LANES),
                   pl.BlockSpec((1, tm // CHUNK, HALO, CHUNK), lambda b, t: (b, t, 0, 0))),
        out_shape=out_shape,
        scratch_shapes=[pltpu.VMEM((tm + HALO, CONV_A_DIM), F32)] + [pltpu.VMEM((tm + HALO, DN_DIM), F32)] * 3,
        compiler_params=pltpu.CompilerParams(
            dimension_semantics=("arbitrary", "arbitrary"), vmem_limit_bytes=VMEM_LIMIT),
        name="inproj",
    )(x, mix_g, w_in16, conv_a_w, conv_a_g, dn_conv_w, alog_vec, dtb_vec, gmat)


def _delta_prep_kernel(q_ref, k_ref, v_ref, gcol_ref, grow_ref,
                       u_ref, w_ref, qd_ref, kd_ref, at_ref):
    tm = q_ref.shape[1]
    ri = lax.broadcasted_iota(I32, (CHUNK, CHUNK), 0)
    ci = lax.broadcasted_iota(I32, (CHUNK, CHUNK), 1)
    causal = ci <= ri
    strict = ci < ri
    eye = jnp.where(ci == ri, 1.0, 0.0).astype(F32)
    lane = lax.broadcasted_iota(I32, (CHUNK, LANES), 1)
    left_half = lane < CHUNK

    def setup(c, h):
        rows = slice(c * CHUNK, (c + 1) * CHUNK)
        cols = slice(h * DN_HEAD_DIM, (h + 1) * DN_HEAD_DIM)
        slab = gcol_ref[0, rows, :]
        beta = jnp.sum(jnp.where(lane == h, slab, 0.0), axis=-1, keepdims=True)
        gcc = jnp.sum(jnp.where(lane == h + DN_HEADS, slab, 0.0), axis=-1, keepdims=True)
        gcr = grow_ref[0, c, h + DN_HEADS:h + DN_HEADS + 1, :]
        diff = gcc - gcr
        decay = jnp.where(causal, jnp.exp(jnp.where(causal, diff, 0.0)), 0.0)
        q = q_ref[0, rows, cols]
        k = k_ref[0, rows, cols]
        v = v_ref[0, rows, cols]
        kb = k * beta
        kq = _dot_nt(jnp.concatenate([kb.astype(BF16), q.astype(BF16)], axis=0), k.astype(BF16))
        L = jnp.where(strict, kq[:CHUNK] * decay, 0.0)
        intra = kq[CHUNK:] * decay
        egc = jnp.exp(gcc)
        gl = gcr[:, CHUNK - 1:CHUNK]
        qd_ref[0, rows, cols] = (q * egc).astype(BF16)
        kd_ref[0, rows, cols] = (k * jnp.exp(gl - gcc)).astype(BF16)
        at_ref[0, rows, h * CHUNK:(h + 1) * CHUNK] = intra.astype(BF16)
        return -L, jnp.concatenate([v * beta, kb * egc], axis=-1).astype(BF16)

    def first_stage(m):
        m16 = m.astype(BF16)
        return jnp.concatenate([_dot(m16, m16), eye + m], axis=-1)

    def stage(r):
        return _dot(r[:, :CHUNK].astype(BF16), r.astype(BF16)) + jnp.where(left_half, 0.0, r)

    def last_stage(r):
        return (stage(r))[:, CHUNK:]

    def solve(c, h, t, rhs):
        rows = slice(c * CHUNK, (c + 1) * CHUNK)
        cols = slice(h * DN_HEAD_DIM, (h + 1) * DN_HEAD_DIM)
        uw = _dot(t.astype(BF16), rhs)
        u_ref[0, rows, cols] = uw[:, :DN_HEAD_DIM]
        w_ref[0, rows, cols] = uw[:, DN_HEAD_DIM:].astype(BF16)

    chains = [(c, h) for c in range(tm // CHUNK) for h in range(DN_HEADS)]
    ms, rhss = zip(*[setup(c, h) for c, h in chains])
    rs = [first_stage(m) for m in ms]
    for _ in range(4):
        rs = [stage(r) for r in rs]
    ts = [last_stage(r) for r in rs]
    for (c, h), t, rhs in zip(chains, ts, rhss):
        solve(c, h, t, rhs)


def _delta_prep(q, k, v, gcol, grow):
    B, S, _ = q.shape
    tm = TM_PREP
    row = lambda width: pl.BlockSpec((1, tm, width), lambda b, t: (b, t, 0))
    return pl.pallas_call(
        _delta_prep_kernel,
        grid=(B, S // tm),
        in_specs=[row(DN_DIM), row(DN_DIM), row(DN_DIM), row(LANES),
                  pl.BlockSpec((1, tm // CHUNK, HALO, CHUNK), lambda b, t: (b, t, 0, 0))],
        out_specs=(row(DN_DIM), row(DN_DIM), row(DN_DIM), row(DN_DIM), row(DN_HEADS * CHUNK)),
        out_shape=(jax.ShapeDtypeStruct((B, S, DN_DIM), F32),
                   jax.ShapeDtypeStruct((B, S, DN_DIM), BF16),
                   jax.ShapeDtypeStruct((B, S, DN_DIM), BF16),
                   jax.ShapeDtypeStruct((B, S, DN_DIM), BF16),
                   jax.ShapeDtypeStruct((B, S, DN_HEADS * CHUNK), BF16)),
        compiler_params=pltpu.CompilerParams(
            dimension_semantics=("arbitrary", "arbitrary"), vmem_limit_bytes=VMEM_LIMIT),
        name="delta_prep",
    )(q, k, v, gcol, grow)


def _delta_scan_kernel(u_ref, w_ref, qd_ref, kd_ref, at_ref, zg_ref, grow_ref, ng_ref, o_ref, st_ref):
    nb, tm = u_ref.shape[0], u_ref.shape[1]

    @pl.when(pl.program_id(1) == 0)
    def _():
        st_ref[...] = jnp.zeros(st_ref.shape, F32)

    def step(c, carry):
        r0 = pl.multiple_of(c * CHUNK, CHUNK)
        rows = pl.ds(r0, CHUNK)
        chains = [(b, h) for b in range(nb) for h in range(DN_HEADS)]
        cols = lambda h: slice(h * DN_HEAD_DIM, (h + 1) * DN_HEAD_DIM)
        sts = [st_ref[b, h] for b, h in chains]
        st16s = [st.astype(BF16) for st in sts]
        wqs = [_dot(jnp.concatenate([w_ref[b, rows, cols(h)], qd_ref[b, rows, cols(h)]], axis=0), st16)
               for (b, h), st16 in zip(chains, st16s)]
        vn16s = [(u_ref[b, rows, cols(h)] - wq[:CHUNK]).astype(BF16) for (b, h), wq in zip(chains, wqs)]
        upd = [_dot_tn(kd_ref[b, rows, cols(h)], vn16) for (b, h), vn16 in zip(chains, vn16s)]
        avs = [_dot(at_ref[b, rows, h * CHUNK:(h + 1) * CHUNK], vn16) for (b, h), vn16 in zip(chains, vn16s)]
        for (b, h), st, up in zip(chains, sts, upd):
            gl = grow_ref[b, c, h + DN_HEADS:h + DN_HEADS + 1, CHUNK - 1:CHUNK]
            st_ref[b, h] = st * jnp.exp(gl) + up
        for (b, h), wq, av in zip(chains, wqs, avs):
            o = wq[CHUNK:] + av
            on = o * lax.rsqrt(jnp.mean(o * o, axis=-1, keepdims=True) + EPS) * ng_ref[...]
            o_ref[b, rows, cols(h)] = (on * zg_ref[b, rows, cols(h)]).astype(o_ref.dtype)
        return carry

    lax.fori_loop(0, tm // CHUNK, step, 0)


def _delta_scan(u, w, qd, kd, at, zg, grow, norm_g):
    B, S, _ = u.shape
    tm = TM_SCAN
    nb = SCAN_BATCH
    row = lambda width: pl.BlockSpec((nb, tm, width), lambda b, t: (b, t, 0))
    return pl.pallas_call(
        _delta_scan_kernel,
        grid=(B // nb, S // tm),
        in_specs=[row(DN_DIM), row(DN_DIM), row(DN_DIM), row(DN_DIM), row(DN_HEADS * CHUNK), row(DN_DIM),
                  pl.BlockSpec((nb, tm // CHUNK, HALO, CHUNK), lambda b, t: (b, t, 0, 0)),
                  pl.BlockSpec((1, DN_HEAD_DIM), lambda b, t: (0, 0))],
        out_specs=row(DN_DIM),
        out_shape=jax.ShapeDtypeStruct((B, S, DN_DIM), BF16),
        scratch_shapes=[pltpu.VMEM((nb, DN_HEADS, DN_HEAD_DIM, DN_HEAD_DIM), F32)],
        compiler_params=pltpu.CompilerParams(
            dimension_semantics=("arbitrary", "arbitrary"), vmem_limit_bytes=VMEM_LIMIT),
        name="delta_scan",
    )(u, w, qd, kd, at, zg, grow, norm_g)


def _outproj_kernel(ya_ref, yb_ref, x_ref, wo_ref, g_ref, rw_ref,
                    x1_ref, h2_ref, ri_ref, rg_ref, cnt_ref, base_ref):
    tm = x_ref.shape[0]
    sub = tm // OUT_SPLIT
    parts = range(OUT_SPLIT)
    rows_of = lambda s: slice(s * sub, (s + 1) * sub)

    @pl.when(pl.program_id(0) == 0)
    def _():
        base_ref[...] = jnp.zeros(base_ref.shape, F32)

    def project(s):
        rows = rows_of(s)
        return _dot(jnp.concatenate([ya_ref[rows, :], yb_ref[rows, :]], axis=-1), wo_ref[...])

    def normalise(s, y):
        rows = rows_of(s)
        x1 = x_ref[rows, :] + y
        x1_ref[rows, :] = x1
        h = x1 * lax.rsqrt(jnp.mean(x1 * x1, axis=-1, keepdims=True) + EPS) * g_ref[...]
        _store_rows(h2_ref.at[pl.ds(s * sub * ROW_SLAB, sub * ROW_SLAB)], h)
        return h

    def router_logits(h):
        h_hi, h_lo = _split_bf16(h, 2)
        hi_prod = _dot(h_hi, rw_ref[...])
        return hi_prod[:, :LANES] + (hi_prod[:, LANES:] + _dot(h_lo, rw_ref[:, :LANES]))

    lane = lax.broadcasted_iota(I32, (sub, LANES), 1)
    lanef = lane.astype(F32)
    rr = lax.broadcasted_iota(I32, (sub, sub), 0)
    cc = lax.broadcasted_iota(I32, (sub, sub), 1)
    tri = jnp.where(cc < rr, 1.0, 0.0).astype(BF16)

    def route(s, logits):
        rows = rows_of(s)
        neg = jnp.float32(-1e30)
        big = jnp.float32(1e9)
        is_g = lane < N_GROUPS
        gl = jnp.where(is_g, logits, neg)
        gmax = jnp.max(gl, axis=-1, keepdims=True)
        gidx = jnp.min(jnp.where(gl == gmax, lanef, big), axis=-1, keepdims=True)
        gsum = jnp.sum(jnp.where(is_g, jnp.exp(gl - gmax), 0.0), axis=-1, keepdims=True)
        gprob = 1.0 / gsum
        lo = N_GROUPS + EXPERTS_PER_GROUP * gidx
        emask = (lanef >= lo) & (lanef < lo + EXPERTS_PER_GROUP)
        el = jnp.where(emask, logits, neg)
        e1 = jnp.max(el, axis=-1, keepdims=True)
        i1 = jnp.min(jnp.where(el == e1, lanef, big), axis=-1, keepdims=True)
        el2 = jnp.where(lanef == i1, neg, el)
        e2 = jnp.max(el2, axis=-1, keepdims=True)
        i2 = jnp.min(jnp.where(el2 == e2, lanef, big), axis=-1, keepdims=True)
        r = jnp.exp(e2 - e1)
        gate1 = gprob / (1.0 + r)
        gate2 = gprob * r / (1.0 + r)
        id1 = i1 - N_GROUPS
        id2 = i2 - N_GROUPS

        oh1 = jnp.where(lanef == id1, 1.0, 0.0).astype(F32)
        oh2 = jnp.where(lanef == id2, 1.0, 0.0).astype(F32)
        oh = oh1 + oh2
        before = _dot(tri, oh.astype(BF16)) + base_ref[...]
        rank1 = jnp.sum(oh1 * before, axis=-1, keepdims=True)
        rank2 = jnp.sum(oh2 * before, axis=-1, keepdims=True)
        base_ref[...] = base_ref[...] + jnp.sum(oh, axis=0, keepdims=True)

        ri = jnp.where(lane == 0, rank1, jnp.where(lane == 1, rank2,
                       jnp.where(lane == 2, id1, jnp.where(lane == 3, id2, 0.0))))
        ri_ref[rows, :] = ri.astype(I32)
        rg_ref[rows, :] = jnp.where(lane == 0, gate1, jnp.where(lane == 1, gate2, 0.0))

    ys = [project(s) for s in parts]
    hs = [normalise(s, y) for s, y in zip(parts, ys)]
    ls = [router_logits(h) for h in hs]
    for s, logits in zip(parts, ls):
        route(s, logits)
    cnt_ref[...] = base_ref[...]


def _outproj(ya, yb, x, w_out16, ffn_g, rw):
    N, D = x.shape
    tm = TM_OUT
    row = lambda width: pl.BlockSpec((tm, width), lambda i: (i, 0))
    full = lambda shape: pl.BlockSpec(shape, lambda i: (0,) * len(shape))
    return pl.pallas_call(
        _outproj_kernel,
        grid=(N // tm,),
        in_specs=[row(CONV_A_DIM), row(DN_DIM), row(D), full((D, D)), full((1, D)), full((D, 2 * LANES))],
        out_specs=(row(D), pl.BlockSpec((tm * ROW_SLAB, LANES), lambda i: (i, 0)), row(LANES), row(LANES),
                   full((1, LANES))),
        out_shape=(jax.ShapeDtypeStruct((N, D), F32),
                   jax.ShapeDtypeStruct((N * ROW_SLAB, LANES), F32),
                   jax.ShapeDtypeStruct((N, LANES), I32),
                   jax.ShapeDtypeStruct((N, LANES), F32),
                   jax.ShapeDtypeStruct((1, LANES), F32)),
        scratch_shapes=[pltpu.VMEM((1, LANES), F32)],
        compiler_params=pltpu.CompilerParams(
            dimension_semantics=("arbitrary",), vmem_limit_bytes=VMEM_LIMIT),
        name="outproj",
    )(ya, yb, x, w_out16, ffn_g, rw)


def _slots_kernel(ri_ref, sel_ref, tab_ref, o_ref):
    rif = ri_ref[...].astype(F32)
    lane = lax.broadcasted_iota(I32, rif.shape, 1)
    lanef = lane.astype(F32)
    ids = _dot(rif.astype(BF16), sel_ref[...])
    starts = [_dot(jnp.where(lanef == ids[:, k * LANES:(k + 1) * LANES], 1.0, 0.0).astype(BF16), tab_ref[...])
              for k in range(TOP_K)]
    slab = FFN_BLOCK * jnp.where(lane == 0, starts[0], starts[1]) + rif
    o_ref[...] = slab.T[0:HALO, :].astype(I32)


def _slots(ri, seg_start):
    N = ri.shape[0]
    tm = TM_SLOTS
    lane = jnp.arange(LANES, dtype=I32)
    sel = jnp.concatenate([(lane[:, None] == TOP_K + k) & (lane[None, :] >= 0) for k in range(TOP_K)], axis=1)
    tab = jnp.zeros((LANES, LANES), F32).at[:N_EXPERTS, :].set((seg_start // FFN_BLOCK)[:, None].astype(F32))
    out = pl.pallas_call(
        _slots_kernel,
        grid=(N // tm,),
        in_specs=[pl.BlockSpec((tm, LANES), lambda i: (i, 0)),
                  pl.BlockSpec((LANES, TOP_K * LANES), lambda i: (0, 0)),
                  pl.BlockSpec((LANES, LANES), lambda i: (0, 0))],
        out_specs=pl.BlockSpec((HALO, tm), lambda i: (0, i)),
        out_shape=jax.ShapeDtypeStruct((HALO, N), I32),
        compiler_params=pltpu.CompilerParams(dimension_semantics=("arbitrary",)),
        name="slots",
    )(ri, sel.astype(BF16), tab.astype(BF16))
    return out[:TOP_K].reshape(-1)


def _dispatch_kernel(seg_ref, slot0_ref, slot1_ref, h_ref, xs_ref, zero_ref, sem, zsem):
    tm = TM_DISPATCH
    bm = FFN_BLOCK
    n_blocks = xs_ref.shape[0] // (bm * ROW_SLAB)

    @pl.when(pl.program_id(0) == 0)
    def _():
        zero_ref[...] = jnp.zeros(zero_ref.shape, F32)

        def run_copy(first_row, n_rows):
            start = pl.multiple_of(first_row * ROW_SLAB, ROW_SLAB)
            return pltpu.make_async_copy(zero_ref.at[pl.ds(0, n_rows * ROW_SLAB)],
                                         xs_ref.at[pl.ds(start, n_rows * ROW_SLAB)], zsem)

        def block_copy(b):
            return pltpu.make_async_copy(zero_ref, _slab_block(xs_ref, b), zsem)

        def each_pad(fn):
            def per_expert(e, carry):
                row = seg_ref[e]
                n_pad = seg_ref[N_EXPERTS + e] - row
                run = bm // 2
                while run >= 1:
                    @pl.when((n_pad & run) != 0)
                    def _(row=row, run=run):
                        fn(run_copy(row, run))
                    row = row + (n_pad & run)
                    run //= 2
                return carry
            lax.fori_loop(0, N_EXPERTS, per_expert, 0)

            def per_block(b, c):
                fn(block_copy(b))
                return c
            lax.fori_loop(seg_ref[2 * N_EXPERTS], n_blocks, per_block, 0)

        each_pad(lambda cp: cp.start())
        each_pad(lambda cp: cp.wait())

    def issue(t, carry):
        for k, slot_ref in enumerate((slot0_ref, slot1_ref)):
            pltpu.make_async_copy(_slab(h_ref, t), _slab(xs_ref, slot_ref[t]), sem).start(priority=k)
        return carry

    lax.fori_loop(0, tm, issue, 0, unroll=ISSUE_UNROLL)
    for _ in range(TOP_K):
        pltpu.make_async_copy(h_ref, xs_ref.at[pl.ds(0, tm * ROW_SLAB)], sem).wait()


def _dispatch(seg, slots, h2, n_slots):
    N = h2.shape[0] // ROW_SLAB
    tm = TM_DISPATCH
    grid_spec = pltpu.PrefetchScalarGridSpec(
        num_scalar_prefetch=1,
        grid=(N // tm,),
        in_specs=[pl.BlockSpec((tm,), lambda i, s: (i,), memory_space=pltpu.SMEM),
                  pl.BlockSpec((tm,), lambda i, s: (N // tm + i,), memory_space=pltpu.SMEM),
                  pl.BlockSpec((tm * ROW_SLAB, LANES), lambda i, s: (i, 0))],
        out_specs=pl.BlockSpec(memory_space=pl.ANY),
        scratch_shapes=[pltpu.VMEM((FFN_BLOCK * ROW_SLAB, LANES), F32),
                        pltpu.SemaphoreType.DMA(()), pltpu.SemaphoreType.DMA(())],
    )
    return pl.pallas_call(
        _dispatch_kernel,
        grid_spec=grid_spec,
        out_shape=jax.ShapeDtypeStruct((n_slots * ROW_SLAB, LANES), F32),
        compiler_params=pltpu.CompilerParams(dimension_semantics=("arbitrary",)),
        name="dispatch",
    )(seg, slots, slots, h2)


def _ffn_kernel(blk_ref, xs_ref, wg_ref, wu_ref, wd_ref, ys_ref,
                xbuf, ybuf, wg16, wu16, wd16, xsem, ysem, zsem):
    e = pl.program_id(0)
    bm = FFN_BLOCK
    ring = FFN_RING
    n_blocks = ys_ref.shape[0] // (bm * ROW_SLAB)
    first = blk_ref[e]
    count = blk_ref[N_EXPERTS + e]
    n_used = blk_ref[2 * N_EXPERTS]

    def x_copy(g):
        return pltpu.make_async_copy(_slab_block(xs_ref, g), xbuf.at[g % ring], xsem.at[g % ring])

    def y_copy(g):
        return pltpu.make_async_copy(ybuf.at[g % ring], _slab_block(ys_ref, g), ysem.at[g % ring])

    def when_block(g, fn):
        @pl.when((g >= 0) & (g < n_used))
        def _():
            fn(g)

    @pl.when(e == 0)
    def _():
        for g in range(ring - 1):
            when_block(g, lambda g: x_copy(g).start(priority=RING_PRIORITY))

    @pl.when(count > 0)
    def _():
        wg16[...] = wg_ref[0].astype(BF16)
        wu16[...] = wu_ref[0].astype(BF16)
        wd16[...] = wd_ref[0].astype(BF16)

    def block(j, carry):
        g = first + j
        x_copy(g).wait()
        when_block(g + ring - 1, lambda g: x_copy(g).start(priority=RING_PRIORITY))
        when_block(g - ring, lambda g: y_copy(g).wait())

        x = _load_rows(xbuf.at[g % ring], bm).astype(BF16)
        sub = bm // FFN_SPLIT
        x_parts = [x[s * sub:(s + 1) * sub] for s in range(FFN_SPLIT)]
        gates = [_dot(xp, wg16[...]) for xp in x_parts]
        ups = [_dot(xp, wu16[...]) for xp in x_parts]
        acts = [(_silu(a) * b).astype(BF16) for a, b in zip(gates, ups)]
        y = jnp.concatenate([_dot(act, wd16[...]) for act in acts], axis=0)
        _store_rows(ybuf.at[g % ring], y)
        y_copy(g).start(priority=RING_PRIORITY)
        return carry

    lax.fori_loop(0, count, block, 0)

    @pl.when(e == pl.num_programs(0) - 1)
    def _():
        for back in range(ring, 0, -1):
            when_block(n_used - back, lambda g: y_copy(g).wait())
        ybuf[0] = jnp.zeros(ybuf.shape[1:], F32)

        def zero_copy(g):
            return pltpu.make_async_copy(ybuf.at[0], _slab_block(ys_ref, g), zsem)

        def start(g, c):
            zero_copy(g).start()
            return c

        def wait(g, c):
            zero_copy(g).wait()
            return c

        lax.fori_loop(n_used, n_blocks, start, 0)
        lax.fori_loop(n_used, n_blocks, wait, 0)


def _ffn(blk, xs, w_gate, w_up, w_down):
    bm = FFN_BLOCK
    D = D_MODEL
    weights = lambda shape: pl.BlockSpec((1,) + shape, lambda e, blk: (e, 0, 0))
    grid_spec = pltpu.PrefetchScalarGridSpec(
        num_scalar_prefetch=1,
        grid=(N_EXPERTS,),
        in_specs=[pl.BlockSpec(memory_space=pl.ANY),
                  weights((D, EXPERT_FF)), weights((D, EXPERT_FF)), weights((EXPERT_FF, D))],
        out_specs=pl.BlockSpec(memory_space=pl.ANY),
        scratch_shapes=[pltpu.VMEM((FFN_RING, bm * ROW_SLAB, LANES), F32),
                        pltpu.VMEM((FFN_RING, bm * ROW_SLAB, LANES), F32),
                        pltpu.VMEM((D, EXPERT_FF), BF16), pltpu.VMEM((D, EXPERT_FF), BF16),
                        pltpu.VMEM((EXPERT_FF, D), BF16),
                        pltpu.SemaphoreType.DMA((FFN_RING,)), pltpu.SemaphoreType.DMA((FFN_RING,)),
                        pltpu.SemaphoreType.DMA(())],
    )
    return pl.pallas_call(
        _ffn_kernel,
        grid_spec=grid_spec,
        out_shape=jax.ShapeDtypeStruct(xs.shape, F32),
        compiler_params=pltpu.CompilerParams(
            dimension_semantics=("arbitrary",), vmem_limit_bytes=VMEM_LIMIT),
        name="ffn",
    )(blk, xs, w_gate, w_up, w_down)


def _combine_kernel(slot0_ref, slot1_ref, next0_ref, next1_ref, ys_ref, x1_ref, rg_ref, g_ref, o_ref,
                    buf_ref, sems):
    tm = TM_COMBINE
    i = pl.program_id(0)
    half = i % 2

    def gather(slot_refs, s):
        def issue(t, carry):
            for k, slot_ref in enumerate(slot_refs):
                pltpu.make_async_copy(_slab(ys_ref, slot_ref[t]), _slab(buf_ref.at[s, k], t),
                                      sems.at[s]).start(priority=k)
            return carry
        lax.fori_loop(0, tm, issue, 0, unroll=ISSUE_UNROLL)

    @pl.when(i == 0)
    def _():
        gather((slot0_ref, slot1_ref), 0)

    @pl.when(i + 1 < pl.num_programs(0))
    def _():
        gather((next0_ref, next1_ref), 1 - half)

    for k in range(TOP_K):
        pltpu.make_async_copy(ys_ref.at[pl.ds(0, tm * ROW_SLAB)], buf_ref.at[half, k], sems.at[half]).wait()

    rg = rg_ref[...]
    moe = (_load_rows(buf_ref.at[half, 0], tm) * rg[:, 0:1]
           + _load_rows(buf_ref.at[half, 1], tm) * rg[:, 1:2])
    x2 = x1_ref[...] + moe
    o_ref[...] = x2 * lax.rsqrt(jnp.mean(x2 * x2, axis=-1, keepdims=True) + EPS) * g_ref[...]


def _combine(slots, ys, x1, rg, final_g):
    N, D = x1.shape
    tm = TM_COMBINE
    n_tiles = N // tm

    def slot_block(k, ahead, i):
        return (k * n_tiles + jnp.minimum(i + ahead, n_tiles - 1),)

    return pl.pallas_call(
        _combine_kernel,
        grid=(n_tiles,),
        in_specs=[pl.BlockSpec((tm,), functools.partial(slot_block, k, ahead), memory_space=pltpu.SMEM)
                  for ahead in (0, 1) for k in range(TOP_K)]
                 + [pl.BlockSpec(memory_space=pl.ANY),
                  pl.BlockSpec((tm, D), lambda i: (i, 0)),
                  pl.BlockSpec((tm, LANES), lambda i: (i, 0)),
                  pl.BlockSpec((1, D), lambda i: (0, 0))],
        out_specs=pl.BlockSpec((tm, D), lambda i: (i, 0)),
        out_shape=jax.ShapeDtypeStruct((N, D), F32),
        scratch_shapes=[pltpu.VMEM((2, TOP_K, tm * ROW_SLAB, LANES), F32), pltpu.SemaphoreType.DMA((2,))],
        compiler_params=pltpu.CompilerParams(
            dimension_semantics=("arbitrary",), vmem_limit_bytes=VMEM_LIMIT),
        name="combine",
    )(slots, slots, slots, slots, ys, x1, rg, final_g)


def _lane_vec(values, offset):
    return jnp.zeros((1, LANES), F32).at[0, offset:offset + values.shape[0]].set(values.astype(F32))


def kernel(x, mix_norm_g, w_in, conv_a_w, conv_a_norm_g, dn_conv_w, dn_a_log, dn_dt_bias, dn_norm_g,
           w_out, ffn_norm_g, router_group_w, router_expert_w, w_gate, w_up, w_down, final_norm_g):
    B, S, D = x.shape
    N = B * S
    depth = w_in.shape[0]
    assert depth == 1, "single-layer block: the final RMSNorm is fused into the layer's combine step"
    group_of = jnp.arange(CONV_A_DIM, dtype=I32) // CONV_A_GROUP_DIM
    gmat = jnp.where(group_of[:, None] == group_of[None, :], 1.0 / CONV_A_GROUP_DIM, 0.0).astype(BF16)
    bm = FFN_BLOCK
    n_blocks = (N * TOP_K) // bm + N_EXPERTS
    for l in range(depth):
        ya, q, k, v, zg, gcol, grow = _inproj(
            x, mix_norm_g[l][None, :], w_in[l].astype(BF16), conv_a_w[l], conv_a_norm_g[l][None, :], dn_conv_w[l],
            _lane_vec(dn_a_log[l], DN_HEADS), _lane_vec(dn_dt_bias[l], DN_HEADS), gmat)
        u, w, qd, kd, at = _delta_prep(q, k, v, gcol, grow)
        yb = _delta_scan(u, w, qd, kd, at, zg, grow, dn_norm_g[l][None, :])
        rw = jnp.pad(jnp.concatenate([router_group_w[l], router_expert_w[l]], axis=1),
                     ((0, 0), (0, LANES - N_GROUPS - N_EXPERTS)))
        rw_hi = rw.astype(BF16)
        rw = jnp.concatenate([rw_hi, (rw - rw_hi.astype(F32)).astype(BF16)], axis=1)
        x1, h2, ri, rg, cnt = _outproj(ya.reshape(N, CONV_A_DIM), yb.reshape(N, DN_DIM), x.reshape(N, D),
                                       w_out[l].astype(BF16), ffn_norm_g[l][None, :], rw)
        counts = cnt[0, :N_EXPERTS].astype(I32)
        padded = (counts + bm - 1) // bm * bm
        seg_end = jnp.cumsum(padded).astype(I32)
        seg_start = seg_end - padded
        n_used = (seg_end[-1:] // bm)
        slots = _slots(ri, seg_start)
        seg = jnp.concatenate([seg_start + counts, seg_end, n_used])
        xs = _dispatch(seg, slots, h2, n_blocks * bm)
        blk = jnp.concatenate([seg_start // bm, padded // bm, n_used])
        ys = _ffn(blk, xs, w_gate[l], w_up[l], w_down[l])
        x = _combine(slots, ys, x1, rg, final_norm_g[None, :]).reshape(B, S, D)
    return x
```
